```python
import jax, jax.numpy as jnp
from jax import lax
import numpy as np

D_MODEL = 2048
BATCH = 8
SEQ = 2048
DEPTH = 2

GRID_W = 64
CTX_LEN = 256
N_MIXERS = 2
N_RWKV = (DEPTH + 1) // 2
N_CONV = DEPTH // 2
HEAD_SIZE = 64
N_HEADS = D_MODEL // HEAD_SIZE
LORA_DECAY = max(32, int(round(1.8 * D_MODEL ** 0.5 / 32)) * 32)
LORA_A = max(32, int(round(1.8 * D_MODEL ** 0.5 / 32)) * 32)
LORA_GATE = max(32, int(round(0.6 * D_MODEL ** 0.8 / 32)) * 32)
CONV_WIDTH = 3
D_FF = ((8 * D_MODEL // 3 + 255) // 256) * 256
NORM_EPS = 1e-6
GN_EPS = 64e-5

kernel_name = "rwkv7_shortconv_hybrid_dit"


def rmsnorm(h, g):
    hf = h.astype(jnp.float32)
    hn = hf * lax.rsqrt(jnp.mean(hf * hf, axis=-1, keepdims=True) + NORM_EPS)
    return hn.astype(h.dtype) * g


def modulate(h, shift, scale):
    return h * (1 + scale) + shift


def split_heads(t):
    return t.reshape(t.shape[0], t.shape[1], N_HEADS, HEAD_SIZE)


def grid_shift(h):
    Bn, T, D = h.shape
    rows = T // GRID_W
    g = h.reshape(Bn, rows, GRID_W, D)
    q = D // 4
    left = jnp.pad(g[:, :, :-1, :q], ((0, 0), (0, 0), (1, 0), (0, 0)))
    right = jnp.pad(g[:, :, 1:, q:2 * q], ((0, 0), (0, 0), (0, 1), (0, 0)))
    up = jnp.pad(g[:, :-1, :, 2 * q:3 * q], ((0, 0), (1, 0), (0, 0), (0, 0)))
    down = jnp.pad(g[:, 1:, :, 3 * q:], ((0, 0), (0, 1), (0, 0), (0, 0)))
    return jnp.concatenate([left, right, up, down], axis=-1).reshape(Bn, T, D)


def seq_shift(h):
    half = h.shape[-1] // 2
    prev = jnp.pad(h[:, :-1, :half], ((0, 0), (1, 0), (0, 0)))
    nxt = jnp.pad(h[:, 1:, half:], ((0, 0), (0, 1), (0, 0)))
    return jnp.concatenate([prev, nxt], axis=-1)


def wkv_scan(state0, r, decay, k, v, a, b, reverse):
    def step(S, inp):
        r_t, w_t, k_t, v_t, a_t, b_t = inp
        sa = jnp.einsum('bhvk,bhk->bhv', S, a_t)
        S = (S * w_t[:, :, None, :] + sa[..., None] * b_t[:, :, None, :]
             + v_t[..., None] * k_t[:, :, None, :])
        return S, jnp.einsum('bhvk,bhk->bhv', S, r_t)
    xs = tuple(jnp.swapaxes(t.astype(jnp.float32), 0, 1) for t in (r, decay, k, v, a, b))
    state, ys = lax.scan(step, state0, xs, reverse=reverse)
    return state, jnp.swapaxes(ys, 0, 1)


def rwkv_shared(h, h_shift, mix, wr, wk, wv, g1, g2, k_k):
    xx = h_shift - h
    xr, xw, xk, xv, xa, xg = [h + xx * mix[m] for m in range(6)]
    r = split_heads(xr @ wr)
    k = xk @ wk
    v = split_heads(xv @ wv)
    g = jax.nn.sigmoid(xg @ g1) @ g2
    kk = split_heads((k * k_k).astype(jnp.float32))
    kk = kk / jnp.maximum(jnp.sqrt(jnp.sum(kk * kk, axis=-1, keepdims=True)), 1e-12)
    return r, k, v, kk, g, xw, xa


def rwkv_direction(xw, xa, k, kk, w0, w1, w2, a0, a1, a2, k_a):
    log_w = -jax.nn.softplus(-(w0 + jnp.tanh(xw @ w1) @ w2).astype(jnp.float32)) - 0.5
    decay = jnp.exp(-jnp.exp(log_w))
    a = jax.nn.sigmoid(a0 + (xa @ a1) @ a2)
    k_d = k * (1 + (a - 1) * k_a)
    a_h = split_heads(a).astype(jnp.float32)
    return split_heads(decay), split_heads(k_d), -kk, kk * a_h


def rwkv_readout(y, r, ksum, v, g, r_k, ln_w, ln_b, wo, dtype):
    Bn, T = y.shape[0], y.shape[1]
    mu = jnp.mean(y, axis=-1, keepdims=True)
    var = jnp.mean(jnp.square(y - mu), axis=-1, keepdims=True)
    o = ((y - mu) * lax.rsqrt(var + GN_EPS)).reshape(Bn, T, D_MODEL) * ln_w + ln_b
    bonus = jnp.sum(r * ksum * r_k, axis=-1, keepdims=True) * v
    o = o + bonus.reshape(Bn, T, D_MODEL)
    return (o.astype(dtype) * g) @ wo


def rwkv7_mix(hx, hc, need_ctx, mix, wr, wk, wv, wo, w0, w1, w2, a0, a1, a2,
              g1, g2, k_k, k_a, r_k, ln_w, ln_b):
    r_c, k_c, v_c, kk_c, g_c, xw_c, xa_c = rwkv_shared(hc, seq_shift(hc), mix, wr, wk, wv, g1, g2, k_k)
    r_x, k_x, v_x, kk_x, g_x, xw_x, xa_x = rwkv_shared(hx, grid_shift(hx), mix, wr, wk, wv, g1, g2, k_k)
    state0 = jnp.zeros((hx.shape[0], N_HEADS, HEAD_SIZE, HEAD_SIZE), jnp.float32)
    y_c = 0.0
    y_x = 0.0
    ksum_c = 0.0
    ksum_x = 0.0
    for d, rev in enumerate((False, True)):
        dec_c, kd_c, aa_c, bb_c = rwkv_direction(xw_c, xa_c, k_c, kk_c, w0[d], w1[d], w2[d], a0[d], a1[d], a2[d], k_a)
        state_c, yd_c = wkv_scan(state0, r_c, dec_c, kd_c, v_c, aa_c, bb_c, rev)
        dec_x, kd_x, aa_x, bb_x = rwkv_direction(xw_x, xa_x, k_x, kk_x, w0[d], w1[d], w2[d], a0[d], a1[d], a2[d], k_a)
        _, yd_x = wkv_scan(state_c, r_x, dec_x, kd_x, v_x, aa_x, bb_x, rev)
        y_x = y_x + yd_x
        ksum_x = ksum_x + kd_x
        if need_ctx:
            y_c = y_c + yd_c
            ksum_c = ksum_c + kd_c
    out_x = rwkv_readout(y_x, r_x, ksum_x, v_x, g_x, r_k, ln_w, ln_b, wo, hx.dtype)
    out_c = rwkv_readout(y_c, r_c, ksum_c, v_c, g_c, r_k, ln_w, ln_b, wo, hc.dtype) if need_ctx else None
    return out_x, out_c


def short_conv(h, w_in, conv_w, w_out):
    gb, gc, u = jnp.split(h @ w_in, 3, axis=-1)
    z = jnp.pad(gc * u, ((0, 0), (1, 1), (0, 0)))
    conv = z[:, :-2] * conv_w[0] + z[:, 1:-1] * conv_w[1] + z[:, 2:] * conv_w[2]
    return (gb * conv) @ w_out


def swiglu(h, w13, w2):
    a, b = jnp.split(h @ w13, 2, axis=-1)
    return (jax.nn.silu(a) * b) @ w2


def _fwd_setup_inputs(seed: int = 0) -> dict:
    key = jax.random.key(seed)
    ks = iter(jax.random.split(key, 48))

    def nrm(shape, scale):
        return jax.random.normal(next(ks), shape, jnp.float32) * scale

    def unif(shape, lo, hi):
        return jax.random.uniform(next(ks), shape, jnp.float32, lo, hi)

    D, F = D_MODEL, D_FF
    inv = D ** -0.5
    return {
        "x": nrm((BATCH, SEQ, D), 1.0),
        "c": nrm((BATCH, D), 1.0),
        "ctx": nrm((BATCH, CTX_LEN, D), 1.0),
        "c_ctx": nrm((D,), 1.0),
        "norm1_g": 1.0 + nrm((DEPTH, D), 0.02),
        "norm2_g": 1.0 + nrm((DEPTH, D), 0.02),
        "ada_w": nrm((DEPTH, D, 6 * D), 0.5 * inv),
        "ada_b": nrm((DEPTH, 6 * D), 0.02),
        "rw_mix": unif((N_RWKV, 6, D), 0.0, 1.0),
        "rw_wr": nrm((N_RWKV, D, D), inv),
        "rw_wk": nrm((N_RWKV, D, D), inv),
        "rw_wv": nrm((N_RWKV, D, D), inv),
        "rw_wo": nrm((N_RWKV, D, D), inv),
        "rw_w0": unif((N_RWKV, 2, D), -6.5, -1.5),
        "rw_w1": nrm((N_RWKV, 2, D, LORA_DECAY), inv),
        "rw_w2": nrm((N_RWKV, 2, LORA_DECAY, D), 0.5 * LORA_DECAY ** -0.5),
        "rw_a0": nrm((N_RWKV, 2, D), 0.1),
        "rw_a1": nrm((N_RWKV, 2, D, LORA_A), inv),
        "rw_a2": nrm((N_RWKV, 2, LORA_A, D), 0.5 * LORA_A ** -0.5),
        "rw_g1": nrm((N_RWKV, D, LORA_GATE), inv),
        "rw_g2": nrm((N_RWKV, LORA_GATE, D), LORA_GATE ** -0.5),
        "rw_kk": 0.85 + nrm((N_RWKV, D), 0.05),
        "rw_ka": 1.0 + nrm((N_RWKV, D), 0.05),
        "rw_rk": nrm((N_RWKV, N_HEADS, HEAD_SIZE), 0.1),
        "rw_lnw": 1.0 + nrm((N_RWKV, D), 0.02),
        "rw_lnb": nrm((N_RWKV, D), 0.02),
        "sc_win": nrm((N_CONV, D, 3 * D), inv),
        "sc_conv": nrm((N_CONV, CONV_WIDTH, D), CONV_WIDTH ** -0.5),
        "sc_wout": nrm((N_CONV, D, D), inv),
        "ffn_w13": nrm((DEPTH, D, 2 * F), inv),
        "ffn_w2": nrm((DEPTH, F, D), F ** -0.5),
        "final_g": 1.0 + nrm((D,), 0.02),
    }


def _fwd_reference(x, c, ctx, c_ctx, norm1_g, norm2_g, ada_w, ada_b,
              rw_mix, rw_wr, rw_wk, rw_wv, rw_wo, rw_w0, rw_w1, rw_w2,
              rw_a0, rw_a1, rw_a2, rw_g1, rw_g2, rw_kk, rw_ka, rw_rk, rw_lnw, rw_lnb,
              sc_win, sc_conv, sc_wout, ffn_w13, ffn_w2, final_g):
    cond_x = jax.nn.silu(c)
    cond_c = jax.nn.silu(c_ctx)
    for i in range(DEPTH):
        last = i == DEPTH - 1
        is_rwkv = i % N_MIXERS == 0
        j = i // N_MIXERS
        mod_x = (cond_x @ ada_w[i] + ada_b[i])[:, None, :]
        sh1_x, sc1_x, gt1_x, sh2_x, sc2_x, gt2_x = jnp.split(mod_x, 6, axis=-1)
        hx = modulate(rmsnorm(x, norm1_g[i]), sh1_x, sc1_x)
        ctx_used = is_rwkv or not last
        if ctx_used:
            mod_c = (cond_c @ ada_w[i] + ada_b[i])[None, None, :]
            sh1_c, sc1_c, gt1_c, sh2_c, sc2_c, gt2_c = jnp.split(mod_c, 6, axis=-1)
            hc = modulate(rmsnorm(ctx, norm1_g[i]), sh1_c, sc1_c)
        if is_rwkv:
            yx, yc = rwkv7_mix(hx, hc, not last, rw_mix[j], rw_wr[j], rw_wk[j], rw_wv[j], rw_wo[j],
                               rw_w0[j], rw_w1[j], rw_w2[j], rw_a0[j], rw_a1[j], rw_a2[j],
                               rw_g1[j], rw_g2[j], rw_kk[j], rw_ka[j], rw_rk[j], rw_lnw[j], rw_lnb[j])
        else:
            yx = short_conv(hx, sc_win[j], sc_conv[j], sc_wout[j])
            yc = short_conv(hc, sc_win[j], sc_conv[j], sc_wout[j]) if not last else None
        x = x + gt1_x * yx
        x = x + gt2_x * swiglu(modulate(rmsnorm(x, norm2_g[i]), sh2_x, sc2_x), ffn_w13[i], ffn_w2[i])
        if not last:
            ctx = ctx + gt1_c * yc
            ctx = ctx + gt2_c * swiglu(modulate(rmsnorm(ctx, norm2_g[i]), sh2_c, sc2_c), ffn_w13[i], ffn_w2[i])
    return rmsnorm(x, final_g)


import jax as _jax
import jax.numpy as _jnp

TWIN_FORMAT = 'train_step'
FWD_PARAMS = ['x', 'c', 'ctx', 'c_ctx', 'norm1_g', 'norm2_g', 'ada_w', 'ada_b', 'rw_mix', 'rw_wr', 'rw_wk', 'rw_wv', 'rw_wo', 'rw_w0', 'rw_w1', 'rw_w2', 'rw_a0', 'rw_a1', 'rw_a2', 'rw_g1', 'rw_g2', 'rw_kk', 'rw_ka', 'rw_rk', 'rw_lnw', 'rw_lnb', 'sc_win', 'sc_conv', 'sc_wout', 'ffn_w13', 'ffn_w2', 'final_g']
TWIN_WEIGHTS = ['c_ctx', 'norm1_g', 'norm2_g', 'ada_w', 'ada_b', 'rw_mix', 'rw_wr', 'rw_wk', 'rw_wv', 'rw_wo', 'rw_w0', 'rw_w1', 'rw_w2', 'rw_a0', 'rw_a1', 'rw_a2', 'rw_g1', 'rw_g2', 'rw_kk', 'rw_ka', 'rw_rk', 'rw_lnw', 'rw_lnb', 'sc_win', 'sc_conv', 'sc_wout', 'ffn_w13', 'ffn_w2', 'final_g']
TWIN_DIFF_INPUT = 'x'
TWIN_INPUTS = ['x', 'c', 'ctx', 'c_ctx', 'norm1_g', 'norm2_g', 'ada_w', 'ada_b', 'rw_mix', 'rw_wr', 'rw_wk', 'rw_wv', 'rw_wo', 'rw_w0', 'rw_w1', 'rw_w2', 'rw_a0', 'rw_a1', 'rw_a2', 'rw_g1', 'rw_g2', 'rw_kk', 'rw_ka', 'rw_rk', 'rw_lnw', 'rw_lnb', 'sc_win', 'sc_conv', 'sc_wout', 'ffn_w13', 'ffn_w2', 'final_g', 'loss_target', 'm_c_ctx', 'm_norm1_g', 'm_norm2_g', 'm_ada_w', 'm_ada_b', 'm_rw_mix', 'm_rw_wr', 'm_rw_wk', 'm_rw_wv', 'm_rw_wo', 'm_rw_w0', 'm_rw_w1', 'm_rw_w2', 'm_rw_a0', 'm_rw_a1', 'm_rw_a2', 'm_rw_g1', 'm_rw_g2', 'm_rw_kk', 'm_rw_ka', 'm_rw_rk', 'm_rw_lnw', 'm_rw_lnb', 'm_sc_win', 'm_sc_conv', 'm_sc_wout', 'm_ffn_w13', 'm_ffn_w2', 'm_final_g', 'v_c_ctx', 'v_norm1_g', 'v_norm2_g', 'v_ada_w', 'v_ada_b', 'v_rw_mix', 'v_rw_wr', 'v_rw_wk', 'v_rw_wv', 'v_rw_wo', 'v_rw_w0', 'v_rw_w1', 'v_rw_w2', 'v_rw_a0', 'v_rw_a1', 'v_rw_a2', 'v_rw_g1', 'v_rw_g2', 'v_rw_kk', 'v_rw_ka', 'v_rw_rk', 'v_rw_lnw', 'v_rw_lnb', 'v_sc_win', 'v_sc_conv', 'v_sc_wout', 'v_ffn_w13', 'v_ffn_w2', 'v_final_g']
TWIN_OUTPUTS = ['loss', 'grad_x', 'grad_c_ctx', 'grad_norm1_g', 'grad_norm2_g', 'grad_ada_w', 'grad_ada_b', 'grad_rw_mix', 'grad_rw_wr', 'grad_rw_wk', 'grad_rw_wv', 'grad_rw_wo', 'grad_rw_w0', 'grad_rw_w1', 'grad_rw_w2', 'grad_rw_a0', 'grad_rw_a1', 'grad_rw_a2', 'grad_rw_g1', 'grad_rw_g2', 'grad_rw_kk', 'grad_rw_ka', 'grad_rw_rk', 'grad_rw_lnw', 'grad_rw_lnb', 'grad_sc_win', 'grad_sc_conv', 'grad_sc_wout', 'grad_ffn_w13', 'grad_ffn_w2', 'grad_final_g', 'delta_c_ctx', 'delta_norm1_g', 'delta_norm2_g', 'delta_ada_w', 'delta_ada_b', 'delta_rw_mix', 'delta_rw_wr', 'delta_rw_wk', 'delta_rw_wv', 'delta_rw_wo', 'delta_rw_w0', 'delta_rw_w1', 'delta_rw_w2', 'delta_rw_a0', 'delta_rw_a1', 'delta_rw_a2', 'delta_rw_g1', 'delta_rw_g2', 'delta_rw_kk', 'delta_rw_ka', 'delta_rw_rk', 'delta_rw_lnw', 'delta_rw_lnb', 'delta_sc_win', 'delta_sc_conv', 'delta_sc_wout', 'delta_ffn_w13', 'delta_ffn_w2', 'delta_final_g', 'new_m_c_ctx', 'new_m_norm1_g', 'new_m_norm2_g', 'new_m_ada_w', 'new_m_ada_b', 'new_m_rw_mix', 'new_m_rw_wr', 'new_m_rw_wk', 'new_m_rw_wv', 'new_m_rw_wo', 'new_m_rw_w0', 'new_m_rw_w1', 'new_m_rw_w2', 'new_m_rw_a0', 'new_m_rw_a1', 'new_m_rw_a2', 'new_m_rw_g1', 'new_m_rw_g2', 'new_m_rw_kk', 'new_m_rw_ka', 'new_m_rw_rk', 'new_m_rw_lnw', 'new_m_rw_lnb', 'new_m_sc_win', 'new_m_sc_conv', 'new_m_sc_wout', 'new_m_ffn_w13', 'new_m_ffn_w2', 'new_m_final_g', 'new_v_c_ctx', 'new_v_norm1_g', 'new_v_norm2_g', 'new_v_ada_w', 'new_v_ada_b', 'new_v_rw_mix', 'new_v_rw_wr', 'new_v_rw_wk', 'new_v_rw_wv', 'new_v_rw_wo', 'new_v_rw_w0', 'new_v_rw_w1', 'new_v_rw_w2', 'new_v_rw_a0', 'new_v_rw_a1', 'new_v_rw_a2', 'new_v_rw_g1', 'new_v_rw_g2', 'new_v_rw_kk', 'new_v_rw_ka', 'new_v_rw_rk', 'new_v_rw_lnw', 'new_v_rw_lnb', 'new_v_sc_win', 'new_v_sc_conv', 'new_v_sc_wout', 'new_v_ffn_w13', 'new_v_ffn_w2', 'new_v_final_g']
TWIN_LEAF_KINDS = {'loss': 'loss', 'grad_x': 'grad_x', 'grad_c_ctx': 'grad_w', 'grad_norm1_g': 'grad_w', 'grad_norm2_g': 'grad_w', 'grad_ada_w': 'grad_w', 'grad_ada_b': 'grad_w', 'grad_rw_mix': 'grad_w', 'grad_rw_wr': 'grad_w', 'grad_rw_wk': 'grad_w', 'grad_rw_wv': 'grad_w', 'grad_rw_wo': 'grad_w', 'grad_rw_w0': 'grad_w', 'grad_rw_w1': 'grad_w', 'grad_rw_w2': 'grad_w', 'grad_rw_a0': 'grad_w', 'grad_rw_a1': 'grad_w', 'grad_rw_a2': 'grad_w', 'grad_rw_g1': 'grad_w', 'grad_rw_g2': 'grad_w', 'grad_rw_kk': 'grad_w', 'grad_rw_ka': 'grad_w', 'grad_rw_rk': 'grad_w', 'grad_rw_lnw': 'grad_w', 'grad_rw_lnb': 'grad_w', 'grad_sc_win': 'grad_w', 'grad_sc_conv': 'grad_w', 'grad_sc_wout': 'grad_w', 'grad_ffn_w13': 'grad_w', 'grad_ffn_w2': 'grad_w', 'grad_final_g': 'grad_w', 'delta_c_ctx': 'delta_w', 'delta_norm1_g': 'delta_w', 'delta_norm2_g': 'delta_w', 'delta_ada_w': 'delta_w', 'delta_ada_b': 'delta_w', 'delta_rw_mix': 'delta_w', 'delta_rw_wr': 'delta_w', 'delta_rw_wk': 'delta_w', 'delta_rw_wv': 'delta_w', 'delta_rw_wo': 'delta_w', 'delta_rw_w0': 'delta_w', 'delta_rw_w1': 'delta_w', 'delta_rw_w2': 'delta_w', 'delta_rw_a0': 'delta_w', 'delta_rw_a1': 'delta_w', 'delta_rw_a2': 'delta_w', 'delta_rw_g1': 'delta_w', 'delta_rw_g2': 'delta_w', 'delta_rw_kk': 'delta_w', 'delta_rw_ka': 'delta_w', 'delta_rw_rk': 'delta_w', 'delta_rw_lnw': 'delta_w', 'delta_rw_lnb': 'delta_w', 'delta_sc_win': 'delta_w', 'delta_sc_conv': 'delta_w', 'delta_sc_wout': 'delta_w', 'delta_ffn_w13': 'delta_w', 'delta_ffn_w2': 'delta_w', 'delta_final_g': 'delta_w', 'new_m_c_ctx': 'new_m', 'new_m_norm1_g': 'new_m', 'new_m_norm2_g': 'new_m', 'new_m_ada_w': 'new_m', 'new_m_ada_b': 'new_m', 'new_m_rw_mix': 'new_m', 'new_m_rw_wr': 'new_m', 'new_m_rw_wk': 'new_m', 'new_m_rw_wv': 'new_m', 'new_m_rw_wo': 'new_m', 'new_m_rw_w0': 'new_m', 'new_m_rw_w1': 'new_m', 'new_m_rw_w2': 'new_m', 'new_m_rw_a0': 'new_m', 'new_m_rw_a1': 'new_m', 'new_m_rw_a2': 'new_m', 'new_m_rw_g1': 'new_m', 'new_m_rw_g2': 'new_m', 'new_m_rw_kk': 'new_m', 'new_m_rw_ka': 'new_m', 'new_m_rw_rk': 'new_m', 'new_m_rw_lnw': 'new_m', 'new_m_rw_lnb': 'new_m', 'new_m_sc_win': 'new_m', 'new_m_sc_conv': 'new_m', 'new_m_sc_wout': 'new_m', 'new_m_ffn_w13': 'new_m', 'new_m_ffn_w2': 'new_m', 'new_m_final_g': 'new_m', 'new_v_c_ctx': 'new_v', 'new_v_norm1_g': 'new_v', 'new_v_norm2_g': 'new_v', 'new_v_ada_w': 'new_v', 'new_v_ada_b': 'new_v', 'new_v_rw_mix': 'new_v', 'new_v_rw_wr': 'new_v', 'new_v_rw_wk': 'new_v', 'new_v_rw_wv': 'new_v', 'new_v_rw_wo': 'new_v', 'new_v_rw_w0': 'new_v', 'new_v_rw_w1': 'new_v', 'new_v_rw_w2': 'new_v', 'new_v_rw_a0': 'new_v', 'new_v_rw_a1': 'new_v', 'new_v_rw_a2': 'new_v', 'new_v_rw_g1': 'new_v', 'new_v_rw_g2': 'new_v', 'new_v_rw_kk': 'new_v', 'new_v_rw_ka': 'new_v', 'new_v_rw_rk': 'new_v', 'new_v_rw_lnw': 'new_v', 'new_v_rw_lnb': 'new_v', 'new_v_sc_win': 'new_v', 'new_v_sc_conv': 'new_v', 'new_v_sc_wout': 'new_v', 'new_v_ffn_w13': 'new_v', 'new_v_ffn_w2': 'new_v', 'new_v_final_g': 'new_v'}


def _forward(args):
    return _fwd_reference(*[args[k] for k in FWD_PARAMS])


def _output_shape():
    out = _jax.eval_shape(lambda: _forward(_fwd_setup_inputs(0)))
    return out.shape, out.dtype

N_MICROBATCH = 1
ADAM_LR = 0.001
ADAM_B1 = 0.9
ADAM_B2 = 0.999
ADAM_EPS = 1e-08
ADAM_WD = 0.01
ADAM_STEP = 10
PER_EXAMPLE_BATCH_AXIS = {'x': 0, 'c': 0, 'ctx': 0, 'loss_target': 0}
SHARED_INPUTS = []
_WEIGHT_DTYPES = {'c_ctx': _jnp.float32, 'norm1_g': _jnp.float32, 'norm2_g': _jnp.float32, 'ada_w': _jnp.float32, 'ada_b': _jnp.float32, 'rw_mix': _jnp.float32, 'rw_wr': _jnp.float32, 'rw_wk': _jnp.float32, 'rw_wv': _jnp.float32, 'rw_wo': _jnp.float32, 'rw_w0': _jnp.float32, 'rw_w1': _jnp.float32, 'rw_w2': _jnp.float32, 'rw_a0': _jnp.float32, 'rw_a1': _jnp.float32, 'rw_a2': _jnp.float32, 'rw_g1': _jnp.float32, 'rw_g2': _jnp.float32, 'rw_kk': _jnp.float32, 'rw_ka': _jnp.float32, 'rw_rk': _jnp.float32, 'rw_lnw': _jnp.float32, 'rw_lnb': _jnp.float32, 'sc_win': _jnp.float32, 'sc_conv': _jnp.float32, 'sc_wout': _jnp.float32, 'ffn_w13': _jnp.float32, 'ffn_w2': _jnp.float32, 'final_g': _jnp.float32}
MOMENT_SCALE = {'c_ctx': 3.575477e-03, 'norm1_g': 3.097563e-02, 'norm2_g': 1.880152e-02, 'ada_w': 2.417651e-02, 'ada_b': 4.129922e-02, 'rw_mix': 1.647983e-02, 'rw_wr': 1.409302e-02, 'rw_wk': 1.508504e-02, 'rw_wv': 1.377170e-02, 'rw_wo': 1.364764e-02, 'rw_w0': 2.814628e-03, 'rw_w1': 1.367154e-03, 'rw_w2': 6.334486e-04, 'rw_a0': 3.337055e-03, 'rw_a1': 8.463818e-03, 'rw_a2': 3.652007e-03, 'rw_g1': 1.415125e-02, 'rw_g2': 1.344733e-02, 'rw_kk': 2.875442e-02, 'rw_ka': 2.531381e-02, 'rw_rk': 6.786014e-02, 'rw_lnw': 1.154062e-02, 'rw_lnb': 1.160903e-02, 'sc_win': 2.244582e-02, 'sc_conv': 2.293244e-02, 'sc_wout': 2.245223e-02, 'ffn_w13': 8.307957e-03, 'ffn_w2': 1.355246e-02, 'final_g': 8.030362e+00}


def _to_microbatches(a, axis):
    t = _jnp.moveaxis(a, axis, 0)
    t = t.reshape((N_MICROBATCH, t.shape[0] // N_MICROBATCH) + t.shape[1:])
    return _jnp.moveaxis(t, 1, axis + 1)


def setup_inputs(seed: int = 0) -> dict:
    inp = _fwd_setup_inputs(seed)
    key = _jax.random.fold_in(_jax.random.key(seed), 7919)
    shape, _ = _output_shape()
    out = dict(inp)
    out["loss_target"] = _jax.random.normal(_jax.random.fold_in(key, 0), shape, _jnp.float32)
    for i, name in enumerate(TWIN_WEIGHTS):
        w = inp[name].astype(_jnp.float32)
        if MOMENT_SCALE is None:
            s = _jnp.sqrt(_jnp.mean(_jnp.square(w)) + 1e-30)
        else:
            s = MOMENT_SCALE[name]
        km, kv = _jax.random.split(_jax.random.fold_in(key, i + 1))
        out[name] = w
        out["m_" + name] = s * _jax.random.normal(km, w.shape, _jnp.float32)
        out["v_" + name] = (s * s) * _jax.random.uniform(kv, w.shape, _jnp.float32, 0.5, 1.5)
    if N_MICROBATCH > 1:
        for name, axis in PER_EXAMPLE_BATCH_AXIS.items():
            out[name] = _to_microbatches(out[name], axis)
    return {'x': out['x'], 'c': out['c'], 'ctx': out['ctx'], 'c_ctx': out['c_ctx'], 'norm1_g': out['norm1_g'], 'norm2_g': out['norm2_g'], 'ada_w': out['ada_w'], 'ada_b': out['ada_b'], 'rw_mix': out['rw_mix'], 'rw_wr': out['rw_wr'], 'rw_wk': out['rw_wk'], 'rw_wv': out['rw_wv'], 'rw_wo': out['rw_wo'], 'rw_w0': out['rw_w0'], 'rw_w1': out['rw_w1'], 'rw_w2': out['rw_w2'], 'rw_a0': out['rw_a0'], 'rw_a1': out['rw_a1'], 'rw_a2': out['rw_a2'], 'rw_g1': out['rw_g1'], 'rw_g2': out['rw_g2'], 'rw_kk': out['rw_kk'], 'rw_ka': out['rw_ka'], 'rw_rk': out['rw_rk'], 'rw_lnw': out['rw_lnw'], 'rw_lnb': out['rw_lnb'], 'sc_win': out['sc_win'], 'sc_conv': out['sc_conv'], 'sc_wout': out['sc_wout'], 'ffn_w13': out['ffn_w13'], 'ffn_w2': out['ffn_w2'], 'final_g': out['final_g'], 'loss_target': out['loss_target'], 'm_c_ctx': out['m_c_ctx'], 'm_norm1_g': out['m_norm1_g'], 'm_norm2_g': out['m_norm2_g'], 'm_ada_w': out['m_ada_w'], 'm_ada_b': out['m_ada_b'], 'm_rw_mix': out['m_rw_mix'], 'm_rw_wr': out['m_rw_wr'], 'm_rw_wk': out['m_rw_wk'], 'm_rw_wv': out['m_rw_wv'], 'm_rw_wo': out['m_rw_wo'], 'm_rw_w0': out['m_rw_w0'], 'm_rw_w1': out['m_rw_w1'], 'm_rw_w2': out['m_rw_w2'], 'm_rw_a0': out['m_rw_a0'], 'm_rw_a1': out['m_rw_a1'], 'm_rw_a2': out['m_rw_a2'], 'm_rw_g1': out['m_rw_g1'], 'm_rw_g2': out['m_rw_g2'], 'm_rw_kk': out['m_rw_kk'], 'm_rw_ka': out['m_rw_ka'], 'm_rw_rk': out['m_rw_rk'], 'm_rw_lnw': out['m_rw_lnw'], 'm_rw_lnb': out['m_rw_lnb'], 'm_sc_win': out['m_sc_win'], 'm_sc_conv': out['m_sc_conv'], 'm_sc_wout': out['m_sc_wout'], 'm_ffn_w13': out['m_ffn_w13'], 'm_ffn_w2': out['m_ffn_w2'], 'm_final_g': out['m_final_g'], 'v_c_ctx': out['v_c_ctx'], 'v_norm1_g': out['v_norm1_g'], 'v_norm2_g': out['v_norm2_g'], 'v_ada_w': out['v_ada_w'], 'v_ada_b': out['v_ada_b'], 'v_rw_mix': out['v_rw_mix'], 'v_rw_wr': out['v_rw_wr'], 'v_rw_wk': out['v_rw_wk'], 'v_rw_wv': out['v_rw_wv'], 'v_rw_wo': out['v_rw_wo'], 'v_rw_w0': out['v_rw_w0'], 'v_rw_w1': out['v_rw_w1'], 'v_rw_w2': out['v_rw_w2'], 'v_rw_a0': out['v_rw_a0'], 'v_rw_a1': out['v_rw_a1'], 'v_rw_a2': out['v_rw_a2'], 'v_rw_g1': out['v_rw_g1'], 'v_rw_g2': out['v_rw_g2'], 'v_rw_kk': out['v_rw_kk'], 'v_rw_ka': out['v_rw_ka'], 'v_rw_rk': out['v_rw_rk'], 'v_rw_lnw': out['v_rw_lnw'], 'v_rw_lnb': out['v_rw_lnb'], 'v_sc_win': out['v_sc_win'], 'v_sc_conv': out['v_sc_conv'], 'v_sc_wout': out['v_sc_wout'], 'v_ffn_w13': out['v_ffn_w13'], 'v_ffn_w2': out['v_ffn_w2'], 'v_final_g': out['v_final_g']}


def _loss(weights, diff, rest, loss_target):
    with _jax.named_scope("forward"):
        args = {**rest, TWIN_DIFF_INPUT: diff, **{k: w.astype(_WEIGHT_DTYPES[k]) for k, w in weights.items()}}
        y = _forward(args)
    with _jax.named_scope("loss_head"):
        err = _jnp.square(y.astype(_jnp.float32) - loss_target)
        return 0.5 * _jnp.sum(_jnp.mean(err, axis=-1)) if err.ndim else 0.5 * err


def _adamw(w, g, m, v):
    m = ADAM_B1 * m + (1.0 - ADAM_B1) * g
    v = ADAM_B2 * v + (1.0 - ADAM_B2) * _jnp.square(g)
    m_hat = m / (1.0 - ADAM_B1 ** ADAM_STEP)
    v_hat = v / (1.0 - ADAM_B2 ** ADAM_STEP)
    delta = -ADAM_LR * (m_hat / (_jnp.sqrt(v_hat) + ADAM_EPS) + ADAM_WD * w)
    return delta, m, v


def reference(x, c, ctx, c_ctx, norm1_g, norm2_g, ada_w, ada_b, rw_mix, rw_wr, rw_wk, rw_wv, rw_wo, rw_w0, rw_w1, rw_w2, rw_a0, rw_a1, rw_a2, rw_g1, rw_g2, rw_kk, rw_ka, rw_rk, rw_lnw, rw_lnb, sc_win, sc_conv, sc_wout, ffn_w13, ffn_w2, final_g, loss_target, m_c_ctx, m_norm1_g, m_norm2_g, m_ada_w, m_ada_b, m_rw_mix, m_rw_wr, m_rw_wk, m_rw_wv, m_rw_wo, m_rw_w0, m_rw_w1, m_rw_w2, m_rw_a0, m_rw_a1, m_rw_a2, m_rw_g1, m_rw_g2, m_rw_kk, m_rw_ka, m_rw_rk, m_rw_lnw, m_rw_lnb, m_sc_win, m_sc_conv, m_sc_wout, m_ffn_w13, m_ffn_w2, m_final_g, v_c_ctx, v_norm1_g, v_norm2_g, v_ada_w, v_ada_b, v_rw_mix, v_rw_wr, v_rw_wk, v_rw_wv, v_rw_wo, v_rw_w0, v_rw_w1, v_rw_w2, v_rw_a0, v_rw_a1, v_rw_a2, v_rw_g1, v_rw_g2, v_rw_kk, v_rw_ka, v_rw_rk, v_rw_lnw, v_rw_lnb, v_sc_win, v_sc_conv, v_sc_wout, v_ffn_w13, v_ffn_w2, v_final_g):
    given = dict(x=x, c=c, ctx=ctx, c_ctx=c_ctx, norm1_g=norm1_g, norm2_g=norm2_g, ada_w=ada_w, ada_b=ada_b, rw_mix=rw_mix, rw_wr=rw_wr, rw_wk=rw_wk, rw_wv=rw_wv, rw_wo=rw_wo, rw_w0=rw_w0, rw_w1=rw_w1, rw_w2=rw_w2, rw_a0=rw_a0, rw_a1=rw_a1, rw_a2=rw_a2, rw_g1=rw_g1, rw_g2=rw_g2, rw_kk=rw_kk, rw_ka=rw_ka, rw_rk=rw_rk, rw_lnw=rw_lnw, rw_lnb=rw_lnb, sc_win=sc_win, sc_conv=sc_conv, sc_wout=sc_wout, ffn_w13=ffn_w13, ffn_w2=ffn_w2, final_g=final_g, loss_target=loss_target, m_c_ctx=m_c_ctx, m_norm1_g=m_norm1_g, m_norm2_g=m_norm2_g, m_ada_w=m_ada_w, m_ada_b=m_ada_b, m_rw_mix=m_rw_mix, m_rw_wr=m_rw_wr, m_rw_wk=m_rw_wk, m_rw_wv=m_rw_wv, m_rw_wo=m_rw_wo, m_rw_w0=m_rw_w0, m_rw_w1=m_rw_w1, m_rw_w2=m_rw_w2, m_rw_a0=m_rw_a0, m_rw_a1=m_rw_a1, m_rw_a2=m_rw_a2, m_rw_g1=m_rw_g1, m_rw_g2=m_rw_g2, m_rw_kk=m_rw_kk, m_rw_ka=m_rw_ka, m_rw_rk=m_rw_rk, m_rw_lnw=m_rw_lnw, m_rw_lnb=m_rw_lnb, m_sc_win=m_sc_win, m_sc_conv=m_sc_conv, m_sc_wout=m_sc_wout, m_ffn_w13=m_ffn_w13, m_ffn_w2=m_ffn_w2, m_final_g=m_final_g, v_c_ctx=v_c_ctx, v_norm1_g=v_norm1_g, v_norm2_g=v_norm2_g, v_ada_w=v_ada_w, v_ada_b=v_ada_b, v_rw_mix=v_rw_mix, v_rw_wr=v_rw_wr, v_rw_wk=v_rw_wk, v_rw_wv=v_rw_wv, v_rw_wo=v_rw_wo, v_rw_w0=v_rw_w0, v_rw_w1=v_rw_w1, v_rw_w2=v_rw_w2, v_rw_a0=v_rw_a0, v_rw_a1=v_rw_a1, v_rw_a2=v_rw_a2, v_rw_g1=v_rw_g1, v_rw_g2=v_rw_g2, v_rw_kk=v_rw_kk, v_rw_ka=v_rw_ka, v_rw_rk=v_rw_rk, v_rw_lnw=v_rw_lnw, v_rw_lnb=v_rw_lnb, v_sc_win=v_sc_win, v_sc_conv=v_sc_conv, v_sc_wout=v_sc_wout, v_ffn_w13=v_ffn_w13, v_ffn_w2=v_ffn_w2, v_final_g=v_final_g)
    weights = {n: given[n] for n in TWIN_WEIGHTS}
    shared = {n: given[n] for n in SHARED_INPUTS}
    per_example = {n: given[n] for n in ['x', 'c', 'ctx']}
    grad_fn = _jax.value_and_grad(_loss, argnums=(0, 1))

    def one_microbatch(ex, loss_target):
        ex = dict(ex)
        diff = ex.pop(TWIN_DIFF_INPUT)
        return grad_fn(weights, diff, {**shared, **ex}, loss_target)

    if N_MICROBATCH == 1:
        loss, (grad_w, grad_x) = one_microbatch(per_example, given["loss_target"])
    else:
        def body(carry, xs):
            loss_sum, grad_sum = carry
            l_k, (gw_k, gx_k) = one_microbatch(xs[0], xs[1])
            with _jax.named_scope("update"):
                return (loss_sum + l_k, _jax.tree.map(_jnp.add, grad_sum, gw_k)), gx_k

        init = (_jnp.zeros((), _jnp.float32), _jax.tree.map(_jnp.zeros_like, weights))
        (loss, grad_w), grad_x = _jax.lax.scan(body, init, (per_example, given["loss_target"]))
    with _jax.named_scope("update"):
        delta_w, new_m, new_v = {}, {}, {}
        for n in TWIN_WEIGHTS:
            delta_w[n], new_m[n], new_v[n] = _adamw(weights[n], grad_w[n], given["m_" + n], given["v_" + n])
    return (loss, grad_x, *[grad_w[n] for n in TWIN_WEIGHTS], *[delta_w[n] for n in TWIN_WEIGHTS],
            *[new_m[n] for n in TWIN_WEIGHTS], *[new_v[n] for n in TWIN_WEIGHTS])
```

```python
import functools
import math

import jax
import jax.numpy as jnp
from jax import lax
from jax.experimental import pallas as pl
from jax.experimental.pallas import tpu as pltpu

F32 = jnp.float32
BF16 = jnp.bfloat16
MESH = pl.DeviceIdType.MESH

GRID_W = 64
HEAD = 64
LANES = 128
N_CHIPS = 4
N_DEV = 8
NORM_EPS = 1e-6
GN_EPS = 64e-5
ADAM_LR, ADAM_B1, ADAM_B2, ADAM_EPS, ADAM_WD, ADAM_STEP = 0.001, 0.9, 0.999, 1e-08, 0.01, 10
VMEM_LIMIT = 56 * 1024 * 1024
HI = lax.Precision.HIGHEST
WEIGHTS = ['c_ctx', 'norm1_g', 'norm2_g', 'ada_w', 'ada_b', 'rw_mix', 'rw_wr', 'rw_wk', 'rw_wv', 'rw_wo', 'rw_w0',
           'rw_w1', 'rw_w2', 'rw_a0', 'rw_a1', 'rw_a2', 'rw_g1', 'rw_g2', 'rw_kk', 'rw_ka', 'rw_rk', 'rw_lnw',
           'rw_lnb', 'sc_win', 'sc_conv', 'sc_wout', 'ffn_w13', 'ffn_w2', 'final_g']
INPUTS = (['x', 'c', 'ctx'] + WEIGHTS + ['loss_target'] + ['m_' + w for w in WEIGHTS]
          + ['v_' + w for w in WEIGHTS])


def _cparams(**kw):
    return pltpu.CompilerParams(vmem_limit_bytes=VMEM_LIMIT, **kw)


def _pick(dim, cands):
    for c in cands:
        if dim % c == 0:
            return c
    return dim


def _place():
    return lax.axis_index("x"), lax.axis_index("y"), lax.axis_index("c")


_TILE_M = (1024, 768, 512, 1408, 256, 128)
_TILE_N = (1408, 1024, 768, 512, 256, 128)
_TILE_K = (512, 1408, 256, 128)


class Stacked:
    def __init__(self, arr, kind, r, layer=0):
        self.arr, self.kind, self.r, self.layer = arr, kind, r, layer
        self.c = arr.shape[2]
        self.shape = {"row": (N_CHIPS * r, self.c), "col": (r, N_CHIPS * self.c), "layer": (r, self.c)}[kind]

    def at(self, layer):
        return Stacked(self.arr, self.kind, self.r, layer)

    def spec(self, t0, t1, swap):
        r, c, layer = self.r, self.c, self.layer
        per_r, per_c = r // t0, c // t1
        assert r % t0 == 0 and c % t1 == 0
        kind = self.kind

        def index(i, j, k):
            ri, ci = (j, k) if swap else (k, j)
            if kind == "row":
                return (ri // per_r, layer * per_r + ri % per_r, ci)
            if kind == "layer":
                return (layer, ri, ci)
            return (ci // per_c, layer * per_r + ri, ci % per_c)

        return pl.BlockSpec((None, t0, t1), index)


def _mm_body(dims, nk, a_ref, b_ref, o_ref, acc_ref):
    k = pl.program_id(2)

    @pl.when(k == 0)
    def _():
        acc_ref[...] = jnp.zeros_like(acc_ref)

    acc_ref[...] += lax.dot_general(a_ref[...].astype(BF16), b_ref[...].astype(BF16), (dims, ((), ())),
                                    preferred_element_type=F32)

    @pl.when(k == nk - 1)
    def _():
        o_ref[...] = acc_ref[...].astype(o_ref.dtype)


def _mm_call(name, dims, grid, in_specs, out_spec, out_shape, acc_shape, operands):
    return pl.pallas_call(
        functools.partial(_mm_body, dims, grid[2]), name=name, grid=grid, in_specs=in_specs, out_specs=out_spec,
        out_shape=out_shape, scratch_shapes=[pltpu.VMEM(acc_shape, F32)],
        compiler_params=_cparams(dimension_semantics=("parallel", "parallel", "arbitrary")),
    )(*operands)


def mm_nn(name, a, b, out_dtype=F32):
    M, K = a.shape
    st = isinstance(b, Stacked)
    N = b.shape[1]
    tm = _pick(M, _TILE_M)
    tn = _pick(b.c if st and b.kind == "col" else N, _TILE_N)
    tk = _pick(b.r if st else K, _TILE_K)
    b_spec = b.spec(tk, tn, False) if st else pl.BlockSpec((tk, tn), lambda i, j, k: (k, j))
    return _mm_call(name, ((1,), (0,)), (M // tm, N // tn, K // tk),
                    [pl.BlockSpec((tm, tk), lambda i, j, k: (i, k)), b_spec],
                    pl.BlockSpec((tm, tn), lambda i, j, k: (i, j)), jax.ShapeDtypeStruct((M, N), out_dtype),
                    (tm, tn), (a, b.arr if st else b))


def mm_nt(name, a, b, out_dtype=F32):
    M, N = a.shape
    st = isinstance(b, Stacked)
    K = b.shape[0]
    tm = _pick(M, _TILE_M)
    to = _pick(b.r if st else K, _TILE_N)
    tc = _pick(b.c if st and b.kind == "col" else N, _TILE_K)
    b_spec = b.spec(to, tc, True) if st else pl.BlockSpec((to, tc), lambda i, j, k: (j, k))
    return _mm_call(name, ((1,), (1,)), (M // tm, K // to, N // tc),
                    [pl.BlockSpec((tm, tc), lambda i, j, k: (i, k)), b_spec],
                    pl.BlockSpec((tm, to), lambda i, j, k: (i, j)), jax.ShapeDtypeStruct((M, K), out_dtype),
                    (tm, to), (a, b.arr if st else b))


def mm_tn(name, a, b, kind=None):
    R, M = a.shape
    N = b.shape[1]
    r, c = (M // N_CHIPS, N) if kind == "row" else (M, N // N_CHIPS) if kind == "col" else (M, N)
    tm, tn, tk = _pick(r, _TILE_M), _pick(c, _TILE_N), _pick(R, (1024, 768, 512, 256, 128))
    if kind:
        per_r, per_c = r // tm, c // tn
        if kind == "row":
            o_spec = pl.BlockSpec((None, tm, tn), lambda i, j, k: (i // per_r, i % per_r, j))
        else:
            o_spec = pl.BlockSpec((None, tm, tn), lambda i, j, k: (j // per_c, i, j % per_c))
        o_shape = jax.ShapeDtypeStruct((N_CHIPS, r, c), F32)
    else:
        o_spec = pl.BlockSpec((tm, tn), lambda i, j, k: (i, j))
        o_shape = jax.ShapeDtypeStruct((M, N), F32)
    return _mm_call(name, ((0,), (0,)), (M // tm, N // tn, R // tk),
                    [pl.BlockSpec((tk, tm), lambda i, j, k: (k, i)), pl.BlockSpec((tk, tn), lambda i, j, k: (k, j))],
                    o_spec, o_shape, (tm, tn), (a, b))


def _shifted(o):
    return lambda i: (i + o, 0)


def row_call(name, f, rows, consts, outs, *, tr, sr, offs=None):
    offs = offs or [0] * len(rows)
    n_rows = min(r.shape[0] - o * tr for r, o in zip(rows, offs))
    nr, nc = len(rows), len(consts)

    def body(*refs):
        row_refs, const_refs, out_refs = refs[:nr], refs[nr:nr + nc], refs[nr + nc:]
        i = pl.program_id(0)
        cvals = [r[...] for r in const_refs]

        def step(j, carry):
            sl = pl.ds(pl.multiple_of(j * sr, sr), sr)
            res = f(i, *[r[sl, :] for r in row_refs], *cvals)
            for o, v in zip(out_refs, res):
                o[sl, :] = v.astype(o.dtype)
            return carry

        lax.fori_loop(0, tr // sr, step, 0)

    in_specs = [pl.BlockSpec((tr, r.shape[1]), _shifted(o)) for r, o in zip(rows, offs)]
    in_specs += [pl.BlockSpec(c.shape, lambda i: (0, 0)) for c in consts]
    return pl.pallas_call(
        body, name=name, grid=(n_rows // tr,), in_specs=in_specs,
        out_specs=[pl.BlockSpec((tr, w), lambda i: (i, 0)) for w, _ in outs],
        out_shape=[jax.ShapeDtypeStruct((n_rows, w), dt) for w, dt in outs],
        compiler_params=_cparams(dimension_semantics=("parallel",)),
    )(*rows, *consts)


def row_vjp(name, f, rows, consts, cots, *, row_mask, const_mask, tr, sr, offs=None):
    offs = offs or [0] * len(rows)
    n_rows = min(r.shape[0] - o * tr for r, o in zip(rows, offs))
    nr, nc = len(rows), len(consts)
    cot_in = [c for c in cots if c is not None]
    nct = len(cot_in)
    d_rows = [i for i in range(nr) if row_mask[i]]
    d_consts = [i for i in range(nc) if const_mask[i]]

    def body(*refs):
        row_refs, const_refs = refs[:nr], refs[nr:nr + nc]
        cot_refs = refs[nr + nc:nr + nc + nct]
        drow_refs = refs[nr + nc + nct:nr + nc + nct + len(d_rows)]
        dconst_refs = refs[nr + nc + nct + len(d_rows):]
        i = pl.program_id(0)

        @pl.when(i == 0)
        def _():
            for r in dconst_refs:
                r[...] = jnp.zeros_like(r)

        cvals = [r[...] for r in const_refs]

        def step(j, carry):
            sl = pl.ds(pl.multiple_of(j * sr, sr), sr)
            rvals = [r[sl, :] for r in row_refs]

            def g(*diff):
                rv, cv = list(rvals), list(cvals)
                for idx, val in zip(d_rows, diff[:len(d_rows)]):
                    rv[idx] = val
                for idx, val in zip(d_consts, diff[len(d_rows):]):
                    cv[idx] = val
                return f(i, *rv, *cv)

            primals = [rvals[idx].astype(F32) for idx in d_rows] + [cvals[idx] for idx in d_consts]
            res, vjp = jax.vjp(g, *primals)
            it = iter(cot_refs)
            cts = tuple(jnp.zeros_like(o) if c is None else next(it)[sl, :].astype(o.dtype) for o, c in zip(res, cots))
            grads = vjp(cts)
            for r, val in zip(drow_refs, grads[:len(d_rows)]):
                r[sl, :] = val.astype(r.dtype)
            for r, val in zip(dconst_refs, grads[len(d_rows):]):
                r[...] += val
            return carry

        lax.fori_loop(0, tr // sr, step, 0)

    in_specs = [pl.BlockSpec((tr, r.shape[1]), _shifted(o)) for r, o in zip(rows, offs)]
    in_specs += [pl.BlockSpec(c.shape, lambda i: (0, 0)) for c in consts]
    in_specs += [pl.BlockSpec((tr, c.shape[1]), lambda i: (i, 0)) for c in cot_in]
    out_specs = [pl.BlockSpec((tr, rows[i].shape[1]), lambda i: (i, 0)) for i in d_rows]
    out_specs += [pl.BlockSpec(consts[i].shape, lambda i: (0, 0)) for i in d_consts]
    out_shape = [jax.ShapeDtypeStruct((n_rows, rows[i].shape[1]), F32) for i in d_rows]
    out_shape += [jax.ShapeDtypeStruct(consts[i].shape, F32) for i in d_consts]
    res = pl.pallas_call(
        body, name=name, grid=(n_rows // tr,), in_specs=in_specs, out_specs=out_specs, out_shape=out_shape,
        compiler_params=_cparams(dimension_semantics=("arbitrary",)),
    )(*rows, *consts, *cot_in)
    return list(res[:len(d_rows)]), list(res[len(d_rows):])


def _sigmoid(x):
    return 1.0 / (1.0 + jnp.exp(-x))


def _softplus(u):
    return jnp.maximum(u, 0.0) + jnp.log(1.0 + jnp.exp(-jnp.abs(u)))


def _rms(x, g):
    ms = jnp.sum(x * x, axis=-1, keepdims=True) * (1.0 / x.shape[-1])
    return x * lax.rsqrt(ms + NORM_EPS) * g


def _hsum(x, e, et):
    return jnp.dot(jnp.dot(x, e, precision=HI, preferred_element_type=F32), et, precision=HI,
                   preferred_element_type=F32)


def f_silu(i, x):
    return (x * _sigmoid(x),)


def f_sigmoid(i, x):
    return (_sigmoid(x),)


def f_tanh(i, x):
    return (jnp.tanh(x),)


def f_norm_mod(n_ctx_tiles, i, xin, g, sh_c, sc_c, sh_x, sc_x):
    is_x = i >= n_ctx_tiles
    sh = jnp.where(is_x, sh_x, sh_c)
    sc = jnp.where(is_x, sc_x, sc_c)
    return (_rms(xin, g) * (1.0 + sc) + sh,)


def f_res_norm_mod(i, x, y, gt, g, sh, sc):
    x1 = x + gt * y
    return x1, _rms(x1, g) * (1.0 + sc) + sh


def f_post(i, k, zw0, zw1, za0, za1, kkp, ka, w00, w01, a00, a01, e, et):
    kq = k * kkp
    kk = kq / jnp.maximum(jnp.sqrt(_hsum(kq * kq, e, et)), 1e-12)

    def direction(zw, za, w0, a0):
        log_w = -_softplus(-(w0 + zw)) - 0.5
        a = _sigmoid(a0 + za)
        return jnp.exp(-jnp.exp(log_w)), k * (1.0 + (a - 1.0) * ka), kk * a

    dec0, kd0, bb0 = direction(zw0, za0, w00, a00)
    dec1, kd1, bb1 = direction(zw1, za1, w01, a01)
    return -kk, -kk, dec0, dec1, kd0, kd1, bb0, bb1, kd0 + kd1


def f_post_fwd(*a):
    return f_post(*a)[1:]


def f_readout(i, y0, y1, r, ksum, v, g, rk, lnw, lnb, e, et):
    y = y0 + y1
    yc = y - _hsum(y, e, et) * (1.0 / HEAD)
    var = _hsum(yc * yc, e, et) * (1.0 / HEAD)
    o = yc * lax.rsqrt(var + GN_EPS) * lnw + lnb
    o = o + _hsum(r * ksum * rk, e, et) * v
    return (o * g,)


def f_sum(i, *xs):
    acc = xs[0].astype(F32)
    for x in xs[1:]:
        acc = acc + x.astype(F32)
    return (acc,)


def sum_cast(name, arrs, dtype, tr, offs=None):
    return row_call(name, f_sum, arrs, [], [(arrs[0].shape[1], dtype)], tr=tr, sr=16, offs=offs)[0]


def swiglu_fwd(name, ab, tr):
    T, F2 = ab.shape
    F = F2 // 2
    sr = 16

    def body(ab_ref, o_ref):
        def step(j, carry):
            sl = pl.ds(pl.multiple_of(j * sr, sr), sr)
            a, b = ab_ref[sl, :F], ab_ref[sl, F:]
            o_ref[sl, :] = (a * _sigmoid(a) * b).astype(o_ref.dtype)
            return carry

        lax.fori_loop(0, tr // sr, step, 0)

    return pl.pallas_call(
        body, name=name, grid=(T // tr,), in_specs=[pl.BlockSpec((tr, F2), lambda i: (i, 0))],
        out_specs=pl.BlockSpec((tr, F), lambda i: (i, 0)), out_shape=jax.ShapeDtypeStruct((T, F), BF16),
        compiler_params=_cparams(dimension_semantics=("parallel",)),
    )(ab)


def swiglu_bwd(name, ab, dsw, tr):
    T, F2 = ab.shape
    F = F2 // 2
    sr = 16

    def body(ab_ref, d_ref, o_ref):
        def step(j, carry):
            sl = pl.ds(pl.multiple_of(j * sr, sr), sr)
            a, b, d = ab_ref[sl, :F], ab_ref[sl, F:], d_ref[sl, :]
            sg = _sigmoid(a)
            o_ref[sl, :F] = d * b * (sg + a * sg * (1.0 - sg))
            o_ref[sl, F:] = d * a * sg
            return carry

        lax.fori_loop(0, tr // sr, step, 0)

    return pl.pallas_call(
        body, name=name, grid=(T // tr,),
        in_specs=[pl.BlockSpec((tr, F2), lambda i: (i, 0)), pl.BlockSpec((tr, F), lambda i: (i, 0))],
        out_specs=pl.BlockSpec((tr, F2), lambda i: (i, 0)), out_shape=jax.ShapeDtypeStruct((T, F2), F32),
        compiler_params=_cparams(dimension_semantics=("parallel",)),
    )(ab, dsw)


def final_call(name, x3, f1, gt, fg, tgt, tr):
    T, D = x3.shape
    sr = 16

    def f(x, y, gtv, g, t):
        err = _rms(x + gtv * y, g) - t
        return 0.5 * jnp.sum(err * err) * (1.0 / D)

    def body(x_ref, y_ref, gt_ref, g_ref, t_ref, dx_ref, dy_ref, dgt_ref, dg_ref, loss_ref):
        @pl.when(pl.program_id(0) == 0)
        def _():
            dgt_ref[...] = jnp.zeros_like(dgt_ref)
            dg_ref[...] = jnp.zeros_like(dg_ref)
            loss_ref[...] = jnp.zeros_like(loss_ref)

        def step(j, carry):
            sl = pl.ds(pl.multiple_of(j * sr, sr), sr)
            val, vjp = jax.vjp(lambda x, y, a, b: f(x, y, a, b, t_ref[sl, :]), x_ref[sl, :], y_ref[sl, :],
                               gt_ref[...], g_ref[...])
            dx, dy, dgt, dg = vjp(jnp.ones((), F32))
            dx_ref[sl, :] = dx
            dy_ref[sl, :] = dy
            dgt_ref[...] += dgt
            dg_ref[...] += dg
            loss_ref[...] += jnp.full(loss_ref.shape, val, F32)
            return carry

        lax.fori_loop(0, tr // sr, step, 0)

    row = pl.BlockSpec((tr, D), lambda i: (i, 0))
    vec = pl.BlockSpec((1, D), lambda i: (0, 0))
    return pl.pallas_call(
        body, name=name, grid=(T // tr,), in_specs=[row, row, vec, vec, row],
        out_specs=[row, row, vec, vec, pl.BlockSpec((8, LANES), lambda i: (0, 0))],
        out_shape=[jax.ShapeDtypeStruct((T, D), F32)] * 2 + [jax.ShapeDtypeStruct((1, D), F32)] * 2
        + [jax.ShapeDtypeStruct((8, LANES), F32)],
        compiler_params=_cparams(dimension_semantics=("arbitrary",)),
    )(x3, f1, gt, fg, tgt)


def _tshift(x, kind, period):
    n = x.shape[0]
    t = lax.broadcasted_iota(jnp.int32, x.shape, 0)
    if kind == 0:
        return jnp.where((t & (period - 1)) == 0, 0.0, pltpu.roll(x, 1, 0))
    if kind == 1:
        return jnp.where(((t & (period - 1)) == period - 1) | (t == n - 1), 0.0, pltpu.roll(x, n - 1, 0))
    if kind == 2:
        return jnp.where(t < GRID_W, 0.0, pltpu.roll(x, GRID_W, 0))
    return jnp.where(t >= n - GRID_W, 0.0, pltpu.roll(x, n - GRID_W, 0))


def _pow2_at_least(n):
    return 1 << (n - 1).bit_length()


def _shift_into(dst_ref, h_ref, n_ctx, cb, D, transpose):
    j = pl.program_id(0)
    quarter = (j * cb * 4) // D
    half = (j * cb * 2) // D
    flip = 1 if transpose else 0
    for q in range(4):
        @pl.when(quarter == q)
        def _(q=q):
            dst_ref[n_ctx:, :] = _tshift(h_ref[n_ctx:, :], q ^ flip, GRID_W)
    for q in range(2):
        @pl.when(half == q)
        def _(q=q):
            dst_ref[:n_ctx, :] = _tshift(h_ref[:n_ctx, :], q ^ flip, _pow2_at_least(n_ctx))


def mix_fwd(name, h, mix, n_ctx):
    R, D = h.shape
    cb = LANES

    def body(h_ref, mix_ref, *rest):
        outs, hs_ref = rest[:6], rest[6]
        _shift_into(hs_ref, h_ref, n_ctx, cb, D, False)
        hv = h_ref[...]
        xx = hs_ref[...] - hv
        for m in range(6):
            outs[m][...] = (hv + xx * mix_ref[m:m + 1, :]).astype(BF16)

    col = pl.BlockSpec((R, cb), lambda j: (0, j))
    return pl.pallas_call(
        body, name=name, grid=(D // cb,), in_specs=[col, pl.BlockSpec((mix.shape[0], cb), lambda j: (0, j))],
        out_specs=[col] * 6, out_shape=[jax.ShapeDtypeStruct((R, D), BF16)] * 6,
        scratch_shapes=[pltpu.VMEM((R, cb), F32)],
        compiler_params=_cparams(dimension_semantics=("parallel",)),
    )(h, mix)


def mix_bwd(name, h, mix, cots, slots, n_ctx):
    R, D = h.shape
    cb = LANES
    nc = len(cots)

    def body(h_ref, mix_ref, *rest):
        cot_refs, dh_ref, dmix_ref, hs_ref, dxx_ref = rest[:nc], rest[nc], rest[nc + 1], rest[nc + 2], rest[nc + 3]
        _shift_into(hs_ref, h_ref, n_ctx, cb, D, False)
        xx = hs_ref[...] - h_ref[...]
        per_slot = [None] * 6
        for cref, m in zip(cot_refs, slots):
            per_slot[m] = cref[...] if per_slot[m] is None else per_slot[m] + cref[...]
        dh = jnp.zeros((R, cb), F32)
        dxx = jnp.zeros((R, cb), F32)
        rows = []
        for m in range(6):
            d = per_slot[m]
            dh = dh + d
            dxx = dxx + d * mix_ref[m:m + 1, :]
            rows.append(jnp.sum(d * xx, axis=0, keepdims=True))
        dmix_ref[...] = jnp.concatenate(rows + [jnp.zeros((2, cb), F32)], axis=0)
        dxx_ref[...] = dxx
        _shift_into(hs_ref, dxx_ref, n_ctx, cb, D, True)
        dh_ref[...] = dh - dxx + hs_ref[...]

    col = pl.BlockSpec((R, cb), lambda j: (0, j))
    return pl.pallas_call(
        body, name=name, grid=(D // cb,), in_specs=[col, pl.BlockSpec((mix.shape[0], cb), lambda j: (0, j))] + [col] * nc,
        out_specs=[col, pl.BlockSpec((8, cb), lambda j: (0, j))],
        out_shape=[jax.ShapeDtypeStruct((R, D), F32), jax.ShapeDtypeStruct((8, D), F32)],
        scratch_shapes=[pltpu.VMEM((R, cb), F32), pltpu.VMEM((R, cb), F32)],
        compiler_params=_cparams(dimension_semantics=("parallel",)),
    )(h, mix, *cots)


def _conv_parts(gb_ref, gc_ref, u_ref, cw_ref):
    T = gb_ref.shape[0]
    z = gc_ref[...] * u_ref[...]
    zp, zn = _tshift(z, 0, _pow2_at_least(T)), _tshift(z, 1, _pow2_at_least(T))
    conv = zp * cw_ref[0:1, :] + z * cw_ref[1:2, :] + zn * cw_ref[2:3, :]
    return z, zp, zn, conv


def conv_fwd(name, gcu, cw):
    T, D3 = gcu.shape
    D = D3 // 3
    cb = LANES
    nb = D // cb

    def body(gb_ref, gc_ref, u_ref, cw_ref, o_ref):
        _, _, _, conv = _conv_parts(gb_ref, gc_ref, u_ref, cw_ref)
        o_ref[...] = (gb_ref[...] * conv).astype(BF16)

    def part(p):
        return pl.BlockSpec((T, cb), lambda j: (0, j + p * nb))

    return pl.pallas_call(
        body, name=name, grid=(nb,),
        in_specs=[part(0), part(1), part(2), pl.BlockSpec((cw.shape[0], cb), lambda j: (0, j))],
        out_specs=pl.BlockSpec((T, cb), lambda j: (0, j)), out_shape=jax.ShapeDtypeStruct((T, D), BF16),
        compiler_params=_cparams(dimension_semantics=("parallel",)),
    )(gcu, gcu, gcu, cw)


def conv_bwd(name, gcu, cw, dp):
    T, D3 = gcu.shape
    D = D3 // 3
    cb = LANES
    nb = D // cb

    def body(gb_ref, gc_ref, u_ref, cw_ref, dp_ref, o_ref, dcw_ref):
        part = pl.program_id(1)
        z, zp, zn, conv = _conv_parts(gb_ref, gc_ref, u_ref, cw_ref)
        dpv = dp_ref[...]
        dconv = dpv * gb_ref[...]
        period = _pow2_at_least(T)
        dz = (_tshift(dconv * cw_ref[0:1, :], 1, period) + dconv * cw_ref[1:2, :]
              + _tshift(dconv * cw_ref[2:3, :], 0, period))

        @pl.when(part == 0)
        def _():
            o_ref[...] = dpv * conv
            dcw_ref[...] = jnp.concatenate(
                [jnp.sum(dconv * s, axis=0, keepdims=True) for s in (zp, z, zn)] + [jnp.zeros((5, cb), F32)], axis=0)

        @pl.when(part == 1)
        def _():
            o_ref[...] = dz * u_ref[...]

        @pl.when(part == 2)
        def _():
            o_ref[...] = dz * gc_ref[...]

    def part_spec(p):
        return pl.BlockSpec((T, cb), lambda j, q: (0, j + p * nb))

    return pl.pallas_call(
        body, name=name, grid=(nb, 3),
        in_specs=[part_spec(0), part_spec(1), part_spec(2), pl.BlockSpec((cw.shape[0], cb), lambda j, q: (0, j)),
                  pl.BlockSpec((T, cb), lambda j, q: (0, j))],
        out_specs=[pl.BlockSpec((T, cb), lambda j, q: (0, j + q * nb)), pl.BlockSpec((8, cb), lambda j, q: (0, j))],
        out_shape=[jax.ShapeDtypeStruct((T, D3), F32), jax.ShapeDtypeStruct((8, D), F32)],
        compiler_params=_cparams(dimension_semantics=("arbitrary", "arbitrary")),
    )(gcu, gcu, gcu, cw, dp)


SCAN_TC = 8


def _scan_consts(hp):
    rows = lax.broadcasted_iota(jnp.int32, (hp * HEAD, LANES), 0)
    cols = lax.broadcasted_iota(jnp.int32, (hp * HEAD, LANES), 1)
    eye = ((rows & (HEAD - 1)) == (cols & (HEAD - 1))).astype(F32)
    r2 = lax.broadcasted_iota(jnp.int32, (LANES, LANES), 0)
    c2 = lax.broadcasted_iota(jnp.int32, (LANES, LANES), 1)
    ones_bd = ((r2 >= HEAD) == (c2 >= HEAD)).astype(BF16)
    return eye, ones_bd


def _ksum(p, ones_bd):
    hi = p.astype(BF16)
    lo = (p - hi.astype(F32)).astype(BF16)
    return (jnp.dot(hi, ones_bd, preferred_element_type=F32) + jnp.dot(lo, ones_bd, preferred_element_type=F32))


def _bc(x_t, hp):
    return jnp.concatenate([jnp.broadcast_to(x_t[h:h + 1, :], (HEAD, LANES)) for h in range(hp)], axis=0)


def _store_colsums(ref, t, x, hp):
    for h in range(hp):
        ref[t, pl.ds(h, 1), :] = jnp.sum(x[h * HEAD:(h + 1) * HEAD, :], axis=0, keepdims=True)


def _order(i, n_ctx, n_all, rev):
    if not rev:
        return i
    return jnp.where(i < n_ctx, n_ctx - 1 - i, n_all - 1 - (i - n_ctx))


def scan_fwd(name, r, w, k, v, a, b, n_ctx_rows, rev):
    R, D = r.shape
    hp, tc = D // LANES, SCAN_TC
    n_all, n_ctx = R // tc, n_ctx_rows // tc
    ins = [t.reshape(R, hp, LANES) for t in (r, w, k, v, a, b)]

    def body(r_ref, w_ref, k_ref, v_ref, a_ref, b_ref, y_ref, st_ref, s_ref):
        @pl.when(pl.program_id(0) == 0)
        def _():
            s_ref[...] = jnp.zeros_like(s_ref)

        eye, ones_bd = _scan_consts(hp)

        def step(q, carry):
            t = tc - 1 - q if rev else q
            s = s_ref[...]
            sa = _ksum(s * _bc(a_ref[t], hp), ones_bd)
            ve = _ksum(_bc(v_ref[t], hp) * eye, ones_bd)
            s = s * _bc(w_ref[t], hp) + sa * _bc(b_ref[t], hp) + ve * _bc(k_ref[t], hp)
            s_ref[...] = s
            st_ref[q] = s
            y = _ksum(s * _bc(r_ref[t], hp), ones_bd)
            _store_colsums(y_ref, t, y * eye, hp)
            return carry

        lax.fori_loop(0, tc, step, 0)

    row_spec = pl.BlockSpec((tc, hp, LANES), lambda i: (_order(i, n_ctx, n_all, rev), 0, 0))
    y, st = pl.pallas_call(
        body, name=name, grid=(n_all,), in_specs=[row_spec] * 6,
        out_specs=[row_spec, pl.BlockSpec((tc, hp * HEAD, LANES), lambda i: (i, 0, 0))],
        out_shape=[jax.ShapeDtypeStruct((R, hp, LANES), F32), jax.ShapeDtypeStruct((R, hp * HEAD, LANES), F32)],
        scratch_shapes=[pltpu.VMEM((hp * HEAD, LANES), F32)],
        compiler_params=_cparams(dimension_semantics=("arbitrary",)),
    )(*ins)
    return y.reshape(R, D), st


def scan_bwd(name, r, w, k, v, a, b, dy, st, n_ctx_rows, rev):
    R, D = r.shape
    hp, tc = D // LANES, SCAN_TC
    n_all, n_ctx = R // tc, n_ctx_rows // tc
    ins = [t.reshape(R, hp, LANES) for t in (r, w, k, v, a, b, dy)]

    def body(r_ref, w_ref, k_ref, v_ref, a_ref, b_ref, dy_ref, st_ref, prev_ref,
             dr_ref, dw_ref, dk_ref, dv_ref, da_ref, db_ref, g_ref):
        i = pl.program_id(0)

        @pl.when(i == 0)
        def _():
            g_ref[...] = jnp.zeros_like(g_ref)

        eye, ones_bd = _scan_consts(hp)
        first_block = i == n_all - 1

        def step(qq, carry):
            q = tc - 1 - qq
            t = tc - 1 - q if rev else q
            s_cur = st_ref[q]
            s_in = jnp.where(first_block, 0.0, prev_ref[0])
            s_prev = jnp.where(q > 0, st_ref[jnp.maximum(q - 1, 0)], s_in)
            rr, ww, kk, aa, bb = (_bc(x[t], hp) for x in (r_ref, w_ref, k_ref, a_ref, b_ref))
            dye = _ksum(_bc(dy_ref[t], hp) * eye, ones_bd)
            ve = _ksum(_bc(v_ref[t], hp) * eye, ones_bd)
            g = g_ref[...] + dye * rr
            _store_colsums(dr_ref, t, s_cur * dye, hp)
            _store_colsums(dv_ref, t, _ksum(g * kk, ones_bd) * eye, hp)
            _store_colsums(dk_ref, t, g * ve, hp)
            _store_colsums(dw_ref, t, g * s_prev, hp)
            sa = _ksum(s_prev * aa, ones_bd)
            _store_colsums(db_ref, t, g * sa, hp)
            dsa = _ksum(g * bb, ones_bd)
            _store_colsums(da_ref, t, s_prev * dsa, hp)
            g_ref[...] = g * ww + dsa * aa
            return carry

        lax.fori_loop(0, tc, step, 0)

    def pos(i):
        return n_all - 1 - i

    row_spec = pl.BlockSpec((tc, hp, LANES), lambda i: (_order(pos(i), n_ctx, n_all, rev), 0, 0))
    outs = pl.pallas_call(
        body, name=name, grid=(n_all,),
        in_specs=[row_spec] * 7 + [
            pl.BlockSpec((tc, hp * HEAD, LANES), lambda i: (pos(i), 0, 0)),
            pl.BlockSpec((1, hp * HEAD, LANES), lambda i: (jnp.maximum(pos(i) * tc - 1, 0), 0, 0))],
        out_specs=[row_spec] * 6,
        out_shape=[jax.ShapeDtypeStruct((R, hp, LANES), F32)] * 6,
        scratch_shapes=[pltpu.VMEM((hp * HEAD, LANES), F32)],
        compiler_params=_cparams(dimension_semantics=("arbitrary",)),
    )(*ins, st, st)
    return [o.reshape(R, D) for o in outs]


ANY = pl.BlockSpec(memory_space=pl.ANY)


def _peer(xi, yi, ci, k):
    return (1 - xi if k & 4 else xi, 1 - yi if k & 2 else yi, 1 - ci if k & 1 else ci)


def _rcopy(src, dst, send_sem, recv_sem, dev):
    return pltpu.make_async_remote_copy(src_ref=src, dst_ref=dst, send_sem=send_sem, recv_sem=recv_sem,
                                        device_id=dev, device_id_type=MESH)


def _drain(copies):
    for cp in copies:
        if cp.is_remote:
            cp.wait_send()
        else:
            cp.wait()


def all_gather8(name, x):
    r, c = x.shape

    def body(x_ref, out_ref, send_sems, recv_sems, local_sem):
        xi, yi, ci = _place()

        def blk(p):
            return out_ref.at[4 * p[0] + 2 * p[1] + p[2]]

        me = (xi, yi, ci)
        mine = pltpu.make_async_copy(x_ref, blk(me), local_sem.at[0])
        mine.start()
        sends = [_rcopy(x_ref, blk(me), send_sems.at[k - 1], recv_sems.at[k - 1], _peer(xi, yi, ci, k))
                 for k in range(1, N_DEV)]
        for cp in sends:
            cp.start()
        for k in range(1, N_DEV):
            p = _peer(xi, yi, ci, k)
            _rcopy(x_ref, blk(p), send_sems.at[k - 1], recv_sems.at[k - 1], p).wait_recv()
        for cp in sends:
            cp.wait_send()
        mine.wait()

    vm = pl.BlockSpec(memory_space=pltpu.VMEM)
    return pl.pallas_call(
        body, name=name, in_specs=[vm], out_specs=vm, out_shape=jax.ShapeDtypeStruct((N_DEV, r, c), x.dtype),
        scratch_shapes=[pltpu.SemaphoreType.DMA((N_DEV - 1,)), pltpu.SemaphoreType.DMA((N_DEV - 1,)),
                        pltpu.SemaphoreType.DMA((1,))],
        compiler_params=_cparams(),
    )(x)


def _chips(xi, yi):
    chips = [(1 - xi, yi), (xi, 1 - yi), (1 - xi, 1 - yi)]
    return chips, [2 * cx + cy for cx, cy in chips]


def gather_weights(name, shards):
    n = len(shards)

    def body(*refs):
        sh, out = refs[:n], refs[n:2 * n]
        send_sems, recv_sems, local_sems = refs[2 * n:]
        xi, yi, ci = _place()
        s = 2 * xi + yi
        chips, sidx = _chips(xi, yi)
        sib = (xi, yi, 1 - ci)
        started = []
        for w in range(n):
            hr = sh[w].shape[0] // 2
            lc = pltpu.make_async_copy(sh[w], out[w].at[s], local_sems.at[w])
            lc.start()
            started.append(lc)
            for j, (cx, cy) in enumerate(chips):
                cp = _rcopy(sh[w].at[pl.ds(ci * hr, hr)], out[w].at[s, pl.ds(ci * hr, hr)],
                            send_sems.at[w, j], recv_sems.at[w, j], (cx, cy, ci))
                cp.start()
                started.append(cp)
        for w in range(n):
            hr = sh[w].shape[0] // 2
            for j, (cx, cy) in enumerate(chips):
                blk = out[w].at[sidx[j], pl.ds(ci * hr, hr)]
                _rcopy(blk, blk, send_sems.at[w, j], recv_sems.at[w, j], (cx, cy, ci)).wait_recv()
                fw = _rcopy(blk, blk, send_sems.at[w, 3 + j], recv_sems.at[w, 3 + j], sib)
                fw.start()
                started.append(fw)
        for w in range(n):
            hr = sh[w].shape[0] // 2
            for j in range(3):
                blk = out[w].at[sidx[j], pl.ds((1 - ci) * hr, hr)]
                _rcopy(blk, blk, send_sems.at[w, 3 + j], recv_sems.at[w, 3 + j], sib).wait_recv()
        _drain(started)

    return pl.pallas_call(
        body, name=name, in_specs=[ANY] * n, out_specs=[ANY] * n,
        out_shape=[jax.ShapeDtypeStruct((N_CHIPS,) + a.shape, a.dtype) for a in shards],
        scratch_shapes=[pltpu.SemaphoreType.DMA((n, 6)), pltpu.SemaphoreType.DMA((n, 6)),
                        pltpu.SemaphoreType.DMA((n,))],
        compiler_params=_cparams(),
    )(*shards)


def rs_pair(name, grads):
    n = len(grads)

    def body(*refs):
        g, out = refs[:n], refs[n:2 * n]
        send_sems, recv_sems = refs[2 * n:]
        xi, yi, ci = _place()
        sib = (xi, yi, 1 - ci)
        cps = [_rcopy(g[w].at[:, 1 - ci], out[w], send_sems.at[w], recv_sems.at[w], sib) for w in range(n)]
        for cp in cps:
            cp.start()
        for cp in cps:
            cp.wait_recv()
        for cp in cps:
            cp.wait_send()

    return pl.pallas_call(
        body, name=name, in_specs=[ANY] * n, out_specs=[ANY] * n,
        out_shape=[jax.ShapeDtypeStruct((N_CHIPS,) + a.shape[2:], a.dtype) for a in grads],
        scratch_shapes=[pltpu.SemaphoreType.DMA((n,)), pltpu.SemaphoreType.DMA((n,))],
        compiler_params=_cparams(),
    )(*grads)


def _rows_tile(rows, cols, unit=16, limit=1 << 20):
    best = None
    for t in range(unit, rows + 1, unit):
        if rows % t == 0 and t * cols * 4 <= limit:
            best = t
    return best or rows


def rs_add_pair(name, g, got, ci):
    _, _, hr, c = g.shape
    th = _rows_tile(hr, c)

    def body(ci_ref, g_ref, r_ref, o32_ref, ob_ref):
        tot = g_ref[...] + r_ref[...]
        o32_ref[...] = tot
        ob_ref[...] = tot.astype(BF16)

    blk = pl.BlockSpec((None, th, c), lambda s, i, ci_ref: (s, i, 0))
    return pl.pallas_call(
        body, name=name,
        grid_spec=pltpu.PrefetchScalarGridSpec(
            num_scalar_prefetch=1, grid=(N_CHIPS, hr // th),
            in_specs=[pl.BlockSpec((None, None, th, c), lambda s, i, ci_ref: (s, ci_ref[0], i, 0)), blk],
            out_specs=[blk, blk]),
        out_shape=[jax.ShapeDtypeStruct((N_CHIPS, hr, c), F32), jax.ShapeDtypeStruct((N_CHIPS, hr, c), BF16)],
        compiler_params=_cparams(dimension_semantics=("parallel", "parallel")),
    )(ci, g, got)


def rs_chips(name, sums_bf16, sums_f32):
    n = len(sums_bf16)

    def body(*refs):
        pb, p32 = refs[:n], refs[n:2 * n]
        outs = refs[2 * n:6 * n]
        send_sems, recv_sems, local_sems = refs[6 * n:]
        xi, yi, ci = _place()
        s = 2 * xi + yi
        chips, sidx = _chips(xi, yi)
        started = []
        for w in range(n):
            own = pltpu.make_async_copy(p32[w].at[s], outs[4 * w + 3], local_sems.at[w])
            own.start()
            started.append(own)
            for j, (cx, cy) in enumerate(chips):
                cp = _rcopy(pb[w].at[sidx[j]], outs[4 * w + j], send_sems.at[w, j], recv_sems.at[w, j], (cx, cy, ci))
                cp.start()
                started.append(cp)
        for w in range(n):
            for j, (cx, cy) in enumerate(chips):
                _rcopy(pb[w].at[sidx[j]], outs[4 * w + j], send_sems.at[w, j], recv_sems.at[w, j],
                       (cx, cy, ci)).wait_recv()
        _drain(started)

    out_shape = []
    for a in sums_bf16:
        out_shape += [jax.ShapeDtypeStruct(a.shape[1:], BF16)] * 3 + [jax.ShapeDtypeStruct(a.shape[1:], F32)]
    res = pl.pallas_call(
        body, name=name, in_specs=[ANY] * (2 * n), out_specs=[ANY] * (4 * n), out_shape=out_shape,
        scratch_shapes=[pltpu.SemaphoreType.DMA((n, 3)), pltpu.SemaphoreType.DMA((n, 3)),
                        pltpu.SemaphoreType.DMA((n,))],
        compiler_params=_cparams(),
    )(*sums_bf16, *sums_f32)
    return [res[4 * w:4 * w + 4] for w in range(n)]


def rs_finish(name, halves, groups):
    n = len(halves)
    where = {}
    out_shape = []
    for gi, members in enumerate(groups):
        for pos, w in enumerate(members):
            where[w] = (gi, pos)
        hr, c = halves[members[0]].shape
        out_shape.append(jax.ShapeDtypeStruct((2 * hr * len(members), c), F32))

    def body(*refs):
        hv, out = refs[:n], refs[n:n + len(groups)]
        send_sems, recv_sems, local_sems = refs[n + len(groups):]
        xi, yi, ci = _place()
        sib = (xi, yi, 1 - ci)
        started = []
        for w in range(n):
            gi, pos = where[w]
            hr = hv[w].shape[0]
            mine = out[gi].at[pl.ds(pos * 2 * hr + ci * hr, hr)]
            lc = pltpu.make_async_copy(hv[w], mine, local_sems.at[w])
            lc.start()
            cp = _rcopy(hv[w], mine, send_sems.at[w], recv_sems.at[w], sib)
            cp.start()
            started += [lc, cp]
        for w in range(n):
            gi, pos = where[w]
            hr = hv[w].shape[0]
            theirs = out[gi].at[pl.ds(pos * 2 * hr + (1 - ci) * hr, hr)]
            _rcopy(hv[w], theirs, send_sems.at[w], recv_sems.at[w], sib).wait_recv()
        _drain(started)

    return pl.pallas_call(
        body, name=name, in_specs=[ANY] * n, out_specs=[ANY] * len(groups), out_shape=out_shape,
        scratch_shapes=[pltpu.SemaphoreType.DMA((n,)), pltpu.SemaphoreType.DMA((n,)), pltpu.SemaphoreType.DMA((n,))],
        compiler_params=_cparams(),
    )(*halves)


def cast_bf16(name, x):
    r, c = x.shape
    tr = _rows_tile(r, c)

    def body(x_ref, o_ref):
        o_ref[...] = x_ref[...].astype(BF16)

    return pl.pallas_call(
        body, name=name, grid=(r // tr,), in_specs=[pl.BlockSpec((tr, c), lambda i: (i, 0))],
        out_specs=pl.BlockSpec((tr, c), lambda i: (i, 0)), out_shape=jax.ShapeDtypeStruct((r, c), BF16),
        compiler_params=_cparams(dimension_semantics=("parallel",)),
    )(x)


def sum_blocks(name, x, picks):
    _, r, c = x.shape

    def body(x_ref, o_ref):
        acc = x_ref[picks[0]]
        for b in picks[1:]:
            acc = acc + x_ref[b]
        o_ref[...] = acc

    vm = pl.BlockSpec(memory_space=pltpu.VMEM)
    return pl.pallas_call(body, name=name, in_specs=[vm], out_specs=vm, out_shape=jax.ShapeDtypeStruct((r, c), F32),
                          compiler_params=_cparams())(x)


def adamw(name, w, g, m, v):
    r, c = w.shape
    tr = _rows_tile(r, c, unit=8, limit=1 << 19)
    c1 = 1.0 / (1.0 - ADAM_B1 ** ADAM_STEP)
    c2 = 1.0 / (1.0 - ADAM_B2 ** ADAM_STEP)

    def body(w_ref, g_ref, m_ref, v_ref, d_ref, mo_ref, vo_ref):
        gv = g_ref[...]
        mn = ADAM_B1 * m_ref[...] + (1.0 - ADAM_B1) * gv
        vn = ADAM_B2 * v_ref[...] + (1.0 - ADAM_B2) * (gv * gv)
        d_ref[...] = -ADAM_LR * ((mn * c1) / (jnp.sqrt(vn * c2) + ADAM_EPS) + ADAM_WD * w_ref[...])
        mo_ref[...] = mn
        vo_ref[...] = vn

    blk = pl.BlockSpec((tr, c), lambda i: (i, 0))
    return pl.pallas_call(
        body, name=name, grid=(r // tr,), in_specs=[blk] * 4, out_specs=[blk] * 3,
        out_shape=[jax.ShapeDtypeStruct((r, c), F32)] * 3,
        compiler_params=_cparams(dimension_semantics=("parallel",)),
    )(w, g, m, v)


def pack_rows(name, parts, rows):
    width = parts[0].shape[1]
    n = len(parts)

    def body(*refs):
        o_ref = refs[n]
        o_ref[...] = jnp.zeros_like(o_ref)
        off = 0
        for r in refs[:n]:
            o_ref[off:off + r.shape[0], :] = r[...]
            off += r.shape[0]

    vm = pl.BlockSpec(memory_space=pltpu.VMEM)
    return pl.pallas_call(body, name=name, in_specs=[vm] * n, out_specs=vm,
                          out_shape=jax.ShapeDtypeStruct((rows, width), F32), compiler_params=_cparams())(*parts)


def _pad_rows(a, rows):
    return jnp.pad(a, ((0, rows - a.shape[0]), (0, 0)))


def _view2d(name, a):
    if name == 'rw_rk' or a.ndim == 1:
        return a.reshape(1, -1)
    return a.reshape(-1, a.shape[-1])


def _reduce_scatter(tag, items, ci_arr):
    names = []
    for nm, _ in items:
        if nm not in names:
            names.append(nm)
    g4 = [g.reshape(N_CHIPS, 2, g.shape[1] // 2, g.shape[2]) for _, g in items]
    got = rs_pair("rs1_" + tag, g4)
    sums = [rs_add_pair(f"rs1add_{tag}{w}", g4[w], got[w], ci_arr) for w in range(len(items))]
    landed = rs_chips("rs2_" + tag, [s_[1] for s_ in sums], [s_[0] for s_ in sums])
    halves = []
    for w, (r0, r1, r2, own) in enumerate(landed):
        hr, c = own.shape
        halves.append(row_call(f"rs2add_{tag}{w}", f_sum, [own, r0, r1, r2], [], [(c, F32)],
                               tr=_rows_tile(hr, c), sr=16)[0])
    groups = [[w for w, (nm, _) in enumerate(items) if nm == name] for name in names]
    outs = rs_finish("rs3_" + tag, halves, groups)
    return dict(zip(names, outs))


def _step(p):
    xi, yi, ci = _place()
    me = 4 * xi + 2 * yi + ci
    s = 2 * xi + yi
    ci_arr = jnp.reshape(ci, (1,)).astype(jnp.int32)
    x, ctx, tgt = p['x'][0], p['ctx'][0], p['loss_target'][0]
    T, D = x.shape
    L = ctx.shape[0]
    H, Dq = D // HEAD, D // N_CHIPS
    TR = math.gcd(math.gcd(L, T), 256)
    TS = min(TR, 64)
    nct = L // TR
    LG = p['rw_g1'].shape[-1]
    LW, LA = p['rw_w1'].shape[-1], p['rw_a1'].shape[-1]
    F4 = p['ffn_w2'].shape[1]

    pack = jnp.concatenate([
        _pad_rows(p['c'].reshape(N_CHIPS, Dq), 8), _pad_rows(p['rw_mix'][0], 8), _pad_rows(p['rw_w0'][0], 8),
        _pad_rows(p['rw_a0'][0], 8), _pad_rows(p['sc_conv'][0], 8)], axis=0)
    got = all_gather8("ag_small", pack)
    c_all = got[:, 0:N_CHIPS, :].reshape(N_DEV, D)
    full = jnp.transpose(got[::2], (1, 0, 2)).reshape(40, D)
    mix_f, w0_f, a0_f, conv_f = full[8:16], full[16:24], full[24:32], full[32:40]

    cond_in = jnp.concatenate([c_all, _pad_rows(p['c_ctx'].reshape(1, D), 8)], axis=0)
    cond = row_call("cond", f_silu, [cond_in], [], [(D, F32)], tr=16, sr=16)[0]
    ada = Stacked(p['ada_w'], "layer", D)
    modp = [mm_nn(f"modp{i}", cond, ada.at(i)) for i in range(2)]
    mg = all_gather8("ag_mod", jnp.concatenate(modp, axis=0))
    mod = jnp.transpose(mg[::2].reshape(N_CHIPS, 2, 16, 6 * Dq), (1, 2, 0, 3)).reshape(2, 16, 6 * D)
    mod = mod + p['ada_b'][:, None, :]
    mod_x = lax.dynamic_index_in_dim(mod, me, axis=1, keepdims=False)
    mod_c = mod[:, 8]

    def chunk(vec, j):
        return vec[j * D:(j + 1) * D].reshape(1, D)

    sh1x, sc1x, gt1x, sh2x, sc2x, gt2x = ([chunk(mod_x[i], j) for i in range(2)] for j in range(6))
    sh1c, sc1c = chunk(mod_c[0], 0), chunk(mod_c[0], 1)

    def gather(tag, names):
        shards = [cast_bf16("cast_" + n, _view2d(n, p[n])) for n in names]
        return dict(zip(names, gather_weights("ag_" + tag, shards)))

    gw = gather("rw", ['rw_wr', 'rw_wk', 'rw_wv', 'rw_wo', 'rw_w1', 'rw_w2', 'rw_a1', 'rw_a2', 'rw_g1', 'rw_g2'])
    gw.update(gather("sc", ['sc_win', 'sc_wout']))
    gw.update(gather("ffn", ['ffn_w13', 'ffn_w2']))
    Wr, Wk, Wv, Wo = (Stacked(gw[n], "row", Dq) for n in ('rw_wr', 'rw_wk', 'rw_wv', 'rw_wo'))
    W1, A1, G1 = (Stacked(gw[n], "row", Dq) for n in ('rw_w1', 'rw_a1', 'rw_g1'))
    W2, A2, G2 = Stacked(gw['rw_w2'], "col", LW), Stacked(gw['rw_a2'], "col", LA), Stacked(gw['rw_g2'], "col", LG)
    Win, Wout = Stacked(gw['sc_win'], "col", D), Stacked(gw['sc_wout'], "row", Dq)
    W13, W2f = Stacked(gw['ffn_w13'], "col", D), Stacked(gw['ffn_w2'], "row", F4)

    e = (lax.broadcasted_iota(jnp.int32, (D, H), 0) // HEAD == lax.broadcasted_iota(jnp.int32, (D, H), 1)).astype(F32)
    et = e.T
    n1g, n2g = p['norm1_g'], p['norm2_g']
    kkp, ka, lnw, lnb = p['rw_kk'], p['rw_ka'], p['rw_lnw'], p['rw_lnb']
    rk = p['rw_rk'].reshape(1, D)
    fg = p['final_g'].reshape(1, D)

    xin = jnp.concatenate([ctx, x], axis=0)
    nm = functools.partial(f_norm_mod, nct)
    nm_consts = [n1g[0:1], sh1c, sc1c, sh1x[0], sc1x[0]]
    h = row_call("l0_norm", nm, [xin], nm_consts, [(D, F32)], tr=TR, sr=16)[0]
    xr, xw, xk, xv, xa, xg = mix_fwd("l0_mix", h, mix_f, L)
    r = mm_nn("l0_r", xr, Wr)
    k = mm_nn("l0_k", xk, Wk)
    v = mm_nn("l0_v", xv, Wv)
    gl = mm_nn("l0_gl", xg, G1)
    sg = row_call("l0_sg", f_sigmoid, [gl], [], [(LG, BF16)], tr=TR, sr=16)[0]
    g = mm_nn("l0_g", sg, G2)
    wl, tw, zw, al, za = [], [], [], [], []
    for d in range(2):
        wl.append(mm_nn(f"l0_wl{d}", xw, W1.at(d)))
        tw.append(row_call(f"l0_tw{d}", f_tanh, [wl[d]], [], [(LW, BF16)], tr=TR, sr=16)[0])
        zw.append(mm_nn(f"l0_zw{d}", tw[d], W2.at(d)))
        al.append(mm_nn(f"l0_al{d}", xa, A1.at(d), BF16))
        za.append(mm_nn(f"l0_za{d}", al[d], A2.at(d)))
    post_rows = [k, zw[0], zw[1], za[0], za[1]]
    post_consts = [kkp, ka, w0_f[0:1], w0_f[1:2], a0_f[0:1], a0_f[1:2], e, et]
    aa, dec0, dec1, kd0, kd1, bb0, bb1, ksum = row_call(
        "l0_post", f_post_fwd, post_rows, post_consts, [(D, F32)] * 8, tr=TS, sr=16)
    dec, kd, bb = (dec0, dec1), (kd0, kd1), (bb0, bb1)
    ys, sts = [], []
    for d in range(2):
        y_d, st_d = scan_fwd(f"l0_scan{d}", r, dec[d], kd[d], v, aa, bb[d], L, bool(d))
        ys.append(y_d)
        sts.append(st_d)
    ro_rows = [ys[0], ys[1], r, ksum, v, g]
    ro_consts = [rk, lnw, lnb, e, et]
    og = row_call("l0_readout", f_readout, ro_rows, ro_consts, [(D, BF16)], tr=TS, sr=16)[0]
    yx = mm_nn("l0_o", og, Wo)
    res0_consts = [gt1x[0], n2g[0:1], sh2x[0], sc2x[0]]
    x1, h2 = row_call("l0_res", f_res_norm_mod, [x, yx], res0_consts, [(D, F32), (D, BF16)], tr=TR, sr=16,
                      offs=[0, nct])
    ab0 = mm_nn("l0_ffn13", h2, W13.at(0))
    sw0 = swiglu_fwd("l0_swiglu", ab0, TS)
    f0 = mm_nn("l0_ffn2", sw0, W2f.at(0))

    res1_consts = [gt2x[0], n1g[1:2], sh1x[1], sc1x[1]]
    x2, hb = row_call("l1_norm", f_res_norm_mod, [x1, f0], res1_consts, [(D, F32), (D, BF16)], tr=TR, sr=16)
    gcu = mm_nn("l1_win", hb, Win)
    pc = conv_fwd("l1_conv", gcu, conv_f)
    yx1 = mm_nn("l1_wout", pc, Wout)
    res2_consts = [gt1x[1], n2g[1:2], sh2x[1], sc2x[1]]
    x3, h2b = row_call("l1_res", f_res_norm_mod, [x2, yx1], res2_consts, [(D, F32), (D, BF16)], tr=TR, sr=16)
    ab1 = mm_nn("l1_ffn13", h2b, W13.at(1))
    sw1 = swiglu_fwd("l1_swiglu", ab1, TS)
    f1 = mm_nn("l1_ffn2", sw1, W2f.at(1))
    dx3, df1, dgt2_1, dfg, loss_blk = final_call("final", x3, f1, gt2x[1], fg, tgt, TR)
    loss = lax.psum(loss_blk[0, 0], ("x", "y", "c"))

    big = []
    dsw1 = mm_nt("b1_dsw", df1, W2f.at(1))
    gW2f1 = mm_tn("b1_gw2", sw1, df1, "row")
    dab1 = swiglu_bwd("b1_swiglu", ab1, dsw1, TS)
    dh2b = mm_nt("b1_dh2", dab1, W13.at(1))
    gW13_1 = mm_tn("b1_gw13", h2b, dab1, "col")
    rm = [True, True]
    cm = [True] * 4
    (dx2, dyx1), (dgt1_1, dn2g1, dsh2_1, dsc2_1) = row_vjp(
        "b1_res", f_res_norm_mod, [x2, yx1], res2_consts, [dx3, dh2b], row_mask=rm, const_mask=cm, tr=TR, sr=16)
    dpc = mm_nt("b1_dpc", dyx1, Wout)
    big.append(('sc_wout', mm_tn("b1_gwout", pc, dyx1, "row")))
    dgcu, dconv = conv_bwd("b1_conv", gcu, conv_f, dpc)
    dhb = mm_nt("b1_dhb", dgcu, Win)
    big.append(('sc_win', mm_tn("b1_gwin", hb, dgcu, "col")))
    (dx1, df0), (dgt2_0, dn1g1, dsh1_1, dsc1_1) = row_vjp(
        "b1_norm", f_res_norm_mod, [x1, f0], res1_consts, [dx2, dhb], row_mask=rm, const_mask=cm, tr=TR, sr=16)

    dsw0 = mm_nt("b0_dsw", df0, W2f.at(0))
    gW2f0 = mm_tn("b0_gw2", sw0, df0, "row")
    dab0 = swiglu_bwd("b0_swiglu", ab0, dsw0, TS)
    dh2 = mm_nt("b0_dh2", dab0, W13.at(0))
    gW13_0 = mm_tn("b0_gw13", h2, dab0, "col")
    (dx_a, dyx), (dgt1_0, dn2g0, dsh2_0, dsc2_0) = row_vjp(
        "b0_res", f_res_norm_mod, [x, yx], res0_consts, [dx1, dh2], row_mask=rm, const_mask=cm, tr=TR, sr=16,
        offs=[0, nct])
    dyx_all = jnp.concatenate([jnp.zeros((L, D), F32), dyx], axis=0)
    dog = mm_nt("b0_dog", dyx_all, Wo)
    gWo = mm_tn("b0_gwo", og, dyx_all, "row")
    (dy, dr_ro, dksum, dv_ro, dg), (drk, dlnw, dlnb) = row_vjp(
        "b0_readout", f_readout, ro_rows, ro_consts, [dog], row_mask=[True, False, True, True, True, True],
        const_mask=[True, True, True, False, False], tr=TS, sr=16)
    sg_ = [scan_bwd(f"b0_scan{d}", r, dec[d], kd[d], v, aa, bb[d], dy, sts[d], L, bool(d)) for d in range(2)]
    (dr0, ddec0, dkd0, dv0, daa0, dbb0), (dr1, ddec1, dkd1, dv1, daa1, dbb1) = sg_
    post_cots = [daa0, daa1, ddec0, ddec1, dkd0, dkd1, dbb0, dbb1, dksum]
    (dk, dzw0, dzw1, dza0, dza1), (dkkp, dka, dw00, dw01, da00, da01) = row_vjp(
        "b0_post", f_post, post_rows, post_consts, post_cots, row_mask=[True] * 5,
        const_mask=[True] * 6 + [False, False], tr=TS, sr=16)
    dzw, dza = (dzw0, dzw1), (dza0, dza1)
    dr_t = sum_cast("b0_drsum", [dr0, dr1, dr_ro], BF16, TR)
    dv_t = sum_cast("b0_dvsum", [dv0, dv1, dv_ro], BF16, TR)
    mix_cots, mix_slots = [], []

    def back(tag, cot, w, xin_m, kind, slot):
        mix_cots.append(mm_nt("b0_dx" + tag, cot, w))
        mix_slots.append(slot)
        return mm_tn("b0_gw" + tag, xin_m, cot, kind)

    big.append(('rw_wr', back("r", dr_t, Wr, xr, "row", 0)))
    big.append(('rw_wk', back("k", dk, Wk, xk, "row", 2)))
    big.append(('rw_wv', back("v", dv_t, Wv, xv, "row", 3)))
    big.append(('rw_wo', gWo))
    dsg = mm_nt("b0_dsg", dg, G2)
    gG2 = mm_tn("b0_gg2", sg, dg, "col")
    (dgl,), _ = row_vjp("b0_sg", f_sigmoid, [gl], [], [dsg], row_mask=[True], const_mask=[], tr=TR, sr=16)
    gG1 = back("g", dgl, G1, xg, "row", 5)
    gW1, gW2, gA1, gA2 = [], [], [], []
    for d in range(2):
        dtw = mm_nt(f"b0_dtw{d}", dzw[d], W2.at(d))
        gW2.append(mm_tn(f"b0_gw2{d}", tw[d], dzw[d], "col"))
        (dwl,), _ = row_vjp(f"b0_tw{d}", f_tanh, [wl[d]], [], [dtw], row_mask=[True], const_mask=[], tr=TR, sr=16)
        gW1.append(back(f"w{d}", dwl, W1.at(d), xw, "row", 1))
        dal = mm_nt(f"b0_dal{d}", dza[d], A2.at(d))
        gA2.append(mm_tn(f"b0_ga2{d}", al[d], dza[d], "col"))
        gA1.append(back(f"a{d}", dal, A1.at(d), xa, "row", 4))
    big += [('rw_w1', gW1[0]), ('rw_w1', gW1[1]), ('rw_w2', gW2[0]), ('rw_w2', gW2[1]),
            ('rw_a1', gA1[0]), ('rw_a1', gA1[1]), ('rw_a2', gA2[0]), ('rw_a2', gA2[1]),
            ('rw_g1', gG1), ('rw_g2', gG2)]
    dh, dmix = mix_bwd("b0_mix", h, mix_f, mix_cots, mix_slots, L)
    (dxin,), (dn1g0, dsh1c, dsc1c, dsh1x, dsc1x) = row_vjp(
        "b0_norm", nm, [xin], nm_consts, [dh], row_mask=[True], const_mask=[True] * 5, tr=TR, sr=16)
    grad_x = sum_cast("b0_dx", [dxin, dx_a], F32, TR, offs=[nct, 0])

    zero = jnp.zeros((1, D), F32)
    parts = [dsh1x, dsc1x, dgt1_0, dsh2_0, dsc2_0, dgt2_0, dsh1c, dsc1c, zero, zero, zero, zero,
             dsh1_1, dsc1_1, dgt1_1, dsh2_1, dsc2_1, dgt2_1, zero, zero, zero, zero, zero, zero,
             dn1g0, dn1g1, dn2g0, dn2g1, dkkp, dka, drk, dlnw, dlnb, dfg,
             dmix[0:6], dw00, dw01, da00, da01, dconv[0:3]]
    got2 = all_gather8("ag_grads", pack_rows("pack_grads", parts, 48))
    small = sum_blocks("sum_grads", got2, list(range(N_DEV)))
    per_ex = got2[:, 0:24].reshape(N_DEV, 2, 2, 6 * D)
    tot = small[0:24].reshape(2, 2, 6 * D)
    cols = lambda a, width: lax.dynamic_slice_in_dim(a, s * width, width, axis=1)
    g_ada_w, dcond_parts = [], []
    for i in range(2):
        dm16 = cols(jnp.concatenate([per_ex[:, i, 0], _pad_rows(tot[i, 1][None], 8)], axis=0), 6 * Dq)
        g_ada_w.append(mm_tn(f"g_ada{i}", cond, dm16))
        dcond_parts.append(mm_nt(f"dcond{i}", dm16, ada.at(i)))
    g_ada_b = sum_cast("g_adab", [_pad_rows(tot[:, 0].reshape(12, D), 16), _pad_rows(tot[:, 1].reshape(12, D), 16)],
                       F32, 16)[0:12].reshape(2, 6 * D)
    dcond_mine = sum_cast("dcond_sum", dcond_parts, F32, 16)
    dcond = sum_blocks("dcond_chips", all_gather8("ag_dcond", dcond_mine), [0, 2, 4, 6])
    (dcin,), _ = row_vjp("b_cond", f_silu, [cond_in], [], [dcond], row_mask=[True], const_mask=[], tr=16, sr=16)

    rw_names = ('rw_wr', 'rw_wk', 'rw_wv', 'rw_wo', 'rw_w1', 'rw_w2', 'rw_a1', 'rw_a2', 'rw_g1', 'rw_g2')
    gsh = _reduce_scatter("rw", [it for it in big if it[0] in rw_names], ci_arr)
    gsh.update(_reduce_scatter("sc", [it for it in big if it[0] in ('sc_win', 'sc_wout')], ci_arr))
    gsh.update(_reduce_scatter("ffn", [('ffn_w13', gW13_0), ('ffn_w13', gW13_1), ('ffn_w2', gW2f0),
                                       ('ffn_w2', gW2f1)], ci_arr))

    grads = dict(gsh)
    grads['c_ctx'] = dcin[8]
    grads['norm1_g'], grads['norm2_g'] = small[24:26], small[26:28]
    grads['ada_w'] = jnp.stack(g_ada_w)
    grads['ada_b'] = g_ada_b
    grads['rw_kk'], grads['rw_ka'], grads['rw_rk'] = small[28:29], small[29:30], small[30:31]
    grads['rw_lnw'], grads['rw_lnb'], grads['final_g'] = small[31:32], small[32:33], small[33]
    sharded = cols(small[34:48], Dq)
    grads['rw_mix'], grads['rw_w0'], grads['rw_a0'], grads['sc_conv'] = (
        sharded[0:6], sharded[6:8], sharded[8:10], sharded[10:13])

    outs_g, outs_d, outs_m, outs_v = [], [], [], []
    for n in WEIGHTS:
        shape = p[n].shape
        g2d = _view2d(n, grads[n].reshape(shape))
        d_, m_, v_ = adamw("adam_" + n, _view2d(n, p[n]), g2d, _view2d(n, p['m_' + n]), _view2d(n, p['v_' + n]))
        outs_g.append(g2d.reshape(shape))
        outs_d.append(d_.reshape(shape))
        outs_m.append(m_.reshape(shape))
        outs_v.append(v_.reshape(shape))
    return (loss, grad_x.reshape(1, T, D), *outs_g, *outs_d, *outs_m, *outs_v)


def kernel(x, c, ctx, c_ctx, norm1_g, norm2_g, ada_w, ada_b, rw_mix, rw_wr, rw_wk, rw_wv, rw_wo, rw_w0, rw_w1, rw_w2, rw_a0, rw_a1, rw_a2, rw_g1, rw_g2, rw_kk, rw_ka, rw_rk, rw_lnw, rw_lnb, sc_win, sc_conv, sc_wout, ffn_w13, ffn_w2, final_g, loss_target, m_c_ctx, m_norm1_g, m_norm2_g, m_ada_w, m_ada_b, m_rw_mix, m_rw_wr, m_rw_wk, m_rw_wv, m_rw_wo, m_rw_w0, m_rw_w1, m_rw_w2, m_rw_a0, m_rw_a1, m_rw_a2, m_rw_g1, m_rw_g2, m_rw_kk, m_rw_ka, m_rw_rk, m_rw_lnw, m_rw_lnb, m_sc_win, m_sc_conv, m_sc_wout, m_ffn_w13, m_ffn_w2, m_final_g, v_c_ctx, v_norm1_g, v_norm2_g, v_ada_w, v_ada_b, v_rw_mix, v_rw_wr, v_rw_wk, v_rw_wv, v_rw_wo, v_rw_w0, v_rw_w1, v_rw_w2, v_rw_a0, v_rw_a1, v_rw_a2, v_rw_g1, v_rw_g2, v_rw_kk, v_rw_ka, v_rw_rk, v_rw_lnw, v_rw_lnb, v_sc_win, v_sc_conv, v_sc_wout, v_ffn_w13, v_ffn_w2, v_final_g):
    values = (x, c, ctx, c_ctx, norm1_g, norm2_g, ada_w, ada_b, rw_mix, rw_wr, rw_wk, rw_wv, rw_wo, rw_w0, rw_w1, rw_w2, rw_a0, rw_a1, rw_a2, rw_g1, rw_g2, rw_kk, rw_ka, rw_rk, rw_lnw, rw_lnb, sc_win, sc_conv, sc_wout, ffn_w13, ffn_w2, final_g, loss_target, m_c_ctx, m_norm1_g, m_norm2_g, m_ada_w, m_ada_b, m_rw_mix, m_rw_wr, m_rw_wk, m_rw_wv, m_rw_wo, m_rw_w0, m_rw_w1, m_rw_w2, m_rw_a0, m_rw_a1, m_rw_a2, m_rw_g1, m_rw_g2, m_rw_kk, m_rw_ka, m_rw_rk, m_rw_lnw, m_rw_lnb, m_sc_win, m_sc_conv, m_sc_wout, m_ffn_w13, m_ffn_w2, m_final_g, v_c_ctx, v_norm1_g, v_norm2_g, v_ada_w, v_ada_b, v_rw_mix, v_rw_wr, v_rw_wk, v_rw_wv, v_rw_wo, v_rw_w0, v_rw_w1, v_rw_w2, v_rw_a0, v_rw_a1, v_rw_a2, v_rw_g1, v_rw_g2, v_rw_kk, v_rw_ka, v_rw_rk, v_rw_lnw, v_rw_lnb, v_sc_win, v_sc_conv, v_sc_wout, v_ffn_w13, v_ffn_w2, v_final_g)
    return _step(dict(zip(INPUTS, values)))
```

```python
import functools
import math

import jax
import jax.numpy as jnp
from jax import lax
from jax.experimental import pallas as pl
from jax.experimental.pallas import tpu as pltpu

F32 = jnp.float32
BF16 = jnp.bfloat16
MESH = pl.DeviceIdType.MESH

GRID_W = 64
HEAD = 64
LANES = 128
N_CHIPS = 4
N_DEV = 8
NORM_EPS = 1e-6
GN_EPS = 64e-5
ADAM_LR, ADAM_B1, ADAM_B2, ADAM_EPS, ADAM_WD, ADAM_STEP = 0.001, 0.9, 0.999, 1e-08, 0.01, 10
VMEM_LIMIT = 56 * 1024 * 1024
HI = lax.Precision.HIGHEST
WEIGHTS = ['c_ctx', 'norm1_g', 'norm2_g', 'ada_w', 'ada_b', 'rw_mix', 'rw_wr', 'rw_wk', 'rw_wv', 'rw_wo', 'rw_w0',
           'rw_w1', 'rw_w2', 'rw_a0', 'rw_a1', 'rw_a2', 'rw_g1', 'rw_g2', 'rw_kk', 'rw_ka', 'rw_rk', 'rw_lnw',
           'rw_lnb', 'sc_win', 'sc_conv', 'sc_wout', 'ffn_w13', 'ffn_w2', 'final_g']
INPUTS = (['x', 'c', 'ctx'] + WEIGHTS + ['loss_target'] + ['m_' + w for w in WEIGHTS]
          + ['v_' + w for w in WEIGHTS])


def _cparams(**kw):
    return pltpu.CompilerParams(vmem_limit_bytes=VMEM_LIMIT, **kw)


def _pick(dim, cands):
    for c in cands:
        if dim % c == 0:
            return c
    return dim


def _place():
    return lax.axis_index("x"), lax.axis_index("y"), lax.axis_index("c")


_TILE_M = (1024, 768, 512, 1408, 256, 128)
_TILE_N = (1408, 1024, 768, 512, 256, 128)
_TILE_K = (512, 1408, 256, 128)


class Stacked:
    def __init__(self, arr, kind, r, layer=0):
        self.arr, self.kind, self.r, self.layer = arr, kind, r, layer
        self.c = arr.shape[2]
        self.shape = {"row": (N_CHIPS * r, self.c), "col": (r, N_CHIPS * self.c), "layer": (r, self.c)}[kind]

    def at(self, layer):
        return Stacked(self.arr, self.kind, self.r, layer)

    def spec(self, t0, t1, swap):
        r, c, layer = self.r, self.c, self.layer
        per_r, per_c = r // t0, c // t1
        assert r % t0 == 0 and c % t1 == 0
        kind = self.kind

        def index(i, j, k):
            ri, ci = (j, k) if swap else (k, j)
            if kind == "row":
                return (ri // per_r, layer * per_r + ri % per_r, ci)
            if kind == "layer":
                return (layer, ri, ci)
            return (ci // per_c, layer * per_r + ri, ci % per_c)

        return pl.BlockSpec((None, t0, t1), index)


def _mm_body(dims, nk, a_ref, b_ref, o_ref, acc_ref):
    k = pl.program_id(2)

    @pl.when(k == 0)
    def _():
        acc_ref[...] = jnp.zeros_like(acc_ref)

    acc_ref[...] += lax.dot_general(a_ref[...].astype(BF16), b_ref[...].astype(BF16), (dims, ((), ())),
                                    preferred_element_type=F32)

    @pl.when(k == nk - 1)
    def _():
        o_ref[...] = acc_ref[...].astype(o_ref.dtype)


def _mm_call(name, dims, grid, in_specs, out_spec, out_shape, acc_shape, operands):
    return pl.pallas_call(
        functools.partial(_mm_body, dims, grid[2]), name=name, grid=grid, in_specs=in_specs, out_specs=out_spec,
        out_shape=out_shape, scratch_shapes=[pltpu.VMEM(acc_shape, F32)],
        compiler_params=_cparams(dimension_semantics=("parallel", "parallel", "arbitrary")),
    )(*operands)


def mm_nn(name, a, b, out_dtype=F32):
    M, K = a.shape
    st = isinstance(b, Stacked)
    N = b.shape[1]
    tm = _pick(M, _TILE_M)
    tn = _pick(b.c if st and b.kind == "col" else N, _TILE_N)
    tk = _pick(b.r if st else K, _TILE_K)
    b_spec = b.spec(tk, tn, False) if st else pl.BlockSpec((tk, tn), lambda i, j, k: (k, j))
    return _mm_call(name, ((1,), (0,)), (M // tm, N // tn, K // tk),
                    [pl.BlockSpec((tm, tk), lambda i, j, k: (i, k)), b_spec],
                    pl.BlockSpec((tm, tn), lambda i, j, k: (i, j)), jax.ShapeDtypeStruct((M, N), out_dtype),
                    (tm, tn), (a, b.arr if st else b))


def mm_nt(name, a, b, out_dtype=F32):
    M, N = a.shape
    st = isinstance(b, Stacked)
    K = b.shape[0]
    tm = _pick(M, _TILE_M)
    to = _pick(b.r if st else K, _TILE_N)
    tc = _pick(b.c if st and b.kind == "col" else N, _TILE_K)
    b_spec = b.spec(to, tc, True) if st else pl.BlockSpec((to, tc), lambda i, j, k: (j, k))
    return _mm_call(name, ((1,), (1,)), (M // tm, K // to, N // tc),
                    [pl.BlockSpec((tm, tc), lambda i, j, k: (i, k)), b_spec],
                    pl.BlockSpec((tm, to), lambda i, j, k: (i, j)), jax.ShapeDtypeStruct((M, K), out_dtype),
                    (tm, to), (a, b.arr if st else b))


def mm_tn(name, a, b, kind=None):
    R, M = a.shape
    N = b.shape[1]
    r, c = (M // N_CHIPS, N) if kind == "row" else (M, N // N_CHIPS) if kind == "col" else (M, N)
    tm, tn, tk = _pick(r, _TILE_M), _pick(c, _TILE_N), _pick(R, (1024, 768, 512, 256, 128))
    if kind:
        per_r, per_c = r // tm, c // tn
        if kind == "row":
            o_spec = pl.BlockSpec((None, tm, tn), lambda i, j, k: (i // per_r, i % per_r, j))
        else:
            o_spec = pl.BlockSpec((None, tm, tn), lambda i, j, k: (j // per_c, i, j % per_c))
        o_shape = jax.ShapeDtypeStruct((N_CHIPS, r, c), F32)
    else:
        o_spec = pl.BlockSpec((tm, tn), lambda i, j, k: (i, j))
        o_shape = jax.ShapeDtypeStruct((M, N), F32)
    return _mm_call(name, ((0,), (0,)), (M // tm, N // tn, R // tk),
                    [pl.BlockSpec((tk, tm), lambda i, j, k: (k, i)), pl.BlockSpec((tk, tn), lambda i, j, k: (k, j))],
                    o_spec, o_shape, (tm, tn), (a, b))


def _shifted(o):
    return lambda i: (i + o, 0)


def row_call(name, f, rows, consts, outs, *, tr, sr, offs=None):
    offs = offs or [0] * len(rows)
    n_rows = min(r.shape[0] - o * tr for r, o in zip(rows, offs))
    nr, nc = len(rows), len(consts)

    def body(*refs):
        row_refs, const_refs, out_refs = refs[:nr], refs[nr:nr + nc], refs[nr + nc:]
        i = pl.program_id(0)
        cvals = [r[...] for r in const_refs]

        def step(j, carry):
            sl = pl.ds(pl.multiple_of(j * sr, sr), sr)
            res = f(i, *[r[sl, :] for r in row_refs], *cvals)
            for o, v in zip(out_refs, res):
                o[sl, :] = v.astype(o.dtype)
            return carry

        lax.fori_loop(0, tr // sr, step, 0)

    in_specs = [pl.BlockSpec((tr, r.shape[1]), _shifted(o)) for r, o in zip(rows, offs)]
    in_specs += [pl.BlockSpec(c.shape, lambda i: (0, 0)) for c in consts]
    return pl.pallas_call(
        body, name=name, grid=(n_rows // tr,), in_specs=in_specs,
        out_specs=[pl.BlockSpec((tr, w), lambda i: (i, 0)) for w, _ in outs],
        out_shape=[jax.ShapeDtypeStruct((n_rows, w), dt) for w, dt in outs],
        compiler_params=_cparams(dimension_semantics=("parallel",)),
    )(*rows, *consts)


def row_vjp(name, f, rows, consts, cots, *, row_mask, const_mask, tr, sr, offs=None):
    offs = offs or [0] * len(rows)
    n_rows = min(r.shape[0] - o * tr for r, o in zip(rows, offs))
    nr, nc = len(rows), len(consts)
    cot_in = [c for c in cots if c is not None]
    nct = len(cot_in)
    d_rows = [i for i in range(nr) if row_mask[i]]
    d_consts = [i for i in range(nc) if const_mask[i]]

    def body(*refs):
        row_refs, const_refs = refs[:nr], refs[nr:nr + nc]
        cot_refs = refs[nr + nc:nr + nc + nct]
        drow_refs = refs[nr + nc + nct:nr + nc + nct + len(d_rows)]
        dconst_refs = refs[nr + nc + nct + len(d_rows):]
        i = pl.program_id(0)

        @pl.when(i == 0)
        def _():
            for r in dconst_refs:
                r[...] = jnp.zeros_like(r)

        cvals = [r[...] for r in const_refs]

        def step(j, carry):
            sl = pl.ds(pl.multiple_of(j * sr, sr), sr)
            rvals = [r[sl, :] for r in row_refs]

            def g(*diff):
                rv, cv = list(rvals), list(cvals)
                for idx, val in zip(d_rows, diff[:len(d_rows)]):
                    rv[idx] = val
                for idx, val in zip(d_consts, diff[len(d_rows):]):
                    cv[idx] = val
                return f(i, *rv, *cv)

            primals = [rvals[idx].astype(F32) for idx in d_rows] + [cvals[idx] for idx in d_consts]
            res, vjp = jax.vjp(g, *primals)
            it = iter(cot_refs)
            cts = tuple(jnp.zeros_like(o) if c is None else next(it)[sl, :].astype(o.dtype) for o, c in zip(res, cots))
            grads = vjp(cts)
            for r, val in zip(drow_refs, grads[:len(d_rows)]):
                r[sl, :] = val.astype(r.dtype)
            for r, val in zip(dconst_refs, grads[len(d_rows):]):
                r[...] += val
            return carry

        lax.fori_loop(0, tr // sr, step, 0)

    in_specs = [pl.BlockSpec((tr, r.shape[1]), _shifted(o)) for r, o in zip(rows, offs)]
    in_specs += [pl.BlockSpec(c.shape, lambda i: (0, 0)) for c in consts]
    in_specs += [pl.BlockSpec((tr, c.shape[1]), lambda i: (i, 0)) for c in cot_in]
    out_specs = [pl.BlockSpec((tr, rows[i].shape[1]), lambda i: (i, 0)) for i in d_rows]
    out_specs += [pl.BlockSpec(consts[i].shape, lambda i: (0, 0)) for i in d_consts]
    out_shape = [jax.ShapeDtypeStruct((n_rows, rows[i].shape[1]), F32) for i in d_rows]
    out_shape += [jax.ShapeDtypeStruct(consts[i].shape, F32) for i in d_consts]
    res = pl.pallas_call(
        body, name=name, grid=(n_rows // tr,), in_specs=in_specs, out_specs=out_specs, out_shape=out_shape,
        compiler_params=_cparams(dimension_semantics=("arbitrary",)),
    )(*rows, *consts, *cot_in)
    return list(res[:len(d_rows)]), list(res[len(d_rows):])


def _sigmoid(x):
    return 1.0 / (1.0 + jnp.exp(-x))


def _softplus(u):
    return jnp.maximum(u, 0.0) + jnp.log(1.0 + jnp.exp(-jnp.abs(u)))


def _rms(x, g):
    ms = jnp.sum(x * x, axis=-1, keepdims=True) * (1.0 / x.shape[-1])
    return x * lax.rsqrt(ms + NORM_EPS) * g


def _hsum_impl(x, ones2):
    rows, width = x.shape
    nch = width // LANES
    xs = jnp.concatenate([x[:, j * LANES:(j + 1) * LANES] for j in range(nch)], axis=0)
    hi = xs.astype(BF16)
    lo = (xs - hi.astype(F32)).astype(BF16)
    ys = jnp.dot(jnp.concatenate([hi, lo], axis=1), ones2, preferred_element_type=F32)
    return jnp.concatenate([ys[j * rows:(j + 1) * rows] for j in range(nch)], axis=1)


@jax.custom_vjp
def _hsum(x, ones2):
    return _hsum_impl(x, ones2)


def _hsum_fwd(x, ones2):
    return _hsum_impl(x, ones2), ones2


def _hsum_bwd(ones2, g):
    return _hsum_impl(g, ones2), jnp.zeros_like(ones2)


_hsum.defvjp(_hsum_fwd, _hsum_bwd)


def f_silu(i, x):
    return (x * _sigmoid(x),)


def f_sigmoid(i, x):
    return (_sigmoid(x),)


def f_tanh(i, x):
    return (jnp.tanh(x),)


def f_norm_mod(n_ctx_tiles, i, xin, g, sh_c, sc_c, sh_x, sc_x):
    is_x = i >= n_ctx_tiles
    sh = jnp.where(is_x, sh_x, sh_c)
    sc = jnp.where(is_x, sc_x, sc_c)
    return (_rms(xin, g) * (1.0 + sc) + sh,)


def f_res_norm_mod(i, x, y, gt, g, sh, sc):
    x1 = x + gt * y
    return x1, _rms(x1, g) * (1.0 + sc) + sh


def f_post(i, k, zw0, zw1, za0, za1, kkp, ka, w00, w01, a00, a01, ones2):
    kq = k * kkp
    kk = kq / jnp.maximum(jnp.sqrt(_hsum(kq * kq, ones2)), 1e-12)

    def direction(zw, za, w0, a0):
        log_w = -_softplus(-(w0 + zw)) - 0.5
        a = _sigmoid(a0 + za)
        return jnp.exp(-jnp.exp(log_w)), k * (1.0 + (a - 1.0) * ka), kk * a

    dec0, kd0, bb0 = direction(zw0, za0, w00, a00)
    dec1, kd1, bb1 = direction(zw1, za1, w01, a01)
    return -kk, -kk, dec0, dec1, kd0, kd1, bb0, bb1, kd0 + kd1


def f_post_fwd(*a):
    return f_post(*a)[1:]


def f_readout(i, y0, y1, r, ksum, v, g, rk, lnw, lnb, ones2):
    y = y0 + y1
    yc = y - _hsum(y, ones2) * (1.0 / HEAD)
    var = _hsum(yc * yc, ones2) * (1.0 / HEAD)
    o = yc * lax.rsqrt(var + GN_EPS) * lnw + lnb
    o = o + _hsum(r * ksum * rk, ones2) * v
    return (o * g,)


def f_sum(i, *xs):
    acc = xs[0].astype(F32)
    for x in xs[1:]:
        acc = acc + x.astype(F32)
    return (acc,)


def sum_cast(name, arrs, dtype, tr, offs=None):
    return row_call(name, f_sum, arrs, [], [(arrs[0].shape[1], dtype)], tr=tr, sr=16, offs=offs)[0]


def swiglu_fwd(name, ab, tr):
    T, F2 = ab.shape
    F = F2 // 2
    sr = 16

    def body(ab_ref, o_ref):
        def step(j, carry):
            sl = pl.ds(pl.multiple_of(j * sr, sr), sr)
            a, b = ab_ref[sl, :F], ab_ref[sl, F:]
            o_ref[sl, :] = (a * _sigmoid(a) * b).astype(o_ref.dtype)
            return carry

        lax.fori_loop(0, tr // sr, step, 0)

    return pl.pallas_call(
        body, name=name, grid=(T // tr,), in_specs=[pl.BlockSpec((tr, F2), lambda i: (i, 0))],
        out_specs=pl.BlockSpec((tr, F), lambda i: (i, 0)), out_shape=jax.ShapeDtypeStruct((T, F), BF16),
        compiler_params=_cparams(dimension_semantics=("parallel",)),
    )(ab)


def swiglu_bwd(name, ab, dsw, tr):
    T, F2 = ab.shape
    F = F2 // 2
    sr = 16

    def body(ab_ref, d_ref, o_ref):
        def step(j, carry):
            sl = pl.ds(pl.multiple_of(j * sr, sr), sr)
            a, b, d = ab_ref[sl, :F], ab_ref[sl, F:], d_ref[sl, :]
            sg = _sigmoid(a)
            o_ref[sl, :F] = d * b * (sg + a * sg * (1.0 - sg))
            o_ref[sl, F:] = d * a * sg
            return carry

        lax.fori_loop(0, tr // sr, step, 0)

    return pl.pallas_call(
        body, name=name, grid=(T // tr,),
        in_specs=[pl.BlockSpec((tr, F2), lambda i: (i, 0)), pl.BlockSpec((tr, F), lambda i: (i, 0))],
        out_specs=pl.BlockSpec((tr, F2), lambda i: (i, 0)), out_shape=jax.ShapeDtypeStruct((T, F2), F32),
        compiler_params=_cparams(dimension_semantics=("parallel",)),
    )(ab, dsw)


def final_call(name, x3, f1, gt, fg, tgt, tr):
    T, D = x3.shape
    sr = 16

    def f(x, y, gtv, g, t):
        err = _rms(x + gtv * y, g) - t
        return 0.5 * jnp.sum(err * err) * (1.0 / D)

    def body(x_ref, y_ref, gt_ref, g_ref, t_ref, dx_ref, dy_ref, dgt_ref, dg_ref, loss_ref):
        @pl.when(pl.program_id(0) == 0)
        def _():
            dgt_ref[...] = jnp.zeros_like(dgt_ref)
            dg_ref[...] = jnp.zeros_like(dg_ref)
            loss_ref[...] = jnp.zeros_like(loss_ref)

        def step(j, carry):
            sl = pl.ds(pl.multiple_of(j * sr, sr), sr)
            val, vjp = jax.vjp(lambda x, y, a, b: f(x, y, a, b, t_ref[sl, :]), x_ref[sl, :], y_ref[sl, :],
                               gt_ref[...], g_ref[...])
            dx, dy, dgt, dg = vjp(jnp.ones((), F32))
            dx_ref[sl, :] = dx
            dy_ref[sl, :] = dy
            dgt_ref[...] += dgt
            dg_ref[...] += dg
            loss_ref[...] += jnp.full(loss_ref.shape, val, F32)
            return carry

        lax.fori_loop(0, tr // sr, step, 0)

    row = pl.BlockSpec((tr, D), lambda i: (i, 0))
    vec = pl.BlockSpec((1, D), lambda i: (0, 0))
    return pl.pallas_call(
        body, name=name, grid=(T // tr,), in_specs=[row, row, vec, vec, row],
        out_specs=[row, row, vec, vec, pl.BlockSpec((8, LANES), lambda i: (0, 0))],
        out_shape=[jax.ShapeDtypeStruct((T, D), F32)] * 2 + [jax.ShapeDtypeStruct((1, D), F32)] * 2
        + [jax.ShapeDtypeStruct((8, LANES), F32)],
        compiler_params=_cparams(dimension_semantics=("arbitrary",)),
    )(x3, f1, gt, fg, tgt)


def _tshift(x, kind, period):
    n = x.shape[0]
    t = lax.broadcasted_iota(jnp.int32, x.shape, 0)
    if kind == 0:
        return jnp.where((t & (period - 1)) == 0, 0.0, pltpu.roll(x, 1, 0))
    if kind == 1:
        return jnp.where(((t & (period - 1)) == period - 1) | (t == n - 1), 0.0, pltpu.roll(x, n - 1, 0))
    if kind == 2:
        return jnp.where(t < GRID_W, 0.0, pltpu.roll(x, GRID_W, 0))
    return jnp.where(t >= n - GRID_W, 0.0, pltpu.roll(x, n - GRID_W, 0))


def _pow2_at_least(n):
    return 1 << (n - 1).bit_length()


def _shift_into(dst_ref, h_ref, n_ctx, cb, D, transpose):
    j = pl.program_id(0)
    quarter = (j * cb * 4) // D
    half = (j * cb * 2) // D
    flip = 1 if transpose else 0
    for q in range(4):
        @pl.when(quarter == q)
        def _(q=q):
            dst_ref[n_ctx:, :] = _tshift(h_ref[n_ctx:, :], q ^ flip, GRID_W)
    for q in range(2):
        @pl.when(half == q)
        def _(q=q):
            dst_ref[:n_ctx, :] = _tshift(h_ref[:n_ctx, :], q ^ flip, _pow2_at_least(n_ctx))


def mix_fwd(name, h, mix, n_ctx):
    R, D = h.shape
    cb = LANES

    def body(h_ref, mix_ref, *rest):
        outs, hs_ref = rest[:6], rest[6]
        _shift_into(hs_ref, h_ref, n_ctx, cb, D, False)
        hv = h_ref[...]
        xx = hs_ref[...] - hv
        for m in range(6):
            outs[m][...] = (hv + xx * mix_ref[m:m + 1, :]).astype(BF16)

    col = pl.BlockSpec((R, cb), lambda j: (0, j))
    return pl.pallas_call(
        body, name=name, grid=(D // cb,), in_specs=[col, pl.BlockSpec((mix.shape[0], cb), lambda j: (0, j))],
        out_specs=[col] * 6, out_shape=[jax.ShapeDtypeStruct((R, D), BF16)] * 6,
        scratch_shapes=[pltpu.VMEM((R, cb), F32)],
        compiler_params=_cparams(dimension_semantics=("parallel",)),
    )(h, mix)


def mix_bwd(name, h, mix, cots, slots, n_ctx):
    R, D = h.shape
    cb = LANES
    nc = len(cots)

    def body(h_ref, mix_ref, *rest):
        cot_refs, dh_ref, dmix_ref, hs_ref, dxx_ref = rest[:nc], rest[nc], rest[nc + 1], rest[nc + 2], rest[nc + 3]
        _shift_into(hs_ref, h_ref, n_ctx, cb, D, False)
        xx = hs_ref[...] - h_ref[...]
        per_slot = [None] * 6
        for cref, m in zip(cot_refs, slots):
            per_slot[m] = cref[...] if per_slot[m] is None else per_slot[m] + cref[...]
        dh = jnp.zeros((R, cb), F32)
        dxx = jnp.zeros((R, cb), F32)
        rows = []
        for m in range(6):
            d = per_slot[m]
            dh = dh + d
            dxx = dxx + d * mix_ref[m:m + 1, :]
            rows.append(jnp.sum(d * xx, axis=0, keepdims=True))
        dmix_ref[...] = jnp.concatenate(rows + [jnp.zeros((2, cb), F32)], axis=0)
        dxx_ref[...] = dxx
        _shift_into(hs_ref, dxx_ref, n_ctx, cb, D, True)
        dh_ref[...] = dh - dxx + hs_ref[...]

    col = pl.BlockSpec((R, cb), lambda j: (0, j))
    return pl.pallas_call(
        body, name=name, grid=(D // cb,), in_specs=[col, pl.BlockSpec((mix.shape[0], cb), lambda j: (0, j))] + [col] * nc,
        out_specs=[col, pl.BlockSpec((8, cb), lambda j: (0, j))],
        out_shape=[jax.ShapeDtypeStruct((R, D), F32), jax.ShapeDtypeStruct((8, D), F32)],
        scratch_shapes=[pltpu.VMEM((R, cb), F32), pltpu.VMEM((R, cb), F32)],
        compiler_params=_cparams(dimension_semantics=("parallel",)),
    )(h, mix, *cots)


def _conv_parts(gb_ref, gc_ref, u_ref, cw_ref):
    T = gb_ref.shape[0]
    z = gc_ref[...] * u_ref[...]
    zp, zn = _tshift(z, 0, _pow2_at_least(T)), _tshift(z, 1, _pow2_at_least(T))
    conv = zp * cw_ref[0:1, :] + z * cw_ref[1:2, :] + zn * cw_ref[2:3, :]
    return z, zp, zn, conv


def conv_fwd(name, gcu, cw):
    T, D3 = gcu.shape
    D = D3 // 3
    cb = LANES
    nb = D // cb

    def body(gb_ref, gc_ref, u_ref, cw_ref, o_ref):
        _, _, _, conv = _conv_parts(gb_ref, gc_ref, u_ref, cw_ref)
        o_ref[...] = (gb_ref[...] * conv).astype(BF16)

    def part(p):
        return pl.BlockSpec((T, cb), lambda j: (0, j + p * nb))

    return pl.pallas_call(
        body, name=name, grid=(nb,),
        in_specs=[part(0), part(1), part(2), pl.BlockSpec((cw.shape[0], cb), lambda j: (0, j))],
        out_specs=pl.BlockSpec((T, cb), lambda j: (0, j)), out_shape=jax.ShapeDtypeStruct((T, D), BF16),
        compiler_params=_cparams(dimension_semantics=("parallel",)),
    )(gcu, gcu, gcu, cw)


def conv_bwd(name, gcu, cw, dp):
    T, D3 = gcu.shape
    D = D3 // 3
    cb = LANES
    nb = D // cb

    def body(gb_ref, gc_ref, u_ref, cw_ref, dp_ref, o_ref, dcw_ref):
        part = pl.program_id(1)
        z, zp, zn, conv = _conv_parts(gb_ref, gc_ref, u_ref, cw_ref)
        dpv = dp_ref[...]
        dconv = dpv * gb_ref[...]
        period = _pow2_at_least(T)
        dz = (_tshift(dconv * cw_ref[0:1, :], 1, period) + dconv * cw_ref[1:2, :]
              + _tshift(dconv * cw_ref[2:3, :], 0, period))

        @pl.when(part == 0)
        def _():
            o_ref[...] = dpv * conv
            dcw_ref[...] = jnp.concatenate(
                [jnp.sum(dconv * s, axis=0, keepdims=True) for s in (zp, z, zn)] + [jnp.zeros((5, cb), F32)], axis=0)

        @pl.when(part == 1)
        def _():
            o_ref[...] = dz * u_ref[...]

        @pl.when(part == 2)
        def _():
            o_ref[...] = dz * gc_ref[...]

    def part_spec(p):
        return pl.BlockSpec((T, cb), lambda j, q: (0, j + p * nb))

    return pl.pallas_call(
        body, name=name, grid=(nb, 3),
        in_specs=[part_spec(0), part_spec(1), part_spec(2), pl.BlockSpec((cw.shape[0], cb), lambda j, q: (0, j)),
                  pl.BlockSpec((T, cb), lambda j, q: (0, j))],
        out_specs=[pl.BlockSpec((T, cb), lambda j, q: (0, j + q * nb)), pl.BlockSpec((8, cb), lambda j, q: (0, j))],
        out_shape=[jax.ShapeDtypeStruct((T, D3), F32), jax.ShapeDtypeStruct((8, D), F32)],
        compiler_params=_cparams(dimension_semantics=("arbitrary", "arbitrary")),
    )(gcu, gcu, gcu, cw, dp)


SCAN_TC = 8


def _scan_consts():
    rows = lax.broadcasted_iota(jnp.int32, (HEAD, LANES), 0)
    cols = lax.broadcasted_iota(jnp.int32, (HEAD, LANES), 1)
    eye = rows == (cols & (HEAD - 1))
    r2 = lax.broadcasted_iota(jnp.int32, (2 * LANES, LANES), 0)
    c2 = lax.broadcasted_iota(jnp.int32, (2 * LANES, LANES), 1)
    ones2 = (((r2 & (LANES - 1)) >= HEAD) == (c2 >= HEAD)).astype(BF16)
    return eye, ones2, ones2[:LANES]


SCAN_ROW_CHUNKS = 4


def _chunks_of_heads(hp):
    per = max(1, hp // SCAN_ROW_CHUNKS)
    return [range(lo, lo + per) for lo in range(0, hp, per)]


def _rows_of(heads):
    return pl.ds(heads[0] * HEAD, len(heads) * HEAD)


def _split2(p):
    hi = p.astype(BF16)
    lo = (p - hi.astype(F32)).astype(BF16)
    return jnp.concatenate([hi, lo], axis=1)


def _head_rows(h):
    return pl.ds(h * HEAD, HEAD)


def _ksum_into(dst, p_ref, ones2):
    dst[...] = jnp.dot(p_ref[...], ones2, preferred_element_type=F32)


def _expand_into(p_ref, row_ref, t, hp, eye):
    for h in range(hp):
        p_ref[_head_rows(h), :] = _split2(jnp.where(eye, row_ref[t, h:h + 1, :], 0.0))


def _colsum_store(ref, t, h, x):
    ref[t, pl.ds(h, 1), :] = jnp.sum(x, axis=0, keepdims=True)


def _order(i, n_ctx, n_all, rev):
    if not rev:
        return i
    return jnp.where(i < n_ctx, n_ctx - 1 - i, n_all - 1 - (i - n_ctx))


def scan_fwd(name, r, w, k, v, a, b, n_ctx_rows, rev):
    R, D = r.shape
    hp, tc = D // LANES, SCAN_TC
    n_all, n_ctx = R // tc, n_ctx_rows // tc
    ins = [t.reshape(R, hp, LANES) for t in (r, w, k, v, a, b)]

    def body(r_ref, w_ref, k_ref, v_ref, a_ref, b_ref, y_ref, st_ref, s_ref, ve_ref, sa_ref, p_ref, p1_ref):
        @pl.when(pl.program_id(0) == 0)
        def _():
            s_ref[...] = jnp.zeros_like(s_ref)

        eye, ones2, ones1 = _scan_consts()

        def row_of(q):
            return tc - 1 - q if rev else q

        for q in range(tc):
            _expand_into(p_ref.at[q % 2], v_ref, row_of(q), hp, eye)
            _ksum_into(ve_ref.at[q], p_ref.at[q % 2], ones2)

        def advance(q, prev_ref):
            t = row_of(q)
            for heads in _chunks_of_heads(hp):
                rows = _rows_of(heads)
                for h in heads:
                    p_ref[q % 2, _head_rows(h), :] = _split2(prev_ref[_head_rows(h), :] * a_ref[t, h:h + 1, :])
                sa_ref[q % 2, rows, :] = jnp.dot(p_ref[q % 2, rows, :], ones2, preferred_element_type=F32)
                for h in heads:
                    hr_ = _head_rows(h)
                    st_ref[q, hr_, :] = (prev_ref[hr_, :] * w_ref[t, h:h + 1, :]
                                         + sa_ref[q % 2, hr_, :] * b_ref[t, h:h + 1, :]
                                         + ve_ref[q, hr_, :] * k_ref[t, h:h + 1, :])

        advance(0, s_ref)
        for q in range(1, tc):
            advance(q, st_ref.at[q - 1])
        s_ref[...] = st_ref[tc - 1]

        for q in range(tc):
            t = row_of(q)
            for h in range(hp):
                p1_ref[q % 2, _head_rows(h), :] = (st_ref[q, _head_rows(h), :] * r_ref[t, h:h + 1, :]).astype(BF16)
            sa_ref[q % 2] = jnp.dot(p1_ref[q % 2], ones1, preferred_element_type=F32)
            for h in range(hp):
                _colsum_store(y_ref, t, h, jnp.where(eye, sa_ref[q % 2, _head_rows(h), :], 0.0))

    row_spec = pl.BlockSpec((tc, hp, LANES), lambda i: (_order(i, n_ctx, n_all, rev), 0, 0))
    n = hp * HEAD
    y, st = pl.pallas_call(
        body, name=name, grid=(n_all,), in_specs=[row_spec] * 6,
        out_specs=[row_spec, pl.BlockSpec((tc, n, LANES), lambda i: (i, 0, 0))],
        out_shape=[jax.ShapeDtypeStruct((R, hp, LANES), F32), jax.ShapeDtypeStruct((R, n, LANES), F32)],
        scratch_shapes=[pltpu.VMEM((n, LANES), F32), pltpu.VMEM((tc, n, LANES), F32), pltpu.VMEM((2, n, LANES), F32),
                        pltpu.VMEM((2, n, 2 * LANES), BF16), pltpu.VMEM((2, n, LANES), BF16)],
        compiler_params=_cparams(dimension_semantics=("arbitrary",)),
    )(*ins)
    return y.reshape(R, D), st


def scan_bwd(name, r, w, k, v, a, b, dy, st, n_ctx_rows, rev):
    R, D = r.shape
    hp, tc = D // LANES, SCAN_TC
    n_all, n_ctx = R // tc, n_ctx_rows // tc
    ins = [t.reshape(R, hp, LANES) for t in (r, w, k, v, a, b, dy)]

    def body(r_ref, w_ref, k_ref, v_ref, a_ref, b_ref, dy_ref, st_ref, prev_ref,
             dr_ref, dw_ref, dk_ref, dv_ref, da_ref, db_ref,
             g_ref, s0_ref, ve_ref, dye_ref, sa_ref, gs_ref, tmp_ref, p_ref, p1_ref):
        i = pl.program_id(0)

        @pl.when(i == 0)
        def _():
            g_ref[...] = jnp.zeros_like(g_ref)

        eye, ones2, ones1 = _scan_consts()

        @pl.when(i == n_all - 1)
        def _():
            s0_ref[...] = jnp.zeros_like(s0_ref)

        @pl.when(i != n_all - 1)
        def _():
            s0_ref[...] = prev_ref[0]

        def row_of(q):
            return tc - 1 - q if rev else q

        def prev_of(q):
            return s0_ref if q == 0 else st_ref.at[q - 1]

        for q in range(tc):
            t, prev = row_of(q), prev_of(q)
            _expand_into(p_ref.at[0], v_ref, t, hp, eye)
            _ksum_into(ve_ref.at[q], p_ref.at[0], ones2)
            _expand_into(p_ref.at[1], dy_ref, t, hp, eye)
            _ksum_into(dye_ref.at[q], p_ref.at[1], ones2)
            for h in range(hp):
                p1_ref[q % 2, _head_rows(h), :] = (prev[_head_rows(h), :] * a_ref[t, h:h + 1, :]).astype(BF16)
            sa_ref[q] = jnp.dot(p1_ref[q % 2], ones1, preferred_element_type=F32)

        for q in reversed(range(tc)):
            t, prev = row_of(q), prev_of(q)
            for heads in _chunks_of_heads(hp):
                rows = _rows_of(heads)
                for h in heads:
                    hr_ = _head_rows(h)
                    g = g_ref[hr_, :] + dye_ref[q, hr_, :] * r_ref[t, h:h + 1, :]
                    gs_ref[q, hr_, :] = g
                    p_ref[q % 2, hr_, :] = _split2(g * b_ref[t, h:h + 1, :])
                tmp_ref[q % 2, rows, :] = jnp.dot(p_ref[q % 2, rows, :], ones2, preferred_element_type=F32)
                for h in heads:
                    hr_ = _head_rows(h)
                    dsa = tmp_ref[q % 2, hr_, :]
                    _colsum_store(da_ref, t, h, prev[hr_, :] * dsa)
                    g_ref[hr_, :] = gs_ref[q, hr_, :] * w_ref[t, h:h + 1, :] + dsa * a_ref[t, h:h + 1, :]

        for q in range(tc):
            t, prev = row_of(q), prev_of(q)
            for h in range(hp):
                hr_ = _head_rows(h)
                g = gs_ref[q, hr_, :]
                p1_ref[q % 2, hr_, :] = (g * k_ref[t, h:h + 1, :]).astype(BF16)
                _colsum_store(dr_ref, t, h, st_ref[q, hr_, :] * dye_ref[q, hr_, :])
                _colsum_store(dk_ref, t, h, g * ve_ref[q, hr_, :])
                _colsum_store(dw_ref, t, h, g * prev[hr_, :])
                _colsum_store(db_ref, t, h, g * sa_ref[q, hr_, :])
            tmp_ref[q % 2] = jnp.dot(p1_ref[q % 2], ones1, preferred_element_type=F32)
            for h in range(hp):
                _colsum_store(dv_ref, t, h, jnp.where(eye, tmp_ref[q % 2, _head_rows(h), :], 0.0))

    def pos(i):
        return n_all - 1 - i

    n = hp * HEAD
    row_spec = pl.BlockSpec((tc, hp, LANES), lambda i: (_order(pos(i), n_ctx, n_all, rev), 0, 0))
    big = pltpu.VMEM((tc, n, LANES), F32)
    one = pltpu.VMEM((n, LANES), F32)
    outs = pl.pallas_call(
        body, name=name, grid=(n_all,),
        in_specs=[row_spec] * 7 + [
            pl.BlockSpec((tc, n, LANES), lambda i: (pos(i), 0, 0)),
            pl.BlockSpec((1, n, LANES), lambda i: (jnp.maximum(pos(i) * tc - 1, 0), 0, 0))],
        out_specs=[row_spec] * 6,
        out_shape=[jax.ShapeDtypeStruct((R, hp, LANES), F32)] * 6,
        scratch_shapes=[one, one, big, big, big, big, pltpu.VMEM((2, n, LANES), F32),
                        pltpu.VMEM((2, n, 2 * LANES), BF16), pltpu.VMEM((2, n, LANES), BF16)],
        compiler_params=_cparams(dimension_semantics=("arbitrary",)),
    )(*ins, st, st)
    return [o.reshape(R, D) for o in outs]


ANY = pl.BlockSpec(memory_space=pl.ANY)


def _peer(xi, yi, ci, k):
    return (1 - xi if k & 4 else xi, 1 - yi if k & 2 else yi, 1 - ci if k & 1 else ci)


def _rcopy(src, dst, send_sem, recv_sem, dev):
    return pltpu.make_async_remote_copy(src_ref=src, dst_ref=dst, send_sem=send_sem, recv_sem=recv_sem,
                                        device_id=dev, device_id_type=MESH)


def _drain(copies):
    for cp in copies:
        if cp.is_remote:
            cp.wait_send()
        else:
            cp.wait()


def all_gather8(name, x):
    r, c = x.shape

    def body(x_ref, out_ref, send_sems, recv_sems, local_sem):
        xi, yi, ci = _place()

        def blk(p):
            return out_ref.at[4 * p[0] + 2 * p[1] + p[2]]

        me = (xi, yi, ci)
        mine = pltpu.make_async_copy(x_ref, blk(me), local_sem.at[0])
        mine.start()
        sends = [_rcopy(x_ref, blk(me), send_sems.at[k - 1], recv_sems.at[k - 1], _peer(xi, yi, ci, k))
                 for k in range(1, N_DEV)]
        for cp in sends:
            cp.start()
        for k in range(1, N_DEV):
            p = _peer(xi, yi, ci, k)
            _rcopy(x_ref, blk(p), send_sems.at[k - 1], recv_sems.at[k - 1], p).wait_recv()
        for cp in sends:
            cp.wait_send()
        mine.wait()

    vm = pl.BlockSpec(memory_space=pltpu.VMEM)
    return pl.pallas_call(
        body, name=name, in_specs=[vm], out_specs=vm, out_shape=jax.ShapeDtypeStruct((N_DEV, r, c), x.dtype),
        scratch_shapes=[pltpu.SemaphoreType.DMA((N_DEV - 1,)), pltpu.SemaphoreType.DMA((N_DEV - 1,)),
                        pltpu.SemaphoreType.DMA((1,))],
        compiler_params=_cparams(),
    )(x)


LOCAL_COPY_CHUNKS = 16


def _row_chunks(rows, n):
    step = -(-rows // n)
    step = -(-step // 16) * 16
    return [(lo, min(step, rows - lo)) for lo in range(0, rows, step)]


def _chips(xi, yi):
    chips = [(1 - xi, yi), (xi, 1 - yi), (1 - xi, 1 - yi)]
    return chips, [2 * cx + cy for cx, cy in chips]


def gather_weights(name, shards):
    n = len(shards)

    def body(*refs):
        sh, out = refs[:n], refs[n:2 * n]
        send_sems, recv_sems, local_sems = refs[2 * n:]
        xi, yi, ci = _place()
        s = 2 * xi + yi
        chips, sidx = _chips(xi, yi)
        sib = (xi, yi, 1 - ci)
        started = []
        for w in range(n):
            hr = sh[w].shape[0] // 2
            for lo_row, rows in _row_chunks(sh[w].shape[0], LOCAL_COPY_CHUNKS):
                pltpu.make_async_copy(sh[w].at[pl.ds(lo_row, rows)], out[w].at[s, pl.ds(lo_row, rows)],
                                      local_sems.at[w]).start()
            started.append(pltpu.make_async_copy(sh[w], out[w].at[s], local_sems.at[w]))
            for j, (cx, cy) in enumerate(chips):
                cp = _rcopy(sh[w].at[pl.ds(ci * hr, hr)], out[w].at[s, pl.ds(ci * hr, hr)],
                            send_sems.at[w, j], recv_sems.at[w, j], (cx, cy, ci))
                cp.start()
                started.append(cp)
        for w in range(n):
            hr = sh[w].shape[0] // 2
            for j, (cx, cy) in enumerate(chips):
                blk = out[w].at[sidx[j], pl.ds(ci * hr, hr)]
                _rcopy(blk, blk, send_sems.at[w, j], recv_sems.at[w, j], (cx, cy, ci)).wait_recv()
                fw = _rcopy(blk, blk, send_sems.at[w, 3 + j], recv_sems.at[w, 3 + j], sib)
                fw.start()
                started.append(fw)
        for w in range(n):
            hr = sh[w].shape[0] // 2
            for j in range(3):
                blk = out[w].at[sidx[j], pl.ds((1 - ci) * hr, hr)]
                _rcopy(blk, blk, send_sems.at[w, 3 + j], recv_sems.at[w, 3 + j], sib).wait_recv()
        _drain(started)

    return pl.pallas_call(
        body, name=name, in_specs=[ANY] * n, out_specs=[ANY] * n,
        out_shape=[jax.ShapeDtypeStruct((N_CHIPS,) + a.shape, a.dtype) for a in shards],
        scratch_shapes=[pltpu.SemaphoreType.DMA((n, 6)), pltpu.SemaphoreType.DMA((n, 6)),
                        pltpu.SemaphoreType.DMA((n,))],
        compiler_params=_cparams(),
    )(*shards)


def rs_pair(name, grads):
    n = len(grads)

    def body(*refs):
        g, out = refs[:n], refs[n:2 * n]
        send_sems, recv_sems = refs[2 * n:]
        xi, yi, ci = _place()
        sib = (xi, yi, 1 - ci)
        cps = [_rcopy(g[w].at[:, 1 - ci], out[w], send_sems.at[w], recv_sems.at[w], sib) for w in range(n)]
        for cp in cps:
            cp.start()
        for cp in cps:
            cp.wait_recv()
        for cp in cps:
            cp.wait_send()

    return pl.pallas_call(
        body, name=name, in_specs=[ANY] * n, out_specs=[ANY] * n,
        out_shape=[jax.ShapeDtypeStruct((N_CHIPS,) + a.shape[2:], a.dtype) for a in grads],
        scratch_shapes=[pltpu.SemaphoreType.DMA((n,)), pltpu.SemaphoreType.DMA((n,))],
        compiler_params=_cparams(),
    )(*grads)


def _rows_tile(rows, cols, unit=16, limit=1 << 20):
    best = None
    for t in range(unit, rows + 1, unit):
        if rows % t == 0 and t * cols * 4 <= limit:
            best = t
    return best or rows


def rs_add_pair(name, g, got, ci):
    _, _, hr, c = g.shape
    th = _rows_tile(hr, c)

    def body(ci_ref, g_ref, r_ref, o32_ref, ob_ref):
        tot = g_ref[...] + r_ref[...]
        o32_ref[...] = tot
        ob_ref[...] = tot.astype(BF16)

    blk = pl.BlockSpec((None, th, c), lambda s, i, ci_ref: (s, i, 0))
    return pl.pallas_call(
        body, name=name,
        grid_spec=pltpu.PrefetchScalarGridSpec(
            num_scalar_prefetch=1, grid=(N_CHIPS, hr // th),
            in_specs=[pl.BlockSpec((None, None, th, c), lambda s, i, ci_ref: (s, ci_ref[0], i, 0)), blk],
            out_specs=[blk, blk]),
        out_shape=[jax.ShapeDtypeStruct((N_CHIPS, hr, c), F32), jax.ShapeDtypeStruct((N_CHIPS, hr, c), BF16)],
        compiler_params=_cparams(dimension_semantics=("parallel", "parallel")),
    )(ci, g, got)


def rs_chips(name, sums_bf16):
    n = len(sums_bf16)

    def body(*refs):
        pb, outs = refs[:n], refs[n:4 * n]
        send_sems, recv_sems = refs[4 * n:]
        xi, yi, ci = _place()
        chips, sidx = _chips(xi, yi)
        started = []
        for w in range(n):
            for j, (cx, cy) in enumerate(chips):
                cp = _rcopy(pb[w].at[sidx[j]], outs[3 * w + j], send_sems.at[w, j], recv_sems.at[w, j], (cx, cy, ci))
                cp.start()
                started.append(cp)
        for w in range(n):
            for j, (cx, cy) in enumerate(chips):
                _rcopy(pb[w].at[sidx[j]], outs[3 * w + j], send_sems.at[w, j], recv_sems.at[w, j],
                       (cx, cy, ci)).wait_recv()
        _drain(started)

    out_shape = []
    for a in sums_bf16:
        out_shape += [jax.ShapeDtypeStruct(a.shape[1:], BF16)] * 3
    res = pl.pallas_call(
        body, name=name, in_specs=[ANY] * n, out_specs=[ANY] * (3 * n), out_shape=out_shape,
        scratch_shapes=[pltpu.SemaphoreType.DMA((n, 3)), pltpu.SemaphoreType.DMA((n, 3))],
        compiler_params=_cparams(),
    )(*sums_bf16)
    return [res[3 * w:3 * w + 3] for w in range(n)]


def rs_swap(name, halves):
    n = len(halves)

    def body(*refs):
        hv, out = refs[:n], refs[n:2 * n]
        send_sems, recv_sems = refs[2 * n:]
        xi, yi, ci = _place()
        sib = (xi, yi, 1 - ci)
        cps = [_rcopy(hv[w], out[w], send_sems.at[w], recv_sems.at[w], sib) for w in range(n)]
        for cp in cps:
            cp.start()
        for cp in cps:
            cp.wait_recv()
        _drain(cps)

    return pl.pallas_call(
        body, name=name, in_specs=[ANY] * n, out_specs=[ANY] * n,
        out_shape=[jax.ShapeDtypeStruct(a.shape, a.dtype) for a in halves],
        scratch_shapes=[pltpu.SemaphoreType.DMA((n,)), pltpu.SemaphoreType.DMA((n,))],
        compiler_params=_cparams(),
    )(*halves)


def cast_bf16(name, x):
    r, c = x.shape
    tr = _rows_tile(r, c)

    def body(x_ref, o_ref):
        o_ref[...] = x_ref[...].astype(BF16)

    return pl.pallas_call(
        body, name=name, grid=(r // tr,), in_specs=[pl.BlockSpec((tr, c), lambda i: (i, 0))],
        out_specs=pl.BlockSpec((tr, c), lambda i: (i, 0)), out_shape=jax.ShapeDtypeStruct((r, c), BF16),
        compiler_params=_cparams(dimension_semantics=("parallel",)),
    )(x)


def sum_blocks(name, x, picks):
    _, r, c = x.shape

    def body(x_ref, o_ref):
        acc = x_ref[picks[0]]
        for b in picks[1:]:
            acc = acc + x_ref[b]
        o_ref[...] = acc

    vm = pl.BlockSpec(memory_space=pltpu.VMEM)
    return pl.pallas_call(body, name=name, in_specs=[vm], out_specs=vm, out_shape=jax.ShapeDtypeStruct((r, c), F32),
                          compiler_params=_cparams())(x)


def adamw(name, w, g, m, v):
    r, c = w.shape
    tr = _rows_tile(r, c, unit=8, limit=1 << 19)

    def body(w_ref, g_ref, m_ref, v_ref, d_ref, mo_ref, vo_ref):
        d_ref[...], mo_ref[...], vo_ref[...] = _adam_update(w_ref[...], g_ref[...], m_ref[...], v_ref[...])

    blk = pl.BlockSpec((tr, c), lambda i: (i, 0))
    return pl.pallas_call(
        body, name=name, grid=(r // tr,), in_specs=[blk] * 4, out_specs=[blk] * 3,
        out_shape=[jax.ShapeDtypeStruct((r, c), F32)] * 3,
        compiler_params=_cparams(dimension_semantics=("parallel",)),
    )(w, g, m, v)


def _adam_update(w, gv, m, v):
    c1 = 1.0 / (1.0 - ADAM_B1 ** ADAM_STEP)
    c2 = 1.0 / (1.0 - ADAM_B2 ** ADAM_STEP)
    mn = ADAM_B1 * m + (1.0 - ADAM_B1) * gv
    vn = ADAM_B2 * v + (1.0 - ADAM_B2) * (gv * gv)
    return -ADAM_LR * ((mn * c1) / (jnp.sqrt(vn * c2) + ADAM_EPS) + ADAM_WD * w), mn, vn


def adamw_halves(name, w, m, v, mine, theirs, ci):
    hr, c = mine[0].shape
    npos = len(mine)
    th = _rows_tile(hr, c, unit=8, limit=1 << 19)
    per = hr // th

    def body(ci_ref, w_ref, m_ref, v_ref, *rest):
        g_refs, (go_ref, d_ref, mo_ref, vo_ref) = rest[:2 * npos], rest[2 * npos:]
        pos, half = pl.program_id(0), pl.program_id(1)
        from_me = half == ci_ref[0]
        gv = jnp.where(from_me, g_refs[0][...], g_refs[npos][...])
        for p in range(1, npos):
            gv = jnp.where(pos == p, jnp.where(from_me, g_refs[p][...], g_refs[npos + p][...]), gv)
        d_ref[...], mo_ref[...], vo_ref[...] = _adam_update(w_ref[...], gv, m_ref[...], v_ref[...])
        go_ref[...] = gv

    full = pl.BlockSpec((th, c), lambda p, h, i, ci_ref: ((p * 2 + h) * per + i, 0))
    part = pl.BlockSpec((th, c), lambda p, h, i, ci_ref: (i, 0))
    rows = 2 * hr * npos
    return pl.pallas_call(
        body, name=name,
        grid_spec=pltpu.PrefetchScalarGridSpec(
            num_scalar_prefetch=1, grid=(npos, 2, per), in_specs=[full] * 3 + [part] * (2 * npos),
            out_specs=[full] * 4),
        out_shape=[jax.ShapeDtypeStruct((rows, c), F32)] * 4,
        compiler_params=_cparams(dimension_semantics=("arbitrary", "arbitrary", "arbitrary")),
    )(ci, w, m, v, *mine, *theirs)


def pack_rows(name, parts, rows):
    width = parts[0].shape[1]
    n = len(parts)

    def body(*refs):
        o_ref = refs[n]
        o_ref[...] = jnp.zeros_like(o_ref)
        off = 0
        for r in refs[:n]:
            o_ref[off:off + r.shape[0], :] = r[...]
            off += r.shape[0]

    vm = pl.BlockSpec(memory_space=pltpu.VMEM)
    return pl.pallas_call(body, name=name, in_specs=[vm] * n, out_specs=vm,
                          out_shape=jax.ShapeDtypeStruct((rows, width), F32), compiler_params=_cparams())(*parts)


def _pad_rows(a, rows):
    return jnp.pad(a, ((0, rows - a.shape[0]), (0, 0)))


def _view2d(name, a):
    if name == 'rw_rk' or a.ndim == 1:
        return a.reshape(1, -1)
    return a.reshape(-1, a.shape[-1])


def _reduce_scatter(tag, items, ci_arr, s):
    names = []
    for nm, _ in items:
        if nm not in names:
            names.append(nm)
    g4 = [g.reshape(N_CHIPS, 2, g.shape[1] // 2, g.shape[2]) for _, g in items]
    got = rs_pair("rs1_" + tag, g4)
    sums = [rs_add_pair(f"rs1add_{tag}{w}", g4[w], got[w], ci_arr) for w in range(len(items))]
    landed = rs_chips("rs2_" + tag, [s_[1] for s_ in sums])
    halves = []
    for w, (r0, r1, r2) in enumerate(landed):
        own = lax.dynamic_index_in_dim(sums[w][0], s, axis=0, keepdims=False)
        hr, c = own.shape
        halves.append(row_call(f"rs2add_{tag}{w}", f_sum, [own, r0, r1, r2], [], [(c, F32)],
                               tr=_rows_tile(hr, c), sr=16)[0])
    theirs = rs_swap("rs3_" + tag, halves)
    return {name: ([halves[w] for w, (nm, _) in enumerate(items) if nm == name],
                   [theirs[w] for w, (nm, _) in enumerate(items) if nm == name]) for name in names}


def _step(p):
    xi, yi, ci = _place()
    me = 4 * xi + 2 * yi + ci
    s = 2 * xi + yi
    ci_arr = jnp.reshape(ci, (1,)).astype(jnp.int32)
    x, ctx, tgt = p['x'][0], p['ctx'][0], p['loss_target'][0]
    T, D = x.shape
    L = ctx.shape[0]
    H, Dq = D // HEAD, D // N_CHIPS
    TR = math.gcd(math.gcd(L, T), 256)
    TS = min(TR, 64)
    nct = L // TR
    LG = p['rw_g1'].shape[-1]
    LW, LA = p['rw_w1'].shape[-1], p['rw_a1'].shape[-1]
    F4 = p['ffn_w2'].shape[1]

    pack = jnp.concatenate([
        _pad_rows(p['c'].reshape(N_CHIPS, Dq), 8), _pad_rows(p['rw_mix'][0], 8), _pad_rows(p['rw_w0'][0], 8),
        _pad_rows(p['rw_a0'][0], 8), _pad_rows(p['sc_conv'][0], 8)], axis=0)
    got = all_gather8("ag_small", pack)
    c_all = got[:, 0:N_CHIPS, :].reshape(N_DEV, D)
    full = jnp.transpose(got[::2], (1, 0, 2)).reshape(40, D)
    mix_f, w0_f, a0_f, conv_f = full[8:16], full[16:24], full[24:32], full[32:40]

    cond_in = jnp.concatenate([c_all, _pad_rows(p['c_ctx'].reshape(1, D), 8)], axis=0)
    cond = row_call("cond", f_silu, [cond_in], [], [(D, F32)], tr=16, sr=16)[0]
    ada = Stacked(p['ada_w'], "layer", D)
    modp = [mm_nn(f"modp{i}", cond, ada.at(i)) for i in range(2)]
    mg = all_gather8("ag_mod", jnp.concatenate(modp, axis=0))
    mod = jnp.transpose(mg[::2].reshape(N_CHIPS, 2, 16, 6 * Dq), (1, 2, 0, 3)).reshape(2, 16, 6 * D)
    mod = mod + p['ada_b'][:, None, :]
    mod_x = lax.dynamic_index_in_dim(mod, me, axis=1, keepdims=False)
    mod_c = mod[:, 8]

    def chunk(vec, j):
        return vec[j * D:(j + 1) * D].reshape(1, D)

    sh1x, sc1x, gt1x, sh2x, sc2x, gt2x = ([chunk(mod_x[i], j) for i in range(2)] for j in range(6))
    sh1c, sc1c = chunk(mod_c[0], 0), chunk(mod_c[0], 1)

    def gather(tag, names):
        shards = [cast_bf16("cast_" + n, _view2d(n, p[n])) for n in names]
        return dict(zip(names, gather_weights("ag_" + tag, shards)))

    gw = gather("rw", ['rw_wr', 'rw_wk', 'rw_wv', 'rw_wo', 'rw_w1', 'rw_w2', 'rw_a1', 'rw_a2', 'rw_g1', 'rw_g2'])
    gw.update(gather("sc", ['sc_win', 'sc_wout']))
    gw.update(gather("ffn", ['ffn_w13', 'ffn_w2']))
    Wr, Wk, Wv, Wo = (Stacked(gw[n], "row", Dq) for n in ('rw_wr', 'rw_wk', 'rw_wv', 'rw_wo'))
    W1, A1, G1 = (Stacked(gw[n], "row", Dq) for n in ('rw_w1', 'rw_a1', 'rw_g1'))
    W2, A2, G2 = Stacked(gw['rw_w2'], "col", LW), Stacked(gw['rw_a2'], "col", LA), Stacked(gw['rw_g2'], "col", LG)
    Win, Wout = Stacked(gw['sc_win'], "col", D), Stacked(gw['sc_wout'], "row", Dq)
    W13, W2f = Stacked(gw['ffn_w13'], "col", D), Stacked(gw['ffn_w2'], "row", F4)

    ones2 = (((lax.broadcasted_iota(jnp.int32, (2 * LANES, LANES), 0) & (LANES - 1)) >= HEAD)
             == (lax.broadcasted_iota(jnp.int32, (2 * LANES, LANES), 1) >= HEAD)).astype(BF16)
    n1g, n2g = p['norm1_g'], p['norm2_g']
    kkp, ka, lnw, lnb = p['rw_kk'], p['rw_ka'], p['rw_lnw'], p['rw_lnb']
    rk = p['rw_rk'].reshape(1, D)
    fg = p['final_g'].reshape(1, D)

    xin = jnp.concatenate([ctx, x], axis=0)
    nm = functools.partial(f_norm_mod, nct)
    nm_consts = [n1g[0:1], sh1c, sc1c, sh1x[0], sc1x[0]]
    h = row_call("l0_norm", nm, [xin], nm_consts, [(D, F32)], tr=TR, sr=16)[0]
    xr, xw, xk, xv, xa, xg = mix_fwd("l0_mix", h, mix_f, L)
    r = mm_nn("l0_r", xr, Wr)
    k = mm_nn("l0_k", xk, Wk)
    v = mm_nn("l0_v", xv, Wv)
    gl = mm_nn("l0_gl", xg, G1)
    sg = row_call("l0_sg", f_sigmoid, [gl], [], [(LG, BF16)], tr=TR, sr=16)[0]
    g = mm_nn("l0_g", sg, G2)
    wl, tw, zw, al, za = [], [], [], [], []
    for d in range(2):
        wl.append(mm_nn(f"l0_wl{d}", xw, W1.at(d)))
        tw.append(row_call(f"l0_tw{d}", f_tanh, [wl[d]], [], [(LW, BF16)], tr=TR, sr=16)[0])
        zw.append(mm_nn(f"l0_zw{d}", tw[d], W2.at(d)))
        al.append(mm_nn(f"l0_al{d}", xa, A1.at(d), BF16))
        za.append(mm_nn(f"l0_za{d}", al[d], A2.at(d)))
    post_rows = [k, zw[0], zw[1], za[0], za[1]]
    post_consts = [kkp, ka, w0_f[0:1], w0_f[1:2], a0_f[0:1], a0_f[1:2], ones2]
    aa, dec0, dec1, kd0, kd1, bb0, bb1, ksum = row_call(
        "l0_post", f_post_fwd, post_rows, post_consts, [(D, F32)] * 8, tr=TS, sr=16)
    dec, kd, bb = (dec0, dec1), (kd0, kd1), (bb0, bb1)
    ys, sts = [], []
    for d in range(2):
        y_d, st_d = scan_fwd(f"l0_scan{d}", r, dec[d], kd[d], v, aa, bb[d], L, bool(d))
        ys.append(y_d)
        sts.append(st_d)
    ro_rows = [ys[0], ys[1], r, ksum, v, g]
    ro_consts = [rk, lnw, lnb, ones2]
    og = row_call("l0_readout", f_readout, ro_rows, ro_consts, [(D, BF16)], tr=TS, sr=16)[0]
    yx = mm_nn("l0_o", og, Wo)
    res0_consts = [gt1x[0], n2g[0:1], sh2x[0], sc2x[0]]
    x1, h2 = row_call("l0_res", f_res_norm_mod, [x, yx], res0_consts, [(D, F32), (D, BF16)], tr=TR, sr=16,
                      offs=[0, nct])
    ab0 = mm_nn("l0_ffn13", h2, W13.at(0))
    sw0 = swiglu_fwd("l0_swiglu", ab0, TS)
    f0 = mm_nn("l0_ffn2", sw0, W2f.at(0))

    res1_consts = [gt2x[0], n1g[1:2], sh1x[1], sc1x[1]]
    x2, hb = row_call("l1_norm", f_res_norm_mod, [x1, f0], res1_consts, [(D, F32), (D, BF16)], tr=TR, sr=16)
    gcu = mm_nn("l1_win", hb, Win)
    pc = conv_fwd("l1_conv", gcu, conv_f)
    yx1 = mm_nn("l1_wout", pc, Wout)
    res2_consts = [gt1x[1], n2g[1:2], sh2x[1], sc2x[1]]
    x3, h2b = row_call("l1_res", f_res_norm_mod, [x2, yx1], res2_consts, [(D, F32), (D, BF16)], tr=TR, sr=16)
    ab1 = mm_nn("l1_ffn13", h2b, W13.at(1))
    sw1 = swiglu_fwd("l1_swiglu", ab1, TS)
    f1 = mm_nn("l1_ffn2", sw1, W2f.at(1))
    dx3, df1, dgt2_1, dfg, loss_blk = final_call("final", x3, f1, gt2x[1], fg, tgt, TR)
    loss = lax.psum(loss_blk[0, 0], ("x", "y", "c"))

    big = []
    dsw1 = mm_nt("b1_dsw", df1, W2f.at(1))
    gW2f1 = mm_tn("b1_gw2", sw1, df1, "row")
    dab1 = swiglu_bwd("b1_swiglu", ab1, dsw1, TS)
    dh2b = mm_nt("b1_dh2", dab1, W13.at(1))
    gW13_1 = mm_tn("b1_gw13", h2b, dab1, "col")
    rm = [True, True]
    cm = [True] * 4
    (dx2, dyx1), (dgt1_1, dn2g1, dsh2_1, dsc2_1) = row_vjp(
        "b1_res", f_res_norm_mod, [x2, yx1], res2_consts, [dx3, dh2b], row_mask=rm, const_mask=cm, tr=TR, sr=16)
    dpc = mm_nt("b1_dpc", dyx1, Wout)
    big.append(('sc_wout', mm_tn("b1_gwout", pc, dyx1, "row")))
    dgcu, dconv = conv_bwd("b1_conv", gcu, conv_f, dpc)
    dhb = mm_nt("b1_dhb", dgcu, Win)
    big.append(('sc_win', mm_tn("b1_gwin", hb, dgcu, "col")))
    (dx1, df0), (dgt2_0, dn1g1, dsh1_1, dsc1_1) = row_vjp(
        "b1_norm", f_res_norm_mod, [x1, f0], res1_consts, [dx2, dhb], row_mask=rm, const_mask=cm, tr=TR, sr=16)

    dsw0 = mm_nt("b0_dsw", df0, W2f.at(0))
    gW2f0 = mm_tn("b0_gw2", sw0, df0, "row")
    dab0 = swiglu_bwd("b0_swiglu", ab0, dsw0, TS)
    dh2 = mm_nt("b0_dh2", dab0, W13.at(0))
    gW13_0 = mm_tn("b0_gw13", h2, dab0, "col")
    (dx_a, dyx), (dgt1_0, dn2g0, dsh2_0, dsc2_0) = row_vjp(
        "b0_res", f_res_norm_mod, [x, yx], res0_consts, [dx1, dh2], row_mask=rm, const_mask=cm, tr=TR, sr=16,
        offs=[0, nct])
    dyx_all = jnp.concatenate([jnp.zeros((L, D), F32), dyx], axis=0)
    dog = mm_nt("b0_dog", dyx_all, Wo)
    gWo = mm_tn("b0_gwo", og, dyx_all, "row")
    (dy, dr_ro, dksum, dv_ro, dg), (drk, dlnw, dlnb) = row_vjp(
        "b0_readout", f_readout, ro_rows, ro_consts, [dog], row_mask=[True, False, True, True, True, True],
        const_mask=[True, True, True, False], tr=TS, sr=16)
    sg_ = [scan_bwd(f"b0_scan{d}", r, dec[d], kd[d], v, aa, bb[d], dy, sts[d], L, bool(d)) for d in range(2)]
    (dr0, ddec0, dkd0, dv0, daa0, dbb0), (dr1, ddec1, dkd1, dv1, daa1, dbb1) = sg_
    post_cots = [daa0, daa1, ddec0, ddec1, dkd0, dkd1, dbb0, dbb1, dksum]
    (dk, dzw0, dzw1, dza0, dza1), (dkkp, dka, dw00, dw01, da00, da01) = row_vjp(
        "b0_post", f_post, post_rows, post_consts, post_cots, row_mask=[True] * 5,
        const_mask=[True] * 6 + [False], tr=TS, sr=16)
    dzw, dza = (dzw0, dzw1), (dza0, dza1)
    dr_t = sum_cast("b0_drsum", [dr0, dr1, dr_ro], BF16, TR)
    dv_t = sum_cast("b0_dvsum", [dv0, dv1, dv_ro], BF16, TR)
    mix_cots, mix_slots = [], []

    def back(tag, cot, w, xin_m, kind, slot):
        mix_cots.append(mm_nt("b0_dx" + tag, cot, w))
        mix_slots.append(slot)
        return mm_tn("b0_gw" + tag, xin_m, cot, kind)

    big.append(('rw_wr', back("r", dr_t, Wr, xr, "row", 0)))
    big.append(('rw_wk', back("k", dk, Wk, xk, "row", 2)))
    big.append(('rw_wv', back("v", dv_t, Wv, xv, "row", 3)))
    big.append(('rw_wo', gWo))
    dsg = mm_nt("b0_dsg", dg, G2)
    gG2 = mm_tn("b0_gg2", sg, dg, "col")
    (dgl,), _ = row_vjp("b0_sg", f_sigmoid, [gl], [], [dsg], row_mask=[True], const_mask=[], tr=TR, sr=16)
    gG1 = back("g", dgl, G1, xg, "row", 5)
    gW1, gW2, gA1, gA2 = [], [], [], []
    for d in range(2):
        dtw = mm_nt(f"b0_dtw{d}", dzw[d], W2.at(d))
        gW2.append(mm_tn(f"b0_gw2{d}", tw[d], dzw[d], "col"))
        (dwl,), _ = row_vjp(f"b0_tw{d}", f_tanh, [wl[d]], [], [dtw], row_mask=[True], const_mask=[], tr=TR, sr=16)
        gW1.append(back(f"w{d}", dwl, W1.at(d), xw, "row", 1))
        dal = mm_nt(f"b0_dal{d}", dza[d], A2.at(d))
        gA2.append(mm_tn(f"b0_ga2{d}", al[d], dza[d], "col"))
        gA1.append(back(f"a{d}", dal, A1.at(d), xa, "row", 4))
    big += [('rw_w1', gW1[0]), ('rw_w1', gW1[1]), ('rw_w2', gW2[0]), ('rw_w2', gW2[1]),
            ('rw_a1', gA1[0]), ('rw_a1', gA1[1]), ('rw_a2', gA2[0]), ('rw_a2', gA2[1]),
            ('rw_g1', gG1), ('rw_g2', gG2)]
    dh, dmix = mix_bwd("b0_mix", h, mix_f, mix_cots, mix_slots, L)
    (dxin,), (dn1g0, dsh1c, dsc1c, dsh1x, dsc1x) = row_vjp(
        "b0_norm", nm, [xin], nm_consts, [dh], row_mask=[True], const_mask=[True] * 5, tr=TR, sr=16)
    grad_x = sum_cast("b0_dx", [dxin, dx_a], F32, TR, offs=[nct, 0])

    zero = jnp.zeros((1, D), F32)
    parts = [dsh1x, dsc1x, dgt1_0, dsh2_0, dsc2_0, dgt2_0, dsh1c, dsc1c, zero, zero, zero, zero,
             dsh1_1, dsc1_1, dgt1_1, dsh2_1, dsc2_1, dgt2_1, zero, zero, zero, zero, zero, zero,
             dn1g0, dn1g1, dn2g0, dn2g1, dkkp, dka, drk, dlnw, dlnb, dfg,
             dmix[0:6], dw00, dw01, da00, da01, dconv[0:3]]
    got2 = all_gather8("ag_grads", pack_rows("pack_grads", parts, 48))
    small = sum_blocks("sum_grads", got2, list(range(N_DEV)))
    per_ex = got2[:, 0:24].reshape(N_DEV, 2, 2, 6 * D)
    tot = small[0:24].reshape(2, 2, 6 * D)
    cols = lambda a, width: lax.dynamic_slice_in_dim(a, s * width, width, axis=1)
    g_ada_w, dcond_parts = [], []
    for i in range(2):
        dm16 = cols(jnp.concatenate([per_ex[:, i, 0], _pad_rows(tot[i, 1][None], 8)], axis=0), 6 * Dq)
        g_ada_w.append(mm_tn(f"g_ada{i}", cond, dm16))
        dcond_parts.append(mm_nt(f"dcond{i}", dm16, ada.at(i)))
    g_ada_b = sum_cast("g_adab", [_pad_rows(tot[:, 0].reshape(12, D), 16), _pad_rows(tot[:, 1].reshape(12, D), 16)],
                       F32, 16)[0:12].reshape(2, 6 * D)
    dcond_mine = sum_cast("dcond_sum", dcond_parts, F32, 16)
    dcond = sum_blocks("dcond_chips", all_gather8("ag_dcond", dcond_mine), [0, 2, 4, 6])
    (dcin,), _ = row_vjp("b_cond", f_silu, [cond_in], [], [dcond], row_mask=[True], const_mask=[], tr=16, sr=16)

    rw_names = ('rw_wr', 'rw_wk', 'rw_wv', 'rw_wo', 'rw_w1', 'rw_w2', 'rw_a1', 'rw_a2', 'rw_g1', 'rw_g2')
    gsh = _reduce_scatter("rw", [it for it in big if it[0] in rw_names], ci_arr, s)
    gsh.update(_reduce_scatter("sc", [it for it in big if it[0] in ('sc_win', 'sc_wout')], ci_arr, s))
    gsh.update(_reduce_scatter("ffn", [('ffn_w13', gW13_0), ('ffn_w13', gW13_1), ('ffn_w2', gW2f0),
                                       ('ffn_w2', gW2f1)], ci_arr, s))

    grads = {}
    grads['c_ctx'] = dcin[8]
    grads['norm1_g'], grads['norm2_g'] = small[24:26], small[26:28]
    grads['ada_w'] = jnp.stack(g_ada_w)
    grads['ada_b'] = g_ada_b
    grads['rw_kk'], grads['rw_ka'], grads['rw_rk'] = small[28:29], small[29:30], small[30:31]
    grads['rw_lnw'], grads['rw_lnb'], grads['final_g'] = small[31:32], small[32:33], small[33]
    sharded = cols(small[34:48], Dq)
    grads['rw_mix'], grads['rw_w0'], grads['rw_a0'], grads['sc_conv'] = (
        sharded[0:6], sharded[6:8], sharded[8:10], sharded[10:13])

    outs_g, outs_d, outs_m, outs_v = [], [], [], []
    for n in WEIGHTS:
        shape = p[n].shape
        w2d, m2d, v2d = _view2d(n, p[n]), _view2d(n, p['m_' + n]), _view2d(n, p['v_' + n])
        if n in gsh:
            g2d, d_, m_, v_ = adamw_halves("adam_" + n, w2d, m2d, v2d, gsh[n][0], gsh[n][1], ci_arr)
        else:
            g2d = _view2d(n, grads[n].reshape(shape))
            d_, m_, v_ = adamw("adam_" + n, w2d, g2d, m2d, v2d)
        outs_g.append(g2d.reshape(shape))
        outs_d.append(d_.reshape(shape))
        outs_m.append(m_.reshape(shape))
        outs_v.append(v_.reshape(shape))
    return (loss, grad_x.reshape(1, T, D), *outs_g, *outs_d, *outs_m, *outs_v)


def kernel(x, c, ctx, c_ctx, norm1_g, norm2_g, ada_w, ada_b, rw_mix, rw_wr, rw_wk, rw_wv, rw_wo, rw_w0, rw_w1, rw_w2, rw_a0, rw_a1, rw_a2, rw_g1, rw_g2, rw_kk, rw_ka, rw_rk, rw_lnw, rw_lnb, sc_win, sc_conv, sc_wout, ffn_w13, ffn_w2, final_g, loss_target, m_c_ctx, m_norm1_g, m_norm2_g, m_ada_w, m_ada_b, m_rw_mix, m_rw_wr, m_rw_wk, m_rw_wv, m_rw_wo, m_rw_w0, m_rw_w1, m_rw_w2, m_rw_a0, m_rw_a1, m_rw_a2, m_rw_g1, m_rw_g2, m_rw_kk, m_rw_ka, m_rw_rk, m_rw_lnw, m_rw_lnb, m_sc_win, m_sc_conv, m_sc_wout, m_ffn_w13, m_ffn_w2, m_final_g, v_c_ctx, v_norm1_g, v_norm2_g, v_ada_w, v_ada_b, v_rw_mix, v_rw_wr, v_rw_wk, v_rw_wv, v_rw_wo, v_rw_w0, v_rw_w1, v_rw_w2, v_rw_a0, v_rw_a1, v_rw_a2, v_rw_g1, v_rw_g2, v_rw_kk, v_rw_ka, v_rw_rk, v_rw_lnw, v_rw_lnb, v_sc_win, v_sc_conv, v_sc_wout, v_ffn_w13, v_ffn_w2, v_final_g):
    values = (x, c, ctx, c_ctx, norm1_g, norm2_g, ada_w, ada_b, rw_mix, rw_wr, rw_wk, rw_wv, rw_wo, rw_w0, rw_w1, rw_w2, rw_a0, rw_a1, rw_a2, rw_g1, rw_g2, rw_kk, rw_ka, rw_rk, rw_lnw, rw_lnb, sc_win, sc_conv, sc_wout, ffn_w13, ffn_w2, final_g, loss_target, m_c_ctx, m_norm1_g, m_norm2_g, m_ada_w, m_ada_b, m_rw_mix, m_rw_wr, m_rw_wk, m_rw_wv, m_rw_wo, m_rw_w0, m_rw_w1, m_rw_w2, m_rw_a0, m_rw_a1, m_rw_a2, m_rw_g1, m_rw_g2, m_rw_kk, m_rw_ka, m_rw_rk, m_rw_lnw, m_rw_lnb, m_sc_win, m_sc_conv, m_sc_wout, m_ffn_w13, m_ffn_w2, m_final_g, v_c_ctx, v_norm1_g, v_norm2_g, v_ada_w, v_ada_b, v_rw_mix, v_rw_wr, v_rw_wk, v_rw_wv, v_rw_wo, v_rw_w0, v_rw_w1, v_rw_w2, v_rw_a0, v_rw_a1, v_rw_a2, v_rw_g1, v_rw_g2, v_rw_kk, v_rw_ka, v_rw_rk, v_rw_lnw, v_rw_lnb, v_sc_win, v_sc_conv, v_sc_wout, v_ffn_w13, v_ffn_w2, v_final_g)
    return _step(dict(zip(INPUTS, values)))
```

```python
import functools
import math

import jax
import jax.numpy as jnp
from jax import lax
from jax.experimental import pallas as pl
from jax.experimental.pallas import tpu as pltpu

F32 = jnp.float32
BF16 = jnp.bfloat16
MESH = pl.DeviceIdType.MESH

GRID_W = 64
HEAD = 64
LANES = 128
N_CHIPS = 4
N_DEV = 8
NORM_EPS = 1e-6
GN_EPS = 64e-5
ADAM_LR, ADAM_B1, ADAM_B2, ADAM_EPS, ADAM_WD, ADAM_STEP = 0.001, 0.9, 0.999, 1e-08, 0.01, 10
VMEM_LIMIT = 56 * 1024 * 1024
HI = lax.Precision.HIGHEST
WEIGHTS = ['c_ctx', 'norm1_g', 'norm2_g', 'ada_w', 'ada_b', 'rw_mix', 'rw_wr', 'rw_wk', 'rw_wv', 'rw_wo', 'rw_w0',
           'rw_w1', 'rw_w2', 'rw_a0', 'rw_a1', 'rw_a2', 'rw_g1', 'rw_g2', 'rw_kk', 'rw_ka', 'rw_rk', 'rw_lnw',
           'rw_lnb', 'sc_win', 'sc_conv', 'sc_wout', 'ffn_w13', 'ffn_w2', 'final_g']
INPUTS = (['x', 'c', 'ctx'] + WEIGHTS + ['loss_target'] + ['m_' + w for w in WEIGHTS]
          + ['v_' + w for w in WEIGHTS])


def _cparams(**kw):
    return pltpu.CompilerParams(vmem_limit_bytes=VMEM_LIMIT, **kw)


def _pick(dim, cands):
    for c in cands:
        if dim % c == 0:
            return c
    return dim


def _place():
    return lax.axis_index("x"), lax.axis_index("y"), lax.axis_index("c")


_TILE_M = (1024, 768, 512, 1408, 256, 128)
_TILE_N = (1408, 1024, 768, 512, 256, 128)
_TILE_K = (512, 1408, 256, 128)


class Stacked:
    def __init__(self, arr, kind, r, layer=0):
        self.arr, self.kind, self.r, self.layer = arr, kind, r, layer
        self.c = arr.shape[2]
        self.shape = {"row": (N_CHIPS * r, self.c), "col": (r, N_CHIPS * self.c), "layer": (r, self.c)}[kind]

    def at(self, layer):
        return Stacked(self.arr, self.kind, self.r, layer)

    def spec(self, t0, t1, swap):
        r, c, layer = self.r, self.c, self.layer
        per_r, per_c = r // t0, c // t1
        assert r % t0 == 0 and c % t1 == 0
        kind = self.kind

        def index(i, j, k):
            ri, ci = (j, k) if swap else (k, j)
            if kind == "row":
                return (ri // per_r, layer * per_r + ri % per_r, ci)
            if kind == "layer":
                return (layer, ri, ci)
            return (ci // per_c, layer * per_r + ri, ci % per_c)

        return pl.BlockSpec((None, t0, t1), index)


def _mm_body(dims, nk, a_ref, b_ref, o_ref, acc_ref):
    k = pl.program_id(2)

    @pl.when(k == 0)
    def _():
        acc_ref[...] = jnp.zeros_like(acc_ref)

    acc_ref[...] += lax.dot_general(a_ref[...].astype(BF16), b_ref[...].astype(BF16), (dims, ((), ())),
                                    preferred_element_type=F32)

    @pl.when(k == nk - 1)
    def _():
        o_ref[...] = acc_ref[...].astype(o_ref.dtype)


def _mm_call(name, dims, grid, in_specs, out_spec, out_shape, acc_shape, operands):
    return pl.pallas_call(
        functools.partial(_mm_body, dims, grid[2]), name=name, grid=grid, in_specs=in_specs, out_specs=out_spec,
        out_shape=out_shape, scratch_shapes=[pltpu.VMEM(acc_shape, F32)],
        compiler_params=_cparams(dimension_semantics=("parallel", "parallel", "arbitrary")),
    )(*operands)


def mm_nn(name, a, b, out_dtype=F32):
    M, K = a.shape
    st = isinstance(b, Stacked)
    N = b.shape[1]
    tm = _pick(M, _TILE_M)
    tn = _pick(b.c if st and b.kind == "col" else N, _TILE_N)
    tk = _pick(b.r if st else K, _TILE_K)
    b_spec = b.spec(tk, tn, False) if st else pl.BlockSpec((tk, tn), lambda i, j, k: (k, j))
    return _mm_call(name, ((1,), (0,)), (M // tm, N // tn, K // tk),
                    [pl.BlockSpec((tm, tk), lambda i, j, k: (i, k)), b_spec],
                    pl.BlockSpec((tm, tn), lambda i, j, k: (i, j)), jax.ShapeDtypeStruct((M, N), out_dtype),
                    (tm, tn), (a, b.arr if st else b))


def mm_nt(name, a, b, out_dtype=F32):
    M, N = a.shape
    st = isinstance(b, Stacked)
    K = b.shape[0]
    tm = _pick(M, _TILE_M)
    to = _pick(b.r if st else K, _TILE_N)
    tc = _pick(b.c if st and b.kind == "col" else N, _TILE_K)
    b_spec = b.spec(to, tc, True) if st else pl.BlockSpec((to, tc), lambda i, j, k: (j, k))
    return _mm_call(name, ((1,), (1,)), (M // tm, K // to, N // tc),
                    [pl.BlockSpec((tm, tc), lambda i, j, k: (i, k)), b_spec],
                    pl.BlockSpec((tm, to), lambda i, j, k: (i, j)), jax.ShapeDtypeStruct((M, K), out_dtype),
                    (tm, to), (a, b.arr if st else b))


def mm_tn(name, a, b, kind=None):
    R, M = a.shape
    N = b.shape[1]
    r, c = (M // N_CHIPS, N) if kind == "row" else (M, N // N_CHIPS) if kind == "col" else (M, N)
    tm, tn, tk = _pick(r, _TILE_M), _pick(c, _TILE_N), _pick(R, (1024, 768, 512, 256, 128))
    if kind:
        per_r, per_c = r // tm, c // tn
        if kind == "row":
            o_spec = pl.BlockSpec((None, tm, tn), lambda i, j, k: (i // per_r, i % per_r, j))
        else:
            o_spec = pl.BlockSpec((None, tm, tn), lambda i, j, k: (j // per_c, i, j % per_c))
        o_shape = jax.ShapeDtypeStruct((N_CHIPS, r, c), F32)
    else:
        o_spec = pl.BlockSpec((tm, tn), lambda i, j, k: (i, j))
        o_shape = jax.ShapeDtypeStruct((M, N), F32)
    return _mm_call(name, ((0,), (0,)), (M // tm, N // tn, R // tk),
                    [pl.BlockSpec((tk, tm), lambda i, j, k: (k, i)), pl.BlockSpec((tk, tn), lambda i, j, k: (k, j))],
                    o_spec, o_shape, (tm, tn), (a, b))


def _shifted(o):
    return lambda i: (i + o, 0)


def row_call(name, f, rows, consts, outs, *, tr, sr, offs=None):
    offs = offs or [0] * len(rows)
    n_rows = min(r.shape[0] - o * tr for r, o in zip(rows, offs))
    nr, nc = len(rows), len(consts)

    def body(*refs):
        row_refs, const_refs, out_refs = refs[:nr], refs[nr:nr + nc], refs[nr + nc:]
        i = pl.program_id(0)
        cvals = [r[...] for r in const_refs]

        def step(j, carry):
            sl = pl.ds(pl.multiple_of(j * sr, sr), sr)
            res = f(i, *[r[sl, :] for r in row_refs], *cvals)
            for o, v in zip(out_refs, res):
                o[sl, :] = v.astype(o.dtype)
            return carry

        lax.fori_loop(0, tr // sr, step, 0)

    in_specs = [pl.BlockSpec((tr, r.shape[1]), _shifted(o)) for r, o in zip(rows, offs)]
    in_specs += [pl.BlockSpec(c.shape, lambda i: (0, 0)) for c in consts]
    return pl.pallas_call(
        body, name=name, grid=(n_rows // tr,), in_specs=in_specs,
        out_specs=[pl.BlockSpec((tr, w), lambda i: (i, 0)) for w, _ in outs],
        out_shape=[jax.ShapeDtypeStruct((n_rows, w), dt) for w, dt in outs],
        compiler_params=_cparams(dimension_semantics=("parallel",)),
    )(*rows, *consts)


def row_vjp(name, f, rows, consts, cots, *, row_mask, const_mask, tr, sr, offs=None):
    offs = offs or [0] * len(rows)
    n_rows = min(r.shape[0] - o * tr for r, o in zip(rows, offs))
    nr, nc = len(rows), len(consts)
    cot_in = [c for c in cots if c is not None]
    nct = len(cot_in)
    d_rows = [i for i in range(nr) if row_mask[i]]
    d_consts = [i for i in range(nc) if const_mask[i]]

    def body(*refs):
        row_refs, const_refs = refs[:nr], refs[nr:nr + nc]
        cot_refs = refs[nr + nc:nr + nc + nct]
        drow_refs = refs[nr + nc + nct:nr + nc + nct + len(d_rows)]
        dconst_refs = refs[nr + nc + nct + len(d_rows):]
        i = pl.program_id(0)

        @pl.when(i == 0)
        def _():
            for r in dconst_refs:
                r[...] = jnp.zeros_like(r)

        cvals = [r[...] for r in const_refs]

        def step(j, carry):
            sl = pl.ds(pl.multiple_of(j * sr, sr), sr)
            rvals = [r[sl, :] for r in row_refs]

            def g(*diff):
                rv, cv = list(rvals), list(cvals)
                for idx, val in zip(d_rows, diff[:len(d_rows)]):
                    rv[idx] = val
                for idx, val in zip(d_consts, diff[len(d_rows):]):
                    cv[idx] = val
                return f(i, *rv, *cv)

            primals = [rvals[idx].astype(F32) for idx in d_rows] + [cvals[idx] for idx in d_consts]
            res, vjp = jax.vjp(g, *primals)
            it = iter(cot_refs)
            cts = tuple(jnp.zeros_like(o) if c is None else next(it)[sl, :].astype(o.dtype) for o, c in zip(res, cots))
            grads = vjp(cts)
            for r, val in zip(drow_refs, grads[:len(d_rows)]):
                r[sl, :] = val.astype(r.dtype)
            for r, val in zip(dconst_refs, grads[len(d_rows):]):
                r[...] += val
            return carry

        lax.fori_loop(0, tr // sr, step, 0)

    in_specs = [pl.BlockSpec((tr, r.shape[1]), _shifted(o)) for r, o in zip(rows, offs)]
    in_specs += [pl.BlockSpec(c.shape, lambda i: (0, 0)) for c in consts]
    in_specs += [pl.BlockSpec((tr, c.shape[1]), lambda i: (i, 0)) for c in cot_in]
    out_specs = [pl.BlockSpec((tr, rows[i].shape[1]), lambda i: (i, 0)) for i in d_rows]
    out_specs += [pl.BlockSpec(consts[i].shape, lambda i: (0, 0)) for i in d_consts]
    out_shape = [jax.ShapeDtypeStruct((n_rows, rows[i].shape[1]), F32) for i in d_rows]
    out_shape += [jax.ShapeDtypeStruct(consts[i].shape, F32) for i in d_consts]
    res = pl.pallas_call(
        body, name=name, grid=(n_rows // tr,), in_specs=in_specs, out_specs=out_specs, out_shape=out_shape,
        compiler_params=_cparams(dimension_semantics=("arbitrary",)),
    )(*rows, *consts, *cot_in)
    return list(res[:len(d_rows)]), list(res[len(d_rows):])


def _sigmoid(x):
    return 1.0 / (1.0 + jnp.exp(-x))


def _softplus(u):
    return jnp.maximum(u, 0.0) + jnp.log(1.0 + jnp.exp(-jnp.abs(u)))


def _rms(x, g):
    ms = jnp.sum(x * x, axis=-1, keepdims=True) * (1.0 / x.shape[-1])
    return x * lax.rsqrt(ms + NORM_EPS) * g


def _hsum_impl(x, ones2):
    rows, width = x.shape
    nch = width // LANES
    xs = jnp.concatenate([x[:, j * LANES:(j + 1) * LANES] for j in range(nch)], axis=0)
    hi = xs.astype(BF16)
    lo = (xs - hi.astype(F32)).astype(BF16)
    ys = jnp.dot(jnp.concatenate([hi, lo], axis=1), ones2, preferred_element_type=F32)
    return jnp.concatenate([ys[j * rows:(j + 1) * rows] for j in range(nch)], axis=1)


@jax.custom_vjp
def _hsum(x, ones2):
    return _hsum_impl(x, ones2)


def _hsum_fwd(x, ones2):
    return _hsum_impl(x, ones2), ones2


def _hsum_bwd(ones2, g):
    return _hsum_impl(g, ones2), jnp.zeros_like(ones2)


_hsum.defvjp(_hsum_fwd, _hsum_bwd)


def f_silu(i, x):
    return (x * _sigmoid(x),)


def f_sigmoid(i, x):
    return (_sigmoid(x),)


def f_tanh(i, x):
    return (jnp.tanh(x),)


def f_norm_mod(n_ctx_tiles, i, xin, g, sh_c, sc_c, sh_x, sc_x):
    is_x = i >= n_ctx_tiles
    sh = jnp.where(is_x, sh_x, sh_c)
    sc = jnp.where(is_x, sc_x, sc_c)
    return (_rms(xin, g) * (1.0 + sc) + sh,)


def f_res_norm_mod(i, x, y, gt, g, sh, sc):
    x1 = x + gt * y
    return x1, _rms(x1, g) * (1.0 + sc) + sh


def f_post(i, k, zw0, zw1, za0, za1, kkp, ka, w00, w01, a00, a01, ones2):
    kq = k * kkp
    kk = kq / jnp.maximum(jnp.sqrt(_hsum(kq * kq, ones2)), 1e-12)

    def direction(zw, za, w0, a0):
        log_w = -_softplus(-(w0 + zw)) - 0.5
        a = _sigmoid(a0 + za)
        return jnp.exp(-jnp.exp(log_w)), k * (1.0 + (a - 1.0) * ka), kk * a

    dec0, kd0, bb0 = direction(zw0, za0, w00, a00)
    dec1, kd1, bb1 = direction(zw1, za1, w01, a01)
    return -kk, -kk, dec0, dec1, kd0, kd1, bb0, bb1, kd0 + kd1


def f_post_fwd(*a):
    return f_post(*a)[1:]


def f_readout(i, y0, y1, r, ksum, v, g, rk, lnw, lnb, ones2):
    y = y0 + y1
    yc = y - _hsum(y, ones2) * (1.0 / HEAD)
    var = _hsum(yc * yc, ones2) * (1.0 / HEAD)
    o = yc * lax.rsqrt(var + GN_EPS) * lnw + lnb
    o = o + _hsum(r * ksum * rk, ones2) * v
    return (o * g,)


def f_sum(i, *xs):
    acc = xs[0].astype(F32)
    for x in xs[1:]:
        acc = acc + x.astype(F32)
    return (acc,)


def sum_cast(name, arrs, dtype, tr, offs=None):
    return row_call(name, f_sum, arrs, [], [(arrs[0].shape[1], dtype)], tr=tr, sr=16, offs=offs)[0]


def swiglu_fwd(name, ab, tr):
    T, F2 = ab.shape
    F = F2 // 2
    sr = 16

    def body(ab_ref, o_ref):
        def step(j, carry):
            sl = pl.ds(pl.multiple_of(j * sr, sr), sr)
            a, b = ab_ref[sl, :F], ab_ref[sl, F:]
            o_ref[sl, :] = (a * _sigmoid(a) * b).astype(o_ref.dtype)
            return carry

        lax.fori_loop(0, tr // sr, step, 0)

    return pl.pallas_call(
        body, name=name, grid=(T // tr,), in_specs=[pl.BlockSpec((tr, F2), lambda i: (i, 0))],
        out_specs=pl.BlockSpec((tr, F), lambda i: (i, 0)), out_shape=jax.ShapeDtypeStruct((T, F), BF16),
        compiler_params=_cparams(dimension_semantics=("parallel",)),
    )(ab)


def swiglu_bwd(name, ab, dsw, tr):
    T, F2 = ab.shape
    F = F2 // 2
    sr = 16

    def body(ab_ref, d_ref, o_ref):
        def step(j, carry):
            sl = pl.ds(pl.multiple_of(j * sr, sr), sr)
            a, b, d = ab_ref[sl, :F], ab_ref[sl, F:], d_ref[sl, :]
            sg = _sigmoid(a)
            o_ref[sl, :F] = d * b * (sg + a * sg * (1.0 - sg))
            o_ref[sl, F:] = d * a * sg
            return carry

        lax.fori_loop(0, tr // sr, step, 0)

    return pl.pallas_call(
        body, name=name, grid=(T // tr,),
        in_specs=[pl.BlockSpec((tr, F2), lambda i: (i, 0)), pl.BlockSpec((tr, F), lambda i: (i, 0))],
        out_specs=pl.BlockSpec((tr, F2), lambda i: (i, 0)), out_shape=jax.ShapeDtypeStruct((T, F2), F32),
        compiler_params=_cparams(dimension_semantics=("parallel",)),
    )(ab, dsw)


def final_call(name, x3, f1, gt, fg, tgt, tr):
    T, D = x3.shape
    sr = 16

    def f(x, y, gtv, g, t):
        err = _rms(x + gtv * y, g) - t
        return 0.5 * jnp.sum(err * err) * (1.0 / D)

    def body(x_ref, y_ref, gt_ref, g_ref, t_ref, dx_ref, dy_ref, dgt_ref, dg_ref, loss_ref):
        @pl.when(pl.program_id(0) == 0)
        def _():
            dgt_ref[...] = jnp.zeros_like(dgt_ref)
            dg_ref[...] = jnp.zeros_like(dg_ref)
            loss_ref[...] = jnp.zeros_like(loss_ref)

        def step(j, carry):
            sl = pl.ds(pl.multiple_of(j * sr, sr), sr)
            val, vjp = jax.vjp(lambda x, y, a, b: f(x, y, a, b, t_ref[sl, :]), x_ref[sl, :], y_ref[sl, :],
                               gt_ref[...], g_ref[...])
            dx, dy, dgt, dg = vjp(jnp.ones((), F32))
            dx_ref[sl, :] = dx
            dy_ref[sl, :] = dy
            dgt_ref[...] += dgt
            dg_ref[...] += dg
            loss_ref[...] += jnp.full(loss_ref.shape, val, F32)
            return carry

        lax.fori_loop(0, tr // sr, step, 0)

    row = pl.BlockSpec((tr, D), lambda i: (i, 0))
    vec = pl.BlockSpec((1, D), lambda i: (0, 0))
    return pl.pallas_call(
        body, name=name, grid=(T // tr,), in_specs=[row, row, vec, vec, row],
        out_specs=[row, row, vec, vec, pl.BlockSpec((8, LANES), lambda i: (0, 0))],
        out_shape=[jax.ShapeDtypeStruct((T, D), F32)] * 2 + [jax.ShapeDtypeStruct((1, D), F32)] * 2
        + [jax.ShapeDtypeStruct((8, LANES), F32)],
        compiler_params=_cparams(dimension_semantics=("arbitrary",)),
    )(x3, f1, gt, fg, tgt)


def _tshift(x, kind, period):
    n = x.shape[0]
    t = lax.broadcasted_iota(jnp.int32, x.shape, 0)
    if kind == 0:
        return jnp.where((t & (period - 1)) == 0, 0.0, pltpu.roll(x, 1, 0))
    if kind == 1:
        return jnp.where(((t & (period - 1)) == period - 1) | (t == n - 1), 0.0, pltpu.roll(x, n - 1, 0))
    if kind == 2:
        return jnp.where(t < GRID_W, 0.0, pltpu.roll(x, GRID_W, 0))
    return jnp.where(t >= n - GRID_W, 0.0, pltpu.roll(x, n - GRID_W, 0))


def _pow2_at_least(n):
    return 1 << (n - 1).bit_length()


def _shift_into(dst_ref, h_ref, n_ctx, cb, D, transpose):
    j = pl.program_id(0)
    quarter = (j * cb * 4) // D
    half = (j * cb * 2) // D
    flip = 1 if transpose else 0
    for q in range(4):
        @pl.when(quarter == q)
        def _(q=q):
            dst_ref[n_ctx:, :] = _tshift(h_ref[n_ctx:, :], q ^ flip, GRID_W)
    for q in range(2):
        @pl.when(half == q)
        def _(q=q):
            dst_ref[:n_ctx, :] = _tshift(h_ref[:n_ctx, :], q ^ flip, _pow2_at_least(n_ctx))


def mix_fwd(name, h, mix, n_ctx):
    R, D = h.shape
    cb = LANES

    def body(h_ref, mix_ref, *rest):
        outs, hs_ref = rest[:6], rest[6]
        _shift_into(hs_ref, h_ref, n_ctx, cb, D, False)
        hv = h_ref[...]
        xx = hs_ref[...] - hv
        for m in range(6):
            outs[m][...] = (hv + xx * mix_ref[m:m + 1, :]).astype(BF16)

    col = pl.BlockSpec((R, cb), lambda j: (0, j))
    return pl.pallas_call(
        body, name=name, grid=(D // cb,), in_specs=[col, pl.BlockSpec((mix.shape[0], cb), lambda j: (0, j))],
        out_specs=[col] * 6, out_shape=[jax.ShapeDtypeStruct((R, D), BF16)] * 6,
        scratch_shapes=[pltpu.VMEM((R, cb), F32)],
        compiler_params=_cparams(dimension_semantics=("parallel",)),
    )(h, mix)


def mix_bwd(name, h, mix, cots, slots, n_ctx):
    R, D = h.shape
    cb = LANES
    nc = len(cots)

    def body(h_ref, mix_ref, *rest):
        cot_refs, dh_ref, dmix_ref, hs_ref, dxx_ref = rest[:nc], rest[nc], rest[nc + 1], rest[nc + 2], rest[nc + 3]
        _shift_into(hs_ref, h_ref, n_ctx, cb, D, False)
        xx = hs_ref[...] - h_ref[...]
        per_slot = [None] * 6
        for cref, m in zip(cot_refs, slots):
            per_slot[m] = cref[...] if per_slot[m] is None else per_slot[m] + cref[...]
        dh = jnp.zeros((R, cb), F32)
        dxx = jnp.zeros((R, cb), F32)
        rows = []
        for m in range(6):
            d = per_slot[m]
            dh = dh + d
            dxx = dxx + d * mix_ref[m:m + 1, :]
            rows.append(jnp.sum(d * xx, axis=0, keepdims=True))
        dmix_ref[...] = jnp.concatenate(rows + [jnp.zeros((2, cb), F32)], axis=0)
        dxx_ref[...] = dxx
        _shift_into(hs_ref, dxx_ref, n_ctx, cb, D, True)
        dh_ref[...] = dh - dxx + hs_ref[...]

    col = pl.BlockSpec((R, cb), lambda j: (0, j))
    return pl.pallas_call(
        body, name=name, grid=(D // cb,), in_specs=[col, pl.BlockSpec((mix.shape[0], cb), lambda j: (0, j))] + [col] * nc,
        out_specs=[col, pl.BlockSpec((8, cb), lambda j: (0, j))],
        out_shape=[jax.ShapeDtypeStruct((R, D), F32), jax.ShapeDtypeStruct((8, D), F32)],
        scratch_shapes=[pltpu.VMEM((R, cb), F32), pltpu.VMEM((R, cb), F32)],
        compiler_params=_cparams(dimension_semantics=("parallel",)),
    )(h, mix, *cots)


def _conv_parts(gb_ref, gc_ref, u_ref, cw_ref):
    T = gb_ref.shape[0]
    z = gc_ref[...] * u_ref[...]
    zp, zn = _tshift(z, 0, _pow2_at_least(T)), _tshift(z, 1, _pow2_at_least(T))
    conv = zp * cw_ref[0:1, :] + z * cw_ref[1:2, :] + zn * cw_ref[2:3, :]
    return z, zp, zn, conv


def conv_fwd(name, gcu, cw):
    T, D3 = gcu.shape
    D = D3 // 3
    cb = LANES
    nb = D // cb

    def body(gb_ref, gc_ref, u_ref, cw_ref, o_ref):
        _, _, _, conv = _conv_parts(gb_ref, gc_ref, u_ref, cw_ref)
        o_ref[...] = (gb_ref[...] * conv).astype(BF16)

    def part(p):
        return pl.BlockSpec((T, cb), lambda j: (0, j + p * nb))

    return pl.pallas_call(
        body, name=name, grid=(nb,),
        in_specs=[part(0), part(1), part(2), pl.BlockSpec((cw.shape[0], cb), lambda j: (0, j))],
        out_specs=pl.BlockSpec((T, cb), lambda j: (0, j)), out_shape=jax.ShapeDtypeStruct((T, D), BF16),
        compiler_params=_cparams(dimension_semantics=("parallel",)),
    )(gcu, gcu, gcu, cw)


def conv_bwd(name, gcu, cw, dp):
    T, D3 = gcu.shape
    D = D3 // 3
    cb = LANES
    nb = D // cb

    def body(gb_ref, gc_ref, u_ref, cw_ref, dp_ref, o_ref, dcw_ref):
        part = pl.program_id(1)
        z, zp, zn, conv = _conv_parts(gb_ref, gc_ref, u_ref, cw_ref)
        dpv = dp_ref[...]
        dconv = dpv * gb_ref[...]
        period = _pow2_at_least(T)
        dz = (_tshift(dconv * cw_ref[0:1, :], 1, period) + dconv * cw_ref[1:2, :]
              + _tshift(dconv * cw_ref[2:3, :], 0, period))

        @pl.when(part == 0)
        def _():
            o_ref[...] = dpv * conv
            dcw_ref[...] = jnp.concatenate(
                [jnp.sum(dconv * s, axis=0, keepdims=True) for s in (zp, z, zn)] + [jnp.zeros((5, cb), F32)], axis=0)

        @pl.when(part == 1)
        def _():
            o_ref[...] = dz * u_ref[...]

        @pl.when(part == 2)
        def _():
            o_ref[...] = dz * gc_ref[...]

    def part_spec(p):
        return pl.BlockSpec((T, cb), lambda j, q: (0, j + p * nb))

    return pl.pallas_call(
        body, name=name, grid=(nb, 3),
        in_specs=[part_spec(0), part_spec(1), part_spec(2), pl.BlockSpec((cw.shape[0], cb), lambda j, q: (0, j)),
                  pl.BlockSpec((T, cb), lambda j, q: (0, j))],
        out_specs=[pl.BlockSpec((T, cb), lambda j, q: (0, j + q * nb)), pl.BlockSpec((8, cb), lambda j, q: (0, j))],
        out_shape=[jax.ShapeDtypeStruct((T, D3), F32), jax.ShapeDtypeStruct((8, D), F32)],
        compiler_params=_cparams(dimension_semantics=("arbitrary", "arbitrary")),
    )(gcu, gcu, gcu, cw, dp)


SCAN_TC = 8


def _scan_consts():
    rows = lax.broadcasted_iota(jnp.int32, (HEAD, LANES), 0)
    cols = lax.broadcasted_iota(jnp.int32, (HEAD, LANES), 1)
    eye = rows == (cols & (HEAD - 1))
    r2 = lax.broadcasted_iota(jnp.int32, (2 * LANES, LANES), 0)
    c2 = lax.broadcasted_iota(jnp.int32, (2 * LANES, LANES), 1)
    ones2 = (((r2 & (LANES - 1)) >= HEAD) == (c2 >= HEAD)).astype(BF16)
    return eye, ones2, ones2[:LANES]


SCAN_ROW_CHUNKS = 4


def _chunks_of_heads(hp):
    per = max(1, hp // SCAN_ROW_CHUNKS)
    return [range(lo, lo + per) for lo in range(0, hp, per)]


def _rows_of(heads):
    return pl.ds(heads[0] * HEAD, len(heads) * HEAD)


def _split2(p):
    hi = p.astype(BF16)
    lo = (p - hi.astype(F32)).astype(BF16)
    return jnp.concatenate([hi, lo], axis=1)


def _head_rows(h):
    return pl.ds(h * HEAD, HEAD)


def _expand_into(dst, p1_ref, row_ref, t, hp, eye, ones1):
    for h in range(hp):
        p1_ref[_head_rows(h), :] = jnp.where(eye, row_ref[t, h:h + 1, :], 0.0).astype(BF16)
    dst[...] = jnp.dot(p1_ref[...], ones1, preferred_element_type=F32)


def _colsum_store(ref, t, h, x):
    ref[t, pl.ds(h, 1), :] = jnp.sum(x, axis=0, keepdims=True)


def _order(i, n_ctx, n_all, rev):
    if not rev:
        return i
    return jnp.where(i < n_ctx, n_ctx - 1 - i, n_all - 1 - (i - n_ctx))


def scan_fwd(name, r, w, k, v, a, b, n_ctx_rows, rev):
    R, D = r.shape
    hp, tc = D // LANES, SCAN_TC
    n_all, n_ctx = R // tc, n_ctx_rows // tc
    ins = [t.reshape(R, hp, LANES) for t in (r, w, k, v, a, b)]

    def body(r_ref, w_ref, k_ref, v_ref, a_ref, b_ref, y_ref, st_ref, s_ref, ve_ref, sa_ref, p_ref, p1_ref):
        @pl.when(pl.program_id(0) == 0)
        def _():
            s_ref[...] = jnp.zeros_like(s_ref)

        eye, ones2, ones1 = _scan_consts()

        def row_of(q):
            return tc - 1 - q if rev else q

        for q in range(tc):
            _expand_into(ve_ref.at[q], p1_ref.at[q % 2], v_ref, row_of(q), hp, eye, ones1)

        def advance(q, prev_ref):
            t = row_of(q)
            for heads in _chunks_of_heads(hp):
                rows = _rows_of(heads)
                for h in heads:
                    p_ref[q % 2, _head_rows(h), :] = _split2(prev_ref[_head_rows(h), :] * a_ref[t, h:h + 1, :])
                sa_ref[q % 2, rows, :] = jnp.dot(p_ref[q % 2, rows, :], ones2, preferred_element_type=F32)
                for h in heads:
                    hr_ = _head_rows(h)
                    st_ref[q, hr_, :] = (prev_ref[hr_, :] * w_ref[t, h:h + 1, :]
                                         + sa_ref[q % 2, hr_, :] * b_ref[t, h:h + 1, :]
                                         + ve_ref[q, hr_, :] * k_ref[t, h:h + 1, :])

        advance(0, s_ref)
        for q in range(1, tc):
            advance(q, st_ref.at[q - 1])
        s_ref[...] = st_ref[tc - 1]

        for q in range(tc):
            t = row_of(q)
            for h in range(hp):
                p1_ref[q % 2, _head_rows(h), :] = (st_ref[q, _head_rows(h), :] * r_ref[t, h:h + 1, :]).astype(BF16)
            sa_ref[q % 2] = jnp.dot(p1_ref[q % 2], ones1, preferred_element_type=F32)
            for h in range(hp):
                _colsum_store(y_ref, t, h, jnp.where(eye, sa_ref[q % 2, _head_rows(h), :], 0.0))

    row_spec = pl.BlockSpec((tc, hp, LANES), lambda i: (_order(i, n_ctx, n_all, rev), 0, 0))
    n = hp * HEAD
    y, st = pl.pallas_call(
        body, name=name, grid=(n_all,), in_specs=[row_spec] * 6,
        out_specs=[row_spec, pl.BlockSpec((tc, n, LANES), lambda i: (i, 0, 0))],
        out_shape=[jax.ShapeDtypeStruct((R, hp, LANES), F32), jax.ShapeDtypeStruct((R, n, LANES), F32)],
        scratch_shapes=[pltpu.VMEM((n, LANES), F32), pltpu.VMEM((tc, n, LANES), F32), pltpu.VMEM((2, n, LANES), F32),
                        pltpu.VMEM((2, n, 2 * LANES), BF16), pltpu.VMEM((2, n, LANES), BF16)],
        compiler_params=_cparams(dimension_semantics=("arbitrary",)),
    )(*ins)
    return y.reshape(R, D), st


def scan_bwd(name, r, w, k, v, a, b, dy, st, n_ctx_rows, rev):
    R, D = r.shape
    hp, tc = D // LANES, SCAN_TC
    n_all, n_ctx = R // tc, n_ctx_rows // tc
    ins = [t.reshape(R, hp, LANES) for t in (r, w, k, v, a, b, dy)]

    def body(r_ref, w_ref, k_ref, v_ref, a_ref, b_ref, dy_ref, st_ref, prev_ref,
             dr_ref, dw_ref, dk_ref, dv_ref, da_ref, db_ref,
             g_ref, s0_ref, ve_ref, dye_ref, sa_ref, gs_ref, tmp_ref, p_ref, p1_ref):
        i = pl.program_id(0)

        @pl.when(i == 0)
        def _():
            g_ref[...] = jnp.zeros_like(g_ref)

        eye, ones2, ones1 = _scan_consts()

        @pl.when(i == n_all - 1)
        def _():
            s0_ref[...] = jnp.zeros_like(s0_ref)

        @pl.when(i != n_all - 1)
        def _():
            s0_ref[...] = prev_ref[0]

        def row_of(q):
            return tc - 1 - q if rev else q

        def prev_of(q):
            return s0_ref if q == 0 else st_ref.at[q - 1]

        for q in range(tc):
            t, prev = row_of(q), prev_of(q)
            _expand_into(ve_ref.at[q], p1_ref.at[2 + q % 2], v_ref, t, hp, eye, ones1)
            _expand_into(dye_ref.at[q], p1_ref.at[4 + q % 2], dy_ref, t, hp, eye, ones1)
            for h in range(hp):
                p1_ref[q % 2, _head_rows(h), :] = (prev[_head_rows(h), :] * a_ref[t, h:h + 1, :]).astype(BF16)
            sa_ref[q] = jnp.dot(p1_ref[q % 2], ones1, preferred_element_type=F32)

        for q in reversed(range(tc)):
            t, prev = row_of(q), prev_of(q)
            for heads in _chunks_of_heads(hp):
                rows = _rows_of(heads)
                for h in heads:
                    hr_ = _head_rows(h)
                    g = g_ref[hr_, :] + dye_ref[q, hr_, :] * r_ref[t, h:h + 1, :]
                    gs_ref[q, hr_, :] = g
                    p_ref[q % 2, hr_, :] = _split2(g * b_ref[t, h:h + 1, :])
                tmp_ref[q % 2, rows, :] = jnp.dot(p_ref[q % 2, rows, :], ones2, preferred_element_type=F32)
                for h in heads:
                    hr_ = _head_rows(h)
                    dsa = tmp_ref[q % 2, hr_, :]
                    _colsum_store(da_ref, t, h, prev[hr_, :] * dsa)
                    g_ref[hr_, :] = gs_ref[q, hr_, :] * w_ref[t, h:h + 1, :] + dsa * a_ref[t, h:h + 1, :]

        for q in range(tc):
            t, prev = row_of(q), prev_of(q)
            for h in range(hp):
                hr_ = _head_rows(h)
                g = gs_ref[q, hr_, :]
                p1_ref[q % 2, hr_, :] = (g * k_ref[t, h:h + 1, :]).astype(BF16)
                _colsum_store(dr_ref, t, h, st_ref[q, hr_, :] * dye_ref[q, hr_, :])
                _colsum_store(dk_ref, t, h, g * ve_ref[q, hr_, :])
                _colsum_store(dw_ref, t, h, g * prev[hr_, :])
                _colsum_store(db_ref, t, h, g * sa_ref[q, hr_, :])
            tmp_ref[q % 2] = jnp.dot(p1_ref[q % 2], ones1, preferred_element_type=F32)
            for h in range(hp):
                _colsum_store(dv_ref, t, h, jnp.where(eye, tmp_ref[q % 2, _head_rows(h), :], 0.0))

    def pos(i):
        return n_all - 1 - i

    n = hp * HEAD
    row_spec = pl.BlockSpec((tc, hp, LANES), lambda i: (_order(pos(i), n_ctx, n_all, rev), 0, 0))
    big = pltpu.VMEM((tc, n, LANES), F32)
    one = pltpu.VMEM((n, LANES), F32)
    outs = pl.pallas_call(
        body, name=name, grid=(n_all,),
        in_specs=[row_spec] * 7 + [
            pl.BlockSpec((tc, n, LANES), lambda i: (pos(i), 0, 0)),
            pl.BlockSpec((1, n, LANES), lambda i: (jnp.maximum(pos(i) * tc - 1, 0), 0, 0))],
        out_specs=[row_spec] * 6,
        out_shape=[jax.ShapeDtypeStruct((R, hp, LANES), F32)] * 6,
        scratch_shapes=[one, one, big, big, big, big, pltpu.VMEM((2, n, LANES), F32),
                        pltpu.VMEM((2, n, 2 * LANES), BF16), pltpu.VMEM((6, n, LANES), BF16)],
        compiler_params=_cparams(dimension_semantics=("arbitrary",)),
    )(*ins, st, st)
    return [o.reshape(R, D) for o in outs]


ANY = pl.BlockSpec(memory_space=pl.ANY)


def _peer(xi, yi, ci, k):
    return (1 - xi if k & 4 else xi, 1 - yi if k & 2 else yi, 1 - ci if k & 1 else ci)


def _rcopy(src, dst, send_sem, recv_sem, dev):
    return pltpu.make_async_remote_copy(src_ref=src, dst_ref=dst, send_sem=send_sem, recv_sem=recv_sem,
                                        device_id=dev, device_id_type=MESH)


def _drain(copies):
    for cp in copies:
        if cp.is_remote:
            cp.wait_send()
        else:
            cp.wait()


def all_gather8(name, x):
    r, c = x.shape

    def body(x_ref, out_ref, send_sems, recv_sems, local_sem):
        xi, yi, ci = _place()

        def blk(p):
            return out_ref.at[4 * p[0] + 2 * p[1] + p[2]]

        me = (xi, yi, ci)
        mine = pltpu.make_async_copy(x_ref, blk(me), local_sem.at[0])
        mine.start()
        sends = [_rcopy(x_ref, blk(me), send_sems.at[k - 1], recv_sems.at[k - 1], _peer(xi, yi, ci, k))
                 for k in range(1, N_DEV)]
        for cp in sends:
            cp.start()
        for k in range(1, N_DEV):
            p = _peer(xi, yi, ci, k)
            _rcopy(x_ref, blk(p), send_sems.at[k - 1], recv_sems.at[k - 1], p).wait_recv()
        for cp in sends:
            cp.wait_send()
        mine.wait()

    vm = pl.BlockSpec(memory_space=pltpu.VMEM)
    return pl.pallas_call(
        body, name=name, in_specs=[vm], out_specs=vm, out_shape=jax.ShapeDtypeStruct((N_DEV, r, c), x.dtype),
        scratch_shapes=[pltpu.SemaphoreType.DMA((N_DEV - 1,)), pltpu.SemaphoreType.DMA((N_DEV - 1,)),
                        pltpu.SemaphoreType.DMA((1,))],
        compiler_params=_cparams(),
    )(x)


def _chips(xi, yi):
    chips = [(1 - xi, yi), (xi, 1 - yi), (1 - xi, 1 - yi)]
    return chips, [2 * cx + cy for cx, cy in chips]


def gather_weights(name, stacked):
    n = len(stacked)

    def body(*refs):
        out = refs[n:2 * n]
        send_sems, recv_sems = refs[2 * n:]
        xi, yi, ci = _place()
        s = 2 * xi + yi
        chips, sidx = _chips(xi, yi)
        sib = (xi, yi, 1 - ci)
        started = []
        for w in range(n):
            hr = out[w].shape[1] // 2
            mine = out[w].at[s, pl.ds(ci * hr, hr)]
            for j, (cx, cy) in enumerate(chips):
                cp = _rcopy(mine, mine, send_sems.at[w, j], recv_sems.at[w, j], (cx, cy, ci))
                cp.start()
                started.append(cp)
        for w in range(n):
            hr = out[w].shape[1] // 2
            for j, (cx, cy) in enumerate(chips):
                blk = out[w].at[sidx[j], pl.ds(ci * hr, hr)]
                _rcopy(blk, blk, send_sems.at[w, j], recv_sems.at[w, j], (cx, cy, ci)).wait_recv()
                fw = _rcopy(blk, blk, send_sems.at[w, 3 + j], recv_sems.at[w, 3 + j], sib)
                fw.start()
                started.append(fw)
        for w in range(n):
            hr = out[w].shape[1] // 2
            for j in range(3):
                blk = out[w].at[sidx[j], pl.ds((1 - ci) * hr, hr)]
                _rcopy(blk, blk, send_sems.at[w, 3 + j], recv_sems.at[w, 3 + j], sib).wait_recv()
        _drain(started)

    return pl.pallas_call(
        body, name=name, in_specs=[ANY] * n, out_specs=[ANY] * n,
        out_shape=[jax.ShapeDtypeStruct(a.shape, a.dtype) for a in stacked],
        input_output_aliases={w: w for w in range(n)},
        scratch_shapes=[pltpu.SemaphoreType.DMA((n, 6)), pltpu.SemaphoreType.DMA((n, 6))],
        compiler_params=_cparams(),
    )(*stacked)


def rs_pair(name, grads):
    n = len(grads)

    def body(*refs):
        g, out = refs[:n], refs[n:2 * n]
        send_sems, recv_sems = refs[2 * n:]
        xi, yi, ci = _place()
        sib = (xi, yi, 1 - ci)
        cps = [_rcopy(g[w].at[:, 1 - ci], out[w], send_sems.at[w], recv_sems.at[w], sib) for w in range(n)]
        for cp in cps:
            cp.start()
        for cp in cps:
            cp.wait_recv()
        for cp in cps:
            cp.wait_send()

    return pl.pallas_call(
        body, name=name, in_specs=[ANY] * n, out_specs=[ANY] * n,
        out_shape=[jax.ShapeDtypeStruct((N_CHIPS,) + a.shape[2:], a.dtype) for a in grads],
        scratch_shapes=[pltpu.SemaphoreType.DMA((n,)), pltpu.SemaphoreType.DMA((n,))],
        compiler_params=_cparams(),
    )(*grads)


def _rows_tile(rows, cols, unit=16, limit=1 << 20):
    best = None
    for t in range(unit, rows + 1, unit):
        if rows % t == 0 and t * cols * 4 <= limit:
            best = t
    return best or rows


def rs_add_pair(name, g, got, ci):
    _, _, hr, c = g.shape
    th = _rows_tile(hr, c)

    def body(ci_ref, g_ref, r_ref, o32_ref, ob_ref):
        tot = g_ref[...] + r_ref[...]
        o32_ref[...] = tot
        ob_ref[...] = tot.astype(BF16)

    blk = pl.BlockSpec((None, th, c), lambda s, i, ci_ref: (s, i, 0))
    return pl.pallas_call(
        body, name=name,
        grid_spec=pltpu.PrefetchScalarGridSpec(
            num_scalar_prefetch=1, grid=(N_CHIPS, hr // th),
            in_specs=[pl.BlockSpec((None, None, th, c), lambda s, i, ci_ref: (s, ci_ref[0], i, 0)), blk],
            out_specs=[blk, blk]),
        out_shape=[jax.ShapeDtypeStruct((N_CHIPS, hr, c), F32), jax.ShapeDtypeStruct((N_CHIPS, hr, c), BF16)],
        compiler_params=_cparams(dimension_semantics=("parallel", "parallel")),
    )(ci, g, got)


def rs_chips(name, sums_bf16):
    n = len(sums_bf16)

    def body(*refs):
        pb, outs = refs[:n], refs[n:4 * n]
        send_sems, recv_sems = refs[4 * n:]
        xi, yi, ci = _place()
        chips, sidx = _chips(xi, yi)
        started = []
        for w in range(n):
            for j, (cx, cy) in enumerate(chips):
                cp = _rcopy(pb[w].at[sidx[j]], outs[3 * w + j], send_sems.at[w, j], recv_sems.at[w, j], (cx, cy, ci))
                cp.start()
                started.append(cp)
        for w in range(n):
            for j, (cx, cy) in enumerate(chips):
                _rcopy(pb[w].at[sidx[j]], outs[3 * w + j], send_sems.at[w, j], recv_sems.at[w, j],
                       (cx, cy, ci)).wait_recv()
        _drain(started)

    out_shape = []
    for a in sums_bf16:
        out_shape += [jax.ShapeDtypeStruct(a.shape[1:], BF16)] * 3
    res = pl.pallas_call(
        body, name=name, in_specs=[ANY] * n, out_specs=[ANY] * (3 * n), out_shape=out_shape,
        scratch_shapes=[pltpu.SemaphoreType.DMA((n, 3)), pltpu.SemaphoreType.DMA((n, 3))],
        compiler_params=_cparams(),
    )(*sums_bf16)
    return [res[3 * w:3 * w + 3] for w in range(n)]


def rs_swap(name, halves):
    n = len(halves)

    def body(*refs):
        hv, out = refs[:n], refs[n:2 * n]
        send_sems, recv_sems = refs[2 * n:]
        xi, yi, ci = _place()
        sib = (xi, yi, 1 - ci)
        cps = [_rcopy(hv[w], out[w], send_sems.at[w], recv_sems.at[w], sib) for w in range(n)]
        for cp in cps:
            cp.start()
        for cp in cps:
            cp.wait_recv()
        _drain(cps)

    return pl.pallas_call(
        body, name=name, in_specs=[ANY] * n, out_specs=[ANY] * n,
        out_shape=[jax.ShapeDtypeStruct(a.shape, a.dtype) for a in halves],
        scratch_shapes=[pltpu.SemaphoreType.DMA((n,)), pltpu.SemaphoreType.DMA((n,))],
        compiler_params=_cparams(),
    )(*halves)


def cast_to_slot(name, x, slot):
    r, c = x.shape
    tr = _rows_tile(r, c)

    def body(slot_ref, x_ref, o_ref):
        o_ref[...] = x_ref[...].astype(BF16)

    return pl.pallas_call(
        body, name=name,
        grid_spec=pltpu.PrefetchScalarGridSpec(
            num_scalar_prefetch=1, grid=(r // tr,),
            in_specs=[pl.BlockSpec((tr, c), lambda i, slot_ref: (i, 0))],
            out_specs=pl.BlockSpec((None, tr, c), lambda i, slot_ref: (slot_ref[0], i, 0))),
        out_shape=jax.ShapeDtypeStruct((N_CHIPS, r, c), BF16),
        compiler_params=_cparams(dimension_semantics=("parallel",)),
    )(slot, x)


def sum_blocks(name, x, picks):
    _, r, c = x.shape

    def body(x_ref, o_ref):
        acc = x_ref[picks[0]]
        for b in picks[1:]:
            acc = acc + x_ref[b]
        o_ref[...] = acc

    vm = pl.BlockSpec(memory_space=pltpu.VMEM)
    return pl.pallas_call(body, name=name, in_specs=[vm], out_specs=vm, out_shape=jax.ShapeDtypeStruct((r, c), F32),
                          compiler_params=_cparams())(x)


def adamw(name, w, g, m, v):
    r, c = w.shape
    tr = _rows_tile(r, c, unit=8, limit=1 << 19)

    def body(w_ref, g_ref, m_ref, v_ref, d_ref, mo_ref, vo_ref):
        d_ref[...], mo_ref[...], vo_ref[...] = _adam_update(w_ref[...], g_ref[...], m_ref[...], v_ref[...])

    blk = pl.BlockSpec((tr, c), lambda i: (i, 0))
    return pl.pallas_call(
        body, name=name, grid=(r // tr,), in_specs=[blk] * 4, out_specs=[blk] * 3,
        out_shape=[jax.ShapeDtypeStruct((r, c), F32)] * 3,
        compiler_params=_cparams(dimension_semantics=("parallel",)),
    )(w, g, m, v)


def _adam_update(w, gv, m, v):
    c1 = 1.0 / (1.0 - ADAM_B1 ** ADAM_STEP)
    c2 = 1.0 / (1.0 - ADAM_B2 ** ADAM_STEP)
    mn = ADAM_B1 * m + (1.0 - ADAM_B1) * gv
    vn = ADAM_B2 * v + (1.0 - ADAM_B2) * (gv * gv)
    return -ADAM_LR * ((mn * c1) / (jnp.sqrt(vn * c2) + ADAM_EPS) + ADAM_WD * w), mn, vn


def adamw_halves(name, w, m, v, mine, theirs, ci):
    hr, c = mine[0].shape
    npos = len(mine)
    th = _rows_tile(hr, c, unit=8, limit=1 << 19)
    per = hr // th

    def body(ci_ref, w_ref, m_ref, v_ref, *rest):
        g_refs, (go_ref, d_ref, mo_ref, vo_ref) = rest[:2 * npos], rest[2 * npos:]
        pos, half = pl.program_id(0), pl.program_id(1)
        from_me = half == ci_ref[0]
        gv = jnp.where(from_me, g_refs[0][...], g_refs[npos][...])
        for p in range(1, npos):
            gv = jnp.where(pos == p, jnp.where(from_me, g_refs[p][...], g_refs[npos + p][...]), gv)
        d_ref[...], mo_ref[...], vo_ref[...] = _adam_update(w_ref[...], gv, m_ref[...], v_ref[...])
        go_ref[...] = gv

    full = pl.BlockSpec((th, c), lambda p, h, i, ci_ref: ((p * 2 + h) * per + i, 0))
    part = pl.BlockSpec((th, c), lambda p, h, i, ci_ref: (i, 0))
    rows = 2 * hr * npos
    return pl.pallas_call(
        body, name=name,
        grid_spec=pltpu.PrefetchScalarGridSpec(
            num_scalar_prefetch=1, grid=(npos, 2, per), in_specs=[full] * 3 + [part] * (2 * npos),
            out_specs=[full] * 4),
        out_shape=[jax.ShapeDtypeStruct((rows, c), F32)] * 4,
        compiler_params=_cparams(dimension_semantics=("arbitrary", "arbitrary", "arbitrary")),
    )(ci, w, m, v, *mine, *theirs)


def pack_rows(name, parts, rows):
    width = parts[0].shape[1]
    n = len(parts)

    def body(*refs):
        o_ref = refs[n]
        o_ref[...] = jnp.zeros_like(o_ref)
        off = 0
        for r in refs[:n]:
            o_ref[off:off + r.shape[0], :] = r[...]
            off += r.shape[0]

    vm = pl.BlockSpec(memory_space=pltpu.VMEM)
    return pl.pallas_call(body, name=name, in_specs=[vm] * n, out_specs=vm,
                          out_shape=jax.ShapeDtypeStruct((rows, width), F32), compiler_params=_cparams())(*parts)


def _pad_rows(a, rows):
    return jnp.pad(a, ((0, rows - a.shape[0]), (0, 0)))


def _view2d(name, a):
    if name == 'rw_rk' or a.ndim == 1:
        return a.reshape(1, -1)
    return a.reshape(-1, a.shape[-1])


def _reduce_scatter(tag, items, ci_arr, s):
    names = []
    for nm, _ in items:
        if nm not in names:
            names.append(nm)
    g4 = [g.reshape(N_CHIPS, 2, g.shape[1] // 2, g.shape[2]) for _, g in items]
    got = rs_pair("rs1_" + tag, g4)
    sums = [rs_add_pair(f"rs1add_{tag}{w}", g4[w], got[w], ci_arr) for w in range(len(items))]
    landed = rs_chips("rs2_" + tag, [s_[1] for s_ in sums])
    halves = []
    for w, (r0, r1, r2) in enumerate(landed):
        own = lax.dynamic_index_in_dim(sums[w][0], s, axis=0, keepdims=False)
        hr, c = own.shape
        halves.append(row_call(f"rs2add_{tag}{w}", f_sum, [own, r0, r1, r2], [], [(c, F32)],
                               tr=_rows_tile(hr, c), sr=16)[0])
    theirs = rs_swap("rs3_" + tag, halves)
    return {name: ([halves[w] for w, (nm, _) in enumerate(items) if nm == name],
                   [theirs[w] for w, (nm, _) in enumerate(items) if nm == name]) for name in names}


def _step(p):
    xi, yi, ci = _place()
    me = 4 * xi + 2 * yi + ci
    s = 2 * xi + yi
    ci_arr = jnp.reshape(ci, (1,)).astype(jnp.int32)
    s_arr = jnp.reshape(s, (1,)).astype(jnp.int32)
    x, ctx, tgt = p['x'][0], p['ctx'][0], p['loss_target'][0]
    T, D = x.shape
    L = ctx.shape[0]
    H, Dq = D // HEAD, D // N_CHIPS
    TR = math.gcd(math.gcd(L, T), 256)
    TS = min(TR, 64)
    nct = L // TR
    LG = p['rw_g1'].shape[-1]
    LW, LA = p['rw_w1'].shape[-1], p['rw_a1'].shape[-1]
    F4 = p['ffn_w2'].shape[1]

    pack = jnp.concatenate([
        _pad_rows(p['c'].reshape(N_CHIPS, Dq), 8), _pad_rows(p['rw_mix'][0], 8), _pad_rows(p['rw_w0'][0], 8),
        _pad_rows(p['rw_a0'][0], 8), _pad_rows(p['sc_conv'][0], 8)], axis=0)
    got = all_gather8("ag_small", pack)
    c_all = got[:, 0:N_CHIPS, :].reshape(N_DEV, D)
    full = jnp.transpose(got[::2], (1, 0, 2)).reshape(40, D)
    mix_f, w0_f, a0_f, conv_f = full[8:16], full[16:24], full[24:32], full[32:40]

    cond_in = jnp.concatenate([c_all, _pad_rows(p['c_ctx'].reshape(1, D), 8)], axis=0)
    cond = row_call("cond", f_silu, [cond_in], [], [(D, F32)], tr=16, sr=16)[0]
    ada = Stacked(p['ada_w'], "layer", D)
    modp = [mm_nn(f"modp{i}", cond, ada.at(i)) for i in range(2)]
    mg = all_gather8("ag_mod", jnp.concatenate(modp, axis=0))
    mod = jnp.transpose(mg[::2].reshape(N_CHIPS, 2, 16, 6 * Dq), (1, 2, 0, 3)).reshape(2, 16, 6 * D)
    mod = mod + p['ada_b'][:, None, :]
    mod_x = lax.dynamic_index_in_dim(mod, me, axis=1, keepdims=False)
    mod_c = mod[:, 8]

    def chunk(vec, j):
        return vec[j * D:(j + 1) * D].reshape(1, D)

    sh1x, sc1x, gt1x, sh2x, sc2x, gt2x = ([chunk(mod_x[i], j) for i in range(2)] for j in range(6))
    sh1c, sc1c = chunk(mod_c[0], 0), chunk(mod_c[0], 1)

    def gather(tag, names):
        shards = [cast_to_slot("cast_" + n, _view2d(n, p[n]), s_arr) for n in names]
        return dict(zip(names, gather_weights("ag_" + tag, shards)))

    gw = gather("rw", ['rw_wr', 'rw_wk', 'rw_wv', 'rw_wo', 'rw_w1', 'rw_w2', 'rw_a1', 'rw_a2', 'rw_g1', 'rw_g2'])
    gw.update(gather("sc", ['sc_win', 'sc_wout']))
    gw.update(gather("ffn", ['ffn_w13', 'ffn_w2']))
    Wr, Wk, Wv, Wo = (Stacked(gw[n], "row", Dq) for n in ('rw_wr', 'rw_wk', 'rw_wv', 'rw_wo'))
    W1, A1, G1 = (Stacked(gw[n], "row", Dq) for n in ('rw_w1', 'rw_a1', 'rw_g1'))
    W2, A2, G2 = Stacked(gw['rw_w2'], "col", LW), Stacked(gw['rw_a2'], "col", LA), Stacked(gw['rw_g2'], "col", LG)
    Win, Wout = Stacked(gw['sc_win'], "col", D), Stacked(gw['sc_wout'], "row", Dq)
    W13, W2f = Stacked(gw['ffn_w13'], "col", D), Stacked(gw['ffn_w2'], "row", F4)

    ones2 = (((lax.broadcasted_iota(jnp.int32, (2 * LANES, LANES), 0) & (LANES - 1)) >= HEAD)
             == (lax.broadcasted_iota(jnp.int32, (2 * LANES, LANES), 1) >= HEAD)).astype(BF16)
    n1g, n2g = p['norm1_g'], p['norm2_g']
    kkp, ka, lnw, lnb = p['rw_kk'], p['rw_ka'], p['rw_lnw'], p['rw_lnb']
    rk = p['rw_rk'].reshape(1, D)
    fg = p['final_g'].reshape(1, D)

    xin = jnp.concatenate([ctx, x], axis=0)
    nm = functools.partial(f_norm_mod, nct)
    nm_consts = [n1g[0:1], sh1c, sc1c, sh1x[0], sc1x[0]]
    h = row_call("l0_norm", nm, [xin], nm_consts, [(D, F32)], tr=TR, sr=16)[0]
    xr, xw, xk, xv, xa, xg = mix_fwd("l0_mix", h, mix_f, L)
    r = mm_nn("l0_r", xr, Wr)
    k = mm_nn("l0_k", xk, Wk)
    v = mm_nn("l0_v", xv, Wv)
    gl = mm_nn("l0_gl", xg, G1)
    sg = row_call("l0_sg", f_sigmoid, [gl], [], [(LG, BF16)], tr=TR, sr=16)[0]
    g = mm_nn("l0_g", sg, G2)
    wl, tw, zw, al, za = [], [], [], [], []
    for d in range(2):
        wl.append(mm_nn(f"l0_wl{d}", xw, W1.at(d)))
        tw.append(row_call(f"l0_tw{d}", f_tanh, [wl[d]], [], [(LW, BF16)], tr=TR, sr=16)[0])
        zw.append(mm_nn(f"l0_zw{d}", tw[d], W2.at(d)))
        al.append(mm_nn(f"l0_al{d}", xa, A1.at(d), BF16))
        za.append(mm_nn(f"l0_za{d}", al[d], A2.at(d)))
    post_rows = [k, zw[0], zw[1], za[0], za[1]]
    post_consts = [kkp, ka, w0_f[0:1], w0_f[1:2], a0_f[0:1], a0_f[1:2], ones2]
    aa, dec0, dec1, kd0, kd1, bb0, bb1, ksum = row_call(
        "l0_post", f_post_fwd, post_rows, post_consts, [(D, F32)] * 8, tr=TS, sr=16)
    dec, kd, bb = (dec0, dec1), (kd0, kd1), (bb0, bb1)
    ys, sts = [], []
    for d in range(2):
        y_d, st_d = scan_fwd(f"l0_scan{d}", r, dec[d], kd[d], v, aa, bb[d], L, bool(d))
        ys.append(y_d)
        sts.append(st_d)
    ro_rows = [ys[0], ys[1], r, ksum, v, g]
    ro_consts = [rk, lnw, lnb, ones2]
    og = row_call("l0_readout", f_readout, ro_rows, ro_consts, [(D, BF16)], tr=TS, sr=16)[0]
    yx = mm_nn("l0_o", og, Wo)
    res0_consts = [gt1x[0], n2g[0:1], sh2x[0], sc2x[0]]
    x1, h2 = row_call("l0_res", f_res_norm_mod, [x, yx], res0_consts, [(D, F32), (D, BF16)], tr=TR, sr=16,
                      offs=[0, nct])
    ab0 = mm_nn("l0_ffn13", h2, W13.at(0))
    sw0 = swiglu_fwd("l0_swiglu", ab0, TS)
    f0 = mm_nn("l0_ffn2", sw0, W2f.at(0))

    res1_consts = [gt2x[0], n1g[1:2], sh1x[1], sc1x[1]]
    x2, hb = row_call("l1_norm", f_res_norm_mod, [x1, f0], res1_consts, [(D, F32), (D, BF16)], tr=TR, sr=16)
    gcu = mm_nn("l1_win", hb, Win)
    pc = conv_fwd("l1_conv", gcu, conv_f)
    yx1 = mm_nn("l1_wout", pc, Wout)
    res2_consts = [gt1x[1], n2g[1:2], sh2x[1], sc2x[1]]
    x3, h2b = row_call("l1_res", f_res_norm_mod, [x2, yx1], res2_consts, [(D, F32), (D, BF16)], tr=TR, sr=16)
    ab1 = mm_nn("l1_ffn13", h2b, W13.at(1))
    sw1 = swiglu_fwd("l1_swiglu", ab1, TS)
    f1 = mm_nn("l1_ffn2", sw1, W2f.at(1))
    dx3, df1, dgt2_1, dfg, loss_blk = final_call("final", x3, f1, gt2x[1], fg, tgt, TR)
    loss = lax.psum(loss_blk[0, 0], ("x", "y", "c"))

    big = []
    dsw1 = mm_nt("b1_dsw", df1, W2f.at(1))
    gW2f1 = mm_tn("b1_gw2", sw1, df1, "row")
    dab1 = swiglu_bwd("b1_swiglu", ab1, dsw1, TS)
    dh2b = mm_nt("b1_dh2", dab1, W13.at(1))
    gW13_1 = mm_tn("b1_gw13", h2b, dab1, "col")
    rm = [True, True]
    cm = [True] * 4
    (dx2, dyx1), (dgt1_1, dn2g1, dsh2_1, dsc2_1) = row_vjp(
        "b1_res", f_res_norm_mod, [x2, yx1], res2_consts, [dx3, dh2b], row_mask=rm, const_mask=cm, tr=TR, sr=16)
    dpc = mm_nt("b1_dpc", dyx1, Wout)
    big.append(('sc_wout', mm_tn("b1_gwout", pc, dyx1, "row")))
    dgcu, dconv = conv_bwd("b1_conv", gcu, conv_f, dpc)
    dhb = mm_nt("b1_dhb", dgcu, Win)
    big.append(('sc_win', mm_tn("b1_gwin", hb, dgcu, "col")))
    (dx1, df0), (dgt2_0, dn1g1, dsh1_1, dsc1_1) = row_vjp(
        "b1_norm", f_res_norm_mod, [x1, f0], res1_consts, [dx2, dhb], row_mask=rm, const_mask=cm, tr=TR, sr=16)

    dsw0 = mm_nt("b0_dsw", df0, W2f.at(0))
    gW2f0 = mm_tn("b0_gw2", sw0, df0, "row")
    dab0 = swiglu_bwd("b0_swiglu", ab0, dsw0, TS)
    dh2 = mm_nt("b0_dh2", dab0, W13.at(0))
    gW13_0 = mm_tn("b0_gw13", h2, dab0, "col")
    (dx_a, dyx), (dgt1_0, dn2g0, dsh2_0, dsc2_0) = row_vjp(
        "b0_res", f_res_norm_mod, [x, yx], res0_consts, [dx1, dh2], row_mask=rm, const_mask=cm, tr=TR, sr=16,
        offs=[0, nct])
    dyx_all = jnp.concatenate([jnp.zeros((L, D), F32), dyx], axis=0)
    dog = mm_nt("b0_dog", dyx_all, Wo)
    gWo = mm_tn("b0_gwo", og, dyx_all, "row")
    (dy, dr_ro, dksum, dv_ro, dg), (drk, dlnw, dlnb) = row_vjp(
        "b0_readout", f_readout, ro_rows, ro_consts, [dog], row_mask=[True, False, True, True, True, True],
        const_mask=[True, True, True, False], tr=TS, sr=16)
    sg_ = [scan_bwd(f"b0_scan{d}", r, dec[d], kd[d], v, aa, bb[d], dy, sts[d], L, bool(d)) for d in range(2)]
    (dr0, ddec0, dkd0, dv0, daa0, dbb0), (dr1, ddec1, dkd1, dv1, daa1, dbb1) = sg_
    post_cots = [daa0, daa1, ddec0, ddec1, dkd0, dkd1, dbb0, dbb1, dksum]
    (dk, dzw0, dzw1, dza0, dza1), (dkkp, dka, dw00, dw01, da00, da01) = row_vjp(
        "b0_post", f_post, post_rows, post_consts, post_cots, row_mask=[True] * 5,
        const_mask=[True] * 6 + [False], tr=TS, sr=16)
    dzw, dza = (dzw0, dzw1), (dza0, dza1)
    dr_t = sum_cast("b0_drsum", [dr0, dr1, dr_ro], BF16, TR)
    dv_t = sum_cast("b0_dvsum", [dv0, dv1, dv_ro], BF16, TR)
    mix_cots, mix_slots = [], []

    def back(tag, cot, w, xin_m, kind, slot):
        mix_cots.append(mm_nt("b0_dx" + tag, cot, w))
        mix_slots.append(slot)
        return mm_tn("b0_gw" + tag, xin_m, cot, kind)

    big.append(('rw_wr', back("r", dr_t, Wr, xr, "row", 0)))
    big.append(('rw_wk', back("k", dk, Wk, xk, "row", 2)))
    big.append(('rw_wv', back("v", dv_t, Wv, xv, "row", 3)))
    big.append(('rw_wo', gWo))
    dsg = mm_nt("b0_dsg", dg, G2)
    gG2 = mm_tn("b0_gg2", sg, dg, "col")
    (dgl,), _ = row_vjp("b0_sg", f_sigmoid, [gl], [], [dsg], row_mask=[True], const_mask=[], tr=TR, sr=16)
    gG1 = back("g", dgl, G1, xg, "row", 5)
    gW1, gW2, gA1, gA2 = [], [], [], []
    for d in range(2):
        dtw = mm_nt(f"b0_dtw{d}", dzw[d], W2.at(d))
        gW2.append(mm_tn(f"b0_gw2{d}", tw[d], dzw[d], "col"))
        (dwl,), _ = row_vjp(f"b0_tw{d}", f_tanh, [wl[d]], [], [dtw], row_mask=[True], const_mask=[], tr=TR, sr=16)
        gW1.append(back(f"w{d}", dwl, W1.at(d), xw, "row", 1))
        dal = mm_nt(f"b0_dal{d}", dza[d], A2.at(d))
        gA2.append(mm_tn(f"b0_ga2{d}", al[d], dza[d], "col"))
        gA1.append(back(f"a{d}", dal, A1.at(d), xa, "row", 4))
    big += [('rw_w1', gW1[0]), ('rw_w1', gW1[1]), ('rw_w2', gW2[0]), ('rw_w2', gW2[1]),
            ('rw_a1', gA1[0]), ('rw_a1', gA1[1]), ('rw_a2', gA2[0]), ('rw_a2', gA2[1]),
            ('rw_g1', gG1), ('rw_g2', gG2)]
    dh, dmix = mix_bwd("b0_mix", h, mix_f, mix_cots, mix_slots, L)
    (dxin,), (dn1g0, dsh1c, dsc1c, dsh1x, dsc1x) = row_vjp(
        "b0_norm", nm, [xin], nm_consts, [dh], row_mask=[True], const_mask=[True] * 5, tr=TR, sr=16)
    grad_x = sum_cast("b0_dx", [dxin, dx_a], F32, TR, offs=[nct, 0])

    zero = jnp.zeros((1, D), F32)
    parts = [dsh1x, dsc1x, dgt1_0, dsh2_0, dsc2_0, dgt2_0, dsh1c, dsc1c, zero, zero, zero, zero,
             dsh1_1, dsc1_1, dgt1_1, dsh2_1, dsc2_1, dgt2_1, zero, zero, zero, zero, zero, zero,
             dn1g0, dn1g1, dn2g0, dn2g1, dkkp, dka, drk, dlnw, dlnb, dfg,
             dmix[0:6], dw00, dw01, da00, da01, dconv[0:3]]
    got2 = all_gather8("ag_grads", pack_rows("pack_grads", parts, 48))
    small = sum_blocks("sum_grads", got2, list(range(N_DEV)))
    per_ex = got2[:, 0:24].reshape(N_DEV, 2, 2, 6 * D)
    tot = small[0:24].reshape(2, 2, 6 * D)
    cols = lambda a, width: lax.dynamic_slice_in_dim(a, s * width, width, axis=1)
    g_ada_w, dcond_parts = [], []
    for i in range(2):
        dm16 = cols(jnp.concatenate([per_ex[:, i, 0], _pad_rows(tot[i, 1][None], 8)], axis=0), 6 * Dq)
        g_ada_w.append(mm_tn(f"g_ada{i}", cond, dm16))
        dcond_parts.append(mm_nt(f"dcond{i}", dm16, ada.at(i)))
    g_ada_b = sum_cast("g_adab", [_pad_rows(tot[:, 0].reshape(12, D), 16), _pad_rows(tot[:, 1].reshape(12, D), 16)],
                       F32, 16)[0:12].reshape(2, 6 * D)
    dcond_mine = sum_cast("dcond_sum", dcond_parts, F32, 16)
    dcond = sum_blocks("dcond_chips", all_gather8("ag_dcond", dcond_mine), [0, 2, 4, 6])
    (dcin,), _ = row_vjp("b_cond", f_silu, [cond_in], [], [dcond], row_mask=[True], const_mask=[], tr=16, sr=16)

    rw_names = ('rw_wr', 'rw_wk', 'rw_wv', 'rw_wo', 'rw_w1', 'rw_w2', 'rw_a1', 'rw_a2', 'rw_g1', 'rw_g2')
    gsh = _reduce_scatter("rw", [it for it in big if it[0] in rw_names], ci_arr, s)
    gsh.update(_reduce_scatter("sc", [it for it in big if it[0] in ('sc_win', 'sc_wout')], ci_arr, s))
    gsh.update(_reduce_scatter("ffn", [('ffn_w13', gW13_0), ('ffn_w13', gW13_1), ('ffn_w2', gW2f0),
                                       ('ffn_w2', gW2f1)], ci_arr, s))

    grads = {}
    grads['c_ctx'] = dcin[8]
    grads['norm1_g'], grads['norm2_g'] = small[24:26], small[26:28]
    grads['ada_w'] = jnp.stack(g_ada_w)
    grads['ada_b'] = g_ada_b
    grads['rw_kk'], grads['rw_ka'], grads['rw_rk'] = small[28:29], small[29:30], small[30:31]
    grads['rw_lnw'], grads['rw_lnb'], grads['final_g'] = small[31:32], small[32:33], small[33]
    sharded = cols(small[34:48], Dq)
    grads['rw_mix'], grads['rw_w0'], grads['rw_a0'], grads['sc_conv'] = (
        sharded[0:6], sharded[6:8], sharded[8:10], sharded[10:13])

    outs_g, outs_d, outs_m, outs_v = [], [], [], []
    for n in WEIGHTS:
        shape = p[n].shape
        w2d, m2d, v2d = _view2d(n, p[n]), _view2d(n, p['m_' + n]), _view2d(n, p['v_' + n])
        if n in gsh:
            g2d, d_, m_, v_ = adamw_halves("adam_" + n, w2d, m2d, v2d, gsh[n][0], gsh[n][1], ci_arr)
        else:
            g2d = _view2d(n, grads[n].reshape(shape))
            d_, m_, v_ = adamw("adam_" + n, w2d, g2d, m2d, v2d)
        outs_g.append(g2d.reshape(shape))
        outs_d.append(d_.reshape(shape))
        outs_m.append(m_.reshape(shape))
        outs_v.append(v_.reshape(shape))
    return (loss, grad_x.reshape(1, T, D), *outs_g, *outs_d, *outs_m, *outs_v)


def kernel(x, c, ctx, c_ctx, norm1_g, norm2_g, ada_w, ada_b, rw_mix, rw_wr, rw_wk, rw_wv, rw_wo, rw_w0, rw_w1, rw_w2, rw_a0, rw_a1, rw_a2, rw_g1, rw_g2, rw_kk, rw_ka, rw_rk, rw_lnw, rw_lnb, sc_win, sc_conv, sc_wout, ffn_w13, ffn_w2, final_g, loss_target, m_c_ctx, m_norm1_g, m_norm2_g, m_ada_w, m_ada_b, m_rw_mix, m_rw_wr, m_rw_wk, m_rw_wv, m_rw_wo, m_rw_w0, m_rw_w1, m_rw_w2, m_rw_a0, m_rw_a1, m_rw_a2, m_rw_g1, m_rw_g2, m_rw_kk, m_rw_ka, m_rw_rk, m_rw_lnw, m_rw_lnb, m_sc_win, m_sc_conv, m_sc_wout, m_ffn_w13, m_ffn_w2, m_final_g, v_c_ctx, v_norm1_g, v_norm2_g, v_ada_w, v_ada_b, v_rw_mix, v_rw_wr, v_rw_wk, v_rw_wv, v_rw_wo, v_rw_w0, v_rw_w1, v_rw_w2, v_rw_a0, v_rw_a1, v_rw_a2, v_rw_g1, v_rw_g2, v_rw_kk, v_rw_ka, v_rw_rk, v_rw_lnw, v_rw_lnb, v_sc_win, v_sc_conv, v_sc_wout, v_ffn_w13, v_ffn_w2, v_final_g):
    values = (x, c, ctx, c_ctx, norm1_g, norm2_g, ada_w, ada_b, rw_mix, rw_wr, rw_wk, rw_wv, rw_wo, rw_w0, rw_w1, rw_w2, rw_a0, rw_a1, rw_a2, rw_g1, rw_g2, rw_kk, rw_ka, rw_rk, rw_lnw, rw_lnb, sc_win, sc_conv, sc_wout, ffn_w13, ffn_w2, final_g, loss_target, m_c_ctx, m_norm1_g, m_norm2_g, m_ada_w, m_ada_b, m_rw_mix, m_rw_wr, m_rw_wk, m_rw_wv, m_rw_wo, m_rw_w0, m_rw_w1, m_rw_w2, m_rw_a0, m_rw_a1, m_rw_a2, m_rw_g1, m_rw_g2, m_rw_kk, m_rw_ka, m_rw_rk, m_rw_lnw, m_rw_lnb, m_sc_win, m_sc_conv, m_sc_wout, m_ffn_w13, m_ffn_w2, m_final_g, v_c_ctx, v_norm1_g, v_norm2_g, v_ada_w, v_ada_b, v_rw_mix, v_rw_wr, v_rw_wk, v_rw_wv, v_rw_wo, v_rw_w0, v_rw_w1, v_rw_w2, v_rw_a0, v_rw_a1, v_rw_a2, v_rw_g1, v_rw_g2, v_rw_kk, v_rw_ka, v_rw_rk, v_rw_lnw, v_rw_lnb, v_sc_win, v_sc_conv, v_sc_wout, v_ffn_w13, v_ffn_w2, v_final_g)
    return _step(dict(zip(INPUTS, values)))
```

```python
import functools
import math

import jax
import jax.numpy as jnp
from jax import lax
from jax.experimental import pallas as pl
from jax.experimental.pallas import tpu as pltpu

F32 = jnp.float32
BF16 = jnp.bfloat16
MESH = pl.DeviceIdType.MESH

GRID_W = 64
HEAD = 64
LANES = 128
N_CHIPS = 4
N_DEV = 8
NORM_EPS = 1e-6
GN_EPS = 64e-5
ADAM_LR, ADAM_B1, ADAM_B2, ADAM_EPS, ADAM_WD, ADAM_STEP = 0.001, 0.9, 0.999, 1e-08, 0.01, 10
VMEM_LIMIT = 56 * 1024 * 1024
HI = lax.Precision.HIGHEST
WEIGHTS = ['c_ctx', 'norm1_g', 'norm2_g', 'ada_w', 'ada_b', 'rw_mix', 'rw_wr', 'rw_wk', 'rw_wv', 'rw_wo', 'rw_w0',
           'rw_w1', 'rw_w2', 'rw_a0', 'rw_a1', 'rw_a2', 'rw_g1', 'rw_g2', 'rw_kk', 'rw_ka', 'rw_rk', 'rw_lnw',
           'rw_lnb', 'sc_win', 'sc_conv', 'sc_wout', 'ffn_w13', 'ffn_w2', 'final_g']
INPUTS = (['x', 'c', 'ctx'] + WEIGHTS + ['loss_target'] + ['m_' + w for w in WEIGHTS]
          + ['v_' + w for w in WEIGHTS])


def _cparams(**kw):
    return pltpu.CompilerParams(vmem_limit_bytes=VMEM_LIMIT, **kw)


def _pick(dim, cands):
    for c in cands:
        if dim % c == 0:
            return c
    return dim


def _place():
    return lax.axis_index("x"), lax.axis_index("y"), lax.axis_index("c")


_TILE_M = (1024, 768, 512, 1408, 256, 128)
_TILE_N = (1408, 1024, 768, 512, 256, 128)
_TILE_K = (2816, 2304, 2048, 1536, 1408, 1152, 1024, 768, 704, 512, 256, 128)
MM_TILE_BYTES = 40 * 1024 * 1024


def _pick_k(unit, tm, tn, a_dtype, b_dtype, o_dtype):
    ab, bb, ob = (jnp.dtype(d).itemsize for d in (a_dtype, b_dtype, o_dtype))
    for tk in _TILE_K:
        if unit % tk == 0 and 2 * (tm * tk * ab + tk * tn * bb) + 2 * tm * tn * ob + tm * tn * 4 <= MM_TILE_BYTES:
            return tk
    return unit


class Stacked:
    def __init__(self, arr, kind, r, layer=0):
        self.arr, self.kind, self.r, self.layer = arr, kind, r, layer
        self.c = arr.shape[2]
        self.shape = {"row": (N_CHIPS * r, self.c), "col": (r, N_CHIPS * self.c), "layer": (r, self.c)}[kind]

    def at(self, layer):
        return Stacked(self.arr, self.kind, self.r, layer)

    def spec(self, t0, t1, swap):
        r, c, layer = self.r, self.c, self.layer
        per_r, per_c = r // t0, c // t1
        assert r % t0 == 0 and c % t1 == 0
        kind = self.kind

        def index(i, j, k):
            ri, ci = (j, k) if swap else (k, j)
            if kind == "row":
                return (ri // per_r, layer * per_r + ri % per_r, ci)
            if kind == "layer":
                return (layer, ri, ci)
            return (ci // per_c, layer * per_r + ri, ci % per_c)

        return pl.BlockSpec((None, t0, t1), index)


def _mm_body(dims, nk, a_ref, b_ref, o_ref, acc_ref):
    if nk == 1:
        o_ref[...] = lax.dot_general(a_ref[...].astype(BF16), b_ref[...].astype(BF16), (dims, ((), ())),
                                     preferred_element_type=F32).astype(o_ref.dtype)
        return
    k = pl.program_id(2)

    @pl.when(k == 0)
    def _():
        acc_ref[...] = jnp.zeros_like(acc_ref)

    acc_ref[...] += lax.dot_general(a_ref[...].astype(BF16), b_ref[...].astype(BF16), (dims, ((), ())),
                                    preferred_element_type=F32)

    @pl.when(k == nk - 1)
    def _():
        o_ref[...] = acc_ref[...].astype(o_ref.dtype)


def _mm_call(name, dims, grid, in_specs, out_spec, out_shape, acc_shape, operands):
    return pl.pallas_call(
        functools.partial(_mm_body, dims, grid[2]), name=name, grid=grid, in_specs=in_specs, out_specs=out_spec,
        out_shape=out_shape, scratch_shapes=[pltpu.VMEM(acc_shape if grid[2] > 1 else (8, LANES), F32)],
        compiler_params=_cparams(dimension_semantics=("parallel", "parallel", "arbitrary")),
    )(*operands)


def mm_nn(name, a, b, out_dtype=F32):
    M, K = a.shape
    st = isinstance(b, Stacked)
    N = b.shape[1]
    tm = _pick(M, _TILE_M)
    tn = _pick(b.c if st and b.kind == "col" else N, _TILE_N)
    tk = _pick_k(b.r if st else K, tm, tn, a.dtype, b.arr.dtype if st else b.dtype, out_dtype)
    b_spec = b.spec(tk, tn, False) if st else pl.BlockSpec((tk, tn), lambda i, j, k: (k, j))
    return _mm_call(name, ((1,), (0,)), (M // tm, N // tn, K // tk),
                    [pl.BlockSpec((tm, tk), lambda i, j, k: (i, k)), b_spec],
                    pl.BlockSpec((tm, tn), lambda i, j, k: (i, j)), jax.ShapeDtypeStruct((M, N), out_dtype),
                    (tm, tn), (a, b.arr if st else b))


def mm_nt(name, a, b, out_dtype=F32):
    M, N = a.shape
    st = isinstance(b, Stacked)
    K = b.shape[0]
    tm = _pick(M, _TILE_M)
    to = _pick(b.r if st else K, _TILE_N)
    tc = _pick_k(b.c if st and b.kind == "col" else N, tm, to, a.dtype, b.arr.dtype if st else b.dtype, out_dtype)
    b_spec = b.spec(to, tc, True) if st else pl.BlockSpec((to, tc), lambda i, j, k: (j, k))
    return _mm_call(name, ((1,), (1,)), (M // tm, K // to, N // tc),
                    [pl.BlockSpec((tm, tc), lambda i, j, k: (i, k)), b_spec],
                    pl.BlockSpec((tm, to), lambda i, j, k: (i, j)), jax.ShapeDtypeStruct((M, K), out_dtype),
                    (tm, to), (a, b.arr if st else b))


def mm_tn(name, a, b, kind=None):
    R, M = a.shape
    N = b.shape[1]
    r, c = (M // N_CHIPS, N) if kind == "row" else (M, N // N_CHIPS) if kind == "col" else (M, N)
    tm, tn = _pick(r, _TILE_M), _pick(c, _TILE_N)
    tk = _pick_k(R, tm, tn, a.dtype, b.dtype, F32)
    if kind:
        per_r, per_c = r // tm, c // tn
        if kind == "row":
            o_spec = pl.BlockSpec((None, tm, tn), lambda i, j, k: (i // per_r, i % per_r, j))
        else:
            o_spec = pl.BlockSpec((None, tm, tn), lambda i, j, k: (j // per_c, i, j % per_c))
        o_shape = jax.ShapeDtypeStruct((N_CHIPS, r, c), F32)
    else:
        o_spec = pl.BlockSpec((tm, tn), lambda i, j, k: (i, j))
        o_shape = jax.ShapeDtypeStruct((M, N), F32)
    return _mm_call(name, ((0,), (0,)), (M // tm, N // tn, R // tk),
                    [pl.BlockSpec((tk, tm), lambda i, j, k: (k, i)), pl.BlockSpec((tk, tn), lambda i, j, k: (k, j))],
                    o_spec, o_shape, (tm, tn), (a, b))


def _shifted(o):
    return lambda i: (i + o, 0)


def row_call(name, f, rows, consts, outs, *, tr, sr, offs=None):
    offs = offs or [0] * len(rows)
    n_rows = min(r.shape[0] - o * tr for r, o in zip(rows, offs))
    nr, nc = len(rows), len(consts)

    def body(*refs):
        row_refs, const_refs, out_refs = refs[:nr], refs[nr:nr + nc], refs[nr + nc:]
        i = pl.program_id(0)
        cvals = [r[...] for r in const_refs]

        def step(j, carry):
            sl = pl.ds(pl.multiple_of(j * sr, sr), sr)
            res = f(i, *[r[sl, :] for r in row_refs], *cvals)
            for o, v in zip(out_refs, res):
                o[sl, :] = v.astype(o.dtype)
            return carry

        lax.fori_loop(0, tr // sr, step, 0)

    in_specs = [pl.BlockSpec((tr, r.shape[1]), _shifted(o)) for r, o in zip(rows, offs)]
    in_specs += [pl.BlockSpec(c.shape, lambda i: (0, 0)) for c in consts]
    return pl.pallas_call(
        body, name=name, grid=(n_rows // tr,), in_specs=in_specs,
        out_specs=[pl.BlockSpec((tr, w), lambda i: (i, 0)) for w, _ in outs],
        out_shape=[jax.ShapeDtypeStruct((n_rows, w), dt) for w, dt in outs],
        compiler_params=_cparams(dimension_semantics=("parallel",)),
    )(*rows, *consts)


def row_vjp(name, f, rows, consts, cots, *, row_mask, const_mask, tr, sr, offs=None, bf16_rows=()):
    offs = offs or [0] * len(rows)
    n_rows = min(r.shape[0] - o * tr for r, o in zip(rows, offs))
    nr, nc = len(rows), len(consts)
    cot_in = [c for c in cots if c is not None]
    nct = len(cot_in)
    d_rows = [i for i in range(nr) if row_mask[i]]
    d_consts = [i for i in range(nc) if const_mask[i]]

    def body(*refs):
        row_refs, const_refs = refs[:nr], refs[nr:nr + nc]
        cot_refs = refs[nr + nc:nr + nc + nct]
        drow_refs = refs[nr + nc + nct:nr + nc + nct + len(d_rows)]
        dconst_refs = refs[nr + nc + nct + len(d_rows):]
        i = pl.program_id(0)

        @pl.when(i == 0)
        def _():
            for r in dconst_refs:
                r[...] = jnp.zeros_like(r)

        cvals = [r[...] for r in const_refs]

        def step(j, carry):
            sl = pl.ds(pl.multiple_of(j * sr, sr), sr)
            rvals = [r[sl, :] for r in row_refs]

            def g(*diff):
                rv, cv = list(rvals), list(cvals)
                for idx, val in zip(d_rows, diff[:len(d_rows)]):
                    rv[idx] = val
                for idx, val in zip(d_consts, diff[len(d_rows):]):
                    cv[idx] = val
                return f(i, *rv, *cv)

            primals = [rvals[idx].astype(F32) for idx in d_rows] + [cvals[idx] for idx in d_consts]
            res, vjp = jax.vjp(g, *primals)
            it = iter(cot_refs)
            cts = tuple(jnp.zeros_like(o) if c is None else next(it)[sl, :].astype(o.dtype) for o, c in zip(res, cots))
            grads = vjp(cts)
            for r, val in zip(drow_refs, grads[:len(d_rows)]):
                r[sl, :] = val.astype(r.dtype)
            for r, val in zip(dconst_refs, grads[len(d_rows):]):
                r[...] += val
            return carry

        lax.fori_loop(0, tr // sr, step, 0)

    in_specs = [pl.BlockSpec((tr, r.shape[1]), _shifted(o)) for r, o in zip(rows, offs)]
    in_specs += [pl.BlockSpec(c.shape, lambda i: (0, 0)) for c in consts]
    in_specs += [pl.BlockSpec((tr, c.shape[1]), lambda i: (i, 0)) for c in cot_in]
    out_specs = [pl.BlockSpec((tr, rows[i].shape[1]), lambda i: (i, 0)) for i in d_rows]
    out_specs += [pl.BlockSpec(consts[i].shape, lambda i: (0, 0)) for i in d_consts]
    out_shape = [jax.ShapeDtypeStruct((n_rows, rows[i].shape[1]), BF16 if i in bf16_rows else F32) for i in d_rows]
    out_shape += [jax.ShapeDtypeStruct(consts[i].shape, F32) for i in d_consts]
    res = pl.pallas_call(
        body, name=name, grid=(n_rows // tr,), in_specs=in_specs, out_specs=out_specs, out_shape=out_shape,
        compiler_params=_cparams(dimension_semantics=("arbitrary",)),
    )(*rows, *consts, *cot_in)
    return list(res[:len(d_rows)]), list(res[len(d_rows):])


def _sigmoid(x):
    return 1.0 / (1.0 + jnp.exp(-x))


def _softplus(u):
    return jnp.maximum(u, 0.0) + jnp.log(1.0 + jnp.exp(-jnp.abs(u)))


def _rms(x, g):
    ms = jnp.sum(x * x, axis=-1, keepdims=True) * (1.0 / x.shape[-1])
    return x * lax.rsqrt(ms + NORM_EPS) * g


def _hsum_impl(x, ones2):
    rows, width = x.shape
    nch = width // LANES
    xs = jnp.concatenate([x[:, j * LANES:(j + 1) * LANES] for j in range(nch)], axis=0)
    hi = xs.astype(BF16)
    lo = (xs - hi.astype(F32)).astype(BF16)
    ys = jnp.dot(jnp.concatenate([hi, lo], axis=1), ones2, preferred_element_type=F32)
    return jnp.concatenate([ys[j * rows:(j + 1) * rows] for j in range(nch)], axis=1)


@jax.custom_vjp
def _hsum(x, ones2):
    return _hsum_impl(x, ones2)


def _hsum_fwd(x, ones2):
    return _hsum_impl(x, ones2), ones2


def _hsum_bwd(ones2, g):
    return _hsum_impl(g, ones2), jnp.zeros_like(ones2)


_hsum.defvjp(_hsum_fwd, _hsum_bwd)


def f_silu(i, x):
    return (x * _sigmoid(x),)


def f_sigmoid(i, x):
    return (_sigmoid(x),)


def f_tanh(i, x):
    return (jnp.tanh(x),)


def f_norm_mod(n_ctx_tiles, i, xin, g, sh_c, sc_c, sh_x, sc_x):
    is_x = i >= n_ctx_tiles
    sh = jnp.where(is_x, sh_x, sh_c)
    sc = jnp.where(is_x, sc_x, sc_c)
    return (_rms(xin, g) * (1.0 + sc) + sh,)


def f_res_norm_mod(i, x, y, gt, g, sh, sc):
    x1 = x + gt * y
    return x1, _rms(x1, g) * (1.0 + sc) + sh


def f_post(i, k, zw0, zw1, za0, za1, kkp, ka, w00, w01, a00, a01, ones2):
    kq = k * kkp
    kk = kq / jnp.maximum(jnp.sqrt(_hsum(kq * kq, ones2)), 1e-12)

    def direction(zw, za, w0, a0):
        log_w = -_softplus(-(w0 + zw)) - 0.5
        a = _sigmoid(a0 + za)
        return jnp.exp(-jnp.exp(log_w)), k * (1.0 + (a - 1.0) * ka), kk * a

    dec0, kd0, bb0 = direction(zw0, za0, w00, a00)
    dec1, kd1, bb1 = direction(zw1, za1, w01, a01)
    return -kk, -kk, dec0, dec1, kd0, kd1, bb0, bb1, kd0 + kd1


def f_post_fwd(*a):
    return f_post(*a)[1:]


def f_readout(i, y0, y1, r, ksum, v, g, rk, lnw, lnb, ones2):
    y = y0 + y1
    yc = y - _hsum(y, ones2) * (1.0 / HEAD)
    var = _hsum(yc * yc, ones2) * (1.0 / HEAD)
    o = yc * lax.rsqrt(var + GN_EPS) * lnw + lnb
    o = o + _hsum(r * ksum * rk, ones2) * v
    return (o * g,)


def f_sum(i, *xs):
    acc = xs[0].astype(F32)
    for x in xs[1:]:
        acc = acc + x.astype(F32)
    return (acc,)


def sum_cast(name, arrs, dtype, tr, offs=None):
    return row_call(name, f_sum, arrs, [], [(arrs[0].shape[1], dtype)], tr=tr, sr=16, offs=offs)[0]


def swiglu_fwd(name, ab, tr):
    T, F2 = ab.shape
    F = F2 // 2
    sr = 16

    def body(ab_ref, o_ref):
        def step(j, carry):
            sl = pl.ds(pl.multiple_of(j * sr, sr), sr)
            a, b = ab_ref[sl, :F], ab_ref[sl, F:]
            o_ref[sl, :] = (a * _sigmoid(a) * b).astype(o_ref.dtype)
            return carry

        lax.fori_loop(0, tr // sr, step, 0)

    return pl.pallas_call(
        body, name=name, grid=(T // tr,), in_specs=[pl.BlockSpec((tr, F2), lambda i: (i, 0))],
        out_specs=pl.BlockSpec((tr, F), lambda i: (i, 0)), out_shape=jax.ShapeDtypeStruct((T, F), BF16),
        compiler_params=_cparams(dimension_semantics=("parallel",)),
    )(ab)


def swiglu_bwd(name, ab, dsw, tr):
    T, F2 = ab.shape
    F = F2 // 2
    sr = 16

    def body(ab_ref, d_ref, o_ref):
        def step(j, carry):
            sl = pl.ds(pl.multiple_of(j * sr, sr), sr)
            a, b, d = ab_ref[sl, :F], ab_ref[sl, F:], d_ref[sl, :]
            sg = _sigmoid(a)
            o_ref[sl, :F] = (d * b * (sg + a * sg * (1.0 - sg))).astype(o_ref.dtype)
            o_ref[sl, F:] = (d * a * sg).astype(o_ref.dtype)
            return carry

        lax.fori_loop(0, tr // sr, step, 0)

    return pl.pallas_call(
        body, name=name, grid=(T // tr,),
        in_specs=[pl.BlockSpec((tr, F2), lambda i: (i, 0)), pl.BlockSpec((tr, F), lambda i: (i, 0))],
        out_specs=pl.BlockSpec((tr, F2), lambda i: (i, 0)), out_shape=jax.ShapeDtypeStruct((T, F2), BF16),
        compiler_params=_cparams(dimension_semantics=("parallel",)),
    )(ab, dsw)


def final_call(name, x3, f1, gt, fg, tgt, tr):
    T, D = x3.shape
    sr = 16

    def f(x, y, gtv, g, t):
        err = _rms(x + gtv * y, g) - t
        return 0.5 * jnp.sum(err * err) * (1.0 / D)

    def body(x_ref, y_ref, gt_ref, g_ref, t_ref, dx_ref, dy_ref, dgt_ref, dg_ref, loss_ref):
        @pl.when(pl.program_id(0) == 0)
        def _():
            dgt_ref[...] = jnp.zeros_like(dgt_ref)
            dg_ref[...] = jnp.zeros_like(dg_ref)
            loss_ref[...] = jnp.zeros_like(loss_ref)

        def step(j, carry):
            sl = pl.ds(pl.multiple_of(j * sr, sr), sr)
            val, vjp = jax.vjp(lambda x, y, a, b: f(x, y, a, b, t_ref[sl, :]), x_ref[sl, :], y_ref[sl, :],
                               gt_ref[...], g_ref[...])
            dx, dy, dgt, dg = vjp(jnp.ones((), F32))
            dx_ref[sl, :] = dx
            dy_ref[sl, :] = dy.astype(dy_ref.dtype)
            dgt_ref[...] += dgt
            dg_ref[...] += dg
            loss_ref[...] += jnp.full(loss_ref.shape, val, F32)
            return carry

        lax.fori_loop(0, tr // sr, step, 0)

    row = pl.BlockSpec((tr, D), lambda i: (i, 0))
    vec = pl.BlockSpec((1, D), lambda i: (0, 0))
    return pl.pallas_call(
        body, name=name, grid=(T // tr,), in_specs=[row, row, vec, vec, row],
        out_specs=[row, row, vec, vec, pl.BlockSpec((8, LANES), lambda i: (0, 0))],
        out_shape=[jax.ShapeDtypeStruct((T, D), F32), jax.ShapeDtypeStruct((T, D), BF16)]
        + [jax.ShapeDtypeStruct((1, D), F32)] * 2
        + [jax.ShapeDtypeStruct((8, LANES), F32)],
        compiler_params=_cparams(dimension_semantics=("arbitrary",)),
    )(x3, f1, gt, fg, tgt)


def _tshift(x, kind, period):
    n = x.shape[0]
    t = lax.broadcasted_iota(jnp.int32, x.shape, 0)
    if kind == 0:
        return jnp.where((t & (period - 1)) == 0, 0.0, pltpu.roll(x, 1, 0))
    if kind == 1:
        return jnp.where(((t & (period - 1)) == period - 1) | (t == n - 1), 0.0, pltpu.roll(x, n - 1, 0))
    if kind == 2:
        return jnp.where(t < GRID_W, 0.0, pltpu.roll(x, GRID_W, 0))
    return jnp.where(t >= n - GRID_W, 0.0, pltpu.roll(x, n - GRID_W, 0))


def _pow2_at_least(n):
    return 1 << (n - 1).bit_length()


def _shift_into(dst_ref, h_ref, n_ctx, cb, D, transpose):
    j = pl.program_id(0)
    quarter = (j * cb * 4) // D
    half = (j * cb * 2) // D
    flip = 1 if transpose else 0
    for q in range(4):
        @pl.when(quarter == q)
        def _(q=q):
            dst_ref[n_ctx:, :] = _tshift(h_ref[n_ctx:, :], q ^ flip, GRID_W)
    for q in range(2):
        @pl.when(half == q)
        def _(q=q):
            dst_ref[:n_ctx, :] = _tshift(h_ref[:n_ctx, :], q ^ flip, _pow2_at_least(n_ctx))


def mix_fwd(name, h, mix, n_ctx):
    R, D = h.shape
    cb = LANES

    def body(h_ref, mix_ref, *rest):
        outs, hs_ref = rest[:6], rest[6]
        _shift_into(hs_ref, h_ref, n_ctx, cb, D, False)
        hv = h_ref[...]
        xx = hs_ref[...] - hv
        for m in range(6):
            outs[m][...] = (hv + xx * mix_ref[m:m + 1, :]).astype(BF16)

    col = pl.BlockSpec((R, cb), lambda j: (0, j))
    return pl.pallas_call(
        body, name=name, grid=(D // cb,), in_specs=[col, pl.BlockSpec((mix.shape[0], cb), lambda j: (0, j))],
        out_specs=[col] * 6, out_shape=[jax.ShapeDtypeStruct((R, D), BF16)] * 6,
        scratch_shapes=[pltpu.VMEM((R, cb), F32)],
        compiler_params=_cparams(dimension_semantics=("parallel",)),
    )(h, mix)


def mix_bwd(name, h, mix, cots, slots, n_ctx):
    R, D = h.shape
    cb = LANES
    nc = len(cots)

    def body(h_ref, mix_ref, *rest):
        cot_refs, dh_ref, dmix_ref, hs_ref, dxx_ref = rest[:nc], rest[nc], rest[nc + 1], rest[nc + 2], rest[nc + 3]
        _shift_into(hs_ref, h_ref, n_ctx, cb, D, False)
        xx = hs_ref[...] - h_ref[...]
        per_slot = [None] * 6
        for cref, m in zip(cot_refs, slots):
            per_slot[m] = cref[...] if per_slot[m] is None else per_slot[m] + cref[...]
        dh = jnp.zeros((R, cb), F32)
        dxx = jnp.zeros((R, cb), F32)
        rows = []
        for m in range(6):
            d = per_slot[m]
            dh = dh + d
            dxx = dxx + d * mix_ref[m:m + 1, :]
            rows.append(jnp.sum(d * xx, axis=0, keepdims=True))
        dmix_ref[...] = jnp.concatenate(rows + [jnp.zeros((2, cb), F32)], axis=0)
        dxx_ref[...] = dxx
        _shift_into(hs_ref, dxx_ref, n_ctx, cb, D, True)
        dh_ref[...] = dh - dxx + hs_ref[...]

    col = pl.BlockSpec((R, cb), lambda j: (0, j))
    return pl.pallas_call(
        body, name=name, grid=(D // cb,), in_specs=[col, pl.BlockSpec((mix.shape[0], cb), lambda j: (0, j))] + [col] * nc,
        out_specs=[col, pl.BlockSpec((8, cb), lambda j: (0, j))],
        out_shape=[jax.ShapeDtypeStruct((R, D), F32), jax.ShapeDtypeStruct((8, D), F32)],
        scratch_shapes=[pltpu.VMEM((R, cb), F32), pltpu.VMEM((R, cb), F32)],
        compiler_params=_cparams(dimension_semantics=("parallel",)),
    )(h, mix, *cots)


def _conv_parts(gb_ref, gc_ref, u_ref, cw_ref):
    T = gb_ref.shape[0]
    z = gc_ref[...] * u_ref[...]
    zp, zn = _tshift(z, 0, _pow2_at_least(T)), _tshift(z, 1, _pow2_at_least(T))
    conv = zp * cw_ref[0:1, :] + z * cw_ref[1:2, :] + zn * cw_ref[2:3, :]
    return z, zp, zn, conv


def conv_fwd(name, gcu, cw):
    T, D3 = gcu.shape
    D = D3 // 3
    cb = LANES
    nb = D // cb

    def body(gb_ref, gc_ref, u_ref, cw_ref, o_ref):
        _, _, _, conv = _conv_parts(gb_ref, gc_ref, u_ref, cw_ref)
        o_ref[...] = (gb_ref[...] * conv).astype(BF16)

    def part(p):
        return pl.BlockSpec((T, cb), lambda j: (0, j + p * nb))

    return pl.pallas_call(
        body, name=name, grid=(nb,),
        in_specs=[part(0), part(1), part(2), pl.BlockSpec((cw.shape[0], cb), lambda j: (0, j))],
        out_specs=pl.BlockSpec((T, cb), lambda j: (0, j)), out_shape=jax.ShapeDtypeStruct((T, D), BF16),
        compiler_params=_cparams(dimension_semantics=("parallel",)),
    )(gcu, gcu, gcu, cw)


def conv_bwd(name, gcu, cw, dp):
    T, D3 = gcu.shape
    D = D3 // 3
    cb = LANES
    nb = D // cb

    def body(gb_ref, gc_ref, u_ref, cw_ref, dp_ref, o_ref, dcw_ref):
        part = pl.program_id(1)
        z, zp, zn, conv = _conv_parts(gb_ref, gc_ref, u_ref, cw_ref)
        dpv = dp_ref[...]
        dconv = dpv * gb_ref[...]
        period = _pow2_at_least(T)
        dz = (_tshift(dconv * cw_ref[0:1, :], 1, period) + dconv * cw_ref[1:2, :]
              + _tshift(dconv * cw_ref[2:3, :], 0, period))

        @pl.when(part == 0)
        def _():
            o_ref[...] = (dpv * conv).astype(o_ref.dtype)
            dcw_ref[...] = jnp.concatenate(
                [jnp.sum(dconv * s, axis=0, keepdims=True) for s in (zp, z, zn)] + [jnp.zeros((5, cb), F32)], axis=0)

        @pl.when(part == 1)
        def _():
            o_ref[...] = (dz * u_ref[...]).astype(o_ref.dtype)

        @pl.when(part == 2)
        def _():
            o_ref[...] = (dz * gc_ref[...]).astype(o_ref.dtype)

    def part_spec(p):
        return pl.BlockSpec((T, cb), lambda j, q: (0, j + p * nb))

    return pl.pallas_call(
        body, name=name, grid=(nb, 3),
        in_specs=[part_spec(0), part_spec(1), part_spec(2), pl.BlockSpec((cw.shape[0], cb), lambda j, q: (0, j)),
                  pl.BlockSpec((T, cb), lambda j, q: (0, j))],
        out_specs=[pl.BlockSpec((T, cb), lambda j, q: (0, j + q * nb)), pl.BlockSpec((8, cb), lambda j, q: (0, j))],
        out_shape=[jax.ShapeDtypeStruct((T, D3), BF16), jax.ShapeDtypeStruct((8, D), F32)],
        compiler_params=_cparams(dimension_semantics=("arbitrary", "arbitrary")),
    )(gcu, gcu, gcu, cw, dp)


SCAN_TC = 8


def _scan_consts():
    rows = lax.broadcasted_iota(jnp.int32, (HEAD, LANES), 0)
    cols = lax.broadcasted_iota(jnp.int32, (HEAD, LANES), 1)
    eye = rows == (cols & (HEAD - 1))
    r2 = lax.broadcasted_iota(jnp.int32, (2 * LANES, LANES), 0)
    c2 = lax.broadcasted_iota(jnp.int32, (2 * LANES, LANES), 1)
    ones2 = (((r2 & (LANES - 1)) >= HEAD) == (c2 >= HEAD)).astype(BF16)
    return eye, ones2, ones2[:LANES]


SCAN_ROW_CHUNKS = 4


def _chunks_of_heads(hp):
    per = max(1, hp // SCAN_ROW_CHUNKS)
    return [range(lo, lo + per) for lo in range(0, hp, per)]


def _rows_of(heads):
    return pl.ds(heads[0] * HEAD, len(heads) * HEAD)


def _split2(p):
    hi = p.astype(BF16)
    lo = (p - hi.astype(F32)).astype(BF16)
    return jnp.concatenate([hi, lo], axis=1)


def _head_rows(h):
    return pl.ds(h * HEAD, HEAD)


def _expand_into(dst, p1_ref, row_ref, t, hp, eye, ones1):
    for h in range(hp):
        p1_ref[_head_rows(h), :] = jnp.where(eye, row_ref[t, h:h + 1, :], 0.0).astype(BF16)
    dst[...] = jnp.dot(p1_ref[...], ones1, preferred_element_type=F32)


def _colsum_store(ref, t, h, x):
    ref[t, pl.ds(h, 1), :] = jnp.sum(x, axis=0, keepdims=True)


def _order(i, n_ctx, n_all, rev):
    if not rev:
        return i
    return jnp.where(i < n_ctx, n_ctx - 1 - i, n_all - 1 - (i - n_ctx))


def scan_fwd(name, r, w, k, v, a, b, n_ctx_rows, rev):
    R, D = r.shape
    hp, tc = D // LANES, SCAN_TC
    n_all, n_ctx = R // tc, n_ctx_rows // tc
    ins = [t.reshape(R, hp, LANES) for t in (r, w, k, v, a, b)]

    def body(r_ref, w_ref, k_ref, v_ref, a_ref, b_ref, y_ref, st_ref, s_ref, ve_ref, sa_ref, p_ref, p1_ref):
        @pl.when(pl.program_id(0) == 0)
        def _():
            s_ref[...] = jnp.zeros_like(s_ref)

        eye, ones2, ones1 = _scan_consts()

        def row_of(q):
            return tc - 1 - q if rev else q

        for q in range(tc):
            _expand_into(ve_ref.at[q], p1_ref.at[q % 2], v_ref, row_of(q), hp, eye, ones1)

        def advance(q, prev_ref):
            t = row_of(q)
            for heads in _chunks_of_heads(hp):
                rows = _rows_of(heads)
                for h in heads:
                    p_ref[q % 2, _head_rows(h), :] = _split2(prev_ref[_head_rows(h), :] * a_ref[t, h:h + 1, :])
                sa_ref[q % 2, rows, :] = jnp.dot(p_ref[q % 2, rows, :], ones2, preferred_element_type=F32)
                for h in heads:
                    hr_ = _head_rows(h)
                    st_ref[q, hr_, :] = (prev_ref[hr_, :] * w_ref[t, h:h + 1, :]
                                         + sa_ref[q % 2, hr_, :] * b_ref[t, h:h + 1, :]
                                         + ve_ref[q, hr_, :] * k_ref[t, h:h + 1, :])

        advance(0, s_ref)
        for q in range(1, tc):
            advance(q, st_ref.at[q - 1])
        s_ref[...] = st_ref[tc - 1]

        for q in range(tc):
            t = row_of(q)
            for h in range(hp):
                p1_ref[q % 2, _head_rows(h), :] = (st_ref[q, _head_rows(h), :] * r_ref[t, h:h + 1, :]).astype(BF16)
            sa_ref[q % 2] = jnp.dot(p1_ref[q % 2], ones1, preferred_element_type=F32)
            for h in range(hp):
                _colsum_store(y_ref, t, h, jnp.where(eye, sa_ref[q % 2, _head_rows(h), :], 0.0))

    row_spec = pl.BlockSpec((tc, hp, LANES), lambda i: (_order(i, n_ctx, n_all, rev), 0, 0))
    n = hp * HEAD
    y, st = pl.pallas_call(
        body, name=name, grid=(n_all,), in_specs=[row_spec] * 6,
        out_specs=[row_spec, pl.BlockSpec((tc, n, LANES), lambda i: (i, 0, 0))],
        out_shape=[jax.ShapeDtypeStruct((R, hp, LANES), F32), jax.ShapeDtypeStruct((R, n, LANES), F32)],
        scratch_shapes=[pltpu.VMEM((n, LANES), F32), pltpu.VMEM((tc, n, LANES), F32), pltpu.VMEM((2, n, LANES), F32),
                        pltpu.VMEM((2, n, 2 * LANES), BF16), pltpu.VMEM((2, n, LANES), BF16)],
        compiler_params=_cparams(dimension_semantics=("arbitrary",)),
    )(*ins)
    return y.reshape(R, D), st


def scan_bwd(name, r, w, k, v, a, b, dy, st, n_ctx_rows, rev):
    R, D = r.shape
    hp, tc = D // LANES, SCAN_TC
    n_all, n_ctx = R // tc, n_ctx_rows // tc
    ins = [t.reshape(R, hp, LANES) for t in (r, w, k, v, a, b, dy)]

    def body(r_ref, w_ref, k_ref, v_ref, a_ref, b_ref, dy_ref, st_ref, prev_ref,
             dr_ref, dw_ref, dk_ref, dv_ref, da_ref, db_ref,
             g_ref, s0_ref, ve_ref, dye_ref, sa_ref, gs_ref, tmp_ref, p_ref, p1_ref):
        i = pl.program_id(0)

        @pl.when(i == 0)
        def _():
            g_ref[...] = jnp.zeros_like(g_ref)

        eye, ones2, ones1 = _scan_consts()

        @pl.when(i == n_all - 1)
        def _():
            s0_ref[...] = jnp.zeros_like(s0_ref)

        @pl.when(i != n_all - 1)
        def _():
            s0_ref[...] = prev_ref[0]

        def row_of(q):
            return tc - 1 - q if rev else q

        def prev_of(q):
            return s0_ref if q == 0 else st_ref.at[q - 1]

        for q in range(tc):
            t, prev = row_of(q), prev_of(q)
            _expand_into(ve_ref.at[q], p1_ref.at[2 + q % 2], v_ref, t, hp, eye, ones1)
            _expand_into(dye_ref.at[q], p1_ref.at[4 + q % 2], dy_ref, t, hp, eye, ones1)
            for h in range(hp):
                p1_ref[q % 2, _head_rows(h), :] = (prev[_head_rows(h), :] * a_ref[t, h:h + 1, :]).astype(BF16)
            sa_ref[q] = jnp.dot(p1_ref[q % 2], ones1, preferred_element_type=F32)

        for q in reversed(range(tc)):
            t, prev = row_of(q), prev_of(q)
            for heads in _chunks_of_heads(hp):
                rows = _rows_of(heads)
                for h in heads:
                    hr_ = _head_rows(h)
                    g = g_ref[hr_, :] + dye_ref[q, hr_, :] * r_ref[t, h:h + 1, :]
                    gs_ref[q, hr_, :] = g
                    p_ref[q % 2, hr_, :] = _split2(g * b_ref[t, h:h + 1, :])
                tmp_ref[q % 2, rows, :] = jnp.dot(p_ref[q % 2, rows, :], ones2, preferred_element_type=F32)
                for h in heads:
                    hr_ = _head_rows(h)
                    dsa = tmp_ref[q % 2, hr_, :]
                    _colsum_store(da_ref, t, h, prev[hr_, :] * dsa)
                    g_ref[hr_, :] = gs_ref[q, hr_, :] * w_ref[t, h:h + 1, :] + dsa * a_ref[t, h:h + 1, :]

        for q in range(tc):
            t, prev = row_of(q), prev_of(q)
            for h in range(hp):
                hr_ = _head_rows(h)
                g = gs_ref[q, hr_, :]
                p1_ref[q % 2, hr_, :] = (g * k_ref[t, h:h + 1, :]).astype(BF16)
                _colsum_store(dr_ref, t, h, st_ref[q, hr_, :] * dye_ref[q, hr_, :])
                _colsum_store(dk_ref, t, h, g * ve_ref[q, hr_, :])
                _colsum_store(dw_ref, t, h, g * prev[hr_, :])
                _colsum_store(db_ref, t, h, g * sa_ref[q, hr_, :])
            tmp_ref[q % 2] = jnp.dot(p1_ref[q % 2], ones1, preferred_element_type=F32)
            for h in range(hp):
                _colsum_store(dv_ref, t, h, jnp.where(eye, tmp_ref[q % 2, _head_rows(h), :], 0.0))

    def pos(i):
        return n_all - 1 - i

    n = hp * HEAD
    row_spec = pl.BlockSpec((tc, hp, LANES), lambda i: (_order(pos(i), n_ctx, n_all, rev), 0, 0))
    big = pltpu.VMEM((tc, n, LANES), F32)
    one = pltpu.VMEM((n, LANES), F32)
    outs = pl.pallas_call(
        body, name=name, grid=(n_all,),
        in_specs=[row_spec] * 7 + [
            pl.BlockSpec((tc, n, LANES), lambda i: (pos(i), 0, 0)),
            pl.BlockSpec((1, n, LANES), lambda i: (jnp.maximum(pos(i) * tc - 1, 0), 0, 0))],
        out_specs=[row_spec] * 6,
        out_shape=[jax.ShapeDtypeStruct((R, hp, LANES), F32)] * 6,
        scratch_shapes=[one, one, big, big, big, big, pltpu.VMEM((2, n, LANES), F32),
                        pltpu.VMEM((2, n, 2 * LANES), BF16), pltpu.VMEM((6, n, LANES), BF16)],
        compiler_params=_cparams(dimension_semantics=("arbitrary",)),
    )(*ins, st, st)
    return [o.reshape(R, D) for o in outs]


ANY = pl.BlockSpec(memory_space=pl.ANY)


def _peer(xi, yi, ci, k):
    return (1 - xi if k & 4 else xi, 1 - yi if k & 2 else yi, 1 - ci if k & 1 else ci)


def _rcopy(src, dst, send_sem, recv_sem, dev):
    return pltpu.make_async_remote_copy(src_ref=src, dst_ref=dst, send_sem=send_sem, recv_sem=recv_sem,
                                        device_id=dev, device_id_type=MESH)


def _drain(copies):
    for cp in copies:
        if cp.is_remote:
            cp.wait_send()
        else:
            cp.wait()


def all_gather8(name, x):
    r, c = x.shape

    def body(x_ref, out_ref, send_sems, recv_sems, local_sem):
        xi, yi, ci = _place()

        def blk(p):
            return out_ref.at[4 * p[0] + 2 * p[1] + p[2]]

        me = (xi, yi, ci)
        mine = pltpu.make_async_copy(x_ref, blk(me), local_sem.at[0])
        mine.start()
        sends = [_rcopy(x_ref, blk(me), send_sems.at[k - 1], recv_sems.at[k - 1], _peer(xi, yi, ci, k))
                 for k in range(1, N_DEV)]
        for cp in sends:
            cp.start()
        for k in range(1, N_DEV):
            p = _peer(xi, yi, ci, k)
            _rcopy(x_ref, blk(p), send_sems.at[k - 1], recv_sems.at[k - 1], p).wait_recv()
        for cp in sends:
            cp.wait_send()
        mine.wait()

    vm = pl.BlockSpec(memory_space=pltpu.VMEM)
    return pl.pallas_call(
        body, name=name, in_specs=[vm], out_specs=vm, out_shape=jax.ShapeDtypeStruct((N_DEV, r, c), x.dtype),
        scratch_shapes=[pltpu.SemaphoreType.DMA((N_DEV - 1,)), pltpu.SemaphoreType.DMA((N_DEV - 1,)),
                        pltpu.SemaphoreType.DMA((1,))],
        compiler_params=_cparams(),
    )(x)


def _chips(xi, yi):
    chips = [(1 - xi, yi), (xi, 1 - yi), (1 - xi, 1 - yi)]
    return chips, [2 * cx + cy for cx, cy in chips]


def gather_weights(name, stacked):
    n = len(stacked)

    def body(*refs):
        out = refs[n:2 * n]
        send_sems, recv_sems = refs[2 * n:]
        xi, yi, ci = _place()
        s = 2 * xi + yi
        chips, sidx = _chips(xi, yi)
        sib = (xi, yi, 1 - ci)
        started = []
        for w in range(n):
            hr = out[w].shape[1] // 2
            mine = out[w].at[s, pl.ds(ci * hr, hr)]
            for j, (cx, cy) in enumerate(chips):
                cp = _rcopy(mine, mine, send_sems.at[w, j], recv_sems.at[w, j], (cx, cy, ci))
                cp.start()
                started.append(cp)
        for w in range(n):
            hr = out[w].shape[1] // 2
            for j, (cx, cy) in enumerate(chips):
                blk = out[w].at[sidx[j], pl.ds(ci * hr, hr)]
                _rcopy(blk, blk, send_sems.at[w, j], recv_sems.at[w, j], (cx, cy, ci)).wait_recv()
                fw = _rcopy(blk, blk, send_sems.at[w, 3 + j], recv_sems.at[w, 3 + j], sib)
                fw.start()
                started.append(fw)
        for w in range(n):
            hr = out[w].shape[1] // 2
            for j in range(3):
                blk = out[w].at[sidx[j], pl.ds((1 - ci) * hr, hr)]
                _rcopy(blk, blk, send_sems.at[w, 3 + j], recv_sems.at[w, 3 + j], sib).wait_recv()
        _drain(started)

    return pl.pallas_call(
        body, name=name, in_specs=[ANY] * n, out_specs=[ANY] * n,
        out_shape=[jax.ShapeDtypeStruct(a.shape, a.dtype) for a in stacked],
        input_output_aliases={w: w for w in range(n)},
        scratch_shapes=[pltpu.SemaphoreType.DMA((n, 6)), pltpu.SemaphoreType.DMA((n, 6))],
        compiler_params=_cparams(),
    )(*stacked)


def rs_pair(name, grads):
    n = len(grads)

    def body(*refs):
        g, out = refs[:n], refs[n:2 * n]
        send_sems, recv_sems = refs[2 * n:]
        xi, yi, ci = _place()
        sib = (xi, yi, 1 - ci)
        cps = [_rcopy(g[w].at[:, 1 - ci], out[w], send_sems.at[w], recv_sems.at[w], sib) for w in range(n)]
        for cp in cps:
            cp.start()
        for cp in cps:
            cp.wait_recv()
        for cp in cps:
            cp.wait_send()

    return pl.pallas_call(
        body, name=name, in_specs=[ANY] * n, out_specs=[ANY] * n,
        out_shape=[jax.ShapeDtypeStruct((N_CHIPS,) + a.shape[2:], a.dtype) for a in grads],
        scratch_shapes=[pltpu.SemaphoreType.DMA((n,)), pltpu.SemaphoreType.DMA((n,))],
        compiler_params=_cparams(),
    )(*grads)


def _rows_tile(rows, cols, unit=16, limit=1 << 20):
    best = None
    for t in range(unit, rows + 1, unit):
        if rows % t == 0 and t * cols * 4 <= limit:
            best = t
    return best or rows


def rs_add_pair(name, g, got, ci):
    _, _, hr, c = g.shape
    th = _rows_tile(hr, c)

    def body(ci_ref, g_ref, r_ref, o32_ref, ob_ref):
        tot = g_ref[...] + r_ref[...]
        o32_ref[...] = tot
        ob_ref[...] = tot.astype(BF16)

    blk = pl.BlockSpec((None, th, c), lambda s, i, ci_ref: (s, i, 0))
    return pl.pallas_call(
        body, name=name,
        grid_spec=pltpu.PrefetchScalarGridSpec(
            num_scalar_prefetch=1, grid=(N_CHIPS, hr // th),
            in_specs=[pl.BlockSpec((None, None, th, c), lambda s, i, ci_ref: (s, ci_ref[0], i, 0)), blk],
            out_specs=[blk, blk]),
        out_shape=[jax.ShapeDtypeStruct((N_CHIPS, hr, c), F32), jax.ShapeDtypeStruct((N_CHIPS, hr, c), BF16)],
        compiler_params=_cparams(dimension_semantics=("parallel", "parallel")),
    )(ci, g, got)


def rs_chips(name, sums_bf16):
    n = len(sums_bf16)

    def body(*refs):
        pb, outs = refs[:n], refs[n:4 * n]
        send_sems, recv_sems = refs[4 * n:]
        xi, yi, ci = _place()
        chips, sidx = _chips(xi, yi)
        started = []
        for w in range(n):
            for j, (cx, cy) in enumerate(chips):
                cp = _rcopy(pb[w].at[sidx[j]], outs[3 * w + j], send_sems.at[w, j], recv_sems.at[w, j], (cx, cy, ci))
                cp.start()
                started.append(cp)
        for w in range(n):
            for j, (cx, cy) in enumerate(chips):
                _rcopy(pb[w].at[sidx[j]], outs[3 * w + j], send_sems.at[w, j], recv_sems.at[w, j],
                       (cx, cy, ci)).wait_recv()
        _drain(started)

    out_shape = []
    for a in sums_bf16:
        out_shape += [jax.ShapeDtypeStruct(a.shape[1:], BF16)] * 3
    res = pl.pallas_call(
        body, name=name, in_specs=[ANY] * n, out_specs=[ANY] * (3 * n), out_shape=out_shape,
        scratch_shapes=[pltpu.SemaphoreType.DMA((n, 3)), pltpu.SemaphoreType.DMA((n, 3))],
        compiler_params=_cparams(),
    )(*sums_bf16)
    return [res[3 * w:3 * w + 3] for w in range(n)]


def rs_swap(name, halves):
    n = len(halves)

    def body(*refs):
        hv, out = refs[:n], refs[n:2 * n]
        send_sems, recv_sems = refs[2 * n:]
        xi, yi, ci = _place()
        sib = (xi, yi, 1 - ci)
        cps = [_rcopy(hv[w], out[w], send_sems.at[w], recv_sems.at[w], sib) for w in range(n)]
        for cp in cps:
            cp.start()
        for cp in cps:
            cp.wait_recv()
        _drain(cps)

    return pl.pallas_call(
        body, name=name, in_specs=[ANY] * n, out_specs=[ANY] * n,
        out_shape=[jax.ShapeDtypeStruct(a.shape, a.dtype) for a in halves],
        scratch_shapes=[pltpu.SemaphoreType.DMA((n,)), pltpu.SemaphoreType.DMA((n,))],
        compiler_params=_cparams(),
    )(*halves)


def cast_to_slot(name, x, slot):
    r, c = x.shape
    tr = _rows_tile(r, c)

    def body(slot_ref, x_ref, o_ref):
        o_ref[...] = x_ref[...].astype(BF16)

    return pl.pallas_call(
        body, name=name,
        grid_spec=pltpu.PrefetchScalarGridSpec(
            num_scalar_prefetch=1, grid=(r // tr,),
            in_specs=[pl.BlockSpec((tr, c), lambda i, slot_ref: (i, 0))],
            out_specs=pl.BlockSpec((None, tr, c), lambda i, slot_ref: (slot_ref[0], i, 0))),
        out_shape=jax.ShapeDtypeStruct((N_CHIPS, r, c), BF16),
        compiler_params=_cparams(dimension_semantics=("parallel",)),
    )(slot, x)


def sum_blocks(name, x, picks):
    _, r, c = x.shape

    def body(x_ref, o_ref):
        acc = x_ref[picks[0]]
        for b in picks[1:]:
            acc = acc + x_ref[b]
        o_ref[...] = acc

    vm = pl.BlockSpec(memory_space=pltpu.VMEM)
    return pl.pallas_call(body, name=name, in_specs=[vm], out_specs=vm, out_shape=jax.ShapeDtypeStruct((r, c), F32),
                          compiler_params=_cparams())(x)


def adamw(name, w, g, m, v):
    r, c = w.shape
    tr = _rows_tile(r, c, unit=8, limit=1 << 19)

    def body(w_ref, g_ref, m_ref, v_ref, d_ref, mo_ref, vo_ref):
        d_ref[...], mo_ref[...], vo_ref[...] = _adam_update(w_ref[...], g_ref[...], m_ref[...], v_ref[...])

    blk = pl.BlockSpec((tr, c), lambda i: (i, 0))
    return pl.pallas_call(
        body, name=name, grid=(r // tr,), in_specs=[blk] * 4, out_specs=[blk] * 3,
        out_shape=[jax.ShapeDtypeStruct((r, c), F32)] * 3,
        compiler_params=_cparams(dimension_semantics=("parallel",)),
    )(w, g, m, v)


def _adam_update(w, gv, m, v):
    c1 = 1.0 / (1.0 - ADAM_B1 ** ADAM_STEP)
    c2 = 1.0 / (1.0 - ADAM_B2 ** ADAM_STEP)
    mn = ADAM_B1 * m + (1.0 - ADAM_B1) * gv
    vn = ADAM_B2 * v + (1.0 - ADAM_B2) * (gv * gv)
    return -ADAM_LR * ((mn * c1) / (jnp.sqrt(vn * c2) + ADAM_EPS) + ADAM_WD * w), mn, vn


def adamw_halves(name, w, m, v, mine, theirs, ci):
    hr, c = mine[0].shape
    npos = len(mine)
    th = _rows_tile(hr, c, unit=8, limit=1 << 19)
    per = hr // th

    def body(ci_ref, w_ref, m_ref, v_ref, *rest):
        g_refs, (go_ref, d_ref, mo_ref, vo_ref) = rest[:2 * npos], rest[2 * npos:]
        pos, half = pl.program_id(0), pl.program_id(1)
        from_me = half == ci_ref[0]
        gv = jnp.where(from_me, g_refs[0][...], g_refs[npos][...])
        for p in range(1, npos):
            gv = jnp.where(pos == p, jnp.where(from_me, g_refs[p][...], g_refs[npos + p][...]), gv)
        d_ref[...], mo_ref[...], vo_ref[...] = _adam_update(w_ref[...], gv, m_ref[...], v_ref[...])
        go_ref[...] = gv

    full = pl.BlockSpec((th, c), lambda p, h, i, ci_ref: ((p * 2 + h) * per + i, 0))
    part = pl.BlockSpec((th, c), lambda p, h, i, ci_ref: (i, 0))
    rows = 2 * hr * npos
    return pl.pallas_call(
        body, name=name,
        grid_spec=pltpu.PrefetchScalarGridSpec(
            num_scalar_prefetch=1, grid=(npos, 2, per), in_specs=[full] * 3 + [part] * (2 * npos),
            out_specs=[full] * 4),
        out_shape=[jax.ShapeDtypeStruct((rows, c), F32)] * 4,
        compiler_params=_cparams(dimension_semantics=("arbitrary", "arbitrary", "arbitrary")),
    )(ci, w, m, v, *mine, *theirs)


def pack_rows(name, parts, rows):
    width = parts[0].shape[1]
    n = len(parts)

    def body(*refs):
        o_ref = refs[n]
        o_ref[...] = jnp.zeros_like(o_ref)
        off = 0
        for r in refs[:n]:
            o_ref[off:off + r.shape[0], :] = r[...]
            off += r.shape[0]

    vm = pl.BlockSpec(memory_space=pltpu.VMEM)
    return pl.pallas_call(body, name=name, in_specs=[vm] * n, out_specs=vm,
                          out_shape=jax.ShapeDtypeStruct((rows, width), F32), compiler_params=_cparams())(*parts)


def _pad_rows(a, rows):
    return jnp.pad(a, ((0, rows - a.shape[0]), (0, 0)))


def _view2d(name, a):
    if name == 'rw_rk' or a.ndim == 1:
        return a.reshape(1, -1)
    return a.reshape(-1, a.shape[-1])


def _reduce_scatter(tag, items, ci_arr, s):
    names = []
    for nm, _ in items:
        if nm not in names:
            names.append(nm)
    g4 = [g.reshape(N_CHIPS, 2, g.shape[1] // 2, g.shape[2]) for _, g in items]
    got = rs_pair("rs1_" + tag, g4)
    sums = [rs_add_pair(f"rs1add_{tag}{w}", g4[w], got[w], ci_arr) for w in range(len(items))]
    landed = rs_chips("rs2_" + tag, [s_[1] for s_ in sums])
    halves = []
    for w, (r0, r1, r2) in enumerate(landed):
        own = lax.dynamic_index_in_dim(sums[w][0], s, axis=0, keepdims=False)
        hr, c = own.shape
        halves.append(row_call(f"rs2add_{tag}{w}", f_sum, [own, r0, r1, r2], [], [(c, F32)],
                               tr=_rows_tile(hr, c), sr=16)[0])
    theirs = rs_swap("rs3_" + tag, halves)
    return {name: ([halves[w] for w, (nm, _) in enumerate(items) if nm == name],
                   [theirs[w] for w, (nm, _) in enumerate(items) if nm == name]) for name in names}


def _step(p):
    xi, yi, ci = _place()
    me = 4 * xi + 2 * yi + ci
    s = 2 * xi + yi
    ci_arr = jnp.reshape(ci, (1,)).astype(jnp.int32)
    s_arr = jnp.reshape(s, (1,)).astype(jnp.int32)
    x, ctx, tgt = p['x'][0], p['ctx'][0], p['loss_target'][0]
    T, D = x.shape
    L = ctx.shape[0]
    H, Dq = D // HEAD, D // N_CHIPS
    TR = math.gcd(math.gcd(L, T), 256)
    TS = min(TR, 64)
    nct = L // TR
    LG = p['rw_g1'].shape[-1]
    LW, LA = p['rw_w1'].shape[-1], p['rw_a1'].shape[-1]
    F4 = p['ffn_w2'].shape[1]

    pack = jnp.concatenate([
        _pad_rows(p['c'].reshape(N_CHIPS, Dq), 8), _pad_rows(p['rw_mix'][0], 8), _pad_rows(p['rw_w0'][0], 8),
        _pad_rows(p['rw_a0'][0], 8), _pad_rows(p['sc_conv'][0], 8)], axis=0)
    got = all_gather8("ag_small", pack)
    c_all = got[:, 0:N_CHIPS, :].reshape(N_DEV, D)
    full = jnp.transpose(got[::2], (1, 0, 2)).reshape(40, D)
    mix_f, w0_f, a0_f, conv_f = full[8:16], full[16:24], full[24:32], full[32:40]

    cond_in = jnp.concatenate([c_all, _pad_rows(p['c_ctx'].reshape(1, D), 8)], axis=0)
    cond = row_call("cond", f_silu, [cond_in], [], [(D, F32)], tr=16, sr=16)[0]
    ada = Stacked(p['ada_w'], "layer", D)
    modp = [mm_nn(f"modp{i}", cond, ada.at(i)) for i in range(2)]
    mg = all_gather8("ag_mod", jnp.concatenate(modp, axis=0))
    mod = jnp.transpose(mg[::2].reshape(N_CHIPS, 2, 16, 6 * Dq), (1, 2, 0, 3)).reshape(2, 16, 6 * D)
    mod = mod + p['ada_b'][:, None, :]
    mod_x = lax.dynamic_index_in_dim(mod, me, axis=1, keepdims=False)
    mod_c = mod[:, 8]

    def chunk(vec, j):
        return vec[j * D:(j + 1) * D].reshape(1, D)

    sh1x, sc1x, gt1x, sh2x, sc2x, gt2x = ([chunk(mod_x[i], j) for i in range(2)] for j in range(6))
    sh1c, sc1c = chunk(mod_c[0], 0), chunk(mod_c[0], 1)

    def gather(tag, names):
        shards = [cast_to_slot("cast_" + n, _view2d(n, p[n]), s_arr) for n in names]
        return dict(zip(names, gather_weights("ag_" + tag, shards)))

    gw = gather("rw", ['rw_wr', 'rw_wk', 'rw_wv', 'rw_wo', 'rw_w1', 'rw_w2', 'rw_a1', 'rw_a2', 'rw_g1', 'rw_g2'])
    gw.update(gather("sc", ['sc_win', 'sc_wout']))
    gw.update(gather("ffn", ['ffn_w13', 'ffn_w2']))
    Wr, Wk, Wv, Wo = (Stacked(gw[n], "row", Dq) for n in ('rw_wr', 'rw_wk', 'rw_wv', 'rw_wo'))
    W1, A1, G1 = (Stacked(gw[n], "row", Dq) for n in ('rw_w1', 'rw_a1', 'rw_g1'))
    W2, A2, G2 = Stacked(gw['rw_w2'], "col", LW), Stacked(gw['rw_a2'], "col", LA), Stacked(gw['rw_g2'], "col", LG)
    Win, Wout = Stacked(gw['sc_win'], "col", D), Stacked(gw['sc_wout'], "row", Dq)
    W13, W2f = Stacked(gw['ffn_w13'], "col", D), Stacked(gw['ffn_w2'], "row", F4)

    ones2 = (((lax.broadcasted_iota(jnp.int32, (2 * LANES, LANES), 0) & (LANES - 1)) >= HEAD)
             == (lax.broadcasted_iota(jnp.int32, (2 * LANES, LANES), 1) >= HEAD)).astype(BF16)
    n1g, n2g = p['norm1_g'], p['norm2_g']
    kkp, ka, lnw, lnb = p['rw_kk'], p['rw_ka'], p['rw_lnw'], p['rw_lnb']
    rk = p['rw_rk'].reshape(1, D)
    fg = p['final_g'].reshape(1, D)

    xin = jnp.concatenate([ctx, x], axis=0)
    nm = functools.partial(f_norm_mod, nct)
    nm_consts = [n1g[0:1], sh1c, sc1c, sh1x[0], sc1x[0]]
    h = row_call("l0_norm", nm, [xin], nm_consts, [(D, F32)], tr=TR, sr=16)[0]
    xr, xw, xk, xv, xa, xg = mix_fwd("l0_mix", h, mix_f, L)
    r = mm_nn("l0_r", xr, Wr)
    k = mm_nn("l0_k", xk, Wk)
    v = mm_nn("l0_v", xv, Wv)
    gl = mm_nn("l0_gl", xg, G1)
    sg = row_call("l0_sg", f_sigmoid, [gl], [], [(LG, BF16)], tr=TR, sr=16)[0]
    g = mm_nn("l0_g", sg, G2)
    wl, tw, zw, al, za = [], [], [], [], []
    for d in range(2):
        wl.append(mm_nn(f"l0_wl{d}", xw, W1.at(d)))
        tw.append(row_call(f"l0_tw{d}", f_tanh, [wl[d]], [], [(LW, BF16)], tr=TR, sr=16)[0])
        zw.append(mm_nn(f"l0_zw{d}", tw[d], W2.at(d)))
        al.append(mm_nn(f"l0_al{d}", xa, A1.at(d), BF16))
        za.append(mm_nn(f"l0_za{d}", al[d], A2.at(d)))
    post_rows = [k, zw[0], zw[1], za[0], za[1]]
    post_consts = [kkp, ka, w0_f[0:1], w0_f[1:2], a0_f[0:1], a0_f[1:2], ones2]
    aa, dec0, dec1, kd0, kd1, bb0, bb1, ksum = row_call(
        "l0_post", f_post_fwd, post_rows, post_consts, [(D, F32)] * 8, tr=TS, sr=16)
    dec, kd, bb = (dec0, dec1), (kd0, kd1), (bb0, bb1)
    ys, sts = [], []
    for d in range(2):
        y_d, st_d = scan_fwd(f"l0_scan{d}", r, dec[d], kd[d], v, aa, bb[d], L, bool(d))
        ys.append(y_d)
        sts.append(st_d)
    ro_rows = [ys[0], ys[1], r, ksum, v, g]
    ro_consts = [rk, lnw, lnb, ones2]
    og = row_call("l0_readout", f_readout, ro_rows, ro_consts, [(D, BF16)], tr=TS, sr=16)[0]
    yx = mm_nn("l0_o", og, Wo)
    res0_consts = [gt1x[0], n2g[0:1], sh2x[0], sc2x[0]]
    x1, h2 = row_call("l0_res", f_res_norm_mod, [x, yx], res0_consts, [(D, F32), (D, BF16)], tr=TR, sr=16,
                      offs=[0, nct])
    ab0 = mm_nn("l0_ffn13", h2, W13.at(0))
    sw0 = swiglu_fwd("l0_swiglu", ab0, TS)
    f0 = mm_nn("l0_ffn2", sw0, W2f.at(0))

    res1_consts = [gt2x[0], n1g[1:2], sh1x[1], sc1x[1]]
    x2, hb = row_call("l1_norm", f_res_norm_mod, [x1, f0], res1_consts, [(D, F32), (D, BF16)], tr=TR, sr=16)
    gcu = mm_nn("l1_win", hb, Win)
    pc = conv_fwd("l1_conv", gcu, conv_f)
    yx1 = mm_nn("l1_wout", pc, Wout)
    res2_consts = [gt1x[1], n2g[1:2], sh2x[1], sc2x[1]]
    x3, h2b = row_call("l1_res", f_res_norm_mod, [x2, yx1], res2_consts, [(D, F32), (D, BF16)], tr=TR, sr=16)
    ab1 = mm_nn("l1_ffn13", h2b, W13.at(1))
    sw1 = swiglu_fwd("l1_swiglu", ab1, TS)
    f1 = mm_nn("l1_ffn2", sw1, W2f.at(1))
    dx3, df1, dgt2_1, dfg, loss_blk = final_call("final", x3, f1, gt2x[1], fg, tgt, TR)
    loss = lax.psum(loss_blk[0, 0], ("x", "y", "c"))

    big = []
    dsw1 = mm_nt("b1_dsw", df1, W2f.at(1))
    gW2f1 = mm_tn("b1_gw2", sw1, df1, "row")
    dab1 = swiglu_bwd("b1_swiglu", ab1, dsw1, TS)
    dh2b = mm_nt("b1_dh2", dab1, W13.at(1))
    gW13_1 = mm_tn("b1_gw13", h2b, dab1, "col")
    rm = [True, True]
    cm = [True] * 4
    (dx2, dyx1), (dgt1_1, dn2g1, dsh2_1, dsc2_1) = row_vjp(
        "b1_res", f_res_norm_mod, [x2, yx1], res2_consts, [dx3, dh2b], row_mask=rm, const_mask=cm, tr=TR, sr=16,
        bf16_rows=(1,))
    dpc = mm_nt("b1_dpc", dyx1, Wout)
    big.append(('sc_wout', mm_tn("b1_gwout", pc, dyx1, "row")))
    dgcu, dconv = conv_bwd("b1_conv", gcu, conv_f, dpc)
    dhb = mm_nt("b1_dhb", dgcu, Win)
    big.append(('sc_win', mm_tn("b1_gwin", hb, dgcu, "col")))
    (dx1, df0), (dgt2_0, dn1g1, dsh1_1, dsc1_1) = row_vjp(
        "b1_norm", f_res_norm_mod, [x1, f0], res1_consts, [dx2, dhb], row_mask=rm, const_mask=cm, tr=TR, sr=16,
        bf16_rows=(1,))

    dsw0 = mm_nt("b0_dsw", df0, W2f.at(0))
    gW2f0 = mm_tn("b0_gw2", sw0, df0, "row")
    dab0 = swiglu_bwd("b0_swiglu", ab0, dsw0, TS)
    dh2 = mm_nt("b0_dh2", dab0, W13.at(0))
    gW13_0 = mm_tn("b0_gw13", h2, dab0, "col")
    (dx_a, dyx), (dgt1_0, dn2g0, dsh2_0, dsc2_0) = row_vjp(
        "b0_res", f_res_norm_mod, [x, yx], res0_consts, [dx1, dh2], row_mask=rm, const_mask=cm, tr=TR, sr=16,
        offs=[0, nct], bf16_rows=(1,))
    dyx_all = jnp.concatenate([jnp.zeros((L, D), BF16), dyx], axis=0)
    dog = mm_nt("b0_dog", dyx_all, Wo)
    gWo = mm_tn("b0_gwo", og, dyx_all, "row")
    (dy, dr_ro, dksum, dv_ro, dg), (drk, dlnw, dlnb) = row_vjp(
        "b0_readout", f_readout, ro_rows, ro_consts, [dog], row_mask=[True, False, True, True, True, True],
        const_mask=[True, True, True, False], tr=TS, sr=16, bf16_rows=(5,))
    sg_ =[scan_bwd(f"b0_scan{d}", r, dec[d], kd[d], v, aa, bb[d], dy, sts[d], L, bool(d)) for d in range(2)]
    (dr0, ddec0, dkd0, dv0, daa0, dbb0), (dr1, ddec1, dkd1, dv1, daa1, dbb1) = sg_
    post_cots = [daa0, daa1, ddec0, ddec1, dkd0, dkd1, dbb0, dbb1, dksum]
    (dk, dzw0, dzw1, dza0, dza1), (dkkp, dka, dw00, dw01, da00, da01) = row_vjp(
        "b0_post", f_post, post_rows, post_consts, post_cots, row_mask=[True] * 5,
        const_mask=[True] * 6 + [False], tr=TS, sr=16, bf16_rows=(0, 1, 2, 3, 4))
    dzw, dza = (dzw0, dzw1), (dza0, dza1)
    dr_t = sum_cast("b0_drsum", [dr0, dr1, dr_ro], BF16, TR)
    dv_t = sum_cast("b0_dvsum", [dv0, dv1, dv_ro], BF16, TR)
    mix_cots, mix_slots = [], []

    def back(tag, cot, w, xin_m, kind, slot):
        mix_cots.append(mm_nt("b0_dx" + tag, cot, w))
        mix_slots.append(slot)
        return mm_tn("b0_gw" + tag, xin_m, cot, kind)

    big.append(('rw_wr', back("r", dr_t, Wr, xr, "row", 0)))
    big.append(('rw_wk', back("k", dk, Wk, xk, "row", 2)))
    big.append(('rw_wv', back("v", dv_t, Wv, xv, "row", 3)))
    big.append(('rw_wo', gWo))
    dsg = mm_nt("b0_dsg", dg, G2)
    gG2 = mm_tn("b0_gg2", sg, dg, "col")
    (dgl,), _ = row_vjp("b0_sg", f_sigmoid, [gl], [], [dsg], row_mask=[True], const_mask=[], tr=TR, sr=16,
                        bf16_rows=(0,))
    gG1 = back("g", dgl, G1, xg, "row", 5)
    gW1, gW2, gA1, gA2 = [], [], [], []
    for d in range(2):
        dtw = mm_nt(f"b0_dtw{d}", dzw[d], W2.at(d))
        gW2.append(mm_tn(f"b0_gw2{d}", tw[d], dzw[d], "col"))
        (dwl,), _ = row_vjp(f"b0_tw{d}", f_tanh, [wl[d]], [], [dtw], row_mask=[True], const_mask=[], tr=TR, sr=16,
                            bf16_rows=(0,))
        gW1.append(back(f"w{d}", dwl, W1.at(d), xw, "row", 1))
        dal = mm_nt(f"b0_dal{d}", dza[d], A2.at(d), BF16)
        gA2.append(mm_tn(f"b0_ga2{d}", al[d], dza[d], "col"))
        gA1.append(back(f"a{d}", dal, A1.at(d), xa, "row", 4))
    big += [('rw_w1', gW1[0]), ('rw_w1', gW1[1]), ('rw_w2', gW2[0]), ('rw_w2', gW2[1]),
            ('rw_a1', gA1[0]), ('rw_a1', gA1[1]), ('rw_a2', gA2[0]), ('rw_a2', gA2[1]),
            ('rw_g1', gG1), ('rw_g2', gG2)]
    dh, dmix = mix_bwd("b0_mix", h, mix_f, mix_cots, mix_slots, L)
    (dxin,), (dn1g0, dsh1c, dsc1c, dsh1x, dsc1x) = row_vjp(
        "b0_norm", nm, [xin], nm_consts, [dh], row_mask=[True], const_mask=[True] * 5, tr=TR, sr=16)
    grad_x = sum_cast("b0_dx", [dxin, dx_a], F32, TR, offs=[nct, 0])

    zero = jnp.zeros((1, D), F32)
    parts = [dsh1x, dsc1x, dgt1_0, dsh2_0, dsc2_0, dgt2_0, dsh1c, dsc1c, zero, zero, zero, zero,
             dsh1_1, dsc1_1, dgt1_1, dsh2_1, dsc2_1, dgt2_1, zero, zero, zero, zero, zero, zero,
             dn1g0, dn1g1, dn2g0, dn2g1, dkkp, dka, drk, dlnw, dlnb, dfg,
             dmix[0:6], dw00, dw01, da00, da01, dconv[0:3]]
    got2 = all_gather8("ag_grads", pack_rows("pack_grads", parts, 48))
    small = sum_blocks("sum_grads", got2, list(range(N_DEV)))
    per_ex = got2[:, 0:24].reshape(N_DEV, 2, 2, 6 * D)
    tot = small[0:24].reshape(2, 2, 6 * D)
    cols = lambda a, width: lax.dynamic_slice_in_dim(a, s * width, width, axis=1)
    g_ada_w, dcond_parts = [], []
    for i in range(2):
        dm16 = cols(jnp.concatenate([per_ex[:, i, 0], _pad_rows(tot[i, 1][None], 8)], axis=0), 6 * Dq)
        g_ada_w.append(mm_tn(f"g_ada{i}", cond, dm16))
        dcond_parts.append(mm_nt(f"dcond{i}", dm16, ada.at(i)))
    g_ada_b = sum_cast("g_adab", [_pad_rows(tot[:, 0].reshape(12, D), 16), _pad_rows(tot[:, 1].reshape(12, D), 16)],
                       F32, 16)[0:12].reshape(2, 6 * D)
    dcond_mine = sum_cast("dcond_sum", dcond_parts, F32, 16)
    dcond = sum_blocks("dcond_chips", all_gather8("ag_dcond", dcond_mine), [0, 2, 4, 6])
    (dcin,), _ = row_vjp("b_cond", f_silu, [cond_in], [], [dcond], row_mask=[True], const_mask=[], tr=16, sr=16)

    rw_names = ('rw_wr', 'rw_wk', 'rw_wv', 'rw_wo', 'rw_w1', 'rw_w2', 'rw_a1', 'rw_a2', 'rw_g1', 'rw_g2')
    gsh = _reduce_scatter("rw", [it for it in big if it[0] in rw_names], ci_arr, s)
    gsh.update(_reduce_scatter("sc", [it for it in big if it[0] in ('sc_win', 'sc_wout')], ci_arr, s))
    gsh.update(_reduce_scatter("ffn", [('ffn_w13', gW13_0), ('ffn_w13', gW13_1), ('ffn_w2', gW2f0),
                                       ('ffn_w2', gW2f1)], ci_arr, s))

    grads = {}
    grads['c_ctx'] = dcin[8]
    grads['norm1_g'], grads['norm2_g'] = small[24:26], small[26:28]
    grads['ada_w'] = jnp.stack(g_ada_w)
    grads['ada_b'] = g_ada_b
    grads['rw_kk'], grads['rw_ka'], grads['rw_rk'] = small[28:29], small[29:30], small[30:31]
    grads['rw_lnw'], grads['rw_lnb'], grads['final_g'] = small[31:32], small[32:33], small[33]
    sharded = cols(small[34:48], Dq)
    grads['rw_mix'], grads['rw_w0'], grads['rw_a0'], grads['sc_conv'] = (
        sharded[0:6], sharded[6:8], sharded[8:10], sharded[10:13])

    outs_g, outs_d, outs_m, outs_v = [], [], [], []
    for n in WEIGHTS:
        shape = p[n].shape
        w2d, m2d, v2d = _view2d(n, p[n]), _view2d(n, p['m_' + n]), _view2d(n, p['v_' + n])
        if n in gsh:
            g2d, d_, m_, v_ = adamw_halves("adam_" + n, w2d, m2d, v2d, gsh[n][0], gsh[n][1], ci_arr)
        else:
            g2d = _view2d(n, grads[n].reshape(shape))
            d_, m_, v_ = adamw("adam_" + n, w2d, g2d, m2d, v2d)
        outs_g.append(g2d.reshape(shape))
        outs_d.append(d_.reshape(shape))
        outs_m.append(m_.reshape(shape))
        outs_v.append(v_.reshape(shape))
    return (loss, grad_x.reshape(1, T, D), *outs_g, *outs_d, *outs_m, *outs_v)


def kernel(x, c, ctx, c_ctx, norm1_g, norm2_g, ada_w, ada_b, rw_mix, rw_wr, rw_wk, rw_wv, rw_wo, rw_w0, rw_w1, rw_w2, rw_a0, rw_a1, rw_a2, rw_g1, rw_g2, rw_kk, rw_ka, rw_rk, rw_lnw, rw_lnb, sc_win, sc_conv, sc_wout, ffn_w13, ffn_w2, final_g, loss_target, m_c_ctx, m_norm1_g, m_norm2_g, m_ada_w, m_ada_b, m_rw_mix, m_rw_wr, m_rw_wk, m_rw_wv, m_rw_wo, m_rw_w0, m_rw_w1, m_rw_w2, m_rw_a0, m_rw_a1, m_rw_a2, m_rw_g1, m_rw_g2, m_rw_kk, m_rw_ka, m_rw_rk, m_rw_lnw, m_rw_lnb, m_sc_win, m_sc_conv, m_sc_wout, m_ffn_w13, m_ffn_w2, m_final_g, v_c_ctx, v_norm1_g, v_norm2_g, v_ada_w, v_ada_b, v_rw_mix, v_rw_wr, v_rw_wk, v_rw_wv, v_rw_wo, v_rw_w0, v_rw_w1, v_rw_w2, v_rw_a0, v_rw_a1, v_rw_a2, v_rw_g1, v_rw_g2, v_rw_kk, v_rw_ka, v_rw_rk, v_rw_lnw, v_rw_lnb, v_sc_win, v_sc_conv, v_sc_wout, v_ffn_w13, v_ffn_w2, v_final_g):
    values = (x, c, ctx, c_ctx, norm1_g, norm2_g, ada_w, ada_b, rw_mix, rw_wr, rw_wk, rw_wv, rw_wo, rw_w0, rw_w1, rw_w2, rw_a0, rw_a1, rw_a2, rw_g1, rw_g2, rw_kk, rw_ka, rw_rk, rw_lnw, rw_lnb, sc_win, sc_conv, sc_wout, ffn_w13, ffn_w2, final_g, loss_target, m_c_ctx, m_norm1_g, m_norm2_g, m_ada_w, m_ada_b, m_rw_mix, m_rw_wr, m_rw_wk, m_rw_wv, m_rw_wo, m_rw_w0, m_rw_w1, m_rw_w2, m_rw_a0, m_rw_a1, m_rw_a2, m_rw_g1, m_rw_g2, m_rw_kk, m_rw_ka, m_rw_rk, m_rw_lnw, m_rw_lnb, m_sc_win, m_sc_conv, m_sc_wout, m_ffn_w13, m_ffn_w2, m_final_g, v_c_ctx, v_norm1_g, v_norm2_g, v_ada_w, v_ada_b, v_rw_mix, v_rw_wr, v_rw_wk, v_rw_wv, v_rw_wo, v_rw_w0, v_rw_w1, v_rw_w2, v_rw_a0, v_rw_a1, v_rw_a2, v_rw_g1, v_rw_g2, v_rw_kk, v_rw_ka, v_rw_rk, v_rw_lnw, v_rw_lnb, v_sc_win, v_sc_conv, v_sc_wout, v_ffn_w13, v_ffn_w2, v_final_g)
    return _step(dict(zip(INPUTS, values)))
```

```python
import functools
import math

import jax
import jax.numpy as jnp
from jax import lax
from jax.experimental import pallas as pl
from jax.experimental.pallas import tpu as pltpu

F32 = jnp.float32
BF16 = jnp.bfloat16
MESH = pl.DeviceIdType.MESH

GRID_W = 64
HEAD = 64
LANES = 128
N_CHIPS = 4
N_DEV = 8
NORM_EPS = 1e-6
GN_EPS = 64e-5
ADAM_LR, ADAM_B1, ADAM_B2, ADAM_EPS, ADAM_WD, ADAM_STEP = 0.001, 0.9, 0.999, 1e-08, 0.01, 10
VMEM_LIMIT = 56 * 1024 * 1024
HI = lax.Precision.HIGHEST
WEIGHTS = ['c_ctx', 'norm1_g', 'norm2_g', 'ada_w', 'ada_b', 'rw_mix', 'rw_wr', 'rw_wk', 'rw_wv', 'rw_wo', 'rw_w0',
           'rw_w1', 'rw_w2', 'rw_a0', 'rw_a1', 'rw_a2', 'rw_g1', 'rw_g2', 'rw_kk', 'rw_ka', 'rw_rk', 'rw_lnw',
           'rw_lnb', 'sc_win', 'sc_conv', 'sc_wout', 'ffn_w13', 'ffn_w2', 'final_g']
INPUTS = (['x', 'c', 'ctx'] + WEIGHTS + ['loss_target'] + ['m_' + w for w in WEIGHTS]
          + ['v_' + w for w in WEIGHTS])


def _cparams(**kw):
    return pltpu.CompilerParams(vmem_limit_bytes=VMEM_LIMIT, **kw)


def _pick(dim, cands):
    for c in cands:
        if dim % c == 0:
            return c
    return dim


def _place():
    return lax.axis_index("x"), lax.axis_index("y"), lax.axis_index("c")


_TILE_M = (1024, 768, 512, 1408, 256, 128)
_TILE_N = (1408, 1024, 768, 512, 256, 128)
_TILE_K = (2816, 2304, 2048, 1536, 1408, 1152, 1024, 768, 704, 512, 256, 128)
MM_TILE_BYTES = 40 * 1024 * 1024


def _pick_k(unit, tm, tn, a_dtype, b_dtype, o_dtype):
    ab, bb, ob = (jnp.dtype(d).itemsize for d in (a_dtype, b_dtype, o_dtype))
    for tk in _TILE_K:
        if unit % tk == 0 and 2 * (tm * tk * ab + tk * tn * bb) + 2 * tm * tn * ob + tm * tn * 4 <= MM_TILE_BYTES:
            return tk
    return unit


class Stacked:
    def __init__(self, arr, kind, r, layer=0):
        self.arr, self.kind, self.r, self.layer = arr, kind, r, layer
        self.c = arr.shape[2]
        self.shape = {"row": (N_CHIPS * r, self.c), "col": (r, N_CHIPS * self.c), "layer": (r, self.c)}[kind]

    def at(self, layer):
        return Stacked(self.arr, self.kind, self.r, layer)

    def spec(self, t0, t1, swap):
        r, c, layer = self.r, self.c, self.layer
        per_r, per_c = r // t0, c // t1
        assert r % t0 == 0 and c % t1 == 0
        kind = self.kind

        def index(i, j, k):
            ri, ci = (j, k) if swap else (k, j)
            if kind == "row":
                return (ri // per_r, layer * per_r + ri % per_r, ci)
            if kind == "layer":
                return (layer, ri, ci)
            return (ci // per_c, layer * per_r + ri, ci % per_c)

        return pl.BlockSpec((None, t0, t1), index)


def _mm_body(dims, nk, a_ref, b_ref, o_ref, acc_ref):
    if nk == 1:
        o_ref[...] = lax.dot_general(a_ref[...].astype(BF16), b_ref[...].astype(BF16), (dims, ((), ())),
                                     preferred_element_type=F32).astype(o_ref.dtype)
        return
    k = pl.program_id(2)

    @pl.when(k == 0)
    def _():
        acc_ref[...] = jnp.zeros_like(acc_ref)

    acc_ref[...] += lax.dot_general(a_ref[...].astype(BF16), b_ref[...].astype(BF16), (dims, ((), ())),
                                    preferred_element_type=F32)

    @pl.when(k == nk - 1)
    def _():
        o_ref[...] = acc_ref[...].astype(o_ref.dtype)


def _mm_call(name, dims, grid, in_specs, out_spec, out_shape, acc_shape, operands):
    return pl.pallas_call(
        functools.partial(_mm_body, dims, grid[2]), name=name, grid=grid, in_specs=in_specs, out_specs=out_spec,
        out_shape=out_shape, scratch_shapes=[pltpu.VMEM(acc_shape if grid[2] > 1 else (8, LANES), F32)],
        compiler_params=_cparams(dimension_semantics=("parallel", "parallel", "arbitrary")),
    )(*operands)


def mm_nn(name, a, b, out_dtype=F32):
    M, K = a.shape
    st = isinstance(b, Stacked)
    N = b.shape[1]
    tm = _pick(M, _TILE_M)
    tn = _pick(b.c if st and b.kind == "col" else N, _TILE_N)
    tk = _pick_k(b.r if st else K, tm, tn, a.dtype, b.arr.dtype if st else b.dtype, out_dtype)
    b_spec = b.spec(tk, tn, False) if st else pl.BlockSpec((tk, tn), lambda i, j, k: (k, j))
    return _mm_call(name, ((1,), (0,)), (M // tm, N // tn, K // tk),
                    [pl.BlockSpec((tm, tk), lambda i, j, k: (i, k)), b_spec],
                    pl.BlockSpec((tm, tn), lambda i, j, k: (i, j)), jax.ShapeDtypeStruct((M, N), out_dtype),
                    (tm, tn), (a, b.arr if st else b))


def mm_nt(name, a, b, out_dtype=F32):
    M, N = a.shape
    st = isinstance(b, Stacked)
    K = b.shape[0]
    tm = _pick(M, _TILE_M)
    to = _pick(b.r if st else K, _TILE_N)
    tc = _pick_k(b.c if st and b.kind == "col" else N, tm, to, a.dtype, b.arr.dtype if st else b.dtype, out_dtype)
    b_spec = b.spec(to, tc, True) if st else pl.BlockSpec((to, tc), lambda i, j, k: (j, k))
    return _mm_call(name, ((1,), (1,)), (M // tm, K // to, N // tc),
                    [pl.BlockSpec((tm, tc), lambda i, j, k: (i, k)), b_spec],
                    pl.BlockSpec((tm, to), lambda i, j, k: (i, j)), jax.ShapeDtypeStruct((M, K), out_dtype),
                    (tm, to), (a, b.arr if st else b))


def mm_tn(name, a, b, kind=None):
    R, M = a.shape
    N = b.shape[1]
    r, c = (M // N_CHIPS, N) if kind == "row" else (M, N // N_CHIPS) if kind == "col" else (M, N)
    tm, tn = _pick(r, _TILE_M), _pick(c, _TILE_N)
    tk = _pick_k(R, tm, tn, a.dtype, b.dtype, F32)
    if kind:
        per_r, per_c = r // tm, c // tn
        if kind == "row":
            o_spec = pl.BlockSpec((None, tm, tn), lambda i, j, k: (i // per_r, i % per_r, j))
        else:
            o_spec = pl.BlockSpec((None, tm, tn), lambda i, j, k: (j // per_c, i, j % per_c))
        o_shape = jax.ShapeDtypeStruct((N_CHIPS, r, c), F32)
    else:
        o_spec = pl.BlockSpec((tm, tn), lambda i, j, k: (i, j))
        o_shape = jax.ShapeDtypeStruct((M, N), F32)
    return _mm_call(name, ((0,), (0,)), (M // tm, N // tn, R // tk),
                    [pl.BlockSpec((tk, tm), lambda i, j, k: (k, i)), pl.BlockSpec((tk, tn), lambda i, j, k: (k, j))],
                    o_spec, o_shape, (tm, tn), (a, b))


def _shifted(o):
    return lambda i: (i + o, 0)


def row_call(name, f, rows, consts, outs, *, tr, sr, offs=None):
    offs = offs or [0] * len(rows)
    n_rows = min(r.shape[0] - o * tr for r, o in zip(rows, offs))
    nr, nc = len(rows), len(consts)

    def body(*refs):
        row_refs, const_refs, out_refs = refs[:nr], refs[nr:nr + nc], refs[nr + nc:]
        i = pl.program_id(0)
        cvals = [r[...] for r in const_refs]

        def step(j, carry):
            sl = pl.ds(pl.multiple_of(j * sr, sr), sr)
            res = f(i, *[r[sl, :] for r in row_refs], *cvals)
            for o, v in zip(out_refs, res):
                o[sl, :] = v.astype(o.dtype)
            return carry

        lax.fori_loop(0, tr // sr, step, 0)

    in_specs = [pl.BlockSpec((tr, r.shape[1]), _shifted(o)) for r, o in zip(rows, offs)]
    in_specs += [pl.BlockSpec(c.shape, lambda i: (0, 0)) for c in consts]
    return pl.pallas_call(
        body, name=name, grid=(n_rows // tr,), in_specs=in_specs,
        out_specs=[pl.BlockSpec((tr, w), lambda i: (i, 0)) for w, _ in outs],
        out_shape=[jax.ShapeDtypeStruct((n_rows, w), dt) for w, dt in outs],
        compiler_params=_cparams(dimension_semantics=("parallel",)),
    )(*rows, *consts)


def row_vjp(name, f, rows, consts, cots, *, row_mask, const_mask, tr, sr, offs=None, bf16_rows=()):
    offs = offs or [0] * len(rows)
    n_rows = min(r.shape[0] - o * tr for r, o in zip(rows, offs))
    nr, nc = len(rows), len(consts)
    cot_in = [c for c in cots if c is not None]
    nct = len(cot_in)
    d_rows = [i for i in range(nr) if row_mask[i]]
    d_consts = [i for i in range(nc) if const_mask[i]]

    def body(*refs):
        row_refs, const_refs = refs[:nr], refs[nr:nr + nc]
        cot_refs = refs[nr + nc:nr + nc + nct]
        drow_refs = refs[nr + nc + nct:nr + nc + nct + len(d_rows)]
        dconst_refs = refs[nr + nc + nct + len(d_rows):]
        i = pl.program_id(0)

        @pl.when(i == 0)
        def _():
            for r in dconst_refs:
                r[...] = jnp.zeros_like(r)

        cvals = [r[...] for r in const_refs]

        def step(j, carry):
            sl = pl.ds(pl.multiple_of(j * sr, sr), sr)
            rvals = [r[sl, :] for r in row_refs]

            def g(*diff):
                rv, cv = list(rvals), list(cvals)
                for idx, val in zip(d_rows, diff[:len(d_rows)]):
                    rv[idx] = val
                for idx, val in zip(d_consts, diff[len(d_rows):]):
                    cv[idx] = val
                return f(i, *rv, *cv)

            primals = [rvals[idx].astype(F32) for idx in d_rows] + [cvals[idx] for idx in d_consts]
            res, vjp = jax.vjp(g, *primals)
            it = iter(cot_refs)
            cts = tuple(jnp.zeros_like(o) if c is None else next(it)[sl, :].astype(o.dtype) for o, c in zip(res, cots))
            grads = vjp(cts)
            for r, val in zip(drow_refs, grads[:len(d_rows)]):
                r[sl, :] = val.astype(r.dtype)
            for r, val in zip(dconst_refs, grads[len(d_rows):]):
                r[...] += val
            return carry

        lax.fori_loop(0, tr // sr, step, 0)

    in_specs = [pl.BlockSpec((tr, r.shape[1]), _shifted(o)) for r, o in zip(rows, offs)]
    in_specs += [pl.BlockSpec(c.shape, lambda i: (0, 0)) for c in consts]
    in_specs += [pl.BlockSpec((tr, c.shape[1]), lambda i: (i, 0)) for c in cot_in]
    out_specs = [pl.BlockSpec((tr, rows[i].shape[1]), lambda i: (i, 0)) for i in d_rows]
    out_specs += [pl.BlockSpec(consts[i].shape, lambda i: (0, 0)) for i in d_consts]
    out_shape = [jax.ShapeDtypeStruct((n_rows, rows[i].shape[1]), BF16 if i in bf16_rows else F32) for i in d_rows]
    out_shape += [jax.ShapeDtypeStruct(consts[i].shape, F32) for i in d_consts]
    res = pl.pallas_call(
        body, name=name, grid=(n_rows // tr,), in_specs=in_specs, out_specs=out_specs, out_shape=out_shape,
        compiler_params=_cparams(dimension_semantics=("arbitrary",)),
    )(*rows, *consts, *cot_in)
    return list(res[:len(d_rows)]), list(res[len(d_rows):])


def _sigmoid(x):
    return 1.0 / (1.0 + jnp.exp(-x))


def _softplus(u):
    return jnp.maximum(u, 0.0) + jnp.log(1.0 + jnp.exp(-jnp.abs(u)))


def _rms(x, g):
    ms = jnp.sum(x * x, axis=-1, keepdims=True) * (1.0 / x.shape[-1])
    return x * lax.rsqrt(ms + NORM_EPS) * g


def _hsum_impl(x, ones2):
    rows, width = x.shape
    nch = width // LANES
    xs = jnp.concatenate([x[:, j * LANES:(j + 1) * LANES] for j in range(nch)], axis=0)
    hi = xs.astype(BF16)
    lo = (xs - hi.astype(F32)).astype(BF16)
    ys = jnp.dot(jnp.concatenate([hi, lo], axis=1), ones2, preferred_element_type=F32)
    return jnp.concatenate([ys[j * rows:(j + 1) * rows] for j in range(nch)], axis=1)


@jax.custom_vjp
def _hsum(x, ones2):
    return _hsum_impl(x, ones2)


def _hsum_fwd(x, ones2):
    return _hsum_impl(x, ones2), ones2


def _hsum_bwd(ones2, g):
    return _hsum_impl(g, ones2), jnp.zeros_like(ones2)


_hsum.defvjp(_hsum_fwd, _hsum_bwd)


def f_silu(i, x):
    return (x * _sigmoid(x),)


def f_sigmoid(i, x):
    return (_sigmoid(x),)


def f_tanh(i, x):
    return (jnp.tanh(x),)


def f_norm_mod(n_ctx_tiles, i, xin, g, sh_c, sc_c, sh_x, sc_x):
    is_x = i >= n_ctx_tiles
    sh = jnp.where(is_x, sh_x, sh_c)
    sc = jnp.where(is_x, sc_x, sc_c)
    return (_rms(xin, g) * (1.0 + sc) + sh,)


def f_res_norm_mod(i, x, y, gt, g, sh, sc):
    x1 = x + gt * y
    return x1, _rms(x1, g) * (1.0 + sc) + sh


def f_post(i, k, zw0, zw1, za0, za1, kkp, ka, w00, w01, a00, a01, ones2):
    kq = k * kkp
    kk = kq / jnp.maximum(jnp.sqrt(_hsum(kq * kq, ones2)), 1e-12)

    def direction(zw, za, w0, a0):
        log_w = -_softplus(-(w0 + zw)) - 0.5
        a = _sigmoid(a0 + za)
        return jnp.exp(-jnp.exp(log_w)), k * (1.0 + (a - 1.0) * ka), kk * a

    dec0, kd0, bb0 = direction(zw0, za0, w00, a00)
    dec1, kd1, bb1 = direction(zw1, za1, w01, a01)
    return -kk, -kk, dec0, dec1, kd0, kd1, bb0, bb1, kd0 + kd1


def f_post_fwd(*a):
    return f_post(*a)[1:]


def f_readout(i, y0, y1, r, ksum, v, g, rk, lnw, lnb, ones2):
    y = y0 + y1
    yc = y - _hsum(y, ones2) * (1.0 / HEAD)
    var = _hsum(yc * yc, ones2) * (1.0 / HEAD)
    o = yc * lax.rsqrt(var + GN_EPS) * lnw + lnb
    o = o + _hsum(r * ksum * rk, ones2) * v
    return (o * g,)


def f_sum(i, *xs):
    acc = xs[0].astype(F32)
    for x in xs[1:]:
        acc = acc + x.astype(F32)
    return (acc,)


def sum_cast(name, arrs, dtype, tr, offs=None):
    return row_call(name, f_sum, arrs, [], [(arrs[0].shape[1], dtype)], tr=tr, sr=16, offs=offs)[0]


def swiglu_fwd(name, ab, tr):
    T, F2 = ab.shape
    F = F2 // 2
    sr = 16

    def body(ab_ref, o_ref):
        def step(j, carry):
            sl = pl.ds(pl.multiple_of(j * sr, sr), sr)
            a, b = ab_ref[sl, :F], ab_ref[sl, F:]
            o_ref[sl, :] = (a * _sigmoid(a) * b).astype(o_ref.dtype)
            return carry

        lax.fori_loop(0, tr // sr, step, 0)

    return pl.pallas_call(
        body, name=name, grid=(T // tr,), in_specs=[pl.BlockSpec((tr, F2), lambda i: (i, 0))],
        out_specs=pl.BlockSpec((tr, F), lambda i: (i, 0)), out_shape=jax.ShapeDtypeStruct((T, F), BF16),
        compiler_params=_cparams(dimension_semantics=("parallel",)),
    )(ab)


def swiglu_bwd(name, ab, dsw, tr):
    T, F2 = ab.shape
    F = F2 // 2
    sr = 16

    def body(ab_ref, d_ref, o_ref):
        def step(j, carry):
            sl = pl.ds(pl.multiple_of(j * sr, sr), sr)
            a, b, d = ab_ref[sl, :F], ab_ref[sl, F:], d_ref[sl, :]
            sg = _sigmoid(a)
            o_ref[sl, :F] = (d * b * (sg + a * sg * (1.0 - sg))).astype(o_ref.dtype)
            o_ref[sl, F:] = (d * a * sg).astype(o_ref.dtype)
            return carry

        lax.fori_loop(0, tr // sr, step, 0)

    return pl.pallas_call(
        body, name=name, grid=(T // tr,),
        in_specs=[pl.BlockSpec((tr, F2), lambda i: (i, 0)), pl.BlockSpec((tr, F), lambda i: (i, 0))],
        out_specs=pl.BlockSpec((tr, F2), lambda i: (i, 0)), out_shape=jax.ShapeDtypeStruct((T, F2), BF16),
        compiler_params=_cparams(dimension_semantics=("parallel",)),
    )(ab, dsw)


def final_call(name, x3, f1, gt, fg, tgt, tr):
    T, D = x3.shape
    sr = 16

    def f(x, y, gtv, g, t):
        err = _rms(x + gtv * y, g) - t
        return 0.5 * jnp.sum(err * err) * (1.0 / D)

    def body(x_ref, y_ref, gt_ref, g_ref, t_ref, dx_ref, dy_ref, dgt_ref, dg_ref, loss_ref):
        @pl.when(pl.program_id(0) == 0)
        def _():
            dgt_ref[...] = jnp.zeros_like(dgt_ref)
            dg_ref[...] = jnp.zeros_like(dg_ref)
            loss_ref[...] = jnp.zeros_like(loss_ref)

        def step(j, carry):
            sl = pl.ds(pl.multiple_of(j * sr, sr), sr)
            val, vjp = jax.vjp(lambda x, y, a, b: f(x, y, a, b, t_ref[sl, :]), x_ref[sl, :], y_ref[sl, :],
                               gt_ref[...], g_ref[...])
            dx, dy, dgt, dg = vjp(jnp.ones((), F32))
            dx_ref[sl, :] = dx
            dy_ref[sl, :] = dy.astype(dy_ref.dtype)
            dgt_ref[...] += dgt
            dg_ref[...] += dg
            loss_ref[...] += jnp.full(loss_ref.shape, val, F32)
            return carry

        lax.fori_loop(0, tr // sr, step, 0)

    row = pl.BlockSpec((tr, D), lambda i: (i, 0))
    vec = pl.BlockSpec((1, D), lambda i: (0, 0))
    return pl.pallas_call(
        body, name=name, grid=(T // tr,), in_specs=[row, row, vec, vec, row],
        out_specs=[row, row, vec, vec, pl.BlockSpec((8, LANES), lambda i: (0, 0))],
        out_shape=[jax.ShapeDtypeStruct((T, D), F32), jax.ShapeDtypeStruct((T, D), BF16)]
        + [jax.ShapeDtypeStruct((1, D), F32)] * 2
        + [jax.ShapeDtypeStruct((8, LANES), F32)],
        compiler_params=_cparams(dimension_semantics=("arbitrary",)),
    )(x3, f1, gt, fg, tgt)


def _tshift(x, kind, period):
    n = x.shape[0]
    t = lax.broadcasted_iota(jnp.int32, x.shape, 0)
    if kind == 0:
        return jnp.where((t & (period - 1)) == 0, 0.0, pltpu.roll(x, 1, 0))
    if kind == 1:
        return jnp.where(((t & (period - 1)) == period - 1) | (t == n - 1), 0.0, pltpu.roll(x, n - 1, 0))
    if kind == 2:
        return jnp.where(t < GRID_W, 0.0, pltpu.roll(x, GRID_W, 0))
    return jnp.where(t >= n - GRID_W, 0.0, pltpu.roll(x, n - GRID_W, 0))


def _pow2_at_least(n):
    return 1 << (n - 1).bit_length()


def _shift_into(dst_ref, h_ref, n_ctx, cb, D, transpose):
    j = pl.program_id(0)
    quarter = (j * cb * 4) // D
    half = (j * cb * 2) // D
    flip = 1 if transpose else 0
    for q in range(4):
        @pl.when(quarter == q)
        def _(q=q):
            dst_ref[n_ctx:, :] = _tshift(h_ref[n_ctx:, :], q ^ flip, GRID_W)
    for q in range(2):
        @pl.when(half == q)
        def _(q=q):
            dst_ref[:n_ctx, :] = _tshift(h_ref[:n_ctx, :], q ^ flip, _pow2_at_least(n_ctx))


def mix_fwd(name, h, mix, n_ctx):
    R, D = h.shape
    cb = LANES

    def body(h_ref, mix_ref, *rest):
        outs, hs_ref = rest[:6], rest[6]
        _shift_into(hs_ref, h_ref, n_ctx, cb, D, False)
        hv = h_ref[...]
        xx = hs_ref[...] - hv
        for m in range(6):
            outs[m][...] = (hv + xx * mix_ref[m:m + 1, :]).astype(BF16)

    col = pl.BlockSpec((R, cb), lambda j: (0, j))
    return pl.pallas_call(
        body, name=name, grid=(D // cb,), in_specs=[col, pl.BlockSpec((mix.shape[0], cb), lambda j: (0, j))],
        out_specs=[col] * 6, out_shape=[jax.ShapeDtypeStruct((R, D), BF16)] * 6,
        scratch_shapes=[pltpu.VMEM((R, cb), F32)],
        compiler_params=_cparams(dimension_semantics=("parallel",)),
    )(h, mix)


def mix_bwd(name, h, mix, cots, slots, n_ctx):
    R, D = h.shape
    cb = LANES
    nc = len(cots)

    def body(h_ref, mix_ref, *rest):
        cot_refs, dh_ref, dmix_ref, hs_ref, dxx_ref = rest[:nc], rest[nc], rest[nc + 1], rest[nc + 2], rest[nc + 3]
        _shift_into(hs_ref, h_ref, n_ctx, cb, D, False)
        xx = hs_ref[...] - h_ref[...]
        per_slot = [None] * 6
        for cref, m in zip(cot_refs, slots):
            per_slot[m] = cref[...] if per_slot[m] is None else per_slot[m] + cref[...]
        dh = jnp.zeros((R, cb), F32)
        dxx = jnp.zeros((R, cb), F32)
        rows = []
        for m in range(6):
            d = per_slot[m]
            dh = dh + d
            dxx = dxx + d * mix_ref[m:m + 1, :]
            rows.append(jnp.sum(d * xx, axis=0, keepdims=True))
        dmix_ref[...] = jnp.concatenate(rows + [jnp.zeros((2, cb), F32)], axis=0)
        dxx_ref[...] = dxx
        _shift_into(hs_ref, dxx_ref, n_ctx, cb, D, True)
        dh_ref[...] = dh - dxx + hs_ref[...]

    col = pl.BlockSpec((R, cb), lambda j: (0, j))
    return pl.pallas_call(
        body, name=name, grid=(D // cb,), in_specs=[col, pl.BlockSpec((mix.shape[0], cb), lambda j: (0, j))] + [col] * nc,
        out_specs=[col, pl.BlockSpec((8, cb), lambda j: (0, j))],
        out_shape=[jax.ShapeDtypeStruct((R, D), F32), jax.ShapeDtypeStruct((8, D), F32)],
        scratch_shapes=[pltpu.VMEM((R, cb), F32), pltpu.VMEM((R, cb), F32)],
        compiler_params=_cparams(dimension_semantics=("parallel",)),
    )(h, mix, *cots)


def _conv_parts(gb_ref, gc_ref, u_ref, cw_ref):
    T = gb_ref.shape[0]
    z = gc_ref[...] * u_ref[...]
    zp, zn = _tshift(z, 0, _pow2_at_least(T)), _tshift(z, 1, _pow2_at_least(T))
    conv = zp * cw_ref[0:1, :] + z * cw_ref[1:2, :] + zn * cw_ref[2:3, :]
    return z, zp, zn, conv


def conv_fwd(name, gcu, cw):
    T, D3 = gcu.shape
    D = D3 // 3
    cb = LANES
    nb = D // cb

    def body(gb_ref, gc_ref, u_ref, cw_ref, o_ref):
        _, _, _, conv = _conv_parts(gb_ref, gc_ref, u_ref, cw_ref)
        o_ref[...] = (gb_ref[...] * conv).astype(BF16)

    def part(p):
        return pl.BlockSpec((T, cb), lambda j: (0, j + p * nb))

    return pl.pallas_call(
        body, name=name, grid=(nb,),
        in_specs=[part(0), part(1), part(2), pl.BlockSpec((cw.shape[0], cb), lambda j: (0, j))],
        out_specs=pl.BlockSpec((T, cb), lambda j: (0, j)), out_shape=jax.ShapeDtypeStruct((T, D), BF16),
        compiler_params=_cparams(dimension_semantics=("parallel",)),
    )(gcu, gcu, gcu, cw)


def conv_bwd(name, gcu, cw, dp):
    T, D3 = gcu.shape
    D = D3 // 3
    cb = LANES
    nb = D // cb

    def body(gb_ref, gc_ref, u_ref, cw_ref, dp_ref, o_ref, dcw_ref):
        part = pl.program_id(1)
        z, zp, zn, conv = _conv_parts(gb_ref, gc_ref, u_ref, cw_ref)
        dpv = dp_ref[...]
        dconv = dpv * gb_ref[...]
        period = _pow2_at_least(T)
        dz = (_tshift(dconv * cw_ref[0:1, :], 1, period) + dconv * cw_ref[1:2, :]
              + _tshift(dconv * cw_ref[2:3, :], 0, period))

        @pl.when(part == 0)
        def _():
            o_ref[...] = (dpv * conv).astype(o_ref.dtype)
            dcw_ref[...] = jnp.concatenate(
                [jnp.sum(dconv * s, axis=0, keepdims=True) for s in (zp, z, zn)] + [jnp.zeros((5, cb), F32)], axis=0)

        @pl.when(part == 1)
        def _():
            o_ref[...] = (dz * u_ref[...]).astype(o_ref.dtype)

        @pl.when(part == 2)
        def _():
            o_ref[...] = (dz * gc_ref[...]).astype(o_ref.dtype)

    def part_spec(p):
        return pl.BlockSpec((T, cb), lambda j, q: (0, j + p * nb))

    return pl.pallas_call(
        body, name=name, grid=(nb, 3),
        in_specs=[part_spec(0), part_spec(1), part_spec(2), pl.BlockSpec((cw.shape[0], cb), lambda j, q: (0, j)),
                  pl.BlockSpec((T, cb), lambda j, q: (0, j))],
        out_specs=[pl.BlockSpec((T, cb), lambda j, q: (0, j + q * nb)), pl.BlockSpec((8, cb), lambda j, q: (0, j))],
        out_shape=[jax.ShapeDtypeStruct((T, D3), BF16), jax.ShapeDtypeStruct((8, D), F32)],
        compiler_params=_cparams(dimension_semantics=("arbitrary", "arbitrary")),
    )(gcu, gcu, gcu, cw, dp)


SCAN_TC = 8


def _scan_consts():
    rows = lax.broadcasted_iota(jnp.int32, (HEAD, LANES), 0)
    cols = lax.broadcasted_iota(jnp.int32, (HEAD, LANES), 1)
    eye = rows == (cols & (HEAD - 1))
    r2 = lax.broadcasted_iota(jnp.int32, (2 * LANES, LANES), 0)
    c2 = lax.broadcasted_iota(jnp.int32, (2 * LANES, LANES), 1)
    ones2 = (((r2 & (LANES - 1)) >= HEAD) == (c2 >= HEAD)).astype(BF16)
    return eye, ones2, ones2[:LANES]


SCAN_ROW_CHUNKS = 4


def _chunks_of_heads(hp):
    per = max(1, hp // SCAN_ROW_CHUNKS)
    return [range(lo, lo + per) for lo in range(0, hp, per)]


def _rows_of(heads):
    return pl.ds(heads[0] * HEAD, len(heads) * HEAD)


def _split2(p):
    hi = p.astype(BF16)
    lo = (p - hi.astype(F32)).astype(BF16)
    return jnp.concatenate([hi, lo], axis=1)


def _head_rows(h):
    return pl.ds(h * HEAD, HEAD)


def _expand_into(dst, p1_ref, row_ref, t, hp, eye, ones1):
    for h in range(hp):
        p1_ref[_head_rows(h), :] = jnp.where(eye, row_ref[t, h:h + 1, :], 0.0).astype(BF16)
    dst[...] = jnp.dot(p1_ref[...], ones1, preferred_element_type=F32)


def _colsum_store(ref, t, h, x):
    ref[t, pl.ds(h, 1), :] = jnp.sum(x, axis=0, keepdims=True)


def _order(i, n_ctx, n_all, rev):
    if not rev:
        return i
    return jnp.where(i < n_ctx, n_ctx - 1 - i, n_all - 1 - (i - n_ctx))


def _with_side(work, side, n_in, n_out, n_scratch, n_steps):
    if side is None:
        return work
    nsi, nso = len(side.ins), len(side.out_shapes)

    def body(*refs):
        ins, side_in = refs[:n_in], refs[n_in:n_in + nsi]
        outs = refs[n_in + nsi:n_in + nsi + n_out]
        side_out = refs[n_in + nsi + n_out:n_in + nsi + n_out + nso]
        scratch = refs[n_in + nsi + n_out + nso:]

        @pl.when(pl.program_id(0) == 0)
        def _():
            side.start(side_in, side_out, scratch[n_scratch:])

        work(*ins, *outs, *scratch[:n_scratch])

        @pl.when(pl.program_id(0) == n_steps - 1)
        def _():
            side.finish(side_in, side_out, scratch[n_scratch:])

    return body


def _side_call_args(side, n_in, n_out):
    if side is None:
        return [], [], [], [], {}, []
    aliases = {n_in + i: n_out + o for i, o in side.aliases.items()}
    return ([ANY] * len(side.ins), [ANY] * len(side.out_shapes), list(side.out_shapes), list(side.sems), aliases,
            list(side.ins))


def scan_fwd(name, r, w, k, v, a, b, n_ctx_rows, rev, side=None):
    R, D = r.shape
    hp, tc = D // LANES, SCAN_TC
    n_all, n_ctx = R // tc, n_ctx_rows // tc
    ins = [t.reshape(R, hp, LANES) for t in (r, w, k, v, a, b)]

    def work(r_ref, w_ref, k_ref, v_ref, a_ref, b_ref, y_ref, st_ref, s_ref, ve_ref, sa_ref, p_ref, p1_ref):
        @pl.when(pl.program_id(0) == 0)
        def _():
            s_ref[...] = jnp.zeros_like(s_ref)

        eye, ones2, ones1 = _scan_consts()

        def row_of(q):
            return tc - 1 - q if rev else q

        for q in range(tc):
            _expand_into(ve_ref.at[q], p1_ref.at[q % 2], v_ref, row_of(q), hp, eye, ones1)

        def advance(q, prev_ref):
            t = row_of(q)
            for heads in _chunks_of_heads(hp):
                rows = _rows_of(heads)
                for h in heads:
                    p_ref[q % 2, _head_rows(h), :] = _split2(prev_ref[_head_rows(h), :] * a_ref[t, h:h + 1, :])
                sa_ref[q % 2, rows, :] = jnp.dot(p_ref[q % 2, rows, :], ones2, preferred_element_type=F32)
                for h in heads:
                    hr_ = _head_rows(h)
                    st_ref[q, hr_, :] = (prev_ref[hr_, :] * w_ref[t, h:h + 1, :]
                                         + sa_ref[q % 2, hr_, :] * b_ref[t, h:h + 1, :]
                                         + ve_ref[q, hr_, :] * k_ref[t, h:h + 1, :])

        advance(0, s_ref)
        for q in range(1, tc):
            advance(q, st_ref.at[q - 1])
        s_ref[...] = st_ref[tc - 1]

        for q in range(tc):
            t = row_of(q)
            for h in range(hp):
                p1_ref[q % 2, _head_rows(h), :] = (st_ref[q, _head_rows(h), :] * r_ref[t, h:h + 1, :]).astype(BF16)
            sa_ref[q % 2] = jnp.dot(p1_ref[q % 2], ones1, preferred_element_type=F32)
            for h in range(hp):
                _colsum_store(y_ref, t, h, jnp.where(eye, sa_ref[q % 2, _head_rows(h), :], 0.0))

    row_spec = pl.BlockSpec((tc, hp, LANES), lambda i: (_order(i, n_ctx, n_all, rev), 0, 0))
    n = hp * HEAD
    s_in, s_out, s_shape, s_scratch, s_alias, s_ops = _side_call_args(side, 6, 2)
    y, st, *side_res = pl.pallas_call(
        _with_side(work, side, 6, 2, 5, n_all), name=name, grid=(n_all,), in_specs=[row_spec] * 6 + s_in,
        out_specs=[row_spec, pl.BlockSpec((tc, n, LANES), lambda i: (i, 0, 0))] + s_out,
        out_shape=[jax.ShapeDtypeStruct((R, hp, LANES), F32), jax.ShapeDtypeStruct((R, n, LANES), F32)] + s_shape,
        scratch_shapes=[pltpu.VMEM((n, LANES), F32), pltpu.VMEM((tc, n, LANES), F32), pltpu.VMEM((2, n, LANES), F32),
                        pltpu.VMEM((2, n, 2 * LANES), BF16), pltpu.VMEM((2, n, LANES), BF16)] + s_scratch,
        input_output_aliases=s_alias,
        compiler_params=_cparams(dimension_semantics=("arbitrary",)),
    )(*ins, *s_ops)
    return y.reshape(R, D), st, side_res


def scan_bwd(name, r, w, k, v, a, b, dy, st, n_ctx_rows, rev, side=None):
    R, D = r.shape
    hp, tc = D // LANES, SCAN_TC
    n_all, n_ctx = R // tc, n_ctx_rows // tc
    ins = [t.reshape(R, hp, LANES) for t in (r, w, k, v, a, b, dy)]

    def work(r_ref, w_ref, k_ref, v_ref, a_ref, b_ref, dy_ref, st_ref, prev_ref,
             dr_ref, dw_ref, dk_ref, dv_ref, da_ref, db_ref,
             g_ref, s0_ref, ve_ref, dye_ref, sa_ref, gs_ref, tmp_ref, p_ref, p1_ref):
        i = pl.program_id(0)

        @pl.when(i == 0)
        def _():
            g_ref[...] = jnp.zeros_like(g_ref)

        eye, ones2, ones1 = _scan_consts()

        @pl.when(i == n_all - 1)
        def _():
            s0_ref[...] = jnp.zeros_like(s0_ref)

        @pl.when(i != n_all - 1)
        def _():
            s0_ref[...] = prev_ref[0]

        def row_of(q):
            return tc - 1 - q if rev else q

        def prev_of(q):
            return s0_ref if q == 0 else st_ref.at[q - 1]

        for q in range(tc):
            t, prev = row_of(q), prev_of(q)
            _expand_into(ve_ref.at[q], p1_ref.at[2 + q % 2], v_ref, t, hp, eye, ones1)
            _expand_into(dye_ref.at[q], p1_ref.at[4 + q % 2], dy_ref, t, hp, eye, ones1)
            for h in range(hp):
                p1_ref[q % 2, _head_rows(h), :] = (prev[_head_rows(h), :] * a_ref[t, h:h + 1, :]).astype(BF16)
            sa_ref[q] = jnp.dot(p1_ref[q % 2], ones1, preferred_element_type=F32)

        for q in reversed(range(tc)):
            t, prev = row_of(q), prev_of(q)
            for heads in _chunks_of_heads(hp):
                rows = _rows_of(heads)
                for h in heads:
                    hr_ = _head_rows(h)
                    g = g_ref[hr_, :] + dye_ref[q, hr_, :] * r_ref[t, h:h + 1, :]
                    gs_ref[q, hr_, :] = g
                    p_ref[q % 2, hr_, :] = _split2(g * b_ref[t, h:h + 1, :])
                tmp_ref[q % 2, rows, :] = jnp.dot(p_ref[q % 2, rows, :], ones2, preferred_element_type=F32)
                for h in heads:
                    hr_ = _head_rows(h)
                    dsa = tmp_ref[q % 2, hr_, :]
                    _colsum_store(da_ref, t, h, prev[hr_, :] * dsa)
                    g_ref[hr_, :] = gs_ref[q, hr_, :] * w_ref[t, h:h + 1, :] + dsa * a_ref[t, h:h + 1, :]

        for q in range(tc):
            t, prev = row_of(q), prev_of(q)
            for h in range(hp):
                hr_ = _head_rows(h)
                g = gs_ref[q, hr_, :]
                p1_ref[q % 2, hr_, :] = (g * k_ref[t, h:h + 1, :]).astype(BF16)
                _colsum_store(dr_ref, t, h, st_ref[q, hr_, :] * dye_ref[q, hr_, :])
                _colsum_store(dk_ref, t, h, g * ve_ref[q, hr_, :])
                _colsum_store(dw_ref, t, h, g * prev[hr_, :])
                _colsum_store(db_ref, t, h, g * sa_ref[q, hr_, :])
            tmp_ref[q % 2] = jnp.dot(p1_ref[q % 2], ones1, preferred_element_type=F32)
            for h in range(hp):
                _colsum_store(dv_ref, t, h, jnp.where(eye, tmp_ref[q % 2, _head_rows(h), :], 0.0))

    def pos(i):
        return n_all - 1 - i

    n = hp * HEAD
    row_spec = pl.BlockSpec((tc, hp, LANES), lambda i: (_order(pos(i), n_ctx, n_all, rev), 0, 0))
    big = pltpu.VMEM((tc, n, LANES), F32)
    one = pltpu.VMEM((n, LANES), F32)
    s_in, s_out, s_shape, s_scratch, s_alias, s_ops = _side_call_args(side, 9, 6)
    outs = pl.pallas_call(
        _with_side(work, side, 9, 6, 9, n_all), name=name, grid=(n_all,),
        in_specs=[row_spec] * 7 + [
            pl.BlockSpec((tc, n, LANES), lambda i: (pos(i), 0, 0)),
            pl.BlockSpec((1, n, LANES), lambda i: (jnp.maximum(pos(i) * tc - 1, 0), 0, 0))] + s_in,
        out_specs=[row_spec] * 6 + s_out,
        out_shape=[jax.ShapeDtypeStruct((R, hp, LANES), F32)] * 6 + s_shape,
        scratch_shapes=[one, one, big, big, big, big, pltpu.VMEM((2, n, LANES), F32),
                        pltpu.VMEM((2, n, 2 * LANES), BF16), pltpu.VMEM((6, n, LANES), BF16)] + s_scratch,
        input_output_aliases=s_alias,
        compiler_params=_cparams(dimension_semantics=("arbitrary",)),
    )(*ins, st, st, *s_ops)
    return [o.reshape(R, D) for o in outs[:6]], list(outs[6:])


ANY = pl.BlockSpec(memory_space=pl.ANY)


def _peer(xi, yi, ci, k):
    return (1 - xi if k & 4 else xi, 1 - yi if k & 2 else yi, 1 - ci if k & 1 else ci)


def _rcopy(src, dst, send_sem, recv_sem, dev):
    return pltpu.make_async_remote_copy(src_ref=src, dst_ref=dst, send_sem=send_sem, recv_sem=recv_sem,
                                        device_id=dev, device_id_type=MESH)


def _drain(copies):
    for cp in copies:
        if cp.is_remote:
            cp.wait_send()
        else:
            cp.wait()


def all_gather8(name, x):
    r, c = x.shape

    def body(x_ref, out_ref, send_sems, recv_sems, local_sem):
        xi, yi, ci = _place()

        def blk(p):
            return out_ref.at[4 * p[0] + 2 * p[1] + p[2]]

        me = (xi, yi, ci)
        mine = pltpu.make_async_copy(x_ref, blk(me), local_sem.at[0])
        mine.start()
        sends = [_rcopy(x_ref, blk(me), send_sems.at[k - 1], recv_sems.at[k - 1], _peer(xi, yi, ci, k))
                 for k in range(1, N_DEV)]
        for cp in sends:
            cp.start()
        for k in range(1, N_DEV):
            p = _peer(xi, yi, ci, k)
            _rcopy(x_ref, blk(p), send_sems.at[k - 1], recv_sems.at[k - 1], p).wait_recv()
        for cp in sends:
            cp.wait_send()
        mine.wait()

    vm = pl.BlockSpec(memory_space=pltpu.VMEM)
    return pl.pallas_call(
        body, name=name, in_specs=[vm], out_specs=vm, out_shape=jax.ShapeDtypeStruct((N_DEV, r, c), x.dtype),
        scratch_shapes=[pltpu.SemaphoreType.DMA((N_DEV - 1,)), pltpu.SemaphoreType.DMA((N_DEV - 1,)),
                        pltpu.SemaphoreType.DMA((1,))],
        compiler_params=_cparams(),
    )(x)


def _chips(xi, yi):
    chips = [(1 - xi, yi), (xi, 1 - yi), (1 - xi, 1 - yi)]
    return chips, [2 * cx + cy for cx, cy in chips]


def gather_weights(name, stacked):
    n = len(stacked)

    def body(*refs):
        _gather_start(refs[n:2 * n], refs[2 * n:])
        _gather_finish(refs[n:2 * n], refs[2 * n:])

    return pl.pallas_call(
        body, name=name, in_specs=[ANY] * n, out_specs=[ANY] * n,
        out_shape=[jax.ShapeDtypeStruct(a.shape, a.dtype) for a in stacked],
        input_output_aliases={w: w for w in range(n)},
        scratch_shapes=_gather_sems(n),
        compiler_params=_cparams(),
    )(*stacked)


def _gather_sems(n):
    return [pltpu.SemaphoreType.DMA((n, 6)), pltpu.SemaphoreType.DMA((n, 6))]


def _gather_sends(out, sems):
    send_sems, recv_sems = sems
    xi, yi, ci = _place()
    s = 2 * xi + yi
    chips, _ = _chips(xi, yi)
    sends = []
    for w in range(len(out)):
        hr = out[w].shape[1] // 2
        mine = out[w].at[s, pl.ds(ci * hr, hr)]
        sends += [_rcopy(mine, mine, send_sems.at[w, j], recv_sems.at[w, j], (cx, cy, ci))
                  for j, (cx, cy) in enumerate(chips)]
    return sends


def _gather_start(out, sems):
    for cp in _gather_sends(out, sems):
        cp.start()


def _gather_finish(out, sems):
    send_sems, recv_sems = sems
    xi, yi, ci = _place()
    chips, sidx = _chips(xi, yi)
    sib = (xi, yi, 1 - ci)
    passed = []
    for w in range(len(out)):
        hr = out[w].shape[1] // 2
        for j, (cx, cy) in enumerate(chips):
            blk = out[w].at[sidx[j], pl.ds(ci * hr, hr)]
            _rcopy(blk, blk, send_sems.at[w, j], recv_sems.at[w, j], (cx, cy, ci)).wait_recv()
            fw = _rcopy(blk, blk, send_sems.at[w, 3 + j], recv_sems.at[w, 3 + j], sib)
            fw.start()
            passed.append(fw)
    for w in range(len(out)):
        hr = out[w].shape[1] // 2
        for j in range(3):
            blk = out[w].at[sidx[j], pl.ds((1 - ci) * hr, hr)]
            _rcopy(blk, blk, send_sems.at[w, 3 + j], recv_sems.at[w, 3 + j], sib).wait_recv()
    _drain(_gather_sends(out, sems) + passed)


class Side:
    def __init__(self, ins, out_shapes, aliases, sems, start, finish):
        self.ins, self.out_shapes, self.aliases, self.sems = ins, out_shapes, aliases, sems
        self.start, self.finish = start, finish


def gather_side(stacked):
    n = len(stacked)
    return Side(stacked, [jax.ShapeDtypeStruct(a.shape, a.dtype) for a in stacked], {w: w for w in range(n)},
                _gather_sems(n), lambda ins, outs, sems: _gather_start(outs, sems),
                lambda ins, outs, sems: _gather_finish(outs, sems))


def rs_pair(name, grads):
    n = len(grads)

    def body(*refs):
        g, out = refs[:n], refs[n:2 * n]
        send_sems, recv_sems = refs[2 * n:]
        xi, yi, ci = _place()
        sib = (xi, yi, 1 - ci)
        cps = [_rcopy(g[w].at[:, 1 - ci], out[w], send_sems.at[w], recv_sems.at[w], sib) for w in range(n)]
        for cp in cps:
            cp.start()
        for cp in cps:
            cp.wait_recv()
        for cp in cps:
            cp.wait_send()

    return pl.pallas_call(
        body, name=name, in_specs=[ANY] * n, out_specs=[ANY] * n,
        out_shape=[jax.ShapeDtypeStruct((N_CHIPS,) + a.shape[2:], a.dtype) for a in grads],
        scratch_shapes=[pltpu.SemaphoreType.DMA((n,)), pltpu.SemaphoreType.DMA((n,))],
        compiler_params=_cparams(),
    )(*grads)


def _rows_tile(rows, cols, unit=16, limit=1 << 20):
    best = None
    for t in range(unit, rows + 1, unit):
        if rows % t == 0 and t * cols * 4 <= limit:
            best = t
    return best or rows


def rs_add_pair(name, g, got, ci):
    _, _, hr, c = g.shape
    th = _rows_tile(hr, c)

    def body(ci_ref, g_ref, r_ref, o32_ref, ob_ref):
        tot = g_ref[...] + r_ref[...]
        o32_ref[...] = tot
        ob_ref[...] = tot.astype(BF16)

    blk = pl.BlockSpec((None, th, c), lambda s, i, ci_ref: (s, i, 0))
    return pl.pallas_call(
        body, name=name,
        grid_spec=pltpu.PrefetchScalarGridSpec(
            num_scalar_prefetch=1, grid=(N_CHIPS, hr // th),
            in_specs=[pl.BlockSpec((None, None, th, c), lambda s, i, ci_ref: (s, ci_ref[0], i, 0)), blk],
            out_specs=[blk, blk]),
        out_shape=[jax.ShapeDtypeStruct((N_CHIPS, hr, c), F32), jax.ShapeDtypeStruct((N_CHIPS, hr, c), BF16)],
        compiler_params=_cparams(dimension_semantics=("parallel", "parallel")),
    )(ci, g, got)


def rs_chips(name, sums_bf16):
    n = len(sums_bf16)
    side = rs2_side(sums_bf16)

    def body(*refs):
        side.start(refs[:n], refs[n:4 * n], refs[4 * n:])
        side.finish(refs[:n], refs[n:4 * n], refs[4 * n:])

    res = pl.pallas_call(
        body, name=name, in_specs=[ANY] * n, out_specs=[ANY] * (3 * n), out_shape=side.out_shapes,
        scratch_shapes=side.sems, compiler_params=_cparams(),
    )(*sums_bf16)
    return [res[3 * w:3 * w + 3] for w in range(n)]


def _rs2_copies(pb, outs, sems):
    send_sems, recv_sems = sems
    xi, yi, ci = _place()
    chips, sidx = _chips(xi, yi)
    return [_rcopy(pb[w].at[sidx[j]], outs[3 * w + j], send_sems.at[w, j], recv_sems.at[w, j], (cx, cy, ci))
            for w in range(len(pb)) for j, (cx, cy) in enumerate(chips)]


def _rs2_start(pb, outs, sems):
    for cp in _rs2_copies(pb, outs, sems):
        cp.start()


def _rs2_finish(pb, outs, sems):
    for cp in _rs2_copies(pb, outs, sems):
        cp.wait_recv()
    _drain(_rs2_copies(pb, outs, sems))


def rs2_side(sums_bf16):
    n = len(sums_bf16)
    out_shapes = [jax.ShapeDtypeStruct(a.shape[1:], BF16) for a in sums_bf16 for _ in range(3)]
    return Side(sums_bf16, out_shapes, {}, [pltpu.SemaphoreType.DMA((n, 3)), pltpu.SemaphoreType.DMA((n, 3))],
                _rs2_start, _rs2_finish)


def rs_swap(name, halves):
    n = len(halves)

    def body(*refs):
        hv, out = refs[:n], refs[n:2 * n]
        send_sems, recv_sems = refs[2 * n:]
        xi, yi, ci = _place()
        sib = (xi, yi, 1 - ci)
        cps = [_rcopy(hv[w], out[w], send_sems.at[w], recv_sems.at[w], sib) for w in range(n)]
        for cp in cps:
            cp.start()
        for cp in cps:
            cp.wait_recv()
        _drain(cps)

    return pl.pallas_call(
        body, name=name, in_specs=[ANY] * n, out_specs=[ANY] * n,
        out_shape=[jax.ShapeDtypeStruct(a.shape, a.dtype) for a in halves],
        scratch_shapes=[pltpu.SemaphoreType.DMA((n,)), pltpu.SemaphoreType.DMA((n,))],
        compiler_params=_cparams(),
    )(*halves)


def cast_to_slot(name, x, slot):
    r, c = x.shape
    tr = _rows_tile(r, c)

    def body(slot_ref, x_ref, o_ref):
        o_ref[...] = x_ref[...].astype(BF16)

    return pl.pallas_call(
        body, name=name,
        grid_spec=pltpu.PrefetchScalarGridSpec(
            num_scalar_prefetch=1, grid=(r // tr,),
            in_specs=[pl.BlockSpec((tr, c), lambda i, slot_ref: (i, 0))],
            out_specs=pl.BlockSpec((None, tr, c), lambda i, slot_ref: (slot_ref[0], i, 0))),
        out_shape=jax.ShapeDtypeStruct((N_CHIPS, r, c), BF16),
        compiler_params=_cparams(dimension_semantics=("parallel",)),
    )(slot, x)


def sum_blocks(name, x, picks):
    _, r, c = x.shape

    def body(x_ref, o_ref):
        acc = x_ref[picks[0]]
        for b in picks[1:]:
            acc = acc + x_ref[b]
        o_ref[...] = acc

    vm = pl.BlockSpec(memory_space=pltpu.VMEM)
    return pl.pallas_call(body, name=name, in_specs=[vm], out_specs=vm, out_shape=jax.ShapeDtypeStruct((r, c), F32),
                          compiler_params=_cparams())(x)


def adamw(name, w, g, m, v):
    r, c = w.shape
    tr = _rows_tile(r, c, unit=8, limit=1 << 19)

    def body(w_ref, g_ref, m_ref, v_ref, d_ref, mo_ref, vo_ref):
        d_ref[...], mo_ref[...], vo_ref[...] = _adam_update(w_ref[...], g_ref[...], m_ref[...], v_ref[...])

    blk = pl.BlockSpec((tr, c), lambda i: (i, 0))
    return pl.pallas_call(
        body, name=name, grid=(r // tr,), in_specs=[blk] * 4, out_specs=[blk] * 3,
        out_shape=[jax.ShapeDtypeStruct((r, c), F32)] * 3,
        compiler_params=_cparams(dimension_semantics=("parallel",)),
    )(w, g, m, v)


def _adam_update(w, gv, m, v):
    c1 = 1.0 / (1.0 - ADAM_B1 ** ADAM_STEP)
    c2 = 1.0 / (1.0 - ADAM_B2 ** ADAM_STEP)
    mn = ADAM_B1 * m + (1.0 - ADAM_B1) * gv
    vn = ADAM_B2 * v + (1.0 - ADAM_B2) * (gv * gv)
    return -ADAM_LR * ((mn * c1) / (jnp.sqrt(vn * c2) + ADAM_EPS) + ADAM_WD * w), mn, vn


def adamw_halves(name, w, m, v, mine, theirs, ci):
    hr, c = mine[0].shape
    npos = len(mine)
    th = _rows_tile(hr, c, unit=8, limit=1 << 19)
    per = hr // th

    def body(ci_ref, w_ref, m_ref, v_ref, *rest):
        g_refs, (go_ref, d_ref, mo_ref, vo_ref) = rest[:2 * npos], rest[2 * npos:]
        pos, half = pl.program_id(0), pl.program_id(1)
        from_me = half == ci_ref[0]
        gv = jnp.where(from_me, g_refs[0][...], g_refs[npos][...])
        for p in range(1, npos):
            gv = jnp.where(pos == p, jnp.where(from_me, g_refs[p][...], g_refs[npos + p][...]), gv)
        d_ref[...], mo_ref[...], vo_ref[...] = _adam_update(w_ref[...], gv, m_ref[...], v_ref[...])
        go_ref[...] = gv

    full = pl.BlockSpec((th, c), lambda p, h, i, ci_ref: ((p * 2 + h) * per + i, 0))
    part = pl.BlockSpec((th, c), lambda p, h, i, ci_ref: (i, 0))
    rows = 2 * hr * npos
    return pl.pallas_call(
        body, name=name,
        grid_spec=pltpu.PrefetchScalarGridSpec(
            num_scalar_prefetch=1, grid=(npos, 2, per), in_specs=[full] * 3 + [part] * (2 * npos),
            out_specs=[full] * 4),
        out_shape=[jax.ShapeDtypeStruct((rows, c), F32)] * 4,
        compiler_params=_cparams(dimension_semantics=("arbitrary", "arbitrary", "arbitrary")),
    )(ci, w, m, v, *mine, *theirs)


def pack_rows(name, parts, rows):
    width = parts[0].shape[1]
    n = len(parts)

    def body(*refs):
        o_ref = refs[n]
        o_ref[...] = jnp.zeros_like(o_ref)
        off = 0
        for r in refs[:n]:
            o_ref[off:off + r.shape[0], :] = r[...]
            off += r.shape[0]

    vm = pl.BlockSpec(memory_space=pltpu.VMEM)
    return pl.pallas_call(body, name=name, in_specs=[vm] * n, out_specs=vm,
                          out_shape=jax.ShapeDtypeStruct((rows, width), F32), compiler_params=_cparams())(*parts)


def _pad_rows(a, rows):
    return jnp.pad(a, ((0, rows - a.shape[0]), (0, 0)))


def _view2d(name, a):
    if name == 'rw_rk' or a.ndim == 1:
        return a.reshape(1, -1)
    return a.reshape(-1, a.shape[-1])


def _reduce_scatter(tag, items, ci_arr, s):
    sums = _rs_pair_sums(tag, items, ci_arr)
    landed = rs_chips("rs2_" + tag, [s_[1] for s_ in sums])
    return _rs_finish(tag, items, sums, landed, s)


def _rs_pair_sums(tag, items, ci_arr):
    g4 = [g.reshape(N_CHIPS, 2, g.shape[1] // 2, g.shape[2]) for _, g in items]
    got = rs_pair("rs1_" + tag, g4)
    return [rs_add_pair(f"rs1add_{tag}{w}", g4[w], got[w], ci_arr) for w in range(len(items))]


def _rs_finish(tag, items, sums, landed, s):
    names = []
    for nm, _ in items:
        if nm not in names:
            names.append(nm)
    halves = []
    for w, (r0, r1, r2) in enumerate(landed):
        own = lax.dynamic_index_in_dim(sums[w][0], s, axis=0, keepdims=False)
        hr, c = own.shape
        halves.append(row_call(f"rs2add_{tag}{w}", f_sum, [own, r0, r1, r2], [], [(c, F32)],
                               tr=_rows_tile(hr, c), sr=16)[0])
    theirs = rs_swap("rs3_" + tag, halves)
    return {name: ([halves[w] for w, (nm, _) in enumerate(items) if nm == name],
                   [theirs[w] for w, (nm, _) in enumerate(items) if nm == name]) for name in names}


def _step(p):
    xi, yi, ci = _place()
    me = 4 * xi + 2 * yi + ci
    s = 2 * xi + yi
    ci_arr = jnp.reshape(ci, (1,)).astype(jnp.int32)
    s_arr = jnp.reshape(s, (1,)).astype(jnp.int32)
    x, ctx, tgt = p['x'][0], p['ctx'][0], p['loss_target'][0]
    T, D = x.shape
    L = ctx.shape[0]
    H, Dq = D // HEAD, D // N_CHIPS
    TR = math.gcd(math.gcd(L, T), 256)
    TS = min(TR, 64)
    nct = L // TR
    LG = p['rw_g1'].shape[-1]
    LW, LA = p['rw_w1'].shape[-1], p['rw_a1'].shape[-1]
    F4 = p['ffn_w2'].shape[1]

    pack = jnp.concatenate([
        _pad_rows(p['c'].reshape(N_CHIPS, Dq), 8), _pad_rows(p['rw_mix'][0], 8), _pad_rows(p['rw_w0'][0], 8),
        _pad_rows(p['rw_a0'][0], 8), _pad_rows(p['sc_conv'][0], 8)], axis=0)
    got = all_gather8("ag_small", pack)
    c_all = got[:, 0:N_CHIPS, :].reshape(N_DEV, D)
    full = jnp.transpose(got[::2], (1, 0, 2)).reshape(40, D)
    mix_f, w0_f, a0_f, conv_f = full[8:16], full[16:24], full[24:32], full[32:40]

    cond_in = jnp.concatenate([c_all, _pad_rows(p['c_ctx'].reshape(1, D), 8)], axis=0)
    cond = row_call("cond", f_silu, [cond_in], [], [(D, F32)], tr=16, sr=16)[0]
    ada = Stacked(p['ada_w'], "layer", D)
    modp = [mm_nn(f"modp{i}", cond, ada.at(i)) for i in range(2)]
    mg = all_gather8("ag_mod", jnp.concatenate(modp, axis=0))
    mod = jnp.transpose(mg[::2].reshape(N_CHIPS, 2, 16, 6 * Dq), (1, 2, 0, 3)).reshape(2, 16, 6 * D)
    mod = mod + p['ada_b'][:, None, :]
    mod_x = lax.dynamic_index_in_dim(mod, me, axis=1, keepdims=False)
    mod_c = mod[:, 8]

    def chunk(vec, j):
        return vec[j * D:(j + 1) * D].reshape(1, D)

    sh1x, sc1x, gt1x, sh2x, sc2x, gt2x = ([chunk(mod_x[i], j) for i in range(2)] for j in range(6))
    sh1c, sc1c = chunk(mod_c[0], 0), chunk(mod_c[0], 1)

    def slots(names):
        return [cast_to_slot("cast_" + n, _view2d(n, p[n]), s_arr) for n in names]

    rw_names = ('rw_wr', 'rw_wk', 'rw_wv', 'rw_wo', 'rw_w1', 'rw_w2', 'rw_a1', 'rw_a2', 'rw_g1', 'rw_g2')
    late_names = ('sc_win', 'sc_wout', 'ffn_w13', 'ffn_w2')
    gw = dict(zip(rw_names, gather_weights("ag_rw", slots(rw_names))))
    late_slots = slots(late_names)
    Wr, Wk, Wv, Wo = (Stacked(gw[n], "row", Dq) for n in ('rw_wr', 'rw_wk', 'rw_wv', 'rw_wo'))
    W1, A1, G1 = (Stacked(gw[n], "row", Dq) for n in ('rw_w1', 'rw_a1', 'rw_g1'))
    W2, A2, G2 = Stacked(gw['rw_w2'], "col", LW), Stacked(gw['rw_a2'], "col", LA), Stacked(gw['rw_g2'], "col", LG)

    ones2 = (((lax.broadcasted_iota(jnp.int32, (2 * LANES, LANES), 0) & (LANES - 1)) >= HEAD)
             == (lax.broadcasted_iota(jnp.int32, (2 * LANES, LANES), 1) >= HEAD)).astype(BF16)
    n1g, n2g = p['norm1_g'], p['norm2_g']
    kkp, ka, lnw, lnb = p['rw_kk'], p['rw_ka'], p['rw_lnw'], p['rw_lnb']
    rk = p['rw_rk'].reshape(1, D)
    fg = p['final_g'].reshape(1, D)

    xin = jnp.concatenate([ctx, x], axis=0)
    nm = functools.partial(f_norm_mod, nct)
    nm_consts = [n1g[0:1], sh1c, sc1c, sh1x[0], sc1x[0]]
    h = row_call("l0_norm", nm, [xin], nm_consts, [(D, F32)], tr=TR, sr=16)[0]
    xr, xw, xk, xv, xa, xg = mix_fwd("l0_mix", h, mix_f, L)
    r = mm_nn("l0_r", xr, Wr)
    k = mm_nn("l0_k", xk, Wk)
    v = mm_nn("l0_v", xv, Wv)
    gl = mm_nn("l0_gl", xg, G1)
    sg = row_call("l0_sg", f_sigmoid, [gl], [], [(LG, BF16)], tr=TR, sr=16)[0]
    g = mm_nn("l0_g", sg, G2)
    wl, tw, zw, al, za = [], [], [], [], []
    for d in range(2):
        wl.append(mm_nn(f"l0_wl{d}", xw, W1.at(d)))
        tw.append(row_call(f"l0_tw{d}", f_tanh, [wl[d]], [], [(LW, BF16)], tr=TR, sr=16)[0])
        zw.append(mm_nn(f"l0_zw{d}", tw[d], W2.at(d)))
        al.append(mm_nn(f"l0_al{d}", xa, A1.at(d), BF16))
        za.append(mm_nn(f"l0_za{d}", al[d], A2.at(d)))
    post_rows = [k, zw[0], zw[1], za[0], za[1]]
    post_consts = [kkp, ka, w0_f[0:1], w0_f[1:2], a0_f[0:1], a0_f[1:2], ones2]
    aa, dec0, dec1, kd0, kd1, bb0, bb1, ksum = row_call(
        "l0_post", f_post_fwd, post_rows, post_consts, [(D, F32)] * 8, tr=TS, sr=16)
    dec, kd, bb = (dec0, dec1), (kd0, kd1), (bb0, bb1)
    ys, sts = [], []
    for d in range(2):
        y_d, st_d, filled = scan_fwd(f"l0_scan{d}", r, dec[d], kd[d], v, aa, bb[d], L, bool(d),
                                     side=gather_side(late_slots) if d == 0 else None)
        ys.append(y_d)
        sts.append(st_d)
        if d == 0:
            gw.update(zip(late_names, filled))
    Win, Wout = Stacked(gw['sc_win'], "col", D), Stacked(gw['sc_wout'], "row", Dq)
    W13, W2f = Stacked(gw['ffn_w13'], "col", D), Stacked(gw['ffn_w2'], "row", F4)
    ro_rows = [ys[0], ys[1], r, ksum, v, g]
    ro_consts = [rk, lnw, lnb, ones2]
    og = row_call("l0_readout", f_readout, ro_rows, ro_consts, [(D, BF16)], tr=TS, sr=16)[0]
    yx = mm_nn("l0_o", og, Wo)
    res0_consts = [gt1x[0], n2g[0:1], sh2x[0], sc2x[0]]
    x1, h2 = row_call("l0_res", f_res_norm_mod, [x, yx], res0_consts, [(D, F32), (D, BF16)], tr=TR, sr=16,
                      offs=[0, nct])
    ab0 = mm_nn("l0_ffn13", h2, W13.at(0))
    sw0 = swiglu_fwd("l0_swiglu", ab0, TS)
    f0 = mm_nn("l0_ffn2", sw0, W2f.at(0))

    res1_consts = [gt2x[0], n1g[1:2], sh1x[1], sc1x[1]]
    x2, hb = row_call("l1_norm", f_res_norm_mod, [x1, f0], res1_consts, [(D, F32), (D, BF16)], tr=TR, sr=16)
    gcu = mm_nn("l1_win", hb, Win)
    pc = conv_fwd("l1_conv", gcu, conv_f)
    yx1 = mm_nn("l1_wout", pc, Wout)
    res2_consts = [gt1x[1], n2g[1:2], sh2x[1], sc2x[1]]
    x3, h2b = row_call("l1_res", f_res_norm_mod, [x2, yx1], res2_consts, [(D, F32), (D, BF16)], tr=TR, sr=16)
    ab1 = mm_nn("l1_ffn13", h2b, W13.at(1))
    sw1 = swiglu_fwd("l1_swiglu", ab1, TS)
    f1 = mm_nn("l1_ffn2", sw1, W2f.at(1))
    dx3, df1, dgt2_1, dfg, loss_blk = final_call("final", x3, f1, gt2x[1], fg, tgt, TR)
    loss = lax.psum(loss_blk[0, 0], ("x", "y", "c"))

    big = []
    dsw1 = mm_nt("b1_dsw", df1, W2f.at(1))
    gW2f1 = mm_tn("b1_gw2", sw1, df1, "row")
    dab1 = swiglu_bwd("b1_swiglu", ab1, dsw1, TS)
    dh2b = mm_nt("b1_dh2", dab1, W13.at(1))
    gW13_1 = mm_tn("b1_gw13", h2b, dab1, "col")
    rm = [True, True]
    cm = [True] * 4
    (dx2, dyx1), (dgt1_1, dn2g1, dsh2_1, dsc2_1) = row_vjp(
        "b1_res", f_res_norm_mod, [x2, yx1], res2_consts, [dx3, dh2b], row_mask=rm, const_mask=cm, tr=TR, sr=16,
        bf16_rows=(1,))
    dpc = mm_nt("b1_dpc", dyx1, Wout)
    big.append(('sc_wout', mm_tn("b1_gwout", pc, dyx1, "row")))
    dgcu, dconv = conv_bwd("b1_conv", gcu, conv_f, dpc)
    dhb = mm_nt("b1_dhb", dgcu, Win)
    big.append(('sc_win', mm_tn("b1_gwin", hb, dgcu, "col")))
    (dx1, df0), (dgt2_0, dn1g1, dsh1_1, dsc1_1) = row_vjp(
        "b1_norm", f_res_norm_mod, [x1, f0], res1_consts, [dx2, dhb], row_mask=rm, const_mask=cm, tr=TR, sr=16,
        bf16_rows=(1,))

    dsw0 = mm_nt("b0_dsw", df0, W2f.at(0))
    gW2f0 = mm_tn("b0_gw2", sw0, df0, "row")
    dab0 = swiglu_bwd("b0_swiglu", ab0, dsw0, TS)
    dh2 = mm_nt("b0_dh2", dab0, W13.at(0))
    gW13_0 = mm_tn("b0_gw13", h2, dab0, "col")
    (dx_a, dyx), (dgt1_0, dn2g0, dsh2_0, dsc2_0) = row_vjp(
        "b0_res", f_res_norm_mod, [x, yx], res0_consts, [dx1, dh2], row_mask=rm, const_mask=cm, tr=TR, sr=16,
        offs=[0, nct], bf16_rows=(1,))
    dyx_all = jnp.concatenate([jnp.zeros((L, D), BF16), dyx], axis=0)
    dog = mm_nt("b0_dog", dyx_all, Wo)
    gWo = mm_tn("b0_gwo", og, dyx_all, "row")
    (dy, dr_ro, dksum, dv_ro, dg), (drk, dlnw, dlnb) = row_vjp(
        "b0_readout", f_readout, ro_rows, ro_consts, [dog], row_mask=[True, False, True, True, True, True],
        const_mask=[True, True, True, False], tr=TS, sr=16, bf16_rows=(5,))
    late_items = [it for it in big if it[0] in ('sc_win', 'sc_wout')] + [
        ('ffn_w13', gW13_0), ('ffn_w13', gW13_1), ('ffn_w2', gW2f0), ('ffn_w2', gW2f1)]
    late_sums = _rs_pair_sums("late", late_items, ci_arr)
    (dr0, ddec0, dkd0, dv0, daa0, dbb0), landed_flat = scan_bwd(
        "b0_scan0", r, dec[0], kd[0], v, aa, bb[0], dy, sts[0], L, False, side=rs2_side([s_[1] for s_ in late_sums]))
    (dr1, ddec1, dkd1, dv1, daa1, dbb1), _ = scan_bwd("b0_scan1", r, dec[1], kd[1], v, aa, bb[1], dy, sts[1], L, True)
    late_landed = [landed_flat[3 * w:3 * w + 3] for w in range(len(late_items))]
    post_cots = [daa0, daa1, ddec0, ddec1, dkd0, dkd1, dbb0, dbb1, dksum]
    (dk, dzw0, dzw1, dza0, dza1), (dkkp, dka, dw00, dw01, da00, da01) = row_vjp(
        "b0_post", f_post, post_rows, post_consts, post_cots, row_mask=[True] * 5,
        const_mask=[True] * 6 + [False], tr=TS, sr=16, bf16_rows=(0, 1, 2, 3, 4))
    dzw, dza = (dzw0, dzw1), (dza0, dza1)
    dr_t = sum_cast("b0_drsum", [dr0, dr1, dr_ro], BF16, TR)
    dv_t = sum_cast("b0_dvsum", [dv0, dv1, dv_ro], BF16, TR)
    mix_cots, mix_slots = [], []

    def back(tag, cot, w, xin_m, kind, slot):
        mix_cots.append(mm_nt("b0_dx" + tag, cot, w))
        mix_slots.append(slot)
        return mm_tn("b0_gw" + tag, xin_m, cot, kind)

    big.append(('rw_wr', back("r", dr_t, Wr, xr, "row", 0)))
    big.append(('rw_wk', back("k", dk, Wk, xk, "row", 2)))
    big.append(('rw_wv', back("v", dv_t, Wv, xv, "row", 3)))
    big.append(('rw_wo', gWo))
    dsg = mm_nt("b0_dsg", dg, G2)
    gG2 = mm_tn("b0_gg2", sg, dg, "col")
    (dgl,), _ = row_vjp("b0_sg", f_sigmoid, [gl], [], [dsg], row_mask=[True], const_mask=[], tr=TR, sr=16,
                        bf16_rows=(0,))
    gG1 = back("g", dgl, G1, xg, "row", 5)
    gW1, gW2, gA1, gA2 = [], [], [], []
    for d in range(2):
        dtw = mm_nt(f"b0_dtw{d}", dzw[d], W2.at(d))
        gW2.append(mm_tn(f"b0_gw2{d}", tw[d], dzw[d], "col"))
        (dwl,), _ = row_vjp(f"b0_tw{d}", f_tanh, [wl[d]], [], [dtw], row_mask=[True], const_mask=[], tr=TR, sr=16,
                            bf16_rows=(0,))
        gW1.append(back(f"w{d}", dwl, W1.at(d), xw, "row", 1))
        dal = mm_nt(f"b0_dal{d}", dza[d], A2.at(d), BF16)
        gA2.append(mm_tn(f"b0_ga2{d}", al[d], dza[d], "col"))
        gA1.append(back(f"a{d}", dal, A1.at(d), xa, "row", 4))
    big += [('rw_w1', gW1[0]), ('rw_w1', gW1[1]), ('rw_w2', gW2[0]), ('rw_w2', gW2[1]),
            ('rw_a1', gA1[0]), ('rw_a1', gA1[1]), ('rw_a2', gA2[0]), ('rw_a2', gA2[1]),
            ('rw_g1', gG1), ('rw_g2', gG2)]
    dh, dmix = mix_bwd("b0_mix", h, mix_f, mix_cots, mix_slots, L)
    (dxin,), (dn1g0, dsh1c, dsc1c, dsh1x, dsc1x) = row_vjp(
        "b0_norm", nm, [xin], nm_consts, [dh], row_mask=[True], const_mask=[True] * 5, tr=TR, sr=16)
    grad_x = sum_cast("b0_dx", [dxin, dx_a], F32, TR, offs=[nct, 0])

    zero = jnp.zeros((1, D), F32)
    parts = [dsh1x, dsc1x, dgt1_0, dsh2_0, dsc2_0, dgt2_0, dsh1c, dsc1c, zero, zero, zero, zero,
             dsh1_1, dsc1_1, dgt1_1, dsh2_1, dsc2_1, dgt2_1, zero, zero, zero, zero, zero, zero,
             dn1g0, dn1g1, dn2g0, dn2g1, dkkp, dka, drk, dlnw, dlnb, dfg,
             dmix[0:6], dw00, dw01, da00, da01, dconv[0:3]]
    got2 = all_gather8("ag_grads", pack_rows("pack_grads", parts, 48))
    small = sum_blocks("sum_grads", got2, list(range(N_DEV)))
    per_ex = got2[:, 0:24].reshape(N_DEV, 2, 2, 6 * D)
    tot = small[0:24].reshape(2, 2, 6 * D)
    cols = lambda a, width: lax.dynamic_slice_in_dim(a, s * width, width, axis=1)
    g_ada_w, dcond_parts = [], []
    for i in range(2):
        dm16 = cols(jnp.concatenate([per_ex[:, i, 0], _pad_rows(tot[i, 1][None], 8)], axis=0), 6 * Dq)
        g_ada_w.append(mm_tn(f"g_ada{i}", cond, dm16))
        dcond_parts.append(mm_nt(f"dcond{i}", dm16, ada.at(i)))
    g_ada_b = sum_cast("g_adab", [_pad_rows(tot[:, 0].reshape(12, D), 16), _pad_rows(tot[:, 1].reshape(12, D), 16)],
                       F32, 16)[0:12].reshape(2, 6 * D)
    dcond_mine = sum_cast("dcond_sum", dcond_parts, F32, 16)
    dcond = sum_blocks("dcond_chips", all_gather8("ag_dcond", dcond_mine), [0, 2, 4, 6])
    (dcin,), _ = row_vjp("b_cond", f_silu, [cond_in], [], [dcond], row_mask=[True], const_mask=[], tr=16, sr=16)

    gsh = _reduce_scatter("rw", [it for it in big if it[0] in rw_names], ci_arr, s)
    gsh.update(_rs_finish("late", late_items, late_sums, late_landed, s))

    grads = {}
    grads['c_ctx'] = dcin[8]
    grads['norm1_g'], grads['norm2_g'] = small[24:26], small[26:28]
    grads['ada_w'] = jnp.stack(g_ada_w)
    grads['ada_b'] = g_ada_b
    grads['rw_kk'], grads['rw_ka'], grads['rw_rk'] = small[28:29], small[29:30], small[30:31]
    grads['rw_lnw'], grads['rw_lnb'], grads['final_g'] = small[31:32], small[32:33], small[33]
    sharded = cols(small[34:48], Dq)
    grads['rw_mix'], grads['rw_w0'], grads['rw_a0'], grads['sc_conv'] = (
        sharded[0:6], sharded[6:8], sharded[8:10], sharded[10:13])

    outs_g, outs_d, outs_m, outs_v = [], [], [], []
    for n in WEIGHTS:
        shape = p[n].shape
        w2d, m2d, v2d = _view2d(n, p[n]), _view2d(n, p['m_' + n]), _view2d(n, p['v_' + n])
        if n in gsh:
            g2d, d_, m_, v_ = adamw_halves("adam_" + n, w2d, m2d, v2d, gsh[n][0], gsh[n][1], ci_arr)
        else:
            g2d = _view2d(n, grads[n].reshape(shape))
            d_, m_, v_ = adamw("adam_" + n, w2d, g2d, m2d, v2d)
        outs_g.append(g2d.reshape(shape))
        outs_d.append(d_.reshape(shape))
        outs_m.append(m_.reshape(shape))
        outs_v.append(v_.reshape(shape))
    return (loss, grad_x.reshape(1, T, D), *outs_g, *outs_d, *outs_m, *outs_v)


def kernel(x, c, ctx, c_ctx, norm1_g, norm2_g, ada_w, ada_b, rw_mix, rw_wr, rw_wk, rw_wv, rw_wo, rw_w0, rw_w1, rw_w2, rw_a0, rw_a1, rw_a2, rw_g1, rw_g2, rw_kk, rw_ka, rw_rk, rw_lnw, rw_lnb, sc_win, sc_conv, sc_wout, ffn_w13, ffn_w2, final_g, loss_target, m_c_ctx, m_norm1_g, m_norm2_g, m_ada_w, m_ada_b, m_rw_mix, m_rw_wr, m_rw_wk, m_rw_wv, m_rw_wo, m_rw_w0, m_rw_w1, m_rw_w2, m_rw_a0, m_rw_a1, m_rw_a2, m_rw_g1, m_rw_g2, m_rw_kk, m_rw_ka, m_rw_rk, m_rw_lnw, m_rw_lnb, m_sc_win, m_sc_conv, m_sc_wout, m_ffn_w13, m_ffn_w2, m_final_g, v_c_ctx, v_norm1_g, v_norm2_g, v_ada_w, v_ada_b, v_rw_mix, v_rw_wr, v_rw_wk, v_rw_wv, v_rw_wo, v_rw_w0, v_rw_w1, v_rw_w2, v_rw_a0, v_rw_a1, v_rw_a2, v_rw_g1, v_rw_g2, v_rw_kk, v_rw_ka, v_rw_rk, v_rw_lnw, v_rw_lnb, v_sc_win, v_sc_conv, v_sc_wout, v_ffn_w13, v_ffn_w2, v_final_g):
    values = (x, c, ctx, c_ctx, norm1_g, norm2_g, ada_w, ada_b, rw_mix, rw_wr, rw_wk, rw_wv, rw_wo, rw_w0, rw_w1, rw_w2, rw_a0, rw_a1, rw_a2, rw_g1, rw_g2, rw_kk, rw_ka, rw_rk, rw_lnw, rw_lnb, sc_win, sc_conv, sc_wout, ffn_w13, ffn_w2, final_g, loss_target, m_c_ctx, m_norm1_g, m_norm2_g, m_ada_w, m_ada_b, m_rw_mix, m_rw_wr, m_rw_wk, m_rw_wv, m_rw_wo, m_rw_w0, m_rw_w1, m_rw_w2, m_rw_a0, m_rw_a1, m_rw_a2, m_rw_g1, m_rw_g2, m_rw_kk, m_rw_ka, m_rw_rk, m_rw_lnw, m_rw_lnb, m_sc_win, m_sc_conv, m_sc_wout, m_ffn_w13, m_ffn_w2, m_final_g, v_c_ctx, v_norm1_g, v_norm2_g, v_ada_w, v_ada_b, v_rw_mix, v_rw_wr, v_rw_wk, v_rw_wv, v_rw_wo, v_rw_w0, v_rw_w1, v_rw_w2, v_rw_a0, v_rw_a1, v_rw_a2, v_rw_g1, v_rw_g2, v_rw_kk, v_rw_ka, v_rw_rk, v_rw_lnw, v_rw_lnb, v_sc_win, v_sc_conv, v_sc_wout, v_ffn_w13, v_ffn_w2, v_final_g)
    return _step(dict(zip(INPUTS, values)))
```

```python
import functools
import math

import jax
import jax.numpy as jnp
from jax import lax
from jax.experimental import pallas as pl
from jax.experimental.pallas import tpu as pltpu

F32 = jnp.float32
BF16 = jnp.bfloat16
MESH = pl.DeviceIdType.MESH

GRID_W = 64
HEAD = 64
LANES = 128
N_CHIPS = 4
N_DEV = 8
NORM_EPS = 1e-6
GN_EPS = 64e-5
ADAM_LR, ADAM_B1, ADAM_B2, ADAM_EPS, ADAM_WD, ADAM_STEP = 0.001, 0.9, 0.999, 1e-08, 0.01, 10
VMEM_LIMIT = 56 * 1024 * 1024
HI = lax.Precision.HIGHEST
WEIGHTS = ['c_ctx', 'norm1_g', 'norm2_g', 'ada_w', 'ada_b', 'rw_mix', 'rw_wr', 'rw_wk', 'rw_wv', 'rw_wo', 'rw_w0',
           'rw_w1', 'rw_w2', 'rw_a0', 'rw_a1', 'rw_a2', 'rw_g1', 'rw_g2', 'rw_kk', 'rw_ka', 'rw_rk', 'rw_lnw',
           'rw_lnb', 'sc_win', 'sc_conv', 'sc_wout', 'ffn_w13', 'ffn_w2', 'final_g']
INPUTS = (['x', 'c', 'ctx'] + WEIGHTS + ['loss_target'] + ['m_' + w for w in WEIGHTS]
          + ['v_' + w for w in WEIGHTS])


def _cparams(**kw):
    return pltpu.CompilerParams(vmem_limit_bytes=VMEM_LIMIT, **kw)


def _pick(dim, cands):
    for c in cands:
        if dim % c == 0:
            return c
    return dim


def _place():
    return lax.axis_index("x"), lax.axis_index("y"), lax.axis_index("c")


_TILE_M = (1024, 768, 512, 1408, 256, 128)
_TILE_N = (1408, 1024, 768, 512, 256, 128)
_TILE_K = (2816, 2304, 2048, 1536, 1408, 1152, 1024, 768, 704, 512, 256, 128)
MM_TILE_BYTES = 40 * 1024 * 1024


def _pick_k(unit, tm, tn, a_dtype, b_dtype, o_dtype):
    ab, bb, ob = (jnp.dtype(d).itemsize for d in (a_dtype, b_dtype, o_dtype))
    for tk in _TILE_K:
        if unit % tk == 0 and 2 * (tm * tk * ab + tk * tn * bb) + 2 * tm * tn * ob + tm * tn * 4 <= MM_TILE_BYTES:
            return tk
    return unit


class Stacked:
    def __init__(self, arr, kind, r, layer=0):
        self.arr, self.kind, self.r, self.layer = arr, kind, r, layer
        self.c = arr.shape[2]
        self.shape = {"row": (N_CHIPS * r, self.c), "col": (r, N_CHIPS * self.c), "layer": (r, self.c)}[kind]

    def at(self, layer):
        return Stacked(self.arr, self.kind, self.r, layer)

    def spec(self, t0, t1, swap):
        r, c, layer = self.r, self.c, self.layer
        per_r, per_c = r // t0, c // t1
        assert r % t0 == 0 and c % t1 == 0
        kind = self.kind

        def index(i, j, k):
            ri, ci = (j, k) if swap else (k, j)
            if kind == "row":
                return (ri // per_r, layer * per_r + ri % per_r, ci)
            if kind == "layer":
                return (layer, ri, ci)
            return (ci // per_c, layer * per_r + ri, ci % per_c)

        return pl.BlockSpec((None, t0, t1), index)


def _mm_body(dims, nk, a_ref, b_ref, o_ref, acc_ref):
    if nk == 1:
        o_ref[...] = lax.dot_general(a_ref[...].astype(BF16), b_ref[...].astype(BF16), (dims, ((), ())),
                                     preferred_element_type=F32).astype(o_ref.dtype)
        return
    k = pl.program_id(2)

    @pl.when(k == 0)
    def _():
        acc_ref[...] = jnp.zeros_like(acc_ref)

    acc_ref[...] += lax.dot_general(a_ref[...].astype(BF16), b_ref[...].astype(BF16), (dims, ((), ())),
                                    preferred_element_type=F32)

    @pl.when(k == nk - 1)
    def _():
        o_ref[...] = acc_ref[...].astype(o_ref.dtype)


def _mm_call(name, dims, grid, in_specs, out_spec, out_shape, acc_shape, operands):
    return pl.pallas_call(
        functools.partial(_mm_body, dims, grid[2]), name=name, grid=grid, in_specs=in_specs, out_specs=out_spec,
        out_shape=out_shape, scratch_shapes=[pltpu.VMEM(acc_shape if grid[2] > 1 else (8, LANES), F32)],
        compiler_params=_cparams(dimension_semantics=("parallel", "parallel", "arbitrary")),
    )(*operands)


def mm_nn(name, a, b, out_dtype=F32):
    M, K = a.shape
    st = isinstance(b, Stacked)
    N = b.shape[1]
    tm = _pick(M, _TILE_M)
    tn = _pick(b.c if st and b.kind == "col" else N, _TILE_N)
    tk = _pick_k(b.r if st else K, tm, tn, a.dtype, b.arr.dtype if st else b.dtype, out_dtype)
    b_spec = b.spec(tk, tn, False) if st else pl.BlockSpec((tk, tn), lambda i, j, k: (k, j))
    return _mm_call(name, ((1,), (0,)), (M // tm, N // tn, K // tk),
                    [pl.BlockSpec((tm, tk), lambda i, j, k: (i, k)), b_spec],
                    pl.BlockSpec((tm, tn), lambda i, j, k: (i, j)), jax.ShapeDtypeStruct((M, N), out_dtype),
                    (tm, tn), (a, b.arr if st else b))


def mm_nt(name, a, b, out_dtype=F32):
    M, N = a.shape
    st = isinstance(b, Stacked)
    K = b.shape[0]
    tm = _pick(M, _TILE_M)
    to = _pick(b.r if st else K, _TILE_N)
    tc = _pick_k(b.c if st and b.kind == "col" else N, tm, to, a.dtype, b.arr.dtype if st else b.dtype, out_dtype)
    b_spec = b.spec(to, tc, True) if st else pl.BlockSpec((to, tc), lambda i, j, k: (j, k))
    return _mm_call(name, ((1,), (1,)), (M // tm, K // to, N // tc),
                    [pl.BlockSpec((tm, tc), lambda i, j, k: (i, k)), b_spec],
                    pl.BlockSpec((tm, to), lambda i, j, k: (i, j)), jax.ShapeDtypeStruct((M, K), out_dtype),
                    (tm, to), (a, b.arr if st else b))


def mm_tn(name, a, b, kind=None):
    R, M = a.shape
    N = b.shape[1]
    r, c = (M // N_CHIPS, N) if kind == "row" else (M, N // N_CHIPS) if kind == "col" else (M, N)
    tm, tn = _pick(r, _TILE_M), _pick(c, _TILE_N)
    tk = _pick_k(R, tm, tn, a.dtype, b.dtype, F32)
    if kind:
        per_r, per_c = r // tm, c // tn
        if kind == "row":
            o_spec = pl.BlockSpec((None, tm, tn), lambda i, j, k: (i // per_r, i % per_r, j))
        else:
            o_spec = pl.BlockSpec((None, tm, tn), lambda i, j, k: (j // per_c, i, j % per_c))
        o_shape = jax.ShapeDtypeStruct((N_CHIPS, r, c), F32)
    else:
        o_spec = pl.BlockSpec((tm, tn), lambda i, j, k: (i, j))
        o_shape = jax.ShapeDtypeStruct((M, N), F32)
    return _mm_call(name, ((0,), (0,)), (M // tm, N // tn, R // tk),
                    [pl.BlockSpec((tk, tm), lambda i, j, k: (k, i)), pl.BlockSpec((tk, tn), lambda i, j, k: (k, j))],
                    o_spec, o_shape, (tm, tn), (a, b))


def _shifted(o):
    return lambda i: (i + o, 0)


def row_call(name, f, rows, consts, outs, *, tr, sr, offs=None):
    offs = offs or [0] * len(rows)
    n_rows = min(r.shape[0] - o * tr for r, o in zip(rows, offs))
    nr, nc = len(rows), len(consts)

    def body(*refs):
        row_refs, const_refs, out_refs = refs[:nr], refs[nr:nr + nc], refs[nr + nc:]
        i = pl.program_id(0)
        cvals = [r[...] for r in const_refs]

        def step(j, carry):
            sl = pl.ds(pl.multiple_of(j * sr, sr), sr)
            res = f(i, *[r[sl, :] for r in row_refs], *cvals)
            for o, v in zip(out_refs, res):
                o[sl, :] = v.astype(o.dtype)
            return carry

        lax.fori_loop(0, tr // sr, step, 0)

    in_specs = [pl.BlockSpec((tr, r.shape[1]), _shifted(o)) for r, o in zip(rows, offs)]
    in_specs += [pl.BlockSpec(c.shape, lambda i: (0, 0)) for c in consts]
    return pl.pallas_call(
        body, name=name, grid=(n_rows // tr,), in_specs=in_specs,
        out_specs=[pl.BlockSpec((tr, w), lambda i: (i, 0)) for w, _ in outs],
        out_shape=[jax.ShapeDtypeStruct((n_rows, w), dt) for w, dt in outs],
        compiler_params=_cparams(dimension_semantics=("parallel",)),
    )(*rows, *consts)


def row_vjp(name, f, rows, consts, cots, *, row_mask, const_mask, tr, sr, offs=None, bf16_rows=()):
    offs = offs or [0] * len(rows)
    n_rows = min(r.shape[0] - o * tr for r, o in zip(rows, offs))
    nr, nc = len(rows), len(consts)
    cot_in = [c for c in cots if c is not None]
    nct = len(cot_in)
    d_rows = [i for i in range(nr) if row_mask[i]]
    d_consts = [i for i in range(nc) if const_mask[i]]

    def body(*refs):
        row_refs, const_refs = refs[:nr], refs[nr:nr + nc]
        cot_refs = refs[nr + nc:nr + nc + nct]
        drow_refs = refs[nr + nc + nct:nr + nc + nct + len(d_rows)]
        dconst_refs = refs[nr + nc + nct + len(d_rows):]
        i = pl.program_id(0)

        @pl.when(i == 0)
        def _():
            for r in dconst_refs:
                r[...] = jnp.zeros_like(r)

        cvals = [r[...] for r in const_refs]

        def step(j, carry):
            sl = pl.ds(pl.multiple_of(j * sr, sr), sr)
            rvals = [r[sl, :] for r in row_refs]

            def g(*diff):
                rv, cv = list(rvals), list(cvals)
                for idx, val in zip(d_rows, diff[:len(d_rows)]):
                    rv[idx] = val
                for idx, val in zip(d_consts, diff[len(d_rows):]):
                    cv[idx] = val
                return f(i, *rv, *cv)

            primals = [rvals[idx].astype(F32) for idx in d_rows] + [cvals[idx] for idx in d_consts]
            res, vjp = jax.vjp(g, *primals)
            it = iter(cot_refs)
            cts = tuple(jnp.zeros_like(o) if c is None else next(it)[sl, :].astype(o.dtype) for o, c in zip(res, cots))
            grads = vjp(cts)
            for r, val in zip(drow_refs, grads[:len(d_rows)]):
                r[sl, :] = val.astype(r.dtype)
            for r, val in zip(dconst_refs, grads[len(d_rows):]):
                r[...] += val
            return carry

        lax.fori_loop(0, tr // sr, step, 0)

    in_specs = [pl.BlockSpec((tr, r.shape[1]), _shifted(o)) for r, o in zip(rows, offs)]
    in_specs += [pl.BlockSpec(c.shape, lambda i: (0, 0)) for c in consts]
    in_specs += [pl.BlockSpec((tr, c.shape[1]), lambda i: (i, 0)) for c in cot_in]
    out_specs = [pl.BlockSpec((tr, rows[i].shape[1]), lambda i: (i, 0)) for i in d_rows]
    out_specs += [pl.BlockSpec(consts[i].shape, lambda i: (0, 0)) for i in d_consts]
    out_shape = [jax.ShapeDtypeStruct((n_rows, rows[i].shape[1]), BF16 if i in bf16_rows else F32) for i in d_rows]
    out_shape += [jax.ShapeDtypeStruct(consts[i].shape, F32) for i in d_consts]
    res = pl.pallas_call(
        body, name=name, grid=(n_rows // tr,), in_specs=in_specs, out_specs=out_specs, out_shape=out_shape,
        compiler_params=_cparams(dimension_semantics=("arbitrary",)),
    )(*rows, *consts, *cot_in)
    return list(res[:len(d_rows)]), list(res[len(d_rows):])


def _sigmoid(x):
    return 1.0 / (1.0 + jnp.exp(-x))


def _softplus(u):
    return jnp.maximum(u, 0.0) + jnp.log(1.0 + jnp.exp(-jnp.abs(u)))


def _rms(x, g):
    ms = jnp.sum(x * x, axis=-1, keepdims=True) * (1.0 / x.shape[-1])
    return x * lax.rsqrt(ms + NORM_EPS) * g


def _hsum_impl(x, ones2):
    rows, width = x.shape
    nch = width // LANES
    xs = jnp.concatenate([x[:, j * LANES:(j + 1) * LANES] for j in range(nch)], axis=0)
    hi = xs.astype(BF16)
    lo = (xs - hi.astype(F32)).astype(BF16)
    ys = jnp.dot(jnp.concatenate([hi, lo], axis=1), ones2, preferred_element_type=F32)
    return jnp.concatenate([ys[j * rows:(j + 1) * rows] for j in range(nch)], axis=1)


@jax.custom_vjp
def _hsum(x, ones2):
    return _hsum_impl(x, ones2)


def _hsum_fwd(x, ones2):
    return _hsum_impl(x, ones2), ones2


def _hsum_bwd(ones2, g):
    return _hsum_impl(g, ones2), jnp.zeros_like(ones2)


_hsum.defvjp(_hsum_fwd, _hsum_bwd)


def f_silu(i, x):
    return (x * _sigmoid(x),)


def f_sigmoid(i, x):
    return (_sigmoid(x),)


def f_tanh(i, x):
    return (jnp.tanh(x),)


def f_norm_mod(n_ctx_tiles, i, xin, g, sh_c, sc_c, sh_x, sc_x):
    is_x = i >= n_ctx_tiles
    sh = jnp.where(is_x, sh_x, sh_c)
    sc = jnp.where(is_x, sc_x, sc_c)
    return (_rms(xin, g) * (1.0 + sc) + sh,)


def f_res_norm_mod(i, x, y, gt, g, sh, sc):
    x1 = x + gt * y
    return x1, _rms(x1, g) * (1.0 + sc) + sh


def f_post(i, k, zw0, zw1, za0, za1, kkp, ka, w00, w01, a00, a01, ones2):
    kq = k * kkp
    kk = kq / jnp.maximum(jnp.sqrt(_hsum(kq * kq, ones2)), 1e-12)

    def direction(zw, za, w0, a0):
        log_w = -_softplus(-(w0 + zw)) - 0.5
        a = _sigmoid(a0 + za)
        return jnp.exp(-jnp.exp(log_w)), k * (1.0 + (a - 1.0) * ka), kk * a

    dec0, kd0, bb0 = direction(zw0, za0, w00, a00)
    dec1, kd1, bb1 = direction(zw1, za1, w01, a01)
    return -kk, -kk, dec0, dec1, kd0, kd1, bb0, bb1, kd0 + kd1


def f_post_fwd(*a):
    return f_post(*a)[1:]


def f_readout(i, y0, y1, r, ksum, v, g, rk, lnw, lnb, ones2):
    y = y0 + y1
    yc = y - _hsum(y, ones2) * (1.0 / HEAD)
    var = _hsum(yc * yc, ones2) * (1.0 / HEAD)
    o = yc * lax.rsqrt(var + GN_EPS) * lnw + lnb
    o = o + _hsum(r * ksum * rk, ones2) * v
    return (o * g,)


def f_sum(i, *xs):
    acc = xs[0].astype(F32)
    for x in xs[1:]:
        acc = acc + x.astype(F32)
    return (acc,)


def sum_cast(name, arrs, dtype, tr, offs=None):
    return row_call(name, f_sum, arrs, [], [(arrs[0].shape[1], dtype)], tr=tr, sr=16, offs=offs)[0]


def swiglu_fwd(name, ab, tr):
    T, F2 = ab.shape
    F = F2 // 2
    sr = 16

    def body(ab_ref, o_ref):
        def step(j, carry):
            sl = pl.ds(pl.multiple_of(j * sr, sr), sr)
            a, b = ab_ref[sl, :F], ab_ref[sl, F:]
            o_ref[sl, :] = (a * _sigmoid(a) * b).astype(o_ref.dtype)
            return carry

        lax.fori_loop(0, tr // sr, step, 0)

    return pl.pallas_call(
        body, name=name, grid=(T // tr,), in_specs=[pl.BlockSpec((tr, F2), lambda i: (i, 0))],
        out_specs=pl.BlockSpec((tr, F), lambda i: (i, 0)), out_shape=jax.ShapeDtypeStruct((T, F), BF16),
        compiler_params=_cparams(dimension_semantics=("parallel",)),
    )(ab)


def swiglu_bwd(name, ab, dsw, tr):
    T, F2 = ab.shape
    F = F2 // 2
    sr = 16

    def body(ab_ref, d_ref, o_ref):
        def step(j, carry):
            sl = pl.ds(pl.multiple_of(j * sr, sr), sr)
            a, b, d = ab_ref[sl, :F], ab_ref[sl, F:], d_ref[sl, :]
            sg = _sigmoid(a)
            o_ref[sl, :F] = (d * b * (sg + a * sg * (1.0 - sg))).astype(o_ref.dtype)
            o_ref[sl, F:] = (d * a * sg).astype(o_ref.dtype)
            return carry

        lax.fori_loop(0, tr // sr, step, 0)

    return pl.pallas_call(
        body, name=name, grid=(T // tr,),
        in_specs=[pl.BlockSpec((tr, F2), lambda i: (i, 0)), pl.BlockSpec((tr, F), lambda i: (i, 0))],
        out_specs=pl.BlockSpec((tr, F2), lambda i: (i, 0)), out_shape=jax.ShapeDtypeStruct((T, F2), BF16),
        compiler_params=_cparams(dimension_semantics=("parallel",)),
    )(ab, dsw)


def final_call(name, x3, f1, gt, fg, tgt, tr):
    T, D = x3.shape
    sr = 16

    def f(x, y, gtv, g, t):
        err = _rms(x + gtv * y, g) - t
        return 0.5 * jnp.sum(err * err) * (1.0 / D)

    def body(x_ref, y_ref, gt_ref, g_ref, t_ref, dx_ref, dy_ref, dgt_ref, dg_ref, loss_ref):
        @pl.when(pl.program_id(0) == 0)
        def _():
            dgt_ref[...] = jnp.zeros_like(dgt_ref)
            dg_ref[...] = jnp.zeros_like(dg_ref)
            loss_ref[...] = jnp.zeros_like(loss_ref)

        def step(j, carry):
            sl = pl.ds(pl.multiple_of(j * sr, sr), sr)
            val, vjp = jax.vjp(lambda x, y, a, b: f(x, y, a, b, t_ref[sl, :]), x_ref[sl, :], y_ref[sl, :],
                               gt_ref[...], g_ref[...])
            dx, dy, dgt, dg = vjp(jnp.ones((), F32))
            dx_ref[sl, :] = dx
            dy_ref[sl, :] = dy.astype(dy_ref.dtype)
            dgt_ref[...] += dgt
            dg_ref[...] += dg
            loss_ref[...] += jnp.full(loss_ref.shape, val, F32)
            return carry

        lax.fori_loop(0, tr // sr, step, 0)

    row = pl.BlockSpec((tr, D), lambda i: (i, 0))
    vec = pl.BlockSpec((1, D), lambda i: (0, 0))
    return pl.pallas_call(
        body, name=name, grid=(T // tr,), in_specs=[row, row, vec, vec, row],
        out_specs=[row, row, vec, vec, pl.BlockSpec((8, LANES), lambda i: (0, 0))],
        out_shape=[jax.ShapeDtypeStruct((T, D), F32), jax.ShapeDtypeStruct((T, D), BF16)]
        + [jax.ShapeDtypeStruct((1, D), F32)] * 2
        + [jax.ShapeDtypeStruct((8, LANES), F32)],
        compiler_params=_cparams(dimension_semantics=("arbitrary",)),
    )(x3, f1, gt, fg, tgt)


def _tshift(x, kind, period):
    n = x.shape[0]
    t = lax.broadcasted_iota(jnp.int32, x.shape, 0)
    if kind == 0:
        return jnp.where((t & (period - 1)) == 0, 0.0, pltpu.roll(x, 1, 0))
    if kind == 1:
        return jnp.where(((t & (period - 1)) == period - 1) | (t == n - 1), 0.0, pltpu.roll(x, n - 1, 0))
    if kind == 2:
        return jnp.where(t < GRID_W, 0.0, pltpu.roll(x, GRID_W, 0))
    return jnp.where(t >= n - GRID_W, 0.0, pltpu.roll(x, n - GRID_W, 0))


def _pow2_at_least(n):
    return 1 << (n - 1).bit_length()


def _shift_into(dst_ref, h_ref, n_ctx, cb, D, transpose):
    j = pl.program_id(0)
    quarter = (j * cb * 4) // D
    half = (j * cb * 2) // D
    flip = 1 if transpose else 0
    for q in range(4):
        @pl.when(quarter == q)
        def _(q=q):
            dst_ref[n_ctx:, :] = _tshift(h_ref[n_ctx:, :], q ^ flip, GRID_W)
    for q in range(2):
        @pl.when(half == q)
        def _(q=q):
            dst_ref[:n_ctx, :] = _tshift(h_ref[:n_ctx, :], q ^ flip, _pow2_at_least(n_ctx))


def mix_fwd(name, h, mix, n_ctx):
    R, D = h.shape
    cb = LANES

    def body(h_ref, mix_ref, *rest):
        outs, hs_ref = rest[:6], rest[6]
        _shift_into(hs_ref, h_ref, n_ctx, cb, D, False)
        hv = h_ref[...]
        xx = hs_ref[...] - hv
        for m in range(6):
            outs[m][...] = (hv + xx * mix_ref[m:m + 1, :]).astype(BF16)

    col = pl.BlockSpec((R, cb), lambda j: (0, j))
    return pl.pallas_call(
        body, name=name, grid=(D // cb,), in_specs=[col, pl.BlockSpec((mix.shape[0], cb), lambda j: (0, j))],
        out_specs=[col] * 6, out_shape=[jax.ShapeDtypeStruct((R, D), BF16)] * 6,
        scratch_shapes=[pltpu.VMEM((R, cb), F32)],
        compiler_params=_cparams(dimension_semantics=("parallel",)),
    )(h, mix)


def mix_bwd(name, h, mix, cots, slots, n_ctx):
    R, D = h.shape
    cb = LANES
    nc = len(cots)

    def body(h_ref, mix_ref, *rest):
        cot_refs, dh_ref, dmix_ref, hs_ref, dxx_ref = rest[:nc], rest[nc], rest[nc + 1], rest[nc + 2], rest[nc + 3]
        _shift_into(hs_ref, h_ref, n_ctx, cb, D, False)
        xx = hs_ref[...] - h_ref[...]
        per_slot = [None] * 6
        for cref, m in zip(cot_refs, slots):
            per_slot[m] = cref[...] if per_slot[m] is None else per_slot[m] + cref[...]
        dh = jnp.zeros((R, cb), F32)
        dxx = jnp.zeros((R, cb), F32)
        rows = []
        for m in range(6):
            d = per_slot[m]
            dh = dh + d
            dxx = dxx + d * mix_ref[m:m + 1, :]
            rows.append(jnp.sum(d * xx, axis=0, keepdims=True))
        dmix_ref[...] = jnp.concatenate(rows + [jnp.zeros((2, cb), F32)], axis=0)
        dxx_ref[...] = dxx
        _shift_into(hs_ref, dxx_ref, n_ctx, cb, D, True)
        dh_ref[...] = dh - dxx + hs_ref[...]

    col = pl.BlockSpec((R, cb), lambda j: (0, j))
    return pl.pallas_call(
        body, name=name, grid=(D // cb,), in_specs=[col, pl.BlockSpec((mix.shape[0], cb), lambda j: (0, j))] + [col] * nc,
        out_specs=[col, pl.BlockSpec((8, cb), lambda j: (0, j))],
        out_shape=[jax.ShapeDtypeStruct((R, D), F32), jax.ShapeDtypeStruct((8, D), F32)],
        scratch_shapes=[pltpu.VMEM((R, cb), F32), pltpu.VMEM((R, cb), F32)],
        compiler_params=_cparams(dimension_semantics=("parallel",)),
    )(h, mix, *cots)


def _conv_parts(gb_ref, gc_ref, u_ref, cw_ref):
    T = gb_ref.shape[0]
    z = gc_ref[...] * u_ref[...]
    zp, zn = _tshift(z, 0, _pow2_at_least(T)), _tshift(z, 1, _pow2_at_least(T))
    conv = zp * cw_ref[0:1, :] + z * cw_ref[1:2, :] + zn * cw_ref[2:3, :]
    return z, zp, zn, conv


def conv_fwd(name, gcu, cw):
    T, D3 = gcu.shape
    D = D3 // 3
    cb = LANES
    nb = D // cb

    def body(gb_ref, gc_ref, u_ref, cw_ref, o_ref):
        _, _, _, conv = _conv_parts(gb_ref, gc_ref, u_ref, cw_ref)
        o_ref[...] = (gb_ref[...] * conv).astype(BF16)

    def part(p):
        return pl.BlockSpec((T, cb), lambda j: (0, j + p * nb))

    return pl.pallas_call(
        body, name=name, grid=(nb,),
        in_specs=[part(0), part(1), part(2), pl.BlockSpec((cw.shape[0], cb), lambda j: (0, j))],
        out_specs=pl.BlockSpec((T, cb), lambda j: (0, j)), out_shape=jax.ShapeDtypeStruct((T, D), BF16),
        compiler_params=_cparams(dimension_semantics=("parallel",)),
    )(gcu, gcu, gcu, cw)


def conv_bwd(name, gcu, cw, dp):
    T, D3 = gcu.shape
    D = D3 // 3
    cb = LANES
    nb = D // cb

    def body(gb_ref, gc_ref, u_ref, cw_ref, dp_ref, o_ref, dcw_ref):
        part = pl.program_id(1)
        z, zp, zn, conv = _conv_parts(gb_ref, gc_ref, u_ref, cw_ref)
        dpv = dp_ref[...]
        dconv = dpv * gb_ref[...]
        period = _pow2_at_least(T)
        dz = (_tshift(dconv * cw_ref[0:1, :], 1, period) + dconv * cw_ref[1:2, :]
              + _tshift(dconv * cw_ref[2:3, :], 0, period))

        @pl.when(part == 0)
        def _():
            o_ref[...] = (dpv * conv).astype(o_ref.dtype)
            dcw_ref[...] = jnp.concatenate(
                [jnp.sum(dconv * s, axis=0, keepdims=True) for s in (zp, z, zn)] + [jnp.zeros((5, cb), F32)], axis=0)

        @pl.when(part == 1)
        def _():
            o_ref[...] = (dz * u_ref[...]).astype(o_ref.dtype)

        @pl.when(part == 2)
        def _():
            o_ref[...] = (dz * gc_ref[...]).astype(o_ref.dtype)

    def part_spec(p):
        return pl.BlockSpec((T, cb), lambda j, q: (0, j + p * nb))

    return pl.pallas_call(
        body, name=name, grid=(nb, 3),
        in_specs=[part_spec(0), part_spec(1), part_spec(2), pl.BlockSpec((cw.shape[0], cb), lambda j, q: (0, j)),
                  pl.BlockSpec((T, cb), lambda j, q: (0, j))],
        out_specs=[pl.BlockSpec((T, cb), lambda j, q: (0, j + q * nb)), pl.BlockSpec((8, cb), lambda j, q: (0, j))],
        out_shape=[jax.ShapeDtypeStruct((T, D3), BF16), jax.ShapeDtypeStruct((8, D), F32)],
        compiler_params=_cparams(dimension_semantics=("arbitrary", "arbitrary")),
    )(gcu, gcu, gcu, cw, dp)


SCAN_TC = 8


def _scan_consts():
    rows = lax.broadcasted_iota(jnp.int32, (HEAD, LANES), 0)
    cols = lax.broadcasted_iota(jnp.int32, (HEAD, LANES), 1)
    eye = rows == (cols & (HEAD - 1))
    r2 = lax.broadcasted_iota(jnp.int32, (2 * LANES, LANES), 0)
    c2 = lax.broadcasted_iota(jnp.int32, (2 * LANES, LANES), 1)
    ones2 = (((r2 & (LANES - 1)) >= HEAD) == (c2 >= HEAD)).astype(BF16)
    return eye, ones2, ones2[:LANES]


SCAN_ROW_CHUNKS = 4


def _chunks_of_heads(hp):
    per = max(1, hp // SCAN_ROW_CHUNKS)
    return [range(lo, lo + per) for lo in range(0, hp, per)]


def _rows_of(heads):
    return pl.ds(heads[0] * HEAD, len(heads) * HEAD)


def _split2(p):
    hi = p.astype(BF16)
    lo = (p - hi.astype(F32)).astype(BF16)
    return jnp.concatenate([hi, lo], axis=1)


def _head_rows(h):
    return pl.ds(h * HEAD, HEAD)


def _expand_into(dst, p1_ref, row_ref, t, hp, eye, ones1):
    for h in range(hp):
        p1_ref[_head_rows(h), :] = jnp.where(eye, row_ref[t, h:h + 1, :], 0.0).astype(BF16)
    dst[...] = jnp.dot(p1_ref[...], ones1, preferred_element_type=F32)


def _colsum_store(ref, t, h, x):
    ref[t, pl.ds(h, 1), :] = jnp.sum(x, axis=0, keepdims=True)


def _order(i, n_ctx, n_all, rev):
    if not rev:
        return i
    return jnp.where(i < n_ctx, n_ctx - 1 - i, n_all - 1 - (i - n_ctx))


def _with_side(work, side, n_in, n_out, n_scratch, grid):
    if side is None:
        return work
    nsi, nso = len(side.ins), len(side.out_shapes)
    grid = (grid,) if isinstance(grid, int) else tuple(grid)

    def at(which):
        cond = None
        for d, g in enumerate(grid):
            c = pl.program_id(d) == (0 if which == "first" else g - 1)
            cond = c if cond is None else cond & c
        return cond

    def body(*refs):
        ins, side_in = refs[:n_in], refs[n_in:n_in + nsi]
        outs = refs[n_in + nsi:n_in + nsi + n_out]
        side_out = refs[n_in + nsi + n_out:n_in + nsi + n_out + nso]
        scratch = refs[n_in + nsi + n_out + nso:]

        @pl.when(at("first"))
        def _():
            side.start(side_in, side_out, scratch[n_scratch:])

        work(*ins, *outs, *scratch[:n_scratch])

        @pl.when(at("last"))
        def _():
            side.finish(side_in, side_out, scratch[n_scratch:])

    return body


def _side_call_args(side, n_in, n_out):
    if side is None:
        return [], [], [], [], {}, []
    aliases = {n_in + i: n_out + o for i, o in side.aliases.items()}
    return ([ANY] * len(side.ins), [ANY] * len(side.out_shapes), list(side.out_shapes), list(side.sems), aliases,
            list(side.ins))


def scan_fwd(name, r, w, k, v, a, b, n_ctx_rows, rev, side=None):
    R, D = r.shape
    hp, tc = D // LANES, SCAN_TC
    n_all, n_ctx = R // tc, n_ctx_rows // tc
    ins = [t.reshape(R, hp, LANES) for t in (r, w, k, v, a, b)]

    def work(r_ref, w_ref, k_ref, v_ref, a_ref, b_ref, y_ref, st_ref, s_ref, ve_ref, sa_ref, p_ref, p1_ref, ys_ref):
        @pl.when(pl.program_id(0) == 0)
        def _():
            s_ref[...] = jnp.zeros_like(s_ref)

        eye, ones2, ones1 = _scan_consts()

        def row_of(q):
            return tc - 1 - q if rev else q

        def expand(q):
            _expand_into(ve_ref.at[q], p1_ref.at[q % 2], v_ref, row_of(q), hp, eye, ones1)

        def advance(q):
            t, prev_ref = row_of(q), (s_ref if q == 0 else st_ref.at[q - 1])
            for heads in _chunks_of_heads(hp):
                rows = _rows_of(heads)
                for h in heads:
                    p_ref[q % 2, _head_rows(h), :] = _split2(prev_ref[_head_rows(h), :] * a_ref[t, h:h + 1, :])
                sa_ref[q % 2, rows, :] = jnp.dot(p_ref[q % 2, rows, :], ones2, preferred_element_type=F32)
                for h in heads:
                    hr_ = _head_rows(h)
                    st_ref[q, hr_, :] = (prev_ref[hr_, :] * w_ref[t, h:h + 1, :]
                                         + sa_ref[q % 2, hr_, :] * b_ref[t, h:h + 1, :]
                                         + ve_ref[q, hr_, :] * k_ref[t, h:h + 1, :])

        def readout(q):
            t = row_of(q)
            for h in range(hp):
                p1_ref[2 + q % 2, _head_rows(h), :] = (st_ref[q, _head_rows(h), :]
                                                       * r_ref[t, h:h + 1, :]).astype(BF16)
            ys_ref[q % 2] = jnp.dot(p1_ref[2 + q % 2], ones1, preferred_element_type=F32)
            for h in range(hp):
                _colsum_store(y_ref, t, h, jnp.where(eye, ys_ref[q % 2, _head_rows(h), :], 0.0))

        expand(0)
        for q in range(tc):
            if q + 1 < tc:
                expand(q + 1)
            advance(q)
            if q > 0:
                readout(q - 1)
        readout(tc - 1)
        s_ref[...] = st_ref[tc - 1]

    row_spec = pl.BlockSpec((tc, hp, LANES), lambda i: (_order(i, n_ctx, n_all, rev), 0, 0))
    n = hp * HEAD
    s_in, s_out, s_shape, s_scratch, s_alias, s_ops = _side_call_args(side, 6, 2)
    y, st, *side_res = pl.pallas_call(
        _with_side(work, side, 6, 2, 6, n_all), name=name, grid=(n_all,), in_specs=[row_spec] * 6 + s_in,
        out_specs=[row_spec, pl.BlockSpec((tc, n, LANES), lambda i: (i, 0, 0))] + s_out,
        out_shape=[jax.ShapeDtypeStruct((R, hp, LANES), F32), jax.ShapeDtypeStruct((R, n, LANES), F32)] + s_shape,
        scratch_shapes=[pltpu.VMEM((n, LANES), F32), pltpu.VMEM((tc, n, LANES), F32), pltpu.VMEM((2, n, LANES), F32),
                        pltpu.VMEM((2, n, 2 * LANES), BF16), pltpu.VMEM((4, n, LANES), BF16),
                        pltpu.VMEM((2, n, LANES), F32)] + s_scratch,
        input_output_aliases=s_alias,
        compiler_params=_cparams(dimension_semantics=("arbitrary",)),
    )(*ins, *s_ops)
    return y.reshape(R, D), st, side_res


def scan_bwd(name, r, w, k, v, a, b, dy, st, n_ctx_rows, rev, side=None):
    R, D = r.shape
    hp, tc = D // LANES, SCAN_TC
    n_all, n_ctx = R // tc, n_ctx_rows // tc
    ins = [t.reshape(R, hp, LANES) for t in (r, w, k, v, a, b, dy)]

    def work(r_ref, w_ref, k_ref, v_ref, a_ref, b_ref, dy_ref, st_ref, prev_ref,
             dr_ref, dw_ref, dk_ref, dv_ref, da_ref, db_ref,
             g_ref, s0_ref, ve_ref, dye_ref, sa_ref, gs_ref, tmp_ref, p_ref, p1_ref, tmp2_ref):
        i = pl.program_id(0)

        @pl.when(i == 0)
        def _():
            g_ref[...] = jnp.zeros_like(g_ref)

        eye, ones2, ones1 = _scan_consts()

        @pl.when(i == n_all - 1)
        def _():
            s0_ref[...] = jnp.zeros_like(s0_ref)

        @pl.when(i != n_all - 1)
        def _():
            s0_ref[...] = prev_ref[0]

        def row_of(q):
            return tc - 1 - q if rev else q

        def prev_of(q):
            return s0_ref if q == 0 else st_ref.at[q - 1]

        def before(q):
            t, prev = row_of(q), prev_of(q)
            _expand_into(ve_ref.at[q], p1_ref.at[2 + q % 2], v_ref, t, hp, eye, ones1)
            _expand_into(dye_ref.at[q], p1_ref.at[4 + q % 2], dy_ref, t, hp, eye, ones1)
            for h in range(hp):
                p1_ref[q % 2, _head_rows(h), :] = (prev[_head_rows(h), :] * a_ref[t, h:h + 1, :]).astype(BF16)
            sa_ref[q] = jnp.dot(p1_ref[q % 2], ones1, preferred_element_type=F32)

        def back(q):
            t, prev = row_of(q), prev_of(q)
            for heads in _chunks_of_heads(hp):
                rows = _rows_of(heads)
                for h in heads:
                    hr_ = _head_rows(h)
                    g = g_ref[hr_, :] + dye_ref[q, hr_, :] * r_ref[t, h:h + 1, :]
                    gs_ref[q, hr_, :] = g
                    p_ref[q % 2, hr_, :] = _split2(g * b_ref[t, h:h + 1, :])
                tmp_ref[q % 2, rows, :] = jnp.dot(p_ref[q % 2, rows, :], ones2, preferred_element_type=F32)
                for h in heads:
                    hr_ = _head_rows(h)
                    dsa = tmp_ref[q % 2, hr_, :]
                    _colsum_store(da_ref, t, h, prev[hr_, :] * dsa)
                    g_ref[hr_, :] = gs_ref[q, hr_, :] * w_ref[t, h:h + 1, :] + dsa * a_ref[t, h:h + 1, :]

        def after(q):
            t, prev = row_of(q), prev_of(q)
            for h in range(hp):
                hr_ = _head_rows(h)
                g = gs_ref[q, hr_, :]
                p1_ref[6 + q % 2, hr_, :] = (g * k_ref[t, h:h + 1, :]).astype(BF16)
                _colsum_store(dr_ref, t, h, st_ref[q, hr_, :] * dye_ref[q, hr_, :])
                _colsum_store(dk_ref, t, h, g * ve_ref[q, hr_, :])
                _colsum_store(dw_ref, t, h, g * prev[hr_, :])
                _colsum_store(db_ref, t, h, g * sa_ref[q, hr_, :])
            tmp2_ref[q % 2] = jnp.dot(p1_ref[6 + q % 2], ones1, preferred_element_type=F32)
            for h in range(hp):
                _colsum_store(dv_ref, t, h, jnp.where(eye, tmp2_ref[q % 2, _head_rows(h), :], 0.0))

        before(tc - 1)
        for q in reversed(range(tc)):
            if q > 0:
                before(q - 1)
            back(q)
            if q < tc - 1:
                after(q + 1)
        after(0)

    def pos(i):
        return n_all - 1 - i

    n = hp * HEAD
    row_spec = pl.BlockSpec((tc, hp, LANES), lambda i: (_order(pos(i), n_ctx, n_all, rev), 0, 0))
    big = pltpu.VMEM((tc, n, LANES), F32)
    one = pltpu.VMEM((n, LANES), F32)
    s_in, s_out, s_shape, s_scratch, s_alias, s_ops = _side_call_args(side, 9, 6)
    outs = pl.pallas_call(
        _with_side(work, side, 9, 6, 10, n_all), name=name, grid=(n_all,),
        in_specs=[row_spec] * 7 + [
            pl.BlockSpec((tc, n, LANES), lambda i: (pos(i), 0, 0)),
            pl.BlockSpec((1, n, LANES), lambda i: (jnp.maximum(pos(i) * tc - 1, 0), 0, 0))] + s_in,
        out_specs=[row_spec] * 6 + s_out,
        out_shape=[jax.ShapeDtypeStruct((R, hp, LANES), F32)] * 6 + s_shape,
        scratch_shapes=[one, one, big, big, big, big, pltpu.VMEM((2, n, LANES), F32),
                        pltpu.VMEM((2, n, 2 * LANES), BF16), pltpu.VMEM((8, n, LANES), BF16),
                        pltpu.VMEM((2, n, LANES), F32)] + s_scratch,
        input_output_aliases=s_alias,
        compiler_params=_cparams(dimension_semantics=("arbitrary",)),
    )(*ins, st, st, *s_ops)
    return [o.reshape(R, D) for o in outs[:6]], list(outs[6:])


ANY = pl.BlockSpec(memory_space=pl.ANY)


def _peer(xi, yi, ci, k):
    return (1 - xi if k & 4 else xi, 1 - yi if k & 2 else yi, 1 - ci if k & 1 else ci)


def _rcopy(src, dst, send_sem, recv_sem, dev):
    return pltpu.make_async_remote_copy(src_ref=src, dst_ref=dst, send_sem=send_sem, recv_sem=recv_sem,
                                        device_id=dev, device_id_type=MESH)


def _drain(copies):
    for cp in copies:
        if cp.is_remote:
            cp.wait_send()
        else:
            cp.wait()


def all_gather8(name, x):
    r, c = x.shape

    def body(x_ref, out_ref, send_sems, recv_sems, local_sem):
        xi, yi, ci = _place()

        def blk(p):
            return out_ref.at[4 * p[0] + 2 * p[1] + p[2]]

        me = (xi, yi, ci)
        mine = pltpu.make_async_copy(x_ref, blk(me), local_sem.at[0])
        mine.start()
        sends = [_rcopy(x_ref, blk(me), send_sems.at[k - 1], recv_sems.at[k - 1], _peer(xi, yi, ci, k))
                 for k in range(1, N_DEV)]
        for cp in sends:
            cp.start()
        for k in range(1, N_DEV):
            p = _peer(xi, yi, ci, k)
            _rcopy(x_ref, blk(p), send_sems.at[k - 1], recv_sems.at[k - 1], p).wait_recv()
        for cp in sends:
            cp.wait_send()
        mine.wait()

    vm = pl.BlockSpec(memory_space=pltpu.VMEM)
    return pl.pallas_call(
        body, name=name, in_specs=[vm], out_specs=vm, out_shape=jax.ShapeDtypeStruct((N_DEV, r, c), x.dtype),
        scratch_shapes=[pltpu.SemaphoreType.DMA((N_DEV - 1,)), pltpu.SemaphoreType.DMA((N_DEV - 1,)),
                        pltpu.SemaphoreType.DMA((1,))],
        compiler_params=_cparams(),
    )(x)


def _chips(xi, yi):
    chips = [(1 - xi, yi), (xi, 1 - yi), (1 - xi, 1 - yi)]
    return chips, [2 * cx + cy for cx, cy in chips]


def gather_weights(name, stacked):
    n = len(stacked)

    def body(*refs):
        _gather_start(refs[n:2 * n], refs[2 * n:])
        _gather_finish(refs[n:2 * n], refs[2 * n:])

    return pl.pallas_call(
        body, name=name, in_specs=[ANY] * n, out_specs=[ANY] * n,
        out_shape=[jax.ShapeDtypeStruct(a.shape, a.dtype) for a in stacked],
        input_output_aliases={w: w for w in range(n)},
        scratch_shapes=_gather_sems(n),
        compiler_params=_cparams(),
    )(*stacked)


def _gather_sems(n):
    return [pltpu.SemaphoreType.DMA((n, 6)), pltpu.SemaphoreType.DMA((n, 6))]


def _gather_sends(out, sems):
    send_sems, recv_sems = sems
    xi, yi, ci = _place()
    s = 2 * xi + yi
    chips, _ = _chips(xi, yi)
    sends = []
    for w in range(len(out)):
        hr = out[w].shape[1] // 2
        mine = out[w].at[s, pl.ds(ci * hr, hr)]
        sends += [_rcopy(mine, mine, send_sems.at[w, j], recv_sems.at[w, j], (cx, cy, ci))
                  for j, (cx, cy) in enumerate(chips)]
    return sends


def _gather_start(out, sems):
    for cp in _gather_sends(out, sems):
        cp.start()


def _gather_finish(out, sems):
    send_sems, recv_sems = sems
    xi, yi, ci = _place()
    chips, sidx = _chips(xi, yi)
    sib = (xi, yi, 1 - ci)
    passed = []
    for w in range(len(out)):
        hr = out[w].shape[1] // 2
        for j, (cx, cy) in enumerate(chips):
            blk = out[w].at[sidx[j], pl.ds(ci * hr, hr)]
            _rcopy(blk, blk, send_sems.at[w, j], recv_sems.at[w, j], (cx, cy, ci)).wait_recv()
            fw = _rcopy(blk, blk, send_sems.at[w, 3 + j], recv_sems.at[w, 3 + j], sib)
            fw.start()
            passed.append(fw)
    for w in range(len(out)):
        hr = out[w].shape[1] // 2
        for j in range(3):
            blk = out[w].at[sidx[j], pl.ds((1 - ci) * hr, hr)]
            _rcopy(blk, blk, send_sems.at[w, 3 + j], recv_sems.at[w, 3 + j], sib).wait_recv()
    _drain(_gather_sends(out, sems) + passed)


class Side:
    def __init__(self, ins, out_shapes, aliases, sems, start, finish):
        self.ins, self.out_shapes, self.aliases, self.sems = ins, out_shapes, aliases, sems
        self.start, self.finish = start, finish


def _gather_wait_ici(out, sems):
    send_sems, recv_sems = sems
    xi, yi, ci = _place()
    chips, sidx = _chips(xi, yi)
    for w in range(len(out)):
        hr = out[w].shape[1] // 2
        for j, (cx, cy) in enumerate(chips):
            blk = out[w].at[sidx[j], pl.ds(ci * hr, hr)]
            _rcopy(blk, blk, send_sems.at[w, j], recv_sems.at[w, j], (cx, cy, ci)).wait_recv()
    _drain(_gather_sends(out, sems))


def _pass_copies(out, sems, half_of):
    send_sems, recv_sems = sems
    xi, yi, ci = _place()
    _, sidx = _chips(xi, yi)
    sib = (xi, yi, 1 - ci)
    cps = []
    for w in range(len(out)):
        hr = out[w].shape[1] // 2
        for j in range(3):
            blk = out[w].at[sidx[j], pl.ds(half_of(ci) * hr, hr)]
            cps.append(_rcopy(blk, blk, send_sems.at[w, j], recv_sems.at[w, j], sib))
    return cps


def _pass_start(out, sems):
    for cp in _pass_copies(out, sems, lambda ci: ci):
        cp.start()


def _pass_finish(out, sems):
    for cp in _pass_copies(out, sems, lambda ci: 1 - ci):
        cp.wait_recv()
    _drain(_pass_copies(out, sems, lambda ci: ci))


def gather_side(stacked):
    n = len(stacked)
    return Side(stacked, [jax.ShapeDtypeStruct(a.shape, a.dtype) for a in stacked], {w: w for w in range(n)},
                _gather_sems(n), lambda ins, outs, sems: _gather_start(outs, sems),
                lambda ins, outs, sems: _gather_wait_ici(outs, sems))


def pass_side(stacked):
    n = len(stacked)
    return Side(stacked, [jax.ShapeDtypeStruct(a.shape, a.dtype) for a in stacked], {w: w for w in range(n)},
                [pltpu.SemaphoreType.DMA((n, 3)), pltpu.SemaphoreType.DMA((n, 3))],
                lambda ins, outs, sems: _pass_start(outs, sems), lambda ins, outs, sems: _pass_finish(outs, sems))


def rs_pair(name, grads):
    n = len(grads)

    def body(*refs):
        g, out = refs[:n], refs[n:2 * n]
        send_sems, recv_sems = refs[2 * n:]
        xi, yi, ci = _place()
        sib = (xi, yi, 1 - ci)
        cps = [_rcopy(g[w].at[:, 1 - ci], out[w], send_sems.at[w], recv_sems.at[w], sib) for w in range(n)]
        for cp in cps:
            cp.start()
        for cp in cps:
            cp.wait_recv()
        for cp in cps:
            cp.wait_send()

    return pl.pallas_call(
        body, name=name, in_specs=[ANY] * n, out_specs=[ANY] * n,
        out_shape=[jax.ShapeDtypeStruct((N_CHIPS,) + a.shape[2:], a.dtype) for a in grads],
        scratch_shapes=[pltpu.SemaphoreType.DMA((n,)), pltpu.SemaphoreType.DMA((n,))],
        compiler_params=_cparams(),
    )(*grads)


def _rows_tile(rows, cols, unit=16, limit=1 << 20):
    best = None
    for t in range(unit, rows + 1, unit):
        if rows % t == 0 and t * cols * 4 <= limit:
            best = t
    return best or rows


def rs_add_pair(name, g, got, ci):
    _, _, hr, c = g.shape
    th = _rows_tile(hr, c)

    def body(ci_ref, g_ref, r_ref, o32_ref, ob_ref):
        tot = g_ref[...] + r_ref[...]
        o32_ref[...] = tot
        ob_ref[...] = tot.astype(BF16)

    blk = pl.BlockSpec((None, th, c), lambda s, i, ci_ref: (s, i, 0))
    return pl.pallas_call(
        body, name=name,
        grid_spec=pltpu.PrefetchScalarGridSpec(
            num_scalar_prefetch=1, grid=(N_CHIPS, hr // th),
            in_specs=[pl.BlockSpec((None, None, th, c), lambda s, i, ci_ref: (s, ci_ref[0], i, 0)), blk],
            out_specs=[blk, blk]),
        out_shape=[jax.ShapeDtypeStruct((N_CHIPS, hr, c), F32), jax.ShapeDtypeStruct((N_CHIPS, hr, c), BF16)],
        compiler_params=_cparams(dimension_semantics=("parallel", "parallel")),
    )(ci, g, got)


def rs_chips(name, sums_bf16):
    n = len(sums_bf16)
    side = rs2_side(sums_bf16)

    def body(*refs):
        side.start(refs[:n], refs[n:4 * n], refs[4 * n:])
        side.finish(refs[:n], refs[n:4 * n], refs[4 * n:])

    res = pl.pallas_call(
        body, name=name, in_specs=[ANY] * n, out_specs=[ANY] * (3 * n), out_shape=side.out_shapes,
        scratch_shapes=side.sems, compiler_params=_cparams(),
    )(*sums_bf16)
    return [res[3 * w:3 * w + 3] for w in range(n)]


def _rs2_copies(pb, outs, sems):
    send_sems, recv_sems = sems
    xi, yi, ci = _place()
    chips, sidx = _chips(xi, yi)
    return [_rcopy(pb[w].at[sidx[j]], outs[3 * w + j], send_sems.at[w, j], recv_sems.at[w, j], (cx, cy, ci))
            for w in range(len(pb)) for j, (cx, cy) in enumerate(chips)]


def _rs2_start(pb, outs, sems):
    for cp in _rs2_copies(pb, outs, sems):
        cp.start()


def _rs2_finish(pb, outs, sems):
    for cp in _rs2_copies(pb, outs, sems):
        cp.wait_recv()
    _drain(_rs2_copies(pb, outs, sems))


def rs2_side(sums_bf16):
    n = len(sums_bf16)
    out_shapes = [jax.ShapeDtypeStruct(a.shape[1:], BF16) for a in sums_bf16 for _ in range(3)]
    return Side(sums_bf16, out_shapes, {}, [pltpu.SemaphoreType.DMA((n, 3)), pltpu.SemaphoreType.DMA((n, 3))],
                _rs2_start, _rs2_finish)


def rs_swap(name, halves):
    n = len(halves)

    def body(*refs):
        hv, out = refs[:n], refs[n:2 * n]
        send_sems, recv_sems = refs[2 * n:]
        xi, yi, ci = _place()
        sib = (xi, yi, 1 - ci)
        cps = [_rcopy(hv[w], out[w], send_sems.at[w], recv_sems.at[w], sib) for w in range(n)]
        for cp in cps:
            cp.start()
        for cp in cps:
            cp.wait_recv()
        _drain(cps)

    return pl.pallas_call(
        body, name=name, in_specs=[ANY] * n, out_specs=[ANY] * n,
        out_shape=[jax.ShapeDtypeStruct(a.shape, a.dtype) for a in halves],
        scratch_shapes=[pltpu.SemaphoreType.DMA((n,)), pltpu.SemaphoreType.DMA((n,))],
        compiler_params=_cparams(),
    )(*halves)


def cast_to_slot(name, x, slot):
    r, c = x.shape
    tr = _rows_tile(r, c)

    def body(slot_ref, x_ref, o_ref):
        o_ref[...] = x_ref[...].astype(BF16)

    return pl.pallas_call(
        body, name=name,
        grid_spec=pltpu.PrefetchScalarGridSpec(
            num_scalar_prefetch=1, grid=(r // tr,),
            in_specs=[pl.BlockSpec((tr, c), lambda i, slot_ref: (i, 0))],
            out_specs=pl.BlockSpec((None, tr, c), lambda i, slot_ref: (slot_ref[0], i, 0))),
        out_shape=jax.ShapeDtypeStruct((N_CHIPS, r, c), BF16),
        compiler_params=_cparams(dimension_semantics=("parallel",)),
    )(slot, x)


def sum_blocks(name, x, picks):
    _, r, c = x.shape

    def body(x_ref, o_ref):
        acc = x_ref[picks[0]]
        for b in picks[1:]:
            acc = acc + x_ref[b]
        o_ref[...] = acc

    vm = pl.BlockSpec(memory_space=pltpu.VMEM)
    return pl.pallas_call(body, name=name, in_specs=[vm], out_specs=vm, out_shape=jax.ShapeDtypeStruct((r, c), F32),
                          compiler_params=_cparams())(x)


def adamw(name, w, g, m, v):
    r, c = w.shape
    tr = _rows_tile(r, c, unit=8, limit=1 << 19)

    def body(w_ref, g_ref, m_ref, v_ref, d_ref, mo_ref, vo_ref):
        d_ref[...], mo_ref[...], vo_ref[...] = _adam_update(w_ref[...], g_ref[...], m_ref[...], v_ref[...])

    blk = pl.BlockSpec((tr, c), lambda i: (i, 0))
    return pl.pallas_call(
        body, name=name, grid=(r // tr,), in_specs=[blk] * 4, out_specs=[blk] * 3,
        out_shape=[jax.ShapeDtypeStruct((r, c), F32)] * 3,
        compiler_params=_cparams(dimension_semantics=("parallel",)),
    )(w, g, m, v)


def _adam_update(w, gv, m, v):
    c1 = 1.0 / (1.0 - ADAM_B1 ** ADAM_STEP)
    c2 = 1.0 / (1.0 - ADAM_B2 ** ADAM_STEP)
    mn = ADAM_B1 * m + (1.0 - ADAM_B1) * gv
    vn = ADAM_B2 * v + (1.0 - ADAM_B2) * (gv * gv)
    return -ADAM_LR * ((mn * c1) / (jnp.sqrt(vn * c2) + ADAM_EPS) + ADAM_WD * w), mn, vn


def adamw_halves(name, w, m, v, mine, theirs, ci, side=None):
    hr, c = mine[0].shape
    npos = len(mine)
    th = _rows_tile(hr, c, unit=8, limit=1 << 19)
    per = hr // th

    def work(ci_ref, w_ref, m_ref, v_ref, *rest):
        g_refs, (go_ref, d_ref, mo_ref, vo_ref) = rest[:2 * npos], rest[2 * npos:]
        pos, half = pl.program_id(0), pl.program_id(1)
        from_me = half == ci_ref[0]
        gv = jnp.where(from_me, g_refs[0][...], g_refs[npos][...])
        for p in range(1, npos):
            gv = jnp.where(pos == p, jnp.where(from_me, g_refs[p][...], g_refs[npos + p][...]), gv)
        d_ref[...], mo_ref[...], vo_ref[...] = _adam_update(w_ref[...], gv, m_ref[...], v_ref[...])
        go_ref[...] = gv

    full = pl.BlockSpec((th, c), lambda p, h, i, ci_ref: ((p * 2 + h) * per + i, 0))
    part = pl.BlockSpec((th, c), lambda p, h, i, ci_ref: (i, 0))
    rows = 2 * hr * npos
    n_in = 4 + 2 * npos
    s_in, s_out, s_shape, s_scratch, _, s_ops = _side_call_args(side, n_in, 4)
    res = pl.pallas_call(
        _with_side(work, side, n_in, 4, 0, (npos, 2, per)), name=name,
        grid_spec=pltpu.PrefetchScalarGridSpec(
            num_scalar_prefetch=1, grid=(npos, 2, per), in_specs=[full] * 3 + [part] * (2 * npos) + s_in,
            out_specs=[full] * 4 + s_out, scratch_shapes=s_scratch),
        out_shape=[jax.ShapeDtypeStruct((rows, c), F32)] * 4 + s_shape,
        compiler_params=_cparams(dimension_semantics=("arbitrary", "arbitrary", "arbitrary")),
    )(ci, w, m, v, *mine, *theirs, *s_ops)
    return res[0], res[1], res[2], res[3], list(res[4:])


def pack_rows(name, parts, rows):
    width = parts[0].shape[1]
    n = len(parts)

    def body(*refs):
        o_ref = refs[n]
        o_ref[...] = jnp.zeros_like(o_ref)
        off = 0
        for r in refs[:n]:
            o_ref[off:off + r.shape[0], :] = r[...]
            off += r.shape[0]

    vm = pl.BlockSpec(memory_space=pltpu.VMEM)
    return pl.pallas_call(body, name=name, in_specs=[vm] * n, out_specs=vm,
                          out_shape=jax.ShapeDtypeStruct((rows, width), F32), compiler_params=_cparams())(*parts)


def _pad_rows(a, rows):
    return jnp.pad(a, ((0, rows - a.shape[0]), (0, 0)))


def _view2d(name, a):
    if name == 'rw_rk' or a.ndim == 1:
        return a.reshape(1, -1)
    return a.reshape(-1, a.shape[-1])


def _reduce_scatter(tag, items, ci_arr, s):
    sums = _rs_pair_sums(tag, items, ci_arr)
    landed = rs_chips("rs2_" + tag, [s_[1] for s_ in sums])
    return _rs_finish(tag, items, sums, landed, s)


def _rs_pair_sums(tag, items, ci_arr):
    g4 = [g.reshape(N_CHIPS, 2, g.shape[1] // 2, g.shape[2]) for _, g in items]
    got = rs_pair("rs1_" + tag, g4)
    return [rs_add_pair(f"rs1add_{tag}{w}", g4[w], got[w], ci_arr) for w in range(len(items))]


def _rs_finish(tag, items, sums, landed, s):
    names = []
    for nm, _ in items:
        if nm not in names:
            names.append(nm)
    halves = []
    for w, (r0, r1, r2) in enumerate(landed):
        own = lax.dynamic_index_in_dim(sums[w][0], s, axis=0, keepdims=False)
        hr, c = own.shape
        halves.append(row_call(f"rs2add_{tag}{w}", f_sum, [own, r0, r1, r2], [], [(c, F32)],
                               tr=_rows_tile(hr, c), sr=16)[0])
    theirs = rs_swap("rs3_" + tag, halves)
    return {name: ([halves[w] for w, (nm, _) in enumerate(items) if nm == name],
                   [theirs[w] for w, (nm, _) in enumerate(items) if nm == name]) for name in names}


def _step(p):
    xi, yi, ci = _place()
    me = 4 * xi + 2 * yi + ci
    s = 2 * xi + yi
    ci_arr = jnp.reshape(ci, (1,)).astype(jnp.int32)
    s_arr = jnp.reshape(s, (1,)).astype(jnp.int32)
    x, ctx, tgt = p['x'][0], p['ctx'][0], p['loss_target'][0]
    T, D = x.shape
    L = ctx.shape[0]
    H, Dq = D // HEAD, D // N_CHIPS
    TR = math.gcd(math.gcd(L, T), 256)
    TS = min(TR, 64)
    nct = L // TR
    LG = p['rw_g1'].shape[-1]
    LW, LA = p['rw_w1'].shape[-1], p['rw_a1'].shape[-1]
    F4 = p['ffn_w2'].shape[1]

    pack = jnp.concatenate([
        _pad_rows(p['c'].reshape(N_CHIPS, Dq), 8), _pad_rows(p['rw_mix'][0], 8), _pad_rows(p['rw_w0'][0], 8),
        _pad_rows(p['rw_a0'][0], 8), _pad_rows(p['sc_conv'][0], 8)], axis=0)
    got = all_gather8("ag_small", pack)
    c_all = got[:, 0:N_CHIPS, :].reshape(N_DEV, D)
    full = jnp.transpose(got[::2], (1, 0, 2)).reshape(40, D)
    mix_f, w0_f, a0_f, conv_f = full[8:16], full[16:24], full[24:32], full[32:40]

    cond_in = jnp.concatenate([c_all, _pad_rows(p['c_ctx'].reshape(1, D), 8)], axis=0)
    cond = row_call("cond", f_silu, [cond_in], [], [(D, F32)], tr=16, sr=16)[0]
    ada = Stacked(p['ada_w'], "layer", D)
    modp = [mm_nn(f"modp{i}", cond, ada.at(i)) for i in range(2)]
    mg = all_gather8("ag_mod", jnp.concatenate(modp, axis=0))
    mod = jnp.transpose(mg[::2].reshape(N_CHIPS, 2, 16, 6 * Dq), (1, 2, 0, 3)).reshape(2, 16, 6 * D)
    mod = mod + p['ada_b'][:, None, :]
    mod_x = lax.dynamic_index_in_dim(mod, me, axis=1, keepdims=False)
    mod_c = mod[:, 8]

    def chunk(vec, j):
        return vec[j * D:(j + 1) * D].reshape(1, D)

    sh1x, sc1x, gt1x, sh2x, sc2x, gt2x = ([chunk(mod_x[i], j) for i in range(2)] for j in range(6))
    sh1c, sc1c = chunk(mod_c[0], 0), chunk(mod_c[0], 1)

    def slots(names):
        return [cast_to_slot("cast_" + n, _view2d(n, p[n]), s_arr) for n in names]

    rw_names = ('rw_wr', 'rw_wk', 'rw_wv', 'rw_wo', 'rw_w1', 'rw_w2', 'rw_a1', 'rw_a2', 'rw_g1', 'rw_g2')
    late_names = ('sc_win', 'sc_wout', 'ffn_w13', 'ffn_w2')
    gw = dict(zip(rw_names, gather_weights("ag_rw", slots(rw_names))))
    late_slots = slots(late_names)
    Wr, Wk, Wv, Wo = (Stacked(gw[n], "row", Dq) for n in ('rw_wr', 'rw_wk', 'rw_wv', 'rw_wo'))
    W1, A1, G1 = (Stacked(gw[n], "row", Dq) for n in ('rw_w1', 'rw_a1', 'rw_g1'))
    W2, A2, G2 = Stacked(gw['rw_w2'], "col", LW), Stacked(gw['rw_a2'], "col", LA), Stacked(gw['rw_g2'], "col", LG)

    ones2 = (((lax.broadcasted_iota(jnp.int32, (2 * LANES, LANES), 0) & (LANES - 1)) >= HEAD)
             == (lax.broadcasted_iota(jnp.int32, (2 * LANES, LANES), 1) >= HEAD)).astype(BF16)
    n1g, n2g = p['norm1_g'], p['norm2_g']
    kkp, ka, lnw, lnb = p['rw_kk'], p['rw_ka'], p['rw_lnw'], p['rw_lnb']
    rk = p['rw_rk'].reshape(1, D)
    fg = p['final_g'].reshape(1, D)

    xin = jnp.concatenate([ctx, x], axis=0)
    nm = functools.partial(f_norm_mod, nct)
    nm_consts = [n1g[0:1], sh1c, sc1c, sh1x[0], sc1x[0]]
    h = row_call("l0_norm", nm, [xin], nm_consts, [(D, F32)], tr=TR, sr=16)[0]
    xr, xw, xk, xv, xa, xg = mix_fwd("l0_mix", h, mix_f, L)
    r = mm_nn("l0_r", xr, Wr)
    k = mm_nn("l0_k", xk, Wk)
    v = mm_nn("l0_v", xv, Wv)
    gl = mm_nn("l0_gl", xg, G1)
    sg = row_call("l0_sg", f_sigmoid, [gl], [], [(LG, BF16)], tr=TR, sr=16)[0]
    g = mm_nn("l0_g", sg, G2)
    wl, tw, zw, al, za = [], [], [], [], []
    for d in range(2):
        wl.append(mm_nn(f"l0_wl{d}", xw, W1.at(d)))
        tw.append(row_call(f"l0_tw{d}", f_tanh, [wl[d]], [], [(LW, BF16)], tr=TR, sr=16)[0])
        zw.append(mm_nn(f"l0_zw{d}", tw[d], W2.at(d)))
        al.append(mm_nn(f"l0_al{d}", xa, A1.at(d), BF16))
        za.append(mm_nn(f"l0_za{d}", al[d], A2.at(d)))
    post_rows = [k, zw[0], zw[1], za[0], za[1]]
    post_consts = [kkp, ka, w0_f[0:1], w0_f[1:2], a0_f[0:1], a0_f[1:2], ones2]
    aa, dec0, dec1, kd0, kd1, bb0, bb1, ksum = row_call(
        "l0_post", f_post_fwd, post_rows, post_consts, [(D, F32)] * 8, tr=TS, sr=16)
    dec, kd, bb = (dec0, dec1), (kd0, kd1), (bb0, bb1)
    ys, sts = [], []
    filled = late_slots
    for d in range(2):
        y_d, st_d, filled = scan_fwd(f"l0_scan{d}", r, dec[d], kd[d], v, aa, bb[d], L, bool(d),
                                     side=gather_side(filled) if d == 0 else pass_side(filled))
        ys.append(y_d)
        sts.append(st_d)
    gw.update(zip(late_names, filled))
    Win, Wout = Stacked(gw['sc_win'], "col", D), Stacked(gw['sc_wout'], "row", Dq)
    W13, W2f = Stacked(gw['ffn_w13'], "col", D), Stacked(gw['ffn_w2'], "row", F4)
    ro_rows = [ys[0], ys[1], r, ksum, v, g]
    ro_consts = [rk, lnw, lnb, ones2]
    og = row_call("l0_readout", f_readout, ro_rows, ro_consts, [(D, BF16)], tr=TS, sr=16)[0]
    yx = mm_nn("l0_o", og, Wo)
    res0_consts = [gt1x[0], n2g[0:1], sh2x[0], sc2x[0]]
    x1, h2 = row_call("l0_res", f_res_norm_mod, [x, yx], res0_consts, [(D, F32), (D, BF16)], tr=TR, sr=16,
                      offs=[0, nct])
    ab0 = mm_nn("l0_ffn13", h2, W13.at(0))
    sw0 = swiglu_fwd("l0_swiglu", ab0, TS)
    f0 = mm_nn("l0_ffn2", sw0, W2f.at(0))

    res1_consts = [gt2x[0], n1g[1:2], sh1x[1], sc1x[1]]
    x2, hb = row_call("l1_norm", f_res_norm_mod, [x1, f0], res1_consts, [(D, F32), (D, BF16)], tr=TR, sr=16)
    gcu = mm_nn("l1_win", hb, Win)
    pc = conv_fwd("l1_conv", gcu, conv_f)
    yx1 = mm_nn("l1_wout", pc, Wout)
    res2_consts = [gt1x[1], n2g[1:2], sh2x[1], sc2x[1]]
    x3, h2b = row_call("l1_res", f_res_norm_mod, [x2, yx1], res2_consts, [(D, F32), (D, BF16)], tr=TR, sr=16)
    ab1 = mm_nn("l1_ffn13", h2b, W13.at(1))
    sw1 = swiglu_fwd("l1_swiglu", ab1, TS)
    f1 = mm_nn("l1_ffn2", sw1, W2f.at(1))
    dx3, df1, dgt2_1, dfg, loss_blk = final_call("final", x3, f1, gt2x[1], fg, tgt, TR)
    loss = lax.psum(loss_blk[0, 0], ("x", "y", "c"))

    big = []
    dsw1 = mm_nt("b1_dsw", df1, W2f.at(1))
    gW2f1 = mm_tn("b1_gw2", sw1, df1, "row")
    dab1 = swiglu_bwd("b1_swiglu", ab1, dsw1, TS)
    dh2b = mm_nt("b1_dh2", dab1, W13.at(1))
    gW13_1 = mm_tn("b1_gw13", h2b, dab1, "col")
    rm = [True, True]
    cm = [True] * 4
    (dx2, dyx1), (dgt1_1, dn2g1, dsh2_1, dsc2_1) = row_vjp(
        "b1_res", f_res_norm_mod, [x2, yx1], res2_consts, [dx3, dh2b], row_mask=rm, const_mask=cm, tr=TR, sr=16,
        bf16_rows=(1,))
    dpc = mm_nt("b1_dpc", dyx1, Wout)
    big.append(('sc_wout', mm_tn("b1_gwout", pc, dyx1, "row")))
    dgcu, dconv = conv_bwd("b1_conv", gcu, conv_f, dpc)
    dhb = mm_nt("b1_dhb", dgcu, Win)
    big.append(('sc_win', mm_tn("b1_gwin", hb, dgcu, "col")))
    (dx1, df0), (dgt2_0, dn1g1, dsh1_1, dsc1_1) = row_vjp(
        "b1_norm", f_res_norm_mod, [x1, f0], res1_consts, [dx2, dhb], row_mask=rm, const_mask=cm, tr=TR, sr=16,
        bf16_rows=(1,))

    dsw0 = mm_nt("b0_dsw", df0, W2f.at(0))
    gW2f0 = mm_tn("b0_gw2", sw0, df0, "row")
    dab0 = swiglu_bwd("b0_swiglu", ab0, dsw0, TS)
    dh2 = mm_nt("b0_dh2", dab0, W13.at(0))
    gW13_0 = mm_tn("b0_gw13", h2, dab0, "col")
    (dx_a, dyx), (dgt1_0, dn2g0, dsh2_0, dsc2_0) = row_vjp(
        "b0_res", f_res_norm_mod, [x, yx], res0_consts, [dx1, dh2], row_mask=rm, const_mask=cm, tr=TR, sr=16,
        offs=[0, nct], bf16_rows=(1,))
    dyx_all = jnp.concatenate([jnp.zeros((L, D), BF16), dyx], axis=0)
    dog = mm_nt("b0_dog", dyx_all, Wo)
    gWo = mm_tn("b0_gwo", og, dyx_all, "row")
    (dy, dr_ro, dksum, dv_ro, dg), (drk, dlnw, dlnb) = row_vjp(
        "b0_readout", f_readout, ro_rows, ro_consts, [dog], row_mask=[True, False, True, True, True, True],
        const_mask=[True, True, True, False], tr=TS, sr=16, bf16_rows=(5,))
    late_items = [it for it in big if it[0] in ('sc_win', 'sc_wout')] + [
        ('ffn_w13', gW13_0), ('ffn_w13', gW13_1), ('ffn_w2', gW2f0), ('ffn_w2', gW2f1)]
    late_sums = _rs_pair_sums("late", late_items, ci_arr)
    (dr0, ddec0, dkd0, dv0, daa0, dbb0), landed_flat = scan_bwd(
        "b0_scan0", r, dec[0], kd[0], v, aa, bb[0], dy, sts[0], L, False, side=rs2_side([s_[1] for s_ in late_sums]))
    (dr1, ddec1, dkd1, dv1, daa1, dbb1), _ = scan_bwd("b0_scan1", r, dec[1], kd[1], v, aa, bb[1], dy, sts[1], L, True)
    late_landed = [landed_flat[3 * w:3 * w + 3] for w in range(len(late_items))]
    post_cots = [daa0, daa1, ddec0, ddec1, dkd0, dkd1, dbb0, dbb1, dksum]
    (dk, dzw0, dzw1, dza0, dza1), (dkkp, dka, dw00, dw01, da00, da01) = row_vjp(
        "b0_post", f_post, post_rows, post_consts, post_cots, row_mask=[True] * 5,
        const_mask=[True] * 6 + [False], tr=TS, sr=16, bf16_rows=(0, 1, 2, 3, 4))
    dzw, dza = (dzw0, dzw1), (dza0, dza1)
    dr_t = sum_cast("b0_drsum", [dr0, dr1, dr_ro], BF16, TR)
    dv_t = sum_cast("b0_dvsum", [dv0, dv1, dv_ro], BF16, TR)
    mix_cots, mix_slots = [], []

    def back(tag, cot, w, xin_m, kind, slot):
        mix_cots.append(mm_nt("b0_dx" + tag, cot, w))
        mix_slots.append(slot)
        return mm_tn("b0_gw" + tag, xin_m, cot, kind)

    big.append(('rw_wr', back("r", dr_t, Wr, xr, "row", 0)))
    big.append(('rw_wk', back("k", dk, Wk, xk, "row", 2)))
    big.append(('rw_wv', back("v", dv_t, Wv, xv, "row", 3)))
    big.append(('rw_wo', gWo))
    dsg = mm_nt("b0_dsg", dg, G2)
    gG2 = mm_tn("b0_gg2", sg, dg, "col")
    (dgl,), _ = row_vjp("b0_sg", f_sigmoid, [gl], [], [dsg], row_mask=[True], const_mask=[], tr=TR, sr=16,
                        bf16_rows=(0,))
    gG1 = back("g", dgl, G1, xg, "row", 5)
    gW1, gW2, gA1, gA2 = [], [], [], []
    for d in range(2):
        dtw = mm_nt(f"b0_dtw{d}", dzw[d], W2.at(d))
        gW2.append(mm_tn(f"b0_gw2{d}", tw[d], dzw[d], "col"))
        (dwl,), _ = row_vjp(f"b0_tw{d}", f_tanh, [wl[d]], [], [dtw], row_mask=[True], const_mask=[], tr=TR, sr=16,
                            bf16_rows=(0,))
        gW1.append(back(f"w{d}", dwl, W1.at(d), xw, "row", 1))
        dal = mm_nt(f"b0_dal{d}", dza[d], A2.at(d), BF16)
        gA2.append(mm_tn(f"b0_ga2{d}", al[d], dza[d], "col"))
        gA1.append(back(f"a{d}", dal, A1.at(d), xa, "row", 4))
    big += [('rw_w1', gW1[0]), ('rw_w1', gW1[1]), ('rw_w2', gW2[0]), ('rw_w2', gW2[1]),
            ('rw_a1', gA1[0]), ('rw_a1', gA1[1]), ('rw_a2', gA2[0]), ('rw_a2', gA2[1]),
            ('rw_g1', gG1), ('rw_g2', gG2)]
    dh, dmix = mix_bwd("b0_mix", h, mix_f, mix_cots, mix_slots, L)
    (dxin,), (dn1g0, dsh1c, dsc1c, dsh1x, dsc1x) = row_vjp(
        "b0_norm", nm, [xin], nm_consts, [dh], row_mask=[True], const_mask=[True] * 5, tr=TR, sr=16)
    grad_x = sum_cast("b0_dx", [dxin, dx_a], F32, TR, offs=[nct, 0])

    zero = jnp.zeros((1, D), F32)
    parts = [dsh1x, dsc1x, dgt1_0, dsh2_0, dsc2_0, dgt2_0, dsh1c, dsc1c, zero, zero, zero, zero,
             dsh1_1, dsc1_1, dgt1_1, dsh2_1, dsc2_1, dgt2_1, zero, zero, zero, zero, zero, zero,
             dn1g0, dn1g1, dn2g0, dn2g1, dkkp, dka, drk, dlnw, dlnb, dfg,
             dmix[0:6], dw00, dw01, da00, da01, dconv[0:3]]
    got2 = all_gather8("ag_grads", pack_rows("pack_grads", parts, 48))
    small = sum_blocks("sum_grads", got2, list(range(N_DEV)))
    per_ex = got2[:, 0:24].reshape(N_DEV, 2, 2, 6 * D)
    tot = small[0:24].reshape(2, 2, 6 * D)
    cols = lambda a, width: lax.dynamic_slice_in_dim(a, s * width, width, axis=1)
    g_ada_w, dcond_parts = [], []
    for i in range(2):
        dm16 = cols(jnp.concatenate([per_ex[:, i, 0], _pad_rows(tot[i, 1][None], 8)], axis=0), 6 * Dq)
        g_ada_w.append(mm_tn(f"g_ada{i}", cond, dm16))
        dcond_parts.append(mm_nt(f"dcond{i}", dm16, ada.at(i)))
    g_ada_b = sum_cast("g_adab", [_pad_rows(tot[:, 0].reshape(12, D), 16), _pad_rows(tot[:, 1].reshape(12, D), 16)],
                       F32, 16)[0:12].reshape(2, 6 * D)
    dcond_mine = sum_cast("dcond_sum", dcond_parts, F32, 16)
    dcond = sum_blocks("dcond_chips", all_gather8("ag_dcond", dcond_mine), [0, 2, 4, 6])
    (dcin,), _ = row_vjp("b_cond", f_silu, [cond_in], [], [dcond], row_mask=[True], const_mask=[], tr=16, sr=16)

    gsh = _rs_finish("late", late_items, late_sums, late_landed, s)
    rw_items = [it for it in big if it[0] in rw_names]
    rw_sums = _rs_pair_sums("rw", rw_items, ci_arr)

    def view(n):
        return _view2d(n, p[n]), _view2d(n, p['m_' + n]), _view2d(n, p['v_' + n])

    done = {}
    *done['ffn_w13'], rw_flat = adamw_halves("adam_ffn_w13", *view('ffn_w13'), gsh['ffn_w13'][0], gsh['ffn_w13'][1],
                                             ci_arr, side=rs2_side([s_[1] for s_ in rw_sums]))
    gsh.update(_rs_finish("rw", rw_items, rw_sums, [rw_flat[3 * w:3 * w + 3] for w in range(len(rw_items))], s))

    grads = {}
    grads['c_ctx'] = dcin[8]
    grads['norm1_g'], grads['norm2_g'] = small[24:26], small[26:28]
    grads['ada_w'] = jnp.stack(g_ada_w)
    grads['ada_b'] = g_ada_b
    grads['rw_kk'], grads['rw_ka'], grads['rw_rk'] = small[28:29], small[29:30], small[30:31]
    grads['rw_lnw'], grads['rw_lnb'], grads['final_g'] = small[31:32], small[32:33], small[33]
    sharded = cols(small[34:48], Dq)
    grads['rw_mix'], grads['rw_w0'], grads['rw_a0'], grads['sc_conv'] = (
        sharded[0:6], sharded[6:8], sharded[8:10], sharded[10:13])

    outs_g, outs_d, outs_m, outs_v = [], [], [], []
    for n in WEIGHTS:
        shape = p[n].shape
        w2d, m2d, v2d = view(n)
        if n in done:
            g2d, d_, m_, v_ = done[n]
        elif n in gsh:
            g2d, d_, m_, v_, _ = adamw_halves("adam_" + n, w2d, m2d, v2d, gsh[n][0], gsh[n][1], ci_arr)
        else:
            g2d = _view2d(n, grads[n].reshape(shape))
            d_, m_, v_ = adamw("adam_" + n, w2d, g2d, m2d, v2d)
        outs_g.append(g2d.reshape(shape))
        outs_d.append(d_.reshape(shape))
        outs_m.append(m_.reshape(shape))
        outs_v.append(v_.reshape(shape))
    return (loss, grad_x.reshape(1, T, D), *outs_g, *outs_d, *outs_m, *outs_v)


def kernel(x, c, ctx, c_ctx, norm1_g, norm2_g, ada_w, ada_b, rw_mix, rw_wr, rw_wk, rw_wv, rw_wo, rw_w0, rw_w1, rw_w2, rw_a0, rw_a1, rw_a2, rw_g1, rw_g2, rw_kk, rw_ka, rw_rk, rw_lnw, rw_lnb, sc_win, sc_conv, sc_wout, ffn_w13, ffn_w2, final_g, loss_target, m_c_ctx, m_norm1_g, m_norm2_g, m_ada_w, m_ada_b, m_rw_mix, m_rw_wr, m_rw_wk, m_rw_wv, m_rw_wo, m_rw_w0, m_rw_w1, m_rw_w2, m_rw_a0, m_rw_a1, m_rw_a2, m_rw_g1, m_rw_g2, m_rw_kk, m_rw_ka, m_rw_rk, m_rw_lnw, m_rw_lnb, m_sc_win, m_sc_conv, m_sc_wout, m_ffn_w13, m_ffn_w2, m_final_g, v_c_ctx, v_norm1_g, v_norm2_g, v_ada_w, v_ada_b, v_rw_mix, v_rw_wr, v_rw_wk, v_rw_wv, v_rw_wo, v_rw_w0, v_rw_w1, v_rw_w2, v_rw_a0, v_rw_a1, v_rw_a2, v_rw_g1, v_rw_g2, v_rw_kk, v_rw_ka, v_rw_rk, v_rw_lnw, v_rw_lnb, v_sc_win, v_sc_conv, v_sc_wout, v_ffn_w13, v_ffn_w2, v_final_g):
    values = (x, c, ctx, c_ctx, norm1_g, norm2_g, ada_w, ada_b, rw_mix, rw_wr, rw_wk, rw_wv, rw_wo, rw_w0, rw_w1, rw_w2, rw_a0, rw_a1, rw_a2, rw_g1, rw_g2, rw_kk, rw_ka, rw_rk, rw_lnw, rw_lnb, sc_win, sc_conv, sc_wout, ffn_w13, ffn_w2, final_g, loss_target, m_c_ctx, m_norm1_g, m_norm2_g, m_ada_w, m_ada_b, m_rw_mix, m_rw_wr, m_rw_wk, m_rw_wv, m_rw_wo, m_rw_w0, m_rw_w1, m_rw_w2, m_rw_a0, m_rw_a1, m_rw_a2, m_rw_g1, m_rw_g2, m_rw_kk, m_rw_ka, m_rw_rk, m_rw_lnw, m_rw_lnb, m_sc_win, m_sc_conv, m_sc_wout, m_ffn_w13, m_ffn_w2, m_final_g, v_c_ctx, v_norm1_g, v_norm2_g, v_ada_w, v_ada_b, v_rw_mix, v_rw_wr, v_rw_wk, v_rw_wv, v_rw_wo, v_rw_w0, v_rw_w1, v_rw_w2, v_rw_a0, v_rw_a1, v_rw_a2, v_rw_g1, v_rw_g2, v_rw_kk, v_rw_ka, v_rw_rk, v_rw_lnw, v_rw_lnb, v_sc_win, v_sc_conv, v_sc_wout, v_ffn_w13, v_ffn_w2, v_final_g)
    return _step(dict(zip(INPUTS, values)))
```

```python
import functools
import math

import jax
import jax.numpy as jnp
from jax import lax
from jax.experimental import pallas as pl
from jax.experimental.pallas import tpu as pltpu

F32 = jnp.float32
BF16 = jnp.bfloat16
MESH = pl.DeviceIdType.MESH

GRID_W = 64
HEAD = 64
LANES = 128
N_CHIPS = 4
N_DEV = 8
NORM_EPS = 1e-6
GN_EPS = 64e-5
ADAM_LR, ADAM_B1, ADAM_B2, ADAM_EPS, ADAM_WD, ADAM_STEP = 0.001, 0.9, 0.999, 1e-08, 0.01, 10
VMEM_LIMIT = 56 * 1024 * 1024
HI = lax.Precision.HIGHEST
WEIGHTS = ['c_ctx', 'norm1_g', 'norm2_g', 'ada_w', 'ada_b', 'rw_mix', 'rw_wr', 'rw_wk', 'rw_wv', 'rw_wo', 'rw_w0',
           'rw_w1', 'rw_w2', 'rw_a0', 'rw_a1', 'rw_a2', 'rw_g1', 'rw_g2', 'rw_kk', 'rw_ka', 'rw_rk', 'rw_lnw',
           'rw_lnb', 'sc_win', 'sc_conv', 'sc_wout', 'ffn_w13', 'ffn_w2', 'final_g']
INPUTS = (['x', 'c', 'ctx'] + WEIGHTS + ['loss_target'] + ['m_' + w for w in WEIGHTS]
          + ['v_' + w for w in WEIGHTS])


def _cparams(**kw):
    return pltpu.CompilerParams(vmem_limit_bytes=VMEM_LIMIT, **kw)


def _pick(dim, cands):
    for c in cands:
        if dim % c == 0:
            return c
    return dim


def _place():
    return lax.axis_index("x"), lax.axis_index("y"), lax.axis_index("c")


_TILE_M = (1024, 768, 512, 1408, 256, 128)
_TILE_N = (1408, 1024, 768, 512, 256, 128)
_TILE_K = (2816, 2304, 2048, 1536, 1408, 1152, 1024, 768, 704, 512, 256, 128)
MM_TILE_BYTES = 40 * 1024 * 1024


def _pick_k(unit, tm, tn, a_dtype, b_dtype, o_dtype):
    ab, bb, ob = (jnp.dtype(d).itemsize for d in (a_dtype, b_dtype, o_dtype))
    for tk in _TILE_K:
        if unit % tk == 0 and 2 * (tm * tk * ab + tk * tn * bb) + 2 * tm * tn * ob + tm * tn * 4 <= MM_TILE_BYTES:
            return tk
    return unit


class Stacked:
    def __init__(self, arr, kind, r, layer=0):
        self.arr, self.kind, self.r, self.layer = arr, kind, r, layer
        self.c = arr.shape[2]
        self.shape = {"row": (N_CHIPS * r, self.c), "col": (r, N_CHIPS * self.c), "layer": (r, self.c)}[kind]

    def at(self, layer):
        return Stacked(self.arr, self.kind, self.r, layer)

    def spec(self, t0, t1, swap):
        r, c, layer = self.r, self.c, self.layer
        per_r, per_c = r // t0, c // t1
        assert r % t0 == 0 and c % t1 == 0
        kind = self.kind

        def index(i, j, k):
            ri, ci = (j, k) if swap else (k, j)
            if kind == "row":
                return (ri // per_r, layer * per_r + ri % per_r, ci)
            if kind == "layer":
                return (layer, ri, ci)
            return (ci // per_c, layer * per_r + ri, ci % per_c)

        return pl.BlockSpec((None, t0, t1), index)


def _mm_body(dims, nk, a_ref, b_ref, o_ref, acc_ref):
    if nk == 1:
        o_ref[...] = lax.dot_general(a_ref[...].astype(BF16), b_ref[...].astype(BF16), (dims, ((), ())),
                                     preferred_element_type=F32).astype(o_ref.dtype)
        return
    k = pl.program_id(2)

    @pl.when(k == 0)
    def _():
        acc_ref[...] = jnp.zeros_like(acc_ref)

    acc_ref[...] += lax.dot_general(a_ref[...].astype(BF16), b_ref[...].astype(BF16), (dims, ((), ())),
                                    preferred_element_type=F32)

    @pl.when(k == nk - 1)
    def _():
        o_ref[...] = acc_ref[...].astype(o_ref.dtype)


def _mm_call(name, dims, grid, in_specs, out_spec, out_shape, acc_shape, operands, side=None):
    acc = pltpu.VMEM(acc_shape if grid[2] > 1 else (8, LANES), F32)
    if side is None:
        return pl.pallas_call(
            functools.partial(_mm_body, dims, grid[2]), name=name, grid=grid, in_specs=in_specs, out_specs=out_spec,
            out_shape=out_shape, scratch_shapes=[acc],
            compiler_params=_cparams(dimension_semantics=("parallel", "parallel", "arbitrary")),
        )(*operands)
    s_in, s_out, s_shape, s_scratch, s_alias, s_ops = _side_call_args(side, 2, 1)
    res = pl.pallas_call(
        _with_side(functools.partial(_mm_body, dims, grid[2]), side, 2, 1, 1, grid), name=name, grid=grid,
        in_specs=in_specs + s_in, out_specs=[out_spec] + s_out, out_shape=[out_shape] + s_shape,
        scratch_shapes=[acc] + s_scratch, input_output_aliases=s_alias,
        compiler_params=_cparams(dimension_semantics=("arbitrary", "arbitrary", "arbitrary")),
    )(*operands, *s_ops)
    return res[0], list(res[1:])


def mm_nn(name, a, b, out_dtype=F32):
    M, K = a.shape
    st = isinstance(b, Stacked)
    N = b.shape[1]
    tm = _pick(M, _TILE_M)
    tn = _pick(b.c if st and b.kind == "col" else N, _TILE_N)
    tk = _pick_k(b.r if st else K, tm, tn, a.dtype, b.arr.dtype if st else b.dtype, out_dtype)
    b_spec = b.spec(tk, tn, False) if st else pl.BlockSpec((tk, tn), lambda i, j, k: (k, j))
    return _mm_call(name, ((1,), (0,)), (M // tm, N // tn, K // tk),
                    [pl.BlockSpec((tm, tk), lambda i, j, k: (i, k)), b_spec],
                    pl.BlockSpec((tm, tn), lambda i, j, k: (i, j)), jax.ShapeDtypeStruct((M, N), out_dtype),
                    (tm, tn), (a, b.arr if st else b))


def mm_nt(name, a, b, out_dtype=F32, side=None):
    M, N = a.shape
    st = isinstance(b, Stacked)
    K = b.shape[0]
    tm = _pick(M, _TILE_M)
    to = _pick(b.r if st else K, _TILE_N)
    tc = _pick_k(b.c if st and b.kind == "col" else N, tm, to, a.dtype, b.arr.dtype if st else b.dtype, out_dtype)
    b_spec = b.spec(to, tc, True) if st else pl.BlockSpec((to, tc), lambda i, j, k: (j, k))
    return _mm_call(name, ((1,), (1,)), (M // tm, K // to, N // tc),
                    [pl.BlockSpec((tm, tc), lambda i, j, k: (i, k)), b_spec],
                    pl.BlockSpec((tm, to), lambda i, j, k: (i, j)), jax.ShapeDtypeStruct((M, K), out_dtype),
                    (tm, to), (a, b.arr if st else b), side)


def mm_tn(name, a, b, kind=None):
    R, M = a.shape
    N = b.shape[1]
    r, c = (M // N_CHIPS, N) if kind == "row" else (M, N // N_CHIPS) if kind == "col" else (M, N)
    tm, tn = _pick(r, _TILE_M), _pick(c, _TILE_N)
    tk = _pick_k(R, tm, tn, a.dtype, b.dtype, F32)
    if kind:
        per_r, per_c = r // tm, c // tn
        if kind == "row":
            o_spec = pl.BlockSpec((None, tm, tn), lambda i, j, k: (i // per_r, i % per_r, j))
        else:
            o_spec = pl.BlockSpec((None, tm, tn), lambda i, j, k: (j // per_c, i, j % per_c))
        o_shape = jax.ShapeDtypeStruct((N_CHIPS, r, c), F32)
    else:
        o_spec = pl.BlockSpec((tm, tn), lambda i, j, k: (i, j))
        o_shape = jax.ShapeDtypeStruct((M, N), F32)
    return _mm_call(name, ((0,), (0,)), (M // tm, N // tn, R // tk),
                    [pl.BlockSpec((tk, tm), lambda i, j, k: (k, i)), pl.BlockSpec((tk, tn), lambda i, j, k: (k, j))],
                    o_spec, o_shape, (tm, tn), (a, b))


def _shifted(o):
    return lambda i: (i + o, 0)


def row_call(name, f, rows, consts, outs, *, tr, sr, offs=None):
    offs = offs or [0] * len(rows)
    n_rows = min(r.shape[0] - o * tr for r, o in zip(rows, offs))
    nr, nc = len(rows), len(consts)

    def body(*refs):
        row_refs, const_refs, out_refs = refs[:nr], refs[nr:nr + nc], refs[nr + nc:]
        i = pl.program_id(0)
        cvals = [r[...] for r in const_refs]

        def step(j, carry):
            sl = pl.ds(pl.multiple_of(j * sr, sr), sr)
            res = f(i, *[r[sl, :] for r in row_refs], *cvals)
            for o, v in zip(out_refs, res):
                o[sl, :] = v.astype(o.dtype)
            return carry

        lax.fori_loop(0, tr // sr, step, 0)

    in_specs = [pl.BlockSpec((tr, r.shape[1]), _shifted(o)) for r, o in zip(rows, offs)]
    in_specs += [pl.BlockSpec(c.shape, lambda i: (0, 0)) for c in consts]
    return pl.pallas_call(
        body, name=name, grid=(n_rows // tr,), in_specs=in_specs,
        out_specs=[pl.BlockSpec((tr, w), lambda i: (i, 0)) for w, _ in outs],
        out_shape=[jax.ShapeDtypeStruct((n_rows, w), dt) for w, dt in outs],
        compiler_params=_cparams(dimension_semantics=("parallel",)),
    )(*rows, *consts)


def row_vjp(name, f, rows, consts, cots, *, row_mask, const_mask, tr, sr, offs=None, bf16_rows=(), side=None):
    offs = offs or [0] * len(rows)
    n_rows = min(r.shape[0] - o * tr for r, o in zip(rows, offs))
    nr, nc = len(rows), len(consts)
    cot_in = [c for c in cots if c is not None]
    nct = len(cot_in)
    d_rows = [i for i in range(nr) if row_mask[i]]
    d_consts = [i for i in range(nc) if const_mask[i]]

    def body(*refs):
        row_refs, const_refs = refs[:nr], refs[nr:nr + nc]
        cot_refs = refs[nr + nc:nr + nc + nct]
        drow_refs = refs[nr + nc + nct:nr + nc + nct + len(d_rows)]
        dconst_refs = refs[nr + nc + nct + len(d_rows):]
        i = pl.program_id(0)

        @pl.when(i == 0)
        def _():
            for r in dconst_refs:
                r[...] = jnp.zeros_like(r)

        cvals = [r[...] for r in const_refs]

        def step(j, carry):
            sl = pl.ds(pl.multiple_of(j * sr, sr), sr)
            rvals = [r[sl, :] for r in row_refs]

            def g(*diff):
                rv, cv = list(rvals), list(cvals)
                for idx, val in zip(d_rows, diff[:len(d_rows)]):
                    rv[idx] = val
                for idx, val in zip(d_consts, diff[len(d_rows):]):
                    cv[idx] = val
                return f(i, *rv, *cv)

            primals = [rvals[idx].astype(F32) for idx in d_rows] + [cvals[idx] for idx in d_consts]
            res, vjp = jax.vjp(g, *primals)
            it = iter(cot_refs)
            cts = tuple(jnp.zeros_like(o) if c is None else next(it)[sl, :].astype(o.dtype) for o, c in zip(res, cots))
            grads = vjp(cts)
            for r, val in zip(drow_refs, grads[:len(d_rows)]):
                r[sl, :] = val.astype(r.dtype)
            for r, val in zip(dconst_refs, grads[len(d_rows):]):
                r[...] += val
            return carry

        lax.fori_loop(0, tr // sr, step, 0)

    in_specs = [pl.BlockSpec((tr, r.shape[1]), _shifted(o)) for r, o in zip(rows, offs)]
    in_specs += [pl.BlockSpec(c.shape, lambda i: (0, 0)) for c in consts]
    in_specs += [pl.BlockSpec((tr, c.shape[1]), lambda i: (i, 0)) for c in cot_in]
    out_specs = [pl.BlockSpec((tr, rows[i].shape[1]), lambda i: (i, 0)) for i in d_rows]
    out_specs += [pl.BlockSpec(consts[i].shape, lambda i: (0, 0)) for i in d_consts]
    out_shape = [jax.ShapeDtypeStruct((n_rows, rows[i].shape[1]), BF16 if i in bf16_rows else F32) for i in d_rows]
    out_shape += [jax.ShapeDtypeStruct(consts[i].shape, F32) for i in d_consts]
    n_in, n_out = nr + nc + nct, len(d_rows) + len(d_consts)
    s_in, s_out, s_shape, s_scratch, s_alias, s_ops = _side_call_args(side, n_in, n_out)
    res = pl.pallas_call(
        _with_side(body, side, n_in, n_out, 0, n_rows // tr), name=name, grid=(n_rows // tr,),
        in_specs=in_specs + s_in, out_specs=out_specs + s_out, out_shape=out_shape + s_shape,
        scratch_shapes=s_scratch, input_output_aliases=s_alias,
        compiler_params=_cparams(dimension_semantics=("arbitrary",)),
    )(*rows, *consts, *cot_in, *s_ops)
    if side is None:
        return list(res[:len(d_rows)]), list(res[len(d_rows):])
    return list(res[:len(d_rows)]), list(res[len(d_rows):n_out]), list(res[n_out:])


def _sigmoid(x):
    return 1.0 / (1.0 + jnp.exp(-x))


def _softplus(u):
    return jnp.maximum(u, 0.0) + jnp.log(1.0 + jnp.exp(-jnp.abs(u)))


def _rms(x, g):
    ms = jnp.sum(x * x, axis=-1, keepdims=True) * (1.0 / x.shape[-1])
    return x * lax.rsqrt(ms + NORM_EPS) * g


def _hsum_impl(x, ones2):
    rows, width = x.shape
    nch = width // LANES
    xs = jnp.concatenate([x[:, j * LANES:(j + 1) * LANES] for j in range(nch)], axis=0)
    hi = xs.astype(BF16)
    lo = (xs - hi.astype(F32)).astype(BF16)
    ys = jnp.dot(jnp.concatenate([hi, lo], axis=1), ones2, preferred_element_type=F32)
    return jnp.concatenate([ys[j * rows:(j + 1) * rows] for j in range(nch)], axis=1)


@jax.custom_vjp
def _hsum(x, ones2):
    return _hsum_impl(x, ones2)


def _hsum_fwd(x, ones2):
    return _hsum_impl(x, ones2), ones2


def _hsum_bwd(ones2, g):
    return _hsum_impl(g, ones2), jnp.zeros_like(ones2)


_hsum.defvjp(_hsum_fwd, _hsum_bwd)


def f_silu(i, x):
    return (x * _sigmoid(x),)


def f_sigmoid(i, x):
    return (_sigmoid(x),)


def f_tanh(i, x):
    return (jnp.tanh(x),)


def f_norm_mod(n_ctx_tiles, i, xin, g, sh_c, sc_c, sh_x, sc_x):
    is_x = i >= n_ctx_tiles
    sh = jnp.where(is_x, sh_x, sh_c)
    sc = jnp.where(is_x, sc_x, sc_c)
    return (_rms(xin, g) * (1.0 + sc) + sh,)


def f_res_norm_mod(i, x, y, gt, g, sh, sc):
    x1 = x + gt * y
    return x1, _rms(x1, g) * (1.0 + sc) + sh


def f_post(i, k, zw0, zw1, za0, za1, kkp, ka, w00, w01, a00, a01, ones2):
    kq = k * kkp
    kk = kq / jnp.maximum(jnp.sqrt(_hsum(kq * kq, ones2)), 1e-12)

    def direction(zw, za, w0, a0):
        log_w = -_softplus(-(w0 + zw)) - 0.5
        a = _sigmoid(a0 + za)
        return jnp.exp(-jnp.exp(log_w)), k * (1.0 + (a - 1.0) * ka), kk * a

    dec0, kd0, bb0 = direction(zw0, za0, w00, a00)
    dec1, kd1, bb1 = direction(zw1, za1, w01, a01)
    return -kk, -kk, dec0, dec1, kd0, kd1, bb0, bb1, kd0 + kd1


def f_post_fwd(*a):
    return f_post(*a)[1:]


def f_readout(i, y0, y1, r, ksum, v, g, rk, lnw, lnb, ones2):
    y = y0 + y1
    yc = y - _hsum(y, ones2) * (1.0 / HEAD)
    var = _hsum(yc * yc, ones2) * (1.0 / HEAD)
    o = yc * lax.rsqrt(var + GN_EPS) * lnw + lnb
    o = o + _hsum(r * ksum * rk, ones2) * v
    return (o * g,)


def f_sum(i, *xs):
    acc = xs[0].astype(F32)
    for x in xs[1:]:
        acc = acc + x.astype(F32)
    return (acc,)


def sum_cast(name, arrs, dtype, tr, offs=None):
    return row_call(name, f_sum, arrs, [], [(arrs[0].shape[1], dtype)], tr=tr, sr=16, offs=offs)[0]


def swiglu_fwd(name, ab, tr):
    T, F2 = ab.shape
    F = F2 // 2
    sr = 16

    def body(ab_ref, o_ref):
        def step(j, carry):
            sl = pl.ds(pl.multiple_of(j * sr, sr), sr)
            a, b = ab_ref[sl, :F], ab_ref[sl, F:]
            o_ref[sl, :] = (a * _sigmoid(a) * b).astype(o_ref.dtype)
            return carry

        lax.fori_loop(0, tr // sr, step, 0)

    return pl.pallas_call(
        body, name=name, grid=(T // tr,), in_specs=[pl.BlockSpec((tr, F2), lambda i: (i, 0))],
        out_specs=pl.BlockSpec((tr, F), lambda i: (i, 0)), out_shape=jax.ShapeDtypeStruct((T, F), BF16),
        compiler_params=_cparams(dimension_semantics=("parallel",)),
    )(ab)


def swiglu_bwd(name, ab, dsw, tr):
    T, F2 = ab.shape
    F = F2 // 2
    sr = 16

    def body(ab_ref, d_ref, o_ref):
        def step(j, carry):
            sl = pl.ds(pl.multiple_of(j * sr, sr), sr)
            a, b, d = ab_ref[sl, :F], ab_ref[sl, F:], d_ref[sl, :]
            sg = _sigmoid(a)
            o_ref[sl, :F] = (d * b * (sg + a * sg * (1.0 - sg))).astype(o_ref.dtype)
            o_ref[sl, F:] = (d * a * sg).astype(o_ref.dtype)
            return carry

        lax.fori_loop(0, tr // sr, step, 0)

    return pl.pallas_call(
        body, name=name, grid=(T // tr,),
        in_specs=[pl.BlockSpec((tr, F2), lambda i: (i, 0)), pl.BlockSpec((tr, F), lambda i: (i, 0))],
        out_specs=pl.BlockSpec((tr, F2), lambda i: (i, 0)), out_shape=jax.ShapeDtypeStruct((T, F2), BF16),
        compiler_params=_cparams(dimension_semantics=("parallel",)),
    )(ab, dsw)


def final_call(name, x3, f1, gt, fg, tgt, tr):
    T, D = x3.shape
    sr = 16

    def f(x, y, gtv, g, t):
        err = _rms(x + gtv * y, g) - t
        return 0.5 * jnp.sum(err * err) * (1.0 / D)

    def body(x_ref, y_ref, gt_ref, g_ref, t_ref, dx_ref, dy_ref, dgt_ref, dg_ref, loss_ref):
        @pl.when(pl.program_id(0) == 0)
        def _():
            dgt_ref[...] = jnp.zeros_like(dgt_ref)
            dg_ref[...] = jnp.zeros_like(dg_ref)
            loss_ref[...] = jnp.zeros_like(loss_ref)

        def step(j, carry):
            sl = pl.ds(pl.multiple_of(j * sr, sr), sr)
            val, vjp = jax.vjp(lambda x, y, a, b: f(x, y, a, b, t_ref[sl, :]), x_ref[sl, :], y_ref[sl, :],
                               gt_ref[...], g_ref[...])
            dx, dy, dgt, dg = vjp(jnp.ones((), F32))
            dx_ref[sl, :] = dx
            dy_ref[sl, :] = dy.astype(dy_ref.dtype)
            dgt_ref[...] += dgt
            dg_ref[...] += dg
            loss_ref[...] += jnp.full(loss_ref.shape, val, F32)
            return carry

        lax.fori_loop(0, tr // sr, step, 0)

    row = pl.BlockSpec((tr, D), lambda i: (i, 0))
    vec = pl.BlockSpec((1, D), lambda i: (0, 0))
    return pl.pallas_call(
        body, name=name, grid=(T // tr,), in_specs=[row, row, vec, vec, row],
        out_specs=[row, row, vec, vec, pl.BlockSpec((8, LANES), lambda i: (0, 0))],
        out_shape=[jax.ShapeDtypeStruct((T, D), F32), jax.ShapeDtypeStruct((T, D), BF16)]
        + [jax.ShapeDtypeStruct((1, D), F32)] * 2
        + [jax.ShapeDtypeStruct((8, LANES), F32)],
        compiler_params=_cparams(dimension_semantics=("arbitrary",)),
    )(x3, f1, gt, fg, tgt)


def _tshift(x, kind, period):
    n = x.shape[0]
    t = lax.broadcasted_iota(jnp.int32, x.shape, 0)
    if kind == 0:
        return jnp.where((t & (period - 1)) == 0, 0.0, pltpu.roll(x, 1, 0))
    if kind == 1:
        return jnp.where(((t & (period - 1)) == period - 1) | (t == n - 1), 0.0, pltpu.roll(x, n - 1, 0))
    if kind == 2:
        return jnp.where(t < GRID_W, 0.0, pltpu.roll(x, GRID_W, 0))
    return jnp.where(t >= n - GRID_W, 0.0, pltpu.roll(x, n - GRID_W, 0))


def _pow2_at_least(n):
    return 1 << (n - 1).bit_length()


def _shift_into(dst_ref, h_ref, n_ctx, cb, D, transpose):
    j = pl.program_id(0)
    quarter = (j * cb * 4) // D
    half = (j * cb * 2) // D
    flip = 1 if transpose else 0
    for q in range(4):
        @pl.when(quarter == q)
        def _(q=q):
            dst_ref[n_ctx:, :] = _tshift(h_ref[n_ctx:, :], q ^ flip, GRID_W)
    for q in range(2):
        @pl.when(half == q)
        def _(q=q):
            dst_ref[:n_ctx, :] = _tshift(h_ref[:n_ctx, :], q ^ flip, _pow2_at_least(n_ctx))


def mix_fwd(name, h, mix, n_ctx):
    R, D = h.shape
    cb = LANES

    def body(h_ref, mix_ref, *rest):
        outs, hs_ref = rest[:6], rest[6]
        _shift_into(hs_ref, h_ref, n_ctx, cb, D, False)
        hv = h_ref[...]
        xx = hs_ref[...] - hv
        for m in range(6):
            outs[m][...] = (hv + xx * mix_ref[m:m + 1, :]).astype(BF16)

    col = pl.BlockSpec((R, cb), lambda j: (0, j))
    return pl.pallas_call(
        body, name=name, grid=(D // cb,), in_specs=[col, pl.BlockSpec((mix.shape[0], cb), lambda j: (0, j))],
        out_specs=[col] * 6, out_shape=[jax.ShapeDtypeStruct((R, D), BF16)] * 6,
        scratch_shapes=[pltpu.VMEM((R, cb), F32)],
        compiler_params=_cparams(dimension_semantics=("parallel",)),
    )(h, mix)


def mix_bwd(name, h, mix, cots, slots, n_ctx):
    R, D = h.shape
    cb = LANES
    nc = len(cots)

    def body(h_ref, mix_ref, *rest):
        cot_refs, dh_ref, dmix_ref, hs_ref, dxx_ref = rest[:nc], rest[nc], rest[nc + 1], rest[nc + 2], rest[nc + 3]
        _shift_into(hs_ref, h_ref, n_ctx, cb, D, False)
        xx = hs_ref[...] - h_ref[...]
        per_slot = [None] * 6
        for cref, m in zip(cot_refs, slots):
            per_slot[m] = cref[...] if per_slot[m] is None else per_slot[m] + cref[...]
        dh = jnp.zeros((R, cb), F32)
        dxx = jnp.zeros((R, cb), F32)
        rows = []
        for m in range(6):
            d = per_slot[m]
            dh = dh + d
            dxx = dxx + d * mix_ref[m:m + 1, :]
            rows.append(jnp.sum(d * xx, axis=0, keepdims=True))
        dmix_ref[...] = jnp.concatenate(rows + [jnp.zeros((2, cb), F32)], axis=0)
        dxx_ref[...] = dxx
        _shift_into(hs_ref, dxx_ref, n_ctx, cb, D, True)
        dh_ref[...] = dh - dxx + hs_ref[...]

    col = pl.BlockSpec((R, cb), lambda j: (0, j))
    return pl.pallas_call(
        body, name=name, grid=(D // cb,), in_specs=[col, pl.BlockSpec((mix.shape[0], cb), lambda j: (0, j))] + [col] * nc,
        out_specs=[col, pl.BlockSpec((8, cb), lambda j: (0, j))],
        out_shape=[jax.ShapeDtypeStruct((R, D), F32), jax.ShapeDtypeStruct((8, D), F32)],
        scratch_shapes=[pltpu.VMEM((R, cb), F32), pltpu.VMEM((R, cb), F32)],
        compiler_params=_cparams(dimension_semantics=("parallel",)),
    )(h, mix, *cots)


def _conv_parts(gb_ref, gc_ref, u_ref, cw_ref):
    T = gb_ref.shape[0]
    z = gc_ref[...] * u_ref[...]
    zp, zn = _tshift(z, 0, _pow2_at_least(T)), _tshift(z, 1, _pow2_at_least(T))
    conv = zp * cw_ref[0:1, :] + z * cw_ref[1:2, :] + zn * cw_ref[2:3, :]
    return z, zp, zn, conv


def conv_fwd(name, gcu, cw):
    T, D3 = gcu.shape
    D = D3 // 3
    cb = LANES
    nb = D // cb

    def body(gb_ref, gc_ref, u_ref, cw_ref, o_ref):
        _, _, _, conv = _conv_parts(gb_ref, gc_ref, u_ref, cw_ref)
        o_ref[...] = (gb_ref[...] * conv).astype(BF16)

    def part(p):
        return pl.BlockSpec((T, cb), lambda j: (0, j + p * nb))

    return pl.pallas_call(
        body, name=name, grid=(nb,),
        in_specs=[part(0), part(1), part(2), pl.BlockSpec((cw.shape[0], cb), lambda j: (0, j))],
        out_specs=pl.BlockSpec((T, cb), lambda j: (0, j)), out_shape=jax.ShapeDtypeStruct((T, D), BF16),
        compiler_params=_cparams(dimension_semantics=("parallel",)),
    )(gcu, gcu, gcu, cw)


def conv_bwd(name, gcu, cw, dp):
    T, D3 = gcu.shape
    D = D3 // 3
    cb = LANES
    nb = D // cb

    def body(gb_ref, gc_ref, u_ref, cw_ref, dp_ref, o_ref, dcw_ref):
        part = pl.program_id(1)
        z, zp, zn, conv = _conv_parts(gb_ref, gc_ref, u_ref, cw_ref)
        dpv = dp_ref[...]
        dconv = dpv * gb_ref[...]
        period = _pow2_at_least(T)
        dz = (_tshift(dconv * cw_ref[0:1, :], 1, period) + dconv * cw_ref[1:2, :]
              + _tshift(dconv * cw_ref[2:3, :], 0, period))

        @pl.when(part == 0)
        def _():
            o_ref[...] = (dpv * conv).astype(o_ref.dtype)
            dcw_ref[...] = jnp.concatenate(
                [jnp.sum(dconv * s, axis=0, keepdims=True) for s in (zp, z, zn)] + [jnp.zeros((5, cb), F32)], axis=0)

        @pl.when(part == 1)
        def _():
            o_ref[...] = (dz * u_ref[...]).astype(o_ref.dtype)

        @pl.when(part == 2)
        def _():
            o_ref[...] = (dz * gc_ref[...]).astype(o_ref.dtype)

    def part_spec(p):
        return pl.BlockSpec((T, cb), lambda j, q: (0, j + p * nb))

    return pl.pallas_call(
        body, name=name, grid=(nb, 3),
        in_specs=[part_spec(0), part_spec(1), part_spec(2), pl.BlockSpec((cw.shape[0], cb), lambda j, q: (0, j)),
                  pl.BlockSpec((T, cb), lambda j, q: (0, j))],
        out_specs=[pl.BlockSpec((T, cb), lambda j, q: (0, j + q * nb)), pl.BlockSpec((8, cb), lambda j, q: (0, j))],
        out_shape=[jax.ShapeDtypeStruct((T, D3), BF16), jax.ShapeDtypeStruct((8, D), F32)],
        compiler_params=_cparams(dimension_semantics=("arbitrary", "arbitrary")),
    )(gcu, gcu, gcu, cw, dp)


SCAN_TC = 8


def _scan_consts():
    rows = lax.broadcasted_iota(jnp.int32, (HEAD, LANES), 0)
    cols = lax.broadcasted_iota(jnp.int32, (HEAD, LANES), 1)
    eye = rows == (cols & (HEAD - 1))
    r2 = lax.broadcasted_iota(jnp.int32, (2 * LANES, LANES), 0)
    c2 = lax.broadcasted_iota(jnp.int32, (2 * LANES, LANES), 1)
    ones2 = (((r2 & (LANES - 1)) >= HEAD) == (c2 >= HEAD)).astype(BF16)
    return eye, ones2, ones2[:LANES]


SCAN_ROW_CHUNKS = 4


def _chunks_of_heads(hp):
    per = max(1, hp // SCAN_ROW_CHUNKS)
    return [range(lo, lo + per) for lo in range(0, hp, per)]


def _rows_of(heads):
    return pl.ds(heads[0] * HEAD, len(heads) * HEAD)


def _split2(p):
    hi = p.astype(BF16)
    lo = (p - hi.astype(F32)).astype(BF16)
    return jnp.concatenate([hi, lo], axis=1)


def _head_rows(h):
    return pl.ds(h * HEAD, HEAD)


def _expand_into(dst, p1_ref, row_ref, t, hp, eye, ones1):
    for h in range(hp):
        p1_ref[_head_rows(h), :] = jnp.where(eye, row_ref[t, h:h + 1, :], 0.0).astype(BF16)
    dst[...] = jnp.dot(p1_ref[...], ones1, preferred_element_type=F32)


def _colsum_store(ref, t, h, x):
    ref[t, pl.ds(h, 1), :] = jnp.sum(x, axis=0, keepdims=True)


def _order(i, n_ctx, n_all, rev):
    if not rev:
        return i
    return jnp.where(i < n_ctx, n_ctx - 1 - i, n_all - 1 - (i - n_ctx))


def _with_side(work, side, n_in, n_out, n_scratch, grid):
    if side is None:
        return work
    nsi, nso = len(side.ins), len(side.out_shapes)
    grid = (grid,) if isinstance(grid, int) else tuple(grid)

    def at(which):
        cond = None
        for d, g in enumerate(grid):
            c = pl.program_id(d) == (0 if which == "first" else g - 1)
            cond = c if cond is None else cond & c
        return cond

    def body(*refs):
        ins, side_in = refs[:n_in], refs[n_in:n_in + nsi]
        outs = refs[n_in + nsi:n_in + nsi + n_out]
        side_out = refs[n_in + nsi + n_out:n_in + nsi + n_out + nso]
        scratch = refs[n_in + nsi + n_out + nso:]

        @pl.when(at("first"))
        def _():
            side.start(side_in, side_out, scratch[n_scratch:])

        work(*ins, *outs, *scratch[:n_scratch])

        @pl.when(at("last"))
        def _():
            side.finish(side_in, side_out, scratch[n_scratch:])

    return body


def _side_call_args(side, n_in, n_out):
    if side is None:
        return [], [], [], [], {}, []
    aliases = {n_in + i: n_out + o for i, o in side.aliases.items()}
    return ([ANY] * len(side.ins), [ANY] * len(side.out_shapes), list(side.out_shapes), list(side.sems), aliases,
            list(side.ins))


def scan_fwd(name, r, w, k, v, a, b, n_ctx_rows, rev, side=None):
    R, D = r.shape
    hp, tc = D // LANES, SCAN_TC
    n_all, n_ctx = R // tc, n_ctx_rows // tc
    ins = [t.reshape(R, hp, LANES) for t in (r, w, k, v, a, b)]

    def work(r_ref, w_ref, k_ref, v_ref, a_ref, b_ref, y_ref, st_ref, s_ref, ve_ref, sa_ref, p_ref, p1_ref, ys_ref):
        @pl.when(pl.program_id(0) == 0)
        def _():
            s_ref[...] = jnp.zeros_like(s_ref)

        eye, ones2, ones1 = _scan_consts()

        def row_of(q):
            return tc - 1 - q if rev else q

        def expand(q):
            _expand_into(ve_ref.at[q], p1_ref.at[q % 2], v_ref, row_of(q), hp, eye, ones1)

        def advance(q):
            t, prev_ref = row_of(q), (s_ref if q == 0 else st_ref.at[q - 1])
            for heads in _chunks_of_heads(hp):
                rows = _rows_of(heads)
                for h in heads:
                    p_ref[q % 2, _head_rows(h), :] = _split2(prev_ref[_head_rows(h), :] * a_ref[t, h:h + 1, :])
                sa_ref[q % 2, rows, :] = jnp.dot(p_ref[q % 2, rows, :], ones2, preferred_element_type=F32)
                for h in heads:
                    hr_ = _head_rows(h)
                    st_ref[q, hr_, :] = (prev_ref[hr_, :] * w_ref[t, h:h + 1, :]
                                         + sa_ref[q % 2, hr_, :] * b_ref[t, h:h + 1, :]
                                         + ve_ref[q, hr_, :] * k_ref[t, h:h + 1, :])

        def readout(q):
            t = row_of(q)
            for h in range(hp):
                p1_ref[2 + q % 2, _head_rows(h), :] = (st_ref[q, _head_rows(h), :]
                                                       * r_ref[t, h:h + 1, :]).astype(BF16)
            ys_ref[q % 2] = jnp.dot(p1_ref[2 + q % 2], ones1, preferred_element_type=F32)
            for h in range(hp):
                _colsum_store(y_ref, t, h, jnp.where(eye, ys_ref[q % 2, _head_rows(h), :], 0.0))

        expand(0)
        for q in range(tc):
            if q + 1 < tc:
                expand(q + 1)
            advance(q)
            if q > 0:
                readout(q - 1)
        readout(tc - 1)
        s_ref[...] = st_ref[tc - 1]

    row_spec = pl.BlockSpec((tc, hp, LANES), lambda i: (_order(i, n_ctx, n_all, rev), 0, 0))
    n = hp * HEAD
    s_in, s_out, s_shape, s_scratch, s_alias, s_ops = _side_call_args(side, 6, 2)
    y, st, *side_res = pl.pallas_call(
        _with_side(work, side, 6, 2, 6, n_all), name=name, grid=(n_all,), in_specs=[row_spec] * 6 + s_in,
        out_specs=[row_spec, pl.BlockSpec((tc, n, LANES), lambda i: (i, 0, 0))] + s_out,
        out_shape=[jax.ShapeDtypeStruct((R, hp, LANES), F32), jax.ShapeDtypeStruct((R, n, LANES), F32)] + s_shape,
        scratch_shapes=[pltpu.VMEM((n, LANES), F32), pltpu.VMEM((tc, n, LANES), F32), pltpu.VMEM((2, n, LANES), F32),
                        pltpu.VMEM((2, n, 2 * LANES), BF16), pltpu.VMEM((4, n, LANES), BF16),
                        pltpu.VMEM((2, n, LANES), F32)] + s_scratch,
        input_output_aliases=s_alias,
        compiler_params=_cparams(dimension_semantics=("arbitrary",)),
    )(*ins, *s_ops)
    return y.reshape(R, D), st, side_res


def scan_bwd(name, r, w, k, v, a, b, dy, st, n_ctx_rows, rev, side=None):
    R, D = r.shape
    hp, tc = D // LANES, SCAN_TC
    n_all, n_ctx = R // tc, n_ctx_rows // tc
    ins = [t.reshape(R, hp, LANES) for t in (r, w, k, v, a, b, dy)]

    def work(r_ref, w_ref, k_ref, v_ref, a_ref, b_ref, dy_ref, st_ref, prev_ref,
             dr_ref, dw_ref, dk_ref, dv_ref, da_ref, db_ref,
             g_ref, s0_ref, ve_ref, dye_ref, sa_ref, gs_ref, tmp_ref, p_ref, p1_ref, tmp2_ref):
        i = pl.program_id(0)

        @pl.when(i == 0)
        def _():
            g_ref[...] = jnp.zeros_like(g_ref)

        eye, ones2, ones1 = _scan_consts()

        @pl.when(i == n_all - 1)
        def _():
            s0_ref[...] = jnp.zeros_like(s0_ref)

        @pl.when(i != n_all - 1)
        def _():
            s0_ref[...] = prev_ref[0]

        def row_of(q):
            return tc - 1 - q if rev else q

        def prev_of(q):
            return s0_ref if q == 0 else st_ref.at[q - 1]

        def before(q):
            t, prev = row_of(q), prev_of(q)
            _expand_into(ve_ref.at[q], p1_ref.at[2 + q % 2], v_ref, t, hp, eye, ones1)
            _expand_into(dye_ref.at[q], p1_ref.at[4 + q % 2], dy_ref, t, hp, eye, ones1)
            for h in range(hp):
                p1_ref[q % 2, _head_rows(h), :] = (prev[_head_rows(h), :] * a_ref[t, h:h + 1, :]).astype(BF16)
            sa_ref[q] = jnp.dot(p1_ref[q % 2], ones1, preferred_element_type=F32)

        def back(q):
            t, prev = row_of(q), prev_of(q)
            for heads in _chunks_of_heads(hp):
                rows = _rows_of(heads)
                for h in heads:
                    hr_ = _head_rows(h)
                    g = g_ref[hr_, :] + dye_ref[q, hr_, :] * r_ref[t, h:h + 1, :]
                    gs_ref[q, hr_, :] = g
                    p_ref[q % 2, hr_, :] = _split2(g * b_ref[t, h:h + 1, :])
                tmp_ref[q % 2, rows, :] = jnp.dot(p_ref[q % 2, rows, :], ones2, preferred_element_type=F32)
                for h in heads:
                    hr_ = _head_rows(h)
                    dsa = tmp_ref[q % 2, hr_, :]
                    _colsum_store(da_ref, t, h, prev[hr_, :] * dsa)
                    g_ref[hr_, :] = gs_ref[q, hr_, :] * w_ref[t, h:h + 1, :] + dsa * a_ref[t, h:h + 1, :]

        def after(q):
            t, prev = row_of(q), prev_of(q)
            for h in range(hp):
                hr_ = _head_rows(h)
                g = gs_ref[q, hr_, :]
                p1_ref[6 + q % 2, hr_, :] = (g * k_ref[t, h:h + 1, :]).astype(BF16)
                _colsum_store(dr_ref, t, h, st_ref[q, hr_, :] * dye_ref[q, hr_, :])
                _colsum_store(dk_ref, t, h, g * ve_ref[q, hr_, :])
                _colsum_store(dw_ref, t, h, g * prev[hr_, :])
                _colsum_store(db_ref, t, h, g * sa_ref[q, hr_, :])
            tmp2_ref[q % 2] = jnp.dot(p1_ref[6 + q % 2], ones1, preferred_element_type=F32)
            for h in range(hp):
                _colsum_store(dv_ref, t, h, jnp.where(eye, tmp2_ref[q % 2, _head_rows(h), :], 0.0))

        before(tc - 1)
        for q in reversed(range(tc)):
            if q > 0:
                before(q - 1)
            back(q)
            if q < tc - 1:
                after(q + 1)
        after(0)

    def pos(i):
        return n_all - 1 - i

    n = hp * HEAD
    row_spec = pl.BlockSpec((tc, hp, LANES), lambda i: (_order(pos(i), n_ctx, n_all, rev), 0, 0))
    big = pltpu.VMEM((tc, n, LANES), F32)
    one = pltpu.VMEM((n, LANES), F32)
    s_in, s_out, s_shape, s_scratch, s_alias, s_ops = _side_call_args(side, 9, 6)
    outs = pl.pallas_call(
        _with_side(work, side, 9, 6, 10, n_all), name=name, grid=(n_all,),
        in_specs=[row_spec] * 7 + [
            pl.BlockSpec((tc, n, LANES), lambda i: (pos(i), 0, 0)),
            pl.BlockSpec((1, n, LANES), lambda i: (jnp.maximum(pos(i) * tc - 1, 0), 0, 0))] + s_in,
        out_specs=[row_spec] * 6 + s_out,
        out_shape=[jax.ShapeDtypeStruct((R, hp, LANES), F32)] * 6 + s_shape,
        scratch_shapes=[one, one, big, big, big, big, pltpu.VMEM((2, n, LANES), F32),
                        pltpu.VMEM((2, n, 2 * LANES), BF16), pltpu.VMEM((8, n, LANES), BF16),
                        pltpu.VMEM((2, n, LANES), F32)] + s_scratch,
        input_output_aliases=s_alias,
        compiler_params=_cparams(dimension_semantics=("arbitrary",)),
    )(*ins, st, st, *s_ops)
    return [o.reshape(R, D) for o in outs[:6]], list(outs[6:])


ANY = pl.BlockSpec(memory_space=pl.ANY)


def _peer(xi, yi, ci, k):
    return (1 - xi if k & 4 else xi, 1 - yi if k & 2 else yi, 1 - ci if k & 1 else ci)


def _rcopy(src, dst, send_sem, recv_sem, dev):
    return pltpu.make_async_remote_copy(src_ref=src, dst_ref=dst, send_sem=send_sem, recv_sem=recv_sem,
                                        device_id=dev, device_id_type=MESH)


def _drain(copies):
    for cp in copies:
        if cp.is_remote:
            cp.wait_send()
        else:
            cp.wait()


def all_gather8(name, x):
    r, c = x.shape

    def body(x_ref, out_ref, send_sems, recv_sems, local_sem):
        xi, yi, ci = _place()

        def blk(p):
            return out_ref.at[4 * p[0] + 2 * p[1] + p[2]]

        me = (xi, yi, ci)
        mine = pltpu.make_async_copy(x_ref, blk(me), local_sem.at[0])
        mine.start()
        sends = [_rcopy(x_ref, blk(me), send_sems.at[k - 1], recv_sems.at[k - 1], _peer(xi, yi, ci, k))
                 for k in range(1, N_DEV)]
        for cp in sends:
            cp.start()
        for k in range(1, N_DEV):
            p = _peer(xi, yi, ci, k)
            _rcopy(x_ref, blk(p), send_sems.at[k - 1], recv_sems.at[k - 1], p).wait_recv()
        for cp in sends:
            cp.wait_send()
        mine.wait()

    vm = pl.BlockSpec(memory_space=pltpu.VMEM)
    return pl.pallas_call(
        body, name=name, in_specs=[vm], out_specs=vm, out_shape=jax.ShapeDtypeStruct((N_DEV, r, c), x.dtype),
        scratch_shapes=[pltpu.SemaphoreType.DMA((N_DEV - 1,)), pltpu.SemaphoreType.DMA((N_DEV - 1,)),
                        pltpu.SemaphoreType.DMA((1,))],
        compiler_params=_cparams(),
    )(x)


def _chips(xi, yi):
    chips = [(1 - xi, yi), (xi, 1 - yi), (1 - xi, 1 - yi)]
    return chips, [2 * cx + cy for cx, cy in chips]


def gather_weights(name, stacked):
    n = len(stacked)

    def body(*refs):
        _gather_start(refs[n:2 * n], refs[2 * n:])
        _gather_finish(refs[n:2 * n], refs[2 * n:])

    return pl.pallas_call(
        body, name=name, in_specs=[ANY] * n, out_specs=[ANY] * n,
        out_shape=[jax.ShapeDtypeStruct(a.shape, a.dtype) for a in stacked],
        input_output_aliases={w: w for w in range(n)},
        scratch_shapes=_gather_sems(n),
        compiler_params=_cparams(),
    )(*stacked)


def _gather_sems(n):
    return [pltpu.SemaphoreType.DMA((n, 6)), pltpu.SemaphoreType.DMA((n, 6))]


def _gather_sends(out, sems):
    send_sems, recv_sems = sems
    xi, yi, ci = _place()
    s = 2 * xi + yi
    chips, _ = _chips(xi, yi)
    sends = []
    for w in range(len(out)):
        hr = out[w].shape[1] // 2
        mine = out[w].at[s, pl.ds(ci * hr, hr)]
        sends += [_rcopy(mine, mine, send_sems.at[w, j], recv_sems.at[w, j], (cx, cy, ci))
                  for j, (cx, cy) in enumerate(chips)]
    return sends


def _gather_start(out, sems):
    for cp in _gather_sends(out, sems):
        cp.start()


def _gather_finish(out, sems):
    send_sems, recv_sems = sems
    xi, yi, ci = _place()
    chips, sidx = _chips(xi, yi)
    sib = (xi, yi, 1 - ci)
    passed = []
    for w in range(len(out)):
        hr = out[w].shape[1] // 2
        for j, (cx, cy) in enumerate(chips):
            blk = out[w].at[sidx[j], pl.ds(ci * hr, hr)]
            _rcopy(blk, blk, send_sems.at[w, j], recv_sems.at[w, j], (cx, cy, ci)).wait_recv()
            fw = _rcopy(blk, blk, send_sems.at[w, 3 + j], recv_sems.at[w, 3 + j], sib)
            fw.start()
            passed.append(fw)
    for w in range(len(out)):
        hr = out[w].shape[1] // 2
        for j in range(3):
            blk = out[w].at[sidx[j], pl.ds((1 - ci) * hr, hr)]
            _rcopy(blk, blk, send_sems.at[w, 3 + j], recv_sems.at[w, 3 + j], sib).wait_recv()
    _drain(_gather_sends(out, sems) + passed)


class Side:
    def __init__(self, ins, out_shapes, aliases, sems, start, finish):
        self.ins, self.out_shapes, self.aliases, self.sems = ins, out_shapes, aliases, sems
        self.start, self.finish = start, finish


def _gather_wait_ici(out, sems):
    send_sems, recv_sems = sems
    xi, yi, ci = _place()
    chips, sidx = _chips(xi, yi)
    for w in range(len(out)):
        hr = out[w].shape[1] // 2
        for j, (cx, cy) in enumerate(chips):
            blk = out[w].at[sidx[j], pl.ds(ci * hr, hr)]
            _rcopy(blk, blk, send_sems.at[w, j], recv_sems.at[w, j], (cx, cy, ci)).wait_recv()
    _drain(_gather_sends(out, sems))


def _pass_copies(out, sems, half_of):
    send_sems, recv_sems = sems
    xi, yi, ci = _place()
    _, sidx = _chips(xi, yi)
    sib = (xi, yi, 1 - ci)
    cps = []
    for w in range(len(out)):
        hr = out[w].shape[1] // 2
        for j in range(3):
            blk = out[w].at[sidx[j], pl.ds(half_of(ci) * hr, hr)]
            cps.append(_rcopy(blk, blk, send_sems.at[w, j], recv_sems.at[w, j], sib))
    return cps


def _pass_start(out, sems):
    for cp in _pass_copies(out, sems, lambda ci: ci):
        cp.start()


def _pass_finish(out, sems):
    for cp in _pass_copies(out, sems, lambda ci: 1 - ci):
        cp.wait_recv()
    _drain(_pass_copies(out, sems, lambda ci: ci))


def gather_side(stacked):
    n = len(stacked)
    return Side(stacked, [jax.ShapeDtypeStruct(a.shape, a.dtype) for a in stacked], {w: w for w in range(n)},
                _gather_sems(n), lambda ins, outs, sems: _gather_start(outs, sems),
                lambda ins, outs, sems: _gather_wait_ici(outs, sems))


def gather_whole_side(stacked):
    n = len(stacked)
    return Side(stacked, [jax.ShapeDtypeStruct(a.shape, a.dtype) for a in stacked], {w: w for w in range(n)},
                _gather_sems(n), lambda ins, outs, sems: _gather_start(outs, sems),
                lambda ins, outs, sems: _gather_finish(outs, sems))


def pass_side(stacked):
    n = len(stacked)
    return Side(stacked, [jax.ShapeDtypeStruct(a.shape, a.dtype) for a in stacked], {w: w for w in range(n)},
                [pltpu.SemaphoreType.DMA((n, 3)), pltpu.SemaphoreType.DMA((n, 3))],
                lambda ins, outs, sems: _pass_start(outs, sems), lambda ins, outs, sems: _pass_finish(outs, sems))


def _rs1_copies(g, out, sems):
    send_sems, recv_sems = sems
    xi, yi, ci = _place()
    return [_rcopy(g[w].at[:, 1 - ci], out[w], send_sems.at[w], recv_sems.at[w], (xi, yi, 1 - ci))
            for w in range(len(g))]


def _rs1_start(g, out, sems):
    for cp in _rs1_copies(g, out, sems):
        cp.start()


def _rs1_finish(g, out, sems):
    for cp in _rs1_copies(g, out, sems):
        cp.wait_recv()
    _drain(_rs1_copies(g, out, sems))


def rs1_side(grads):
    n = len(grads)
    return Side(grads, [jax.ShapeDtypeStruct((N_CHIPS,) + a.shape[2:], a.dtype) for a in grads], {},
                [pltpu.SemaphoreType.DMA((n,)), pltpu.SemaphoreType.DMA((n,))], _rs1_start, _rs1_finish)


def join_sides(a, b):
    na, nao, nas = len(a.ins), len(a.out_shapes), len(a.sems)
    aliases = dict(a.aliases)
    aliases.update({na + i: nao + o for i, o in b.aliases.items()})

    def start(ins, outs, sems):
        a.start(ins[:na], outs[:nao], sems[:nas])
        b.start(ins[na:], outs[nao:], sems[nas:])

    def finish(ins, outs, sems):
        a.finish(ins[:na], outs[:nao], sems[:nas])
        b.finish(ins[na:], outs[nao:], sems[nas:])

    return Side(list(a.ins) + list(b.ins), list(a.out_shapes) + list(b.out_shapes), aliases,
                list(a.sems) + list(b.sems), start, finish)


def rs_pair(name, grads):
    n = len(grads)
    side = rs1_side(grads)

    def body(*refs):
        side.start(refs[:n], refs[n:2 * n], refs[2 * n:])
        side.finish(refs[:n], refs[n:2 * n], refs[2 * n:])

    return pl.pallas_call(
        body, name=name, in_specs=[ANY] * n, out_specs=[ANY] * n, out_shape=side.out_shapes,
        scratch_shapes=side.sems, compiler_params=_cparams(),
    )(*grads)


def _rows_tile(rows, cols, unit=16, limit=1 << 20):
    best = None
    for t in range(unit, rows + 1, unit):
        if rows % t == 0 and t * cols * 4 <= limit:
            best = t
    return best or rows


def rs_add_pair(name, g, got, ci):
    _, _, hr, c = g.shape
    th = _rows_tile(hr, c, limit=1 << 21)

    def body(ci_ref, g_ref, r_ref, o32_ref, ob_ref):
        tot = g_ref[...] + r_ref[...]
        o32_ref[...] = tot
        ob_ref[...] = tot.astype(BF16)

    blk = pl.BlockSpec((None, th, c), lambda s, i, ci_ref: (s, i, 0))
    return pl.pallas_call(
        body, name=name,
        grid_spec=pltpu.PrefetchScalarGridSpec(
            num_scalar_prefetch=1, grid=(N_CHIPS, hr // th),
            in_specs=[pl.BlockSpec((None, None, th, c), lambda s, i, ci_ref: (s, ci_ref[0], i, 0)), blk],
            out_specs=[blk, blk]),
        out_shape=[jax.ShapeDtypeStruct((N_CHIPS, hr, c), F32), jax.ShapeDtypeStruct((N_CHIPS, hr, c), BF16)],
        compiler_params=_cparams(dimension_semantics=("parallel", "parallel")),
    )(ci, g, got)


def rs_chips(name, sums_bf16):
    n = len(sums_bf16)
    side = rs2_side(sums_bf16)

    def body(*refs):
        side.start(refs[:n], refs[n:4 * n], refs[4 * n:])
        side.finish(refs[:n], refs[n:4 * n], refs[4 * n:])

    res = pl.pallas_call(
        body, name=name, in_specs=[ANY] * n, out_specs=[ANY] * (3 * n), out_shape=side.out_shapes,
        scratch_shapes=side.sems, compiler_params=_cparams(),
    )(*sums_bf16)
    return [res[3 * w:3 * w + 3] for w in range(n)]


def _rs2_copies(pb, outs, sems):
    send_sems, recv_sems = sems
    xi, yi, ci = _place()
    chips, sidx = _chips(xi, yi)
    return [_rcopy(pb[w].at[sidx[j]], outs[3 * w + j], send_sems.at[w, j], recv_sems.at[w, j], (cx, cy, ci))
            for w in range(len(pb)) for j, (cx, cy) in enumerate(chips)]


def _rs2_start(pb, outs, sems):
    for cp in _rs2_copies(pb, outs, sems):
        cp.start()


def _rs2_finish(pb, outs, sems):
    for cp in _rs2_copies(pb, outs, sems):
        cp.wait_recv()
    _drain(_rs2_copies(pb, outs, sems))


def rs2_side(sums_bf16):
    n = len(sums_bf16)
    out_shapes = [jax.ShapeDtypeStruct(a.shape[1:], BF16) for a in sums_bf16 for _ in range(3)]
    return Side(sums_bf16, out_shapes, {}, [pltpu.SemaphoreType.DMA((n, 3)), pltpu.SemaphoreType.DMA((n, 3))],
                _rs2_start, _rs2_finish)


def rs_swap(name, halves):
    n = len(halves)

    def body(*refs):
        hv, out = refs[:n], refs[n:2 * n]
        send_sems, recv_sems = refs[2 * n:]
        xi, yi, ci = _place()
        sib = (xi, yi, 1 - ci)
        cps = [_rcopy(hv[w], out[w], send_sems.at[w], recv_sems.at[w], sib) for w in range(n)]
        for cp in cps:
            cp.start()
        for cp in cps:
            cp.wait_recv()
        _drain(cps)

    return pl.pallas_call(
        body, name=name, in_specs=[ANY] * n, out_specs=[ANY] * n,
        out_shape=[jax.ShapeDtypeStruct(a.shape, a.dtype) for a in halves],
        scratch_shapes=[pltpu.SemaphoreType.DMA((n,)), pltpu.SemaphoreType.DMA((n,))],
        compiler_params=_cparams(),
    )(*halves)


def cast_to_slot(name, x, slot):
    r, c = x.shape
    tr = _rows_tile(r, c, limit=1 << 22)

    def body(slot_ref, x_ref, o_ref):
        o_ref[...] = x_ref[...].astype(BF16)

    return pl.pallas_call(
        body, name=name,
        grid_spec=pltpu.PrefetchScalarGridSpec(
            num_scalar_prefetch=1, grid=(r // tr,),
            in_specs=[pl.BlockSpec((tr, c), lambda i, slot_ref: (i, 0))],
            out_specs=pl.BlockSpec((None, tr, c), lambda i, slot_ref: (slot_ref[0], i, 0))),
        out_shape=jax.ShapeDtypeStruct((N_CHIPS, r, c), BF16),
        compiler_params=_cparams(dimension_semantics=("parallel",)),
    )(slot, x)


def sum_blocks(name, x, picks):
    _, r, c = x.shape

    def body(x_ref, o_ref):
        acc = x_ref[picks[0]]
        for b in picks[1:]:
            acc = acc + x_ref[b]
        o_ref[...] = acc

    vm = pl.BlockSpec(memory_space=pltpu.VMEM)
    return pl.pallas_call(body, name=name, in_specs=[vm], out_specs=vm, out_shape=jax.ShapeDtypeStruct((r, c), F32),
                          compiler_params=_cparams())(x)


def adamw(name, w, g, m, v):
    r, c = w.shape
    tr = _rows_tile(r, c, unit=8, limit=1 << 20)

    def body(w_ref, g_ref, m_ref, v_ref, d_ref, mo_ref, vo_ref):
        d_ref[...], mo_ref[...], vo_ref[...] = _adam_update(w_ref[...], g_ref[...], m_ref[...], v_ref[...])

    blk = pl.BlockSpec((tr, c), lambda i: (i, 0))
    return pl.pallas_call(
        body, name=name, grid=(r // tr,), in_specs=[blk] * 4, out_specs=[blk] * 3,
        out_shape=[jax.ShapeDtypeStruct((r, c), F32)] * 3,
        compiler_params=_cparams(dimension_semantics=("parallel",)),
    )(w, g, m, v)


def _adam_update(w, gv, m, v):
    c1 = 1.0 / (1.0 - ADAM_B1 ** ADAM_STEP)
    c2 = 1.0 / (1.0 - ADAM_B2 ** ADAM_STEP)
    mn = ADAM_B1 * m + (1.0 - ADAM_B1) * gv
    vn = ADAM_B2 * v + (1.0 - ADAM_B2) * (gv * gv)
    return -ADAM_LR * ((mn * c1) / (jnp.sqrt(vn * c2) + ADAM_EPS) + ADAM_WD * w), mn, vn


def adamw_halves(name, w, m, v, mine, theirs, ci, side=None):
    hr, c = mine[0].shape
    npos = len(mine)
    th = _rows_tile(hr, c, unit=8, limit=1 << 20)
    per = hr // th

    def work(ci_ref, w_ref, m_ref, v_ref, *rest):
        g_refs, (go_ref, d_ref, mo_ref, vo_ref) = rest[:2 * npos], rest[2 * npos:]
        pos, half = pl.program_id(0), pl.program_id(1)
        from_me = half == ci_ref[0]
        gv = jnp.where(from_me, g_refs[0][...], g_refs[npos][...])
        for p in range(1, npos):
            gv = jnp.where(pos == p, jnp.where(from_me, g_refs[p][...], g_refs[npos + p][...]), gv)
        d_ref[...], mo_ref[...], vo_ref[...] = _adam_update(w_ref[...], gv, m_ref[...], v_ref[...])
        go_ref[...] = gv

    full = pl.BlockSpec((th, c), lambda p, h, i, ci_ref: ((p * 2 + h) * per + i, 0))
    part = pl.BlockSpec((th, c), lambda p, h, i, ci_ref: (i, 0))
    rows = 2 * hr * npos
    n_in = 4 + 2 * npos
    s_in, s_out, s_shape, s_scratch, _, s_ops = _side_call_args(side, n_in, 4)
    res = pl.pallas_call(
        _with_side(work, side, n_in, 4, 0, (npos, 2, per)), name=name,
        grid_spec=pltpu.PrefetchScalarGridSpec(
            num_scalar_prefetch=1, grid=(npos, 2, per), in_specs=[full] * 3 + [part] * (2 * npos) + s_in,
            out_specs=[full] * 4 + s_out, scratch_shapes=s_scratch),
        out_shape=[jax.ShapeDtypeStruct((rows, c), F32)] * 4 + s_shape,
        compiler_params=_cparams(dimension_semantics=("arbitrary", "arbitrary", "arbitrary")),
    )(ci, w, m, v, *mine, *theirs, *s_ops)
    return res[0], res[1], res[2], res[3], list(res[4:])


def pack_rows(name, parts, rows):
    width = parts[0].shape[1]
    n = len(parts)

    def body(*refs):
        o_ref = refs[n]
        o_ref[...] = jnp.zeros_like(o_ref)
        off = 0
        for r in refs[:n]:
            o_ref[off:off + r.shape[0], :] = r[...]
            off += r.shape[0]

    vm = pl.BlockSpec(memory_space=pltpu.VMEM)
    return pl.pallas_call(body, name=name, in_specs=[vm] * n, out_specs=vm,
                          out_shape=jax.ShapeDtypeStruct((rows, width), F32), compiler_params=_cparams())(*parts)


def _pad_rows(a, rows):
    return jnp.pad(a, ((0, rows - a.shape[0]), (0, 0)))


def _view2d(name, a):
    if name == 'rw_rk' or a.ndim == 1:
        return a.reshape(1, -1)
    return a.reshape(-1, a.shape[-1])


def _reduce_scatter(tag, items, ci_arr, s):
    sums = _rs_pair_sums(tag, items, ci_arr)
    landed = rs_chips("rs2_" + tag, [s_[1] for s_ in sums])
    return _rs_finish(tag, items, sums, landed, s)


def _rs_pair_sums(tag, items, ci_arr, got=None):
    g4 = [g.reshape(N_CHIPS, 2, g.shape[1] // 2, g.shape[2]) for _, g in items]
    if got is None:
        got = rs_pair("rs1_" + tag, g4)
    return [rs_add_pair(f"rs1add_{tag}{w}", g4[w], got[w], ci_arr) for w in range(len(items))]


def _rs_finish(tag, items, sums, landed, s):
    names = []
    for nm, _ in items:
        if nm not in names:
            names.append(nm)
    halves = []
    for w, (r0, r1, r2) in enumerate(landed):
        own = lax.dynamic_index_in_dim(sums[w][0], s, axis=0, keepdims=False)
        hr, c = own.shape
        halves.append(row_call(f"rs2add_{tag}{w}", f_sum, [own, r0, r1, r2], [], [(c, F32)],
                               tr=_rows_tile(hr, c), sr=16)[0])
    theirs = rs_swap("rs3_" + tag, halves)
    return {name: ([halves[w] for w, (nm, _) in enumerate(items) if nm == name],
                   [theirs[w] for w, (nm, _) in enumerate(items) if nm == name]) for name in names}


def _step(p):
    xi, yi, ci = _place()
    me = 4 * xi + 2 * yi + ci
    s = 2 * xi + yi
    ci_arr = jnp.reshape(ci, (1,)).astype(jnp.int32)
    s_arr = jnp.reshape(s, (1,)).astype(jnp.int32)
    x, ctx, tgt = p['x'][0], p['ctx'][0], p['loss_target'][0]
    T, D = x.shape
    L = ctx.shape[0]
    H, Dq = D // HEAD, D // N_CHIPS
    TR = math.gcd(math.gcd(L, T), 256)
    TS = min(TR, 64)
    nct = L // TR
    LG = p['rw_g1'].shape[-1]
    LW, LA = p['rw_w1'].shape[-1], p['rw_a1'].shape[-1]
    F4 = p['ffn_w2'].shape[1]

    pack = jnp.concatenate([
        _pad_rows(p['c'].reshape(N_CHIPS, Dq), 8), _pad_rows(p['rw_mix'][0], 8), _pad_rows(p['rw_w0'][0], 8),
        _pad_rows(p['rw_a0'][0], 8), _pad_rows(p['sc_conv'][0], 8)], axis=0)
    got = all_gather8("ag_small", pack)
    c_all = got[:, 0:N_CHIPS, :].reshape(N_DEV, D)
    full = jnp.transpose(got[::2], (1, 0, 2)).reshape(40, D)
    mix_f, w0_f, a0_f, conv_f = full[8:16], full[16:24], full[24:32], full[32:40]

    cond_in = jnp.concatenate([c_all, _pad_rows(p['c_ctx'].reshape(1, D), 8)], axis=0)
    cond = row_call("cond", f_silu, [cond_in], [], [(D, F32)], tr=16, sr=16)[0]
    ada = Stacked(p['ada_w'], "layer", D)
    modp = [mm_nn(f"modp{i}", cond, ada.at(i)) for i in range(2)]
    mg = all_gather8("ag_mod", jnp.concatenate(modp, axis=0))
    mod = jnp.transpose(mg[::2].reshape(N_CHIPS, 2, 16, 6 * Dq), (1, 2, 0, 3)).reshape(2, 16, 6 * D)
    mod = mod + p['ada_b'][:, None, :]
    mod_x = lax.dynamic_index_in_dim(mod, me, axis=1, keepdims=False)
    mod_c = mod[:, 8]

    def chunk(vec, j):
        return vec[j * D:(j + 1) * D].reshape(1, D)

    sh1x, sc1x, gt1x, sh2x, sc2x, gt2x = ([chunk(mod_x[i], j) for i in range(2)] for j in range(6))
    sh1c, sc1c = chunk(mod_c[0], 0), chunk(mod_c[0], 1)

    def slots(names):
        return [cast_to_slot("cast_" + n, _view2d(n, p[n]), s_arr) for n in names]

    rw_names = ('rw_wr', 'rw_wk', 'rw_wv', 'rw_wo', 'rw_w1', 'rw_w2', 'rw_a1', 'rw_a2', 'rw_g1', 'rw_g2')
    late_names = ('sc_win', 'sc_wout', 'ffn_w13', 'ffn_w2')
    gw = dict(zip(rw_names, gather_weights("ag_rw", slots(rw_names))))
    late_slots = slots(late_names)
    Wr, Wk, Wv, Wo = (Stacked(gw[n], "row", Dq) for n in ('rw_wr', 'rw_wk', 'rw_wv', 'rw_wo'))
    W1, A1, G1 = (Stacked(gw[n], "row", Dq) for n in ('rw_w1', 'rw_a1', 'rw_g1'))
    W2, A2, G2 = Stacked(gw['rw_w2'], "col", LW), Stacked(gw['rw_a2'], "col", LA), Stacked(gw['rw_g2'], "col", LG)

    ones2 = (((lax.broadcasted_iota(jnp.int32, (2 * LANES, LANES), 0) & (LANES - 1)) >= HEAD)
             == (lax.broadcasted_iota(jnp.int32, (2 * LANES, LANES), 1) >= HEAD)).astype(BF16)
    n1g, n2g = p['norm1_g'], p['norm2_g']
    kkp, ka, lnw, lnb = p['rw_kk'], p['rw_ka'], p['rw_lnw'], p['rw_lnb']
    rk = p['rw_rk'].reshape(1, D)
    fg = p['final_g'].reshape(1, D)

    xin = jnp.concatenate([ctx, x], axis=0)
    nm = functools.partial(f_norm_mod, nct)
    nm_consts = [n1g[0:1], sh1c, sc1c, sh1x[0], sc1x[0]]
    h = row_call("l0_norm", nm, [xin], nm_consts, [(D, F32)], tr=TR, sr=16)[0]
    xr, xw, xk, xv, xa, xg = mix_fwd("l0_mix", h, mix_f, L)
    r = mm_nn("l0_r", xr, Wr)
    k = mm_nn("l0_k", xk, Wk)
    v = mm_nn("l0_v", xv, Wv)
    gl = mm_nn("l0_gl", xg, G1)
    sg = row_call("l0_sg", f_sigmoid, [gl], [], [(LG, BF16)], tr=TR, sr=16)[0]
    g = mm_nn("l0_g", sg, G2)
    wl, tw, zw, al, za = [], [], [], [], []
    for d in range(2):
        wl.append(mm_nn(f"l0_wl{d}", xw, W1.at(d)))
        tw.append(row_call(f"l0_tw{d}", f_tanh, [wl[d]], [], [(LW, BF16)], tr=TR, sr=16)[0])
        zw.append(mm_nn(f"l0_zw{d}", tw[d], W2.at(d)))
        al.append(mm_nn(f"l0_al{d}", xa, A1.at(d), BF16))
        za.append(mm_nn(f"l0_za{d}", al[d], A2.at(d)))
    post_rows = [k, zw[0], zw[1], za[0], za[1]]
    post_consts = [kkp, ka, w0_f[0:1], w0_f[1:2], a0_f[0:1], a0_f[1:2], ones2]
    aa, dec0, dec1, kd0, kd1, bb0, bb1, ksum = row_call(
        "l0_post", f_post_fwd, post_rows, post_consts, [(D, F32)] * 8, tr=TS, sr=16)
    dec, kd, bb = (dec0, dec1), (kd0, kd1), (bb0, bb1)
    ys, sts = [], []
    y_d, st_d, first = scan_fwd("l0_scan0", r, dec[0], kd[0], v, aa, bb[0], L, False, side=gather_side(late_slots[:3]))
    ys.append(y_d)
    sts.append(st_d)
    y_d, st_d, filled = scan_fwd("l0_scan1", r, dec[1], kd[1], v, aa, bb[1], L, True,
                                 side=join_sides(pass_side(first), gather_whole_side(late_slots[3:])))
    ys.append(y_d)
    sts.append(st_d)
    gw.update(zip(late_names, filled))
    Win, Wout = Stacked(gw['sc_win'], "col", D), Stacked(gw['sc_wout'], "row", Dq)
    W13, W2f = Stacked(gw['ffn_w13'], "col", D), Stacked(gw['ffn_w2'], "row", F4)
    ro_rows = [ys[0], ys[1], r, ksum, v, g]
    ro_consts = [rk, lnw, lnb, ones2]
    og = row_call("l0_readout", f_readout, ro_rows, ro_consts, [(D, BF16)], tr=TS, sr=16)[0]
    yx = mm_nn("l0_o", og, Wo)
    res0_consts = [gt1x[0], n2g[0:1], sh2x[0], sc2x[0]]
    x1, h2 = row_call("l0_res", f_res_norm_mod, [x, yx], res0_consts, [(D, F32), (D, BF16)], tr=TR, sr=16,
                      offs=[0, nct])
    ab0 = mm_nn("l0_ffn13", h2, W13.at(0))
    sw0 = swiglu_fwd("l0_swiglu", ab0, TS)
    f0 = mm_nn("l0_ffn2", sw0, W2f.at(0))

    res1_consts = [gt2x[0], n1g[1:2], sh1x[1], sc1x[1]]
    x2, hb = row_call("l1_norm", f_res_norm_mod, [x1, f0], res1_consts, [(D, F32), (D, BF16)], tr=TR, sr=16)
    gcu = mm_nn("l1_win", hb, Win)
    pc = conv_fwd("l1_conv", gcu, conv_f)
    yx1 = mm_nn("l1_wout", pc, Wout)
    res2_consts = [gt1x[1], n2g[1:2], sh2x[1], sc2x[1]]
    x3, h2b = row_call("l1_res", f_res_norm_mod, [x2, yx1], res2_consts, [(D, F32), (D, BF16)], tr=TR, sr=16)
    ab1 = mm_nn("l1_ffn13", h2b, W13.at(1))
    sw1 = swiglu_fwd("l1_swiglu", ab1, TS)
    f1 = mm_nn("l1_ffn2", sw1, W2f.at(1))
    dx3, df1, dgt2_1, dfg, loss_blk = final_call("final", x3, f1, gt2x[1], fg, tgt, TR)
    loss = lax.psum(loss_blk[0, 0], ("x", "y", "c"))

    big = []
    dsw1 = mm_nt("b1_dsw", df1, W2f.at(1))
    gW2f1 = mm_tn("b1_gw2", sw1, df1, "row")
    dab1 = swiglu_bwd("b1_swiglu", ab1, dsw1, TS)
    dh2b = mm_nt("b1_dh2", dab1, W13.at(1))
    gW13_1 = mm_tn("b1_gw13", h2b, dab1, "col")
    rm = [True, True]
    cm = [True] * 4
    (dx2, dyx1), (dgt1_1, dn2g1, dsh2_1, dsc2_1) = row_vjp(
        "b1_res", f_res_norm_mod, [x2, yx1], res2_consts, [dx3, dh2b], row_mask=rm, const_mask=cm, tr=TR, sr=16,
        bf16_rows=(1,))
    dpc = mm_nt("b1_dpc", dyx1, Wout)
    big.append(('sc_wout', mm_tn("b1_gwout", pc, dyx1, "row")))
    dgcu, dconv = conv_bwd("b1_conv", gcu, conv_f, dpc)
    dhb = mm_nt("b1_dhb", dgcu, Win)
    big.append(('sc_win', mm_tn("b1_gwin", hb, dgcu, "col")))
    (dx1, df0), (dgt2_0, dn1g1, dsh1_1, dsc1_1) = row_vjp(
        "b1_norm", f_res_norm_mod, [x1, f0], res1_consts, [dx2, dhb], row_mask=rm, const_mask=cm, tr=TR, sr=16,
        bf16_rows=(1,))

    dsw0 = mm_nt("b0_dsw", df0, W2f.at(0))
    gW2f0 = mm_tn("b0_gw2", sw0, df0, "row")
    dab0 = swiglu_bwd("b0_swiglu", ab0, dsw0, TS)
    def halves4(g):
        return g.reshape(N_CHIPS, 2, g.shape[1] // 2, g.shape[2])

    early_items = [it for it in big if it[0] in ('sc_win', 'sc_wout')] + [('ffn_w13', gW13_1), ('ffn_w2', gW2f1)]
    dh2, early_got = mm_nt("b0_dh2", dab0, W13.at(0), side=rs1_side([halves4(g) for _, g in early_items]))
    gW13_0 = mm_tn("b0_gw13", h2, dab0, "col")
    (dx_a, dyx), (dgt1_0, dn2g0, dsh2_0, dsc2_0) = row_vjp(
        "b0_res", f_res_norm_mod, [x, yx], res0_consts, [dx1, dh2], row_mask=rm, const_mask=cm, tr=TR, sr=16,
        offs=[0, nct], bf16_rows=(1,))
    dyx_all = jnp.concatenate([jnp.zeros((L, D), BF16), dyx], axis=0)
    dog = mm_nt("b0_dog", dyx_all, Wo)
    gWo = mm_tn("b0_gwo", og, dyx_all, "row")
    ffn0_items = [('ffn_w13', gW13_0), ('ffn_w2', gW2f0)]
    (dy, dr_ro, dksum, dv_ro, dg), (drk, dlnw, dlnb), ffn0_got = row_vjp(
        "b0_readout", f_readout, ro_rows, ro_consts, [dog], row_mask=[True, False, True, True, True, True],
        const_mask=[True, True, True, False], tr=TS, sr=16, bf16_rows=(5,),
        side=rs1_side([halves4(g) for _, g in ffn0_items]))
    late_items = early_items[:2] + [ffn0_items[0], early_items[2], ffn0_items[1], early_items[3]]
    late_got = early_got[:2] + [ffn0_got[0], early_got[2], ffn0_got[1], early_got[3]]
    late_sums = _rs_pair_sums("late", late_items, ci_arr, late_got)
    (dr0, ddec0, dkd0, dv0, daa0, dbb0), landed_flat = scan_bwd(
        "b0_scan0", r, dec[0], kd[0], v, aa, bb[0], dy, sts[0], L, False, side=rs2_side([s_[1] for s_ in late_sums]))
    (dr1, ddec1, dkd1, dv1, daa1, dbb1), _ = scan_bwd("b0_scan1", r, dec[1], kd[1], v, aa, bb[1], dy, sts[1], L, True)
    late_landed = [landed_flat[3 * w:3 * w + 3] for w in range(len(late_items))]
    post_cots = [daa0, daa1, ddec0, ddec1, dkd0, dkd1, dbb0, dbb1, dksum]
    (dk, dzw0, dzw1, dza0, dza1), (dkkp, dka, dw00, dw01, da00, da01) = row_vjp(
        "b0_post", f_post, post_rows, post_consts, post_cots, row_mask=[True] * 5,
        const_mask=[True] * 6 + [False], tr=TS, sr=16, bf16_rows=(0, 1, 2, 3, 4))
    dzw, dza = (dzw0, dzw1), (dza0, dza1)
    dr_t = sum_cast("b0_drsum", [dr0, dr1, dr_ro], BF16, TR)
    dv_t = sum_cast("b0_dvsum", [dv0, dv1, dv_ro], BF16, TR)
    mix_cots, mix_slots = [], []

    def back(tag, cot, w, xin_m, kind, slot):
        mix_cots.append(mm_nt("b0_dx" + tag, cot, w))
        mix_slots.append(slot)
        return mm_tn("b0_gw" + tag, xin_m, cot, kind)

    big.append(('rw_wr', back("r", dr_t, Wr, xr, "row", 0)))
    big.append(('rw_wk', back("k", dk, Wk, xk, "row", 2)))
    big.append(('rw_wv', back("v", dv_t, Wv, xv, "row", 3)))
    big.append(('rw_wo', gWo))
    dsg = mm_nt("b0_dsg", dg, G2)
    gG2 = mm_tn("b0_gg2", sg, dg, "col")
    (dgl,), _ = row_vjp("b0_sg", f_sigmoid, [gl], [], [dsg], row_mask=[True], const_mask=[], tr=TR, sr=16,
                        bf16_rows=(0,))
    gG1 = back("g", dgl, G1, xg, "row", 5)
    gW1, gW2, gA1, gA2 = [], [], [], []
    for d in range(2):
        dtw = mm_nt(f"b0_dtw{d}", dzw[d], W2.at(d))
        gW2.append(mm_tn(f"b0_gw2{d}", tw[d], dzw[d], "col"))
        (dwl,), _ = row_vjp(f"b0_tw{d}", f_tanh, [wl[d]], [], [dtw], row_mask=[True], const_mask=[], tr=TR, sr=16,
                            bf16_rows=(0,))
        gW1.append(back(f"w{d}", dwl, W1.at(d), xw, "row", 1))
        dal = mm_nt(f"b0_dal{d}", dza[d], A2.at(d), BF16)
        gA2.append(mm_tn(f"b0_ga2{d}", al[d], dza[d], "col"))
        gA1.append(back(f"a{d}", dal, A1.at(d), xa, "row", 4))
    big += [('rw_w1', gW1[0]), ('rw_w1', gW1[1]), ('rw_w2', gW2[0]), ('rw_w2', gW2[1]),
            ('rw_a1', gA1[0]), ('rw_a1', gA1[1]), ('rw_a2', gA2[0]), ('rw_a2', gA2[1]),
            ('rw_g1', gG1), ('rw_g2', gG2)]
    dh, dmix = mix_bwd("b0_mix", h, mix_f, mix_cots, mix_slots, L)
    (dxin,), (dn1g0, dsh1c, dsc1c, dsh1x, dsc1x) = row_vjp(
        "b0_norm", nm, [xin], nm_consts, [dh], row_mask=[True], const_mask=[True] * 5, tr=TR, sr=16)
    grad_x = sum_cast("b0_dx", [dxin, dx_a], F32, TR, offs=[nct, 0])

    zero = jnp.zeros((1, D), F32)
    parts = [dsh1x, dsc1x, dgt1_0, dsh2_0, dsc2_0, dgt2_0, dsh1c, dsc1c, zero, zero, zero, zero,
             dsh1_1, dsc1_1, dgt1_1, dsh2_1, dsc2_1, dgt2_1, zero, zero, zero, zero, zero, zero,
             dn1g0, dn1g1, dn2g0, dn2g1, dkkp, dka, drk, dlnw, dlnb, dfg,
             dmix[0:6], dw00, dw01, da00, da01, dconv[0:3]]
    got2 = all_gather8("ag_grads", pack_rows("pack_grads", parts, 48))
    small = sum_blocks("sum_grads", got2, list(range(N_DEV)))
    per_ex = got2[:, 0:24].reshape(N_DEV, 2, 2, 6 * D)
    tot = small[0:24].reshape(2, 2, 6 * D)
    cols = lambda a, width: lax.dynamic_slice_in_dim(a, s * width, width, axis=1)
    g_ada_w, dcond_parts = [], []
    for i in range(2):
        dm16 = cols(jnp.concatenate([per_ex[:, i, 0], _pad_rows(tot[i, 1][None], 8)], axis=0), 6 * Dq)
        g_ada_w.append(mm_tn(f"g_ada{i}", cond, dm16))
        dcond_parts.append(mm_nt(f"dcond{i}", dm16, ada.at(i)))
    g_ada_b = sum_cast("g_adab", [_pad_rows(tot[:, 0].reshape(12, D), 16), _pad_rows(tot[:, 1].reshape(12, D), 16)],
                       F32, 16)[0:12].reshape(2, 6 * D)
    dcond_mine = sum_cast("dcond_sum", dcond_parts, F32, 16)
    dcond = sum_blocks("dcond_chips", all_gather8("ag_dcond", dcond_mine), [0, 2, 4, 6])
    (dcin,), _ = row_vjp("b_cond", f_silu, [cond_in], [], [dcond], row_mask=[True], const_mask=[], tr=16, sr=16)

    gsh = _rs_finish("late", late_items, late_sums, late_landed, s)
    rw_items = [it for it in big if it[0] in rw_names]
    rw_sums = _rs_pair_sums("rw", rw_items, ci_arr)

    def view(n):
        return _view2d(n, p[n]), _view2d(n, p['m_' + n]), _view2d(n, p['v_' + n])

    done = {}
    *done['ffn_w13'], rw_flat = adamw_halves("adam_ffn_w13", *view('ffn_w13'), gsh['ffn_w13'][0], gsh['ffn_w13'][1],
                                             ci_arr, side=rs2_side([s_[1] for s_ in rw_sums]))
    gsh.update(_rs_finish("rw", rw_items, rw_sums, [rw_flat[3 * w:3 * w + 3] for w in range(len(rw_items))], s))

    grads = {}
    grads['c_ctx'] = dcin[8]
    grads['norm1_g'], grads['norm2_g'] = small[24:26], small[26:28]
    grads['ada_w'] = jnp.stack(g_ada_w)
    grads['ada_b'] = g_ada_b
    grads['rw_kk'], grads['rw_ka'], grads['rw_rk'] = small[28:29], small[29:30], small[30:31]
    grads['rw_lnw'], grads['rw_lnb'], grads['final_g'] = small[31:32], small[32:33], small[33]
    sharded = cols(small[34:48], Dq)
    grads['rw_mix'], grads['rw_w0'], grads['rw_a0'], grads['sc_conv'] = (
        sharded[0:6], sharded[6:8], sharded[8:10], sharded[10:13])

    outs_g, outs_d, outs_m, outs_v = [], [], [], []
    for n in WEIGHTS:
        shape = p[n].shape
        w2d, m2d, v2d = view(n)
        if n in done:
            g2d, d_, m_, v_ = done[n]
        elif n in gsh:
            g2d, d_, m_, v_, _ = adamw_halves("adam_" + n, w2d, m2d, v2d, gsh[n][0], gsh[n][1], ci_arr)
        else:
            g2d = _view2d(n, grads[n].reshape(shape))
            d_, m_, v_ = adamw("adam_" + n, w2d, g2d, m2d, v2d)
        outs_g.append(g2d.reshape(shape))
        outs_d.append(d_.reshape(shape))
        outs_m.append(m_.reshape(shape))
        outs_v.append(v_.reshape(shape))
    return (loss, grad_x.reshape(1, T, D), *outs_g, *outs_d, *outs_m, *outs_v)


def kernel(x, c, ctx, c_ctx, norm1_g, norm2_g, ada_w, ada_b, rw_mix, rw_wr, rw_wk, rw_wv, rw_wo, rw_w0, rw_w1, rw_w2, rw_a0, rw_a1, rw_a2, rw_g1, rw_g2, rw_kk, rw_ka, rw_rk, rw_lnw, rw_lnb, sc_win, sc_conv, sc_wout, ffn_w13, ffn_w2, final_g, loss_target, m_c_ctx, m_norm1_g, m_norm2_g, m_ada_w, m_ada_b, m_rw_mix, m_rw_wr, m_rw_wk, m_rw_wv, m_rw_wo, m_rw_w0, m_rw_w1, m_rw_w2, m_rw_a0, m_rw_a1, m_rw_a2, m_rw_g1, m_rw_g2, m_rw_kk, m_rw_ka, m_rw_rk, m_rw_lnw, m_rw_lnb, m_sc_win, m_sc_conv, m_sc_wout, m_ffn_w13, m_ffn_w2, m_final_g, v_c_ctx, v_norm1_g, v_norm2_g, v_ada_w, v_ada_b, v_rw_mix, v_rw_wr, v_rw_wk, v_rw_wv, v_rw_wo, v_rw_w0, v_rw_w1, v_rw_w2, v_rw_a0, v_rw_a1, v_rw_a2, v_rw_g1, v_rw_g2, v_rw_kk, v_rw_ka, v_rw_rk, v_rw_lnw, v_rw_lnb, v_sc_win, v_sc_conv, v_sc_wout, v_ffn_w13, v_ffn_w2, v_final_g):
    values = (x, c, ctx, c_ctx, norm1_g, norm2_g, ada_w, ada_b, rw_mix, rw_wr, rw_wk, rw_wv, rw_wo, rw_w0, rw_w1, rw_w2, rw_a0, rw_a1, rw_a2, rw_g1, rw_g2, rw_kk, rw_ka, rw_rk, rw_lnw, rw_lnb, sc_win, sc_conv, sc_wout, ffn_w13, ffn_w2, final_g, loss_target, m_c_ctx, m_norm1_g, m_norm2_g, m_ada_w, m_ada_b, m_rw_mix, m_rw_wr, m_rw_wk, m_rw_wv, m_rw_wo, m_rw_w0, m_rw_w1, m_rw_w2, m_rw_a0, m_rw_a1, m_rw_a2, m_rw_g1, m_rw_g2, m_rw_kk, m_rw_ka, m_rw_rk, m_rw_lnw, m_rw_lnb, m_sc_win, m_sc_conv, m_sc_wout, m_ffn_w13, m_ffn_w2, m_final_g, v_c_ctx, v_norm1_g, v_norm2_g, v_ada_w, v_ada_b, v_rw_mix, v_rw_wr, v_rw_wk, v_rw_wv, v_rw_wo, v_rw_w0, v_rw_w1, v_rw_w2, v_rw_a0, v_rw_a1, v_rw_a2, v_rw_g1, v_rw_g2, v_rw_kk, v_rw_ka, v_rw_rk, v_rw_lnw, v_rw_lnb, v_sc_win, v_sc_conv, v_sc_wout, v_ffn_w13, v_ffn_w2, v_final_g)
    return _step(dict(zip(INPUTS, values)))
```

```python
import functools
import math

import jax
import jax.numpy as jnp
from jax import lax
from jax.experimental import pallas as pl
from jax.experimental.pallas import tpu as pltpu

F32 = jnp.float32
BF16 = jnp.bfloat16
MESH = pl.DeviceIdType.MESH

GRID_W = 64
HEAD = 64
LANES = 128
N_CHIPS = 4
N_DEV = 8
NORM_EPS = 1e-6
GN_EPS = 64e-5
ADAM_LR, ADAM_B1, ADAM_B2, ADAM_EPS, ADAM_WD, ADAM_STEP = 0.001, 0.9, 0.999, 1e-08, 0.01, 10
VMEM_LIMIT = 56 * 1024 * 1024
HI = lax.Precision.HIGHEST
WEIGHTS = ['c_ctx', 'norm1_g', 'norm2_g', 'ada_w', 'ada_b', 'rw_mix', 'rw_wr', 'rw_wk', 'rw_wv', 'rw_wo', 'rw_w0',
           'rw_w1', 'rw_w2', 'rw_a0', 'rw_a1', 'rw_a2', 'rw_g1', 'rw_g2', 'rw_kk', 'rw_ka', 'rw_rk', 'rw_lnw',
           'rw_lnb', 'sc_win', 'sc_conv', 'sc_wout', 'ffn_w13', 'ffn_w2', 'final_g']
INPUTS = (['x', 'c', 'ctx'] + WEIGHTS + ['loss_target'] + ['m_' + w for w in WEIGHTS]
          + ['v_' + w for w in WEIGHTS])


def _cparams(**kw):
    return pltpu.CompilerParams(vmem_limit_bytes=VMEM_LIMIT, **kw)


def _pick(dim, cands):
    for c in cands:
        if dim % c == 0:
            return c
    return dim


def _place():
    return lax.axis_index("x"), lax.axis_index("y"), lax.axis_index("c")


_TILE_M = (1024, 768, 512, 1408, 256, 128)
_TILE_N = (1408, 1024, 768, 512, 256, 128)
_TILE_K = (2816, 2304, 2048, 1536, 1408, 1152, 1024, 768, 704, 512, 256, 128)
MM_TILE_BYTES = 40 * 1024 * 1024


def _pick_k(unit, tm, tn, a_dtype, b_dtype, o_dtype):
    ab, bb, ob = (jnp.dtype(d).itemsize for d in (a_dtype, b_dtype, o_dtype))
    for tk in _TILE_K:
        if unit % tk == 0 and 2 * (tm * tk * ab + tk * tn * bb) + 2 * tm * tn * ob + tm * tn * 4 <= MM_TILE_BYTES:
            return tk
    return unit


class Stacked:
    def __init__(self, arr, kind, r, layer=0):
        self.arr, self.kind, self.r, self.layer = arr, kind, r, layer
        self.c = arr.shape[2]
        self.shape = {"row": (N_CHIPS * r, self.c), "col": (r, N_CHIPS * self.c), "layer": (r, self.c)}[kind]

    def at(self, layer):
        return Stacked(self.arr, self.kind, self.r, layer)

    def spec(self, t0, t1, swap):
        r, c, layer = self.r, self.c, self.layer
        per_r, per_c = r // t0, c // t1
        assert r % t0 == 0 and c % t1 == 0
        kind = self.kind

        def index(i, j, k):
            ri, ci = (j, k) if swap else (k, j)
            if kind == "row":
                return (ri // per_r, layer * per_r + ri % per_r, ci)
            if kind == "layer":
                return (layer, ri, ci)
            return (ci // per_c, layer * per_r + ri, ci % per_c)

        return pl.BlockSpec((None, t0, t1), index)


def _mm_body(dims, nk, a_ref, b_ref, o_ref, acc_ref):
    if nk == 1:
        o_ref[...] = lax.dot_general(a_ref[...].astype(BF16), b_ref[...].astype(BF16), (dims, ((), ())),
                                     preferred_element_type=F32).astype(o_ref.dtype)
        return
    k = pl.program_id(2)

    @pl.when(k == 0)
    def _():
        acc_ref[...] = jnp.zeros_like(acc_ref)

    acc_ref[...] += lax.dot_general(a_ref[...].astype(BF16), b_ref[...].astype(BF16), (dims, ((), ())),
                                    preferred_element_type=F32)

    @pl.when(k == nk - 1)
    def _():
        o_ref[...] = acc_ref[...].astype(o_ref.dtype)


def _mm_call(name, dims, grid, in_specs, out_spec, out_shape, acc_shape, operands, side=None):
    acc = pltpu.VMEM(acc_shape if grid[2] > 1 else (8, LANES), F32)
    if side is None:
        return pl.pallas_call(
            functools.partial(_mm_body, dims, grid[2]), name=name, grid=grid, in_specs=in_specs, out_specs=out_spec,
            out_shape=out_shape, scratch_shapes=[acc],
            compiler_params=_cparams(dimension_semantics=("parallel", "parallel", "arbitrary")),
        )(*operands)
    s_in, s_out, s_shape, s_scratch, s_alias, s_ops = _side_call_args(side, 2, 1)
    res = pl.pallas_call(
        _with_side(functools.partial(_mm_body, dims, grid[2]), side, 2, 1, 1, grid), name=name, grid=grid,
        in_specs=in_specs + s_in, out_specs=[out_spec] + s_out, out_shape=[out_shape] + s_shape,
        scratch_shapes=[acc] + s_scratch, input_output_aliases=s_alias,
        compiler_params=_cparams(dimension_semantics=("arbitrary", "arbitrary", "arbitrary")),
    )(*operands, *s_ops)
    return res[0], list(res[1:])


def mm_nn(name, a, b, out_dtype=F32):
    M, K = a.shape
    st = isinstance(b, Stacked)
    N = b.shape[1]
    tm = _pick(M, _TILE_M)
    tn = _pick(b.c if st and b.kind == "col" else N, _TILE_N)
    tk = _pick_k(b.r if st else K, tm, tn, a.dtype, b.arr.dtype if st else b.dtype, out_dtype)
    b_spec = b.spec(tk, tn, False) if st else pl.BlockSpec((tk, tn), lambda i, j, k: (k, j))
    return _mm_call(name, ((1,), (0,)), (M // tm, N // tn, K // tk),
                    [pl.BlockSpec((tm, tk), lambda i, j, k: (i, k)), b_spec],
                    pl.BlockSpec((tm, tn), lambda i, j, k: (i, j)), jax.ShapeDtypeStruct((M, N), out_dtype),
                    (tm, tn), (a, b.arr if st else b))


def mm_nt(name, a, b, out_dtype=F32, side=None):
    M, N = a.shape
    st = isinstance(b, Stacked)
    K = b.shape[0]
    tm = _pick(M, _TILE_M)
    to = _pick(b.r if st else K, _TILE_N)
    tc = _pick_k(b.c if st and b.kind == "col" else N, tm, to, a.dtype, b.arr.dtype if st else b.dtype, out_dtype)
    b_spec = b.spec(to, tc, True) if st else pl.BlockSpec((to, tc), lambda i, j, k: (j, k))
    return _mm_call(name, ((1,), (1,)), (M // tm, K // to, N // tc),
                    [pl.BlockSpec((tm, tc), lambda i, j, k: (i, k)), b_spec],
                    pl.BlockSpec((tm, to), lambda i, j, k: (i, j)), jax.ShapeDtypeStruct((M, K), out_dtype),
                    (tm, to), (a, b.arr if st else b), side)


def mm_tn(name, a, b, kind=None, side=None):
    R, M = a.shape
    N = b.shape[1]
    r, c = (M // N_CHIPS, N) if kind == "row" else (M, N // N_CHIPS) if kind == "col" else (M, N)
    tm, tn = _pick(r, _TILE_M), _pick(c, _TILE_N)
    tk = _pick_k(R, tm, tn, a.dtype, b.dtype, F32)
    if kind:
        per_r, per_c = r // tm, c // tn
        if kind == "row":
            o_spec = pl.BlockSpec((None, tm, tn), lambda i, j, k: (i // per_r, i % per_r, j))
        else:
            o_spec = pl.BlockSpec((None, tm, tn), lambda i, j, k: (j // per_c, i, j % per_c))
        o_shape = jax.ShapeDtypeStruct((N_CHIPS, r, c), F32)
    else:
        o_spec = pl.BlockSpec((tm, tn), lambda i, j, k: (i, j))
        o_shape = jax.ShapeDtypeStruct((M, N), F32)
    return _mm_call(name, ((0,), (0,)), (M // tm, N // tn, R // tk),
                    [pl.BlockSpec((tk, tm), lambda i, j, k: (k, i)), pl.BlockSpec((tk, tn), lambda i, j, k: (k, j))],
                    o_spec, o_shape, (tm, tn), (a, b), side)


def _shifted(o):
    return lambda i: (i + o, 0)


def row_call(name, f, rows, consts, outs, *, tr, sr, offs=None):
    offs = offs or [0] * len(rows)
    n_rows = min(r.shape[0] - o * tr for r, o in zip(rows, offs))
    nr, nc = len(rows), len(consts)

    def body(*refs):
        row_refs, const_refs, out_refs = refs[:nr], refs[nr:nr + nc], refs[nr + nc:]
        i = pl.program_id(0)
        cvals = [r[...] for r in const_refs]

        def step(j, carry):
            sl = pl.ds(pl.multiple_of(j * sr, sr), sr)
            res = f(i, *[r[sl, :] for r in row_refs], *cvals)
            for o, v in zip(out_refs, res):
                o[sl, :] = v.astype(o.dtype)
            return carry

        lax.fori_loop(0, tr // sr, step, 0)

    in_specs = [pl.BlockSpec((tr, r.shape[1]), _shifted(o)) for r, o in zip(rows, offs)]
    in_specs += [pl.BlockSpec(c.shape, lambda i: (0, 0)) for c in consts]
    return pl.pallas_call(
        body, name=name, grid=(n_rows // tr,), in_specs=in_specs,
        out_specs=[pl.BlockSpec((tr, w), lambda i: (i, 0)) for w, _ in outs],
        out_shape=[jax.ShapeDtypeStruct((n_rows, w), dt) for w, dt in outs],
        compiler_params=_cparams(dimension_semantics=("parallel",)),
    )(*rows, *consts)


def row_vjp(name, f, rows, consts, cots, *, row_mask, const_mask, tr, sr, offs=None, bf16_rows=(), side=None):
    offs = offs or [0] * len(rows)
    n_rows = min(r.shape[0] - o * tr for r, o in zip(rows, offs))
    nr, nc = len(rows), len(consts)
    cot_in = [c for c in cots if c is not None]
    nct = len(cot_in)
    d_rows = [i for i in range(nr) if row_mask[i]]
    d_consts = [i for i in range(nc) if const_mask[i]]

    def body(*refs):
        row_refs, const_refs = refs[:nr], refs[nr:nr + nc]
        cot_refs = refs[nr + nc:nr + nc + nct]
        drow_refs = refs[nr + nc + nct:nr + nc + nct + len(d_rows)]
        dconst_refs = refs[nr + nc + nct + len(d_rows):]
        i = pl.program_id(0)

        @pl.when(i == 0)
        def _():
            for r in dconst_refs:
                r[...] = jnp.zeros_like(r)

        cvals = [r[...] for r in const_refs]

        def step(j, carry):
            sl = pl.ds(pl.multiple_of(j * sr, sr), sr)
            rvals = [r[sl, :] for r in row_refs]

            def g(*diff):
                rv, cv = list(rvals), list(cvals)
                for idx, val in zip(d_rows, diff[:len(d_rows)]):
                    rv[idx] = val
                for idx, val in zip(d_consts, diff[len(d_rows):]):
                    cv[idx] = val
                return f(i, *rv, *cv)

            primals = [rvals[idx].astype(F32) for idx in d_rows] + [cvals[idx] for idx in d_consts]
            res, vjp = jax.vjp(g, *primals)
            it = iter(cot_refs)
            cts = tuple(jnp.zeros_like(o) if c is None else next(it)[sl, :].astype(o.dtype) for o, c in zip(res, cots))
            grads = vjp(cts)
            for r, val in zip(drow_refs, grads[:len(d_rows)]):
                r[sl, :] = val.astype(r.dtype)
            for r, val in zip(dconst_refs, grads[len(d_rows):]):
                r[...] += val
            return carry

        lax.fori_loop(0, tr // sr, step, 0)

    in_specs = [pl.BlockSpec((tr, r.shape[1]), _shifted(o)) for r, o in zip(rows, offs)]
    in_specs += [pl.BlockSpec(c.shape, lambda i: (0, 0)) for c in consts]
    in_specs += [pl.BlockSpec((tr, c.shape[1]), lambda i: (i, 0)) for c in cot_in]
    out_specs = [pl.BlockSpec((tr, rows[i].shape[1]), lambda i: (i, 0)) for i in d_rows]
    out_specs += [pl.BlockSpec(consts[i].shape, lambda i: (0, 0)) for i in d_consts]
    out_shape = [jax.ShapeDtypeStruct((n_rows, rows[i].shape[1]), BF16 if i in bf16_rows else F32) for i in d_rows]
    out_shape += [jax.ShapeDtypeStruct(consts[i].shape, F32) for i in d_consts]
    n_in, n_out = nr + nc + nct, len(d_rows) + len(d_consts)
    s_in, s_out, s_shape, s_scratch, s_alias, s_ops = _side_call_args(side, n_in, n_out)
    res = pl.pallas_call(
        _with_side(body, side, n_in, n_out, 0, n_rows // tr), name=name, grid=(n_rows // tr,),
        in_specs=in_specs + s_in, out_specs=out_specs + s_out, out_shape=out_shape + s_shape,
        scratch_shapes=s_scratch, input_output_aliases=s_alias,
        compiler_params=_cparams(dimension_semantics=("arbitrary",)),
    )(*rows, *consts, *cot_in, *s_ops)
    if side is None:
        return list(res[:len(d_rows)]), list(res[len(d_rows):])
    return list(res[:len(d_rows)]), list(res[len(d_rows):n_out]), list(res[n_out:])


def _sigmoid(x):
    return 1.0 / (1.0 + jnp.exp(-x))


def _softplus(u):
    return jnp.maximum(u, 0.0) + jnp.log(1.0 + jnp.exp(-jnp.abs(u)))


def _rms(x, g):
    ms = jnp.sum(x * x, axis=-1, keepdims=True) * (1.0 / x.shape[-1])
    return x * lax.rsqrt(ms + NORM_EPS) * g


def _hsum_impl(x, ones2):
    rows, width = x.shape
    nch = width // LANES
    xs = jnp.concatenate([x[:, j * LANES:(j + 1) * LANES] for j in range(nch)], axis=0)
    hi = xs.astype(BF16)
    lo = (xs - hi.astype(F32)).astype(BF16)
    ys = jnp.dot(jnp.concatenate([hi, lo], axis=1), ones2, preferred_element_type=F32)
    return jnp.concatenate([ys[j * rows:(j + 1) * rows] for j in range(nch)], axis=1)


@jax.custom_vjp
def _hsum(x, ones2):
    return _hsum_impl(x, ones2)


def _hsum_fwd(x, ones2):
    return _hsum_impl(x, ones2), ones2


def _hsum_bwd(ones2, g):
    return _hsum_impl(g, ones2), jnp.zeros_like(ones2)


_hsum.defvjp(_hsum_fwd, _hsum_bwd)


def f_silu(i, x):
    return (x * _sigmoid(x),)


def f_sigmoid(i, x):
    return (_sigmoid(x),)


def f_tanh(i, x):
    return (jnp.tanh(x),)


def f_norm_mod(n_ctx_tiles, i, xin, g, sh_c, sc_c, sh_x, sc_x):
    is_x = i >= n_ctx_tiles
    sh = jnp.where(is_x, sh_x, sh_c)
    sc = jnp.where(is_x, sc_x, sc_c)
    return (_rms(xin, g) * (1.0 + sc) + sh,)


def f_res_norm_mod(i, x, y, gt, g, sh, sc):
    x1 = x + gt * y
    return x1, _rms(x1, g) * (1.0 + sc) + sh


def f_post(i, k, zw0, zw1, za0, za1, kkp, ka, w00, w01, a00, a01, ones2):
    kq = k * kkp
    kk = kq / jnp.maximum(jnp.sqrt(_hsum(kq * kq, ones2)), 1e-12)

    def direction(zw, za, w0, a0):
        log_w = -_softplus(-(w0 + zw)) - 0.5
        a = _sigmoid(a0 + za)
        return jnp.exp(-jnp.exp(log_w)), k * (1.0 + (a - 1.0) * ka), kk * a

    dec0, kd0, bb0 = direction(zw0, za0, w00, a00)
    dec1, kd1, bb1 = direction(zw1, za1, w01, a01)
    return -kk, -kk, dec0, dec1, kd0, kd1, bb0, bb1, kd0 + kd1


def f_post_fwd(*a):
    return f_post(*a)[1:]


def f_readout(i, y0, y1, r, ksum, v, g, rk, lnw, lnb, ones2):
    y = y0 + y1
    yc = y - _hsum(y, ones2) * (1.0 / HEAD)
    var = _hsum(yc * yc, ones2) * (1.0 / HEAD)
    o = yc * lax.rsqrt(var + GN_EPS) * lnw + lnb
    o = o + _hsum(r * ksum * rk, ones2) * v
    return (o * g,)


def f_sum(i, *xs):
    acc = xs[0].astype(F32)
    for x in xs[1:]:
        acc = acc + x.astype(F32)
    return (acc,)


def sum_cast(name, arrs, dtype, tr, offs=None):
    return row_call(name, f_sum, arrs, [], [(arrs[0].shape[1], dtype)], tr=tr, sr=16, offs=offs)[0]


def swiglu_fwd(name, ab, tr):
    T, F2 = ab.shape
    F = F2 // 2
    sr = 16

    def body(ab_ref, o_ref):
        def step(j, carry):
            sl = pl.ds(pl.multiple_of(j * sr, sr), sr)
            a, b = ab_ref[sl, :F], ab_ref[sl, F:]
            o_ref[sl, :] = (a * _sigmoid(a) * b).astype(o_ref.dtype)
            return carry

        lax.fori_loop(0, tr // sr, step, 0)

    return pl.pallas_call(
        body, name=name, grid=(T // tr,), in_specs=[pl.BlockSpec((tr, F2), lambda i: (i, 0))],
        out_specs=pl.BlockSpec((tr, F), lambda i: (i, 0)), out_shape=jax.ShapeDtypeStruct((T, F), BF16),
        compiler_params=_cparams(dimension_semantics=("parallel",)),
    )(ab)


def swiglu_bwd(name, ab, dsw, tr):
    T, F2 = ab.shape
    F = F2 // 2
    sr = 16

    def body(ab_ref, d_ref, o_ref):
        def step(j, carry):
            sl = pl.ds(pl.multiple_of(j * sr, sr), sr)
            a, b, d = ab_ref[sl, :F], ab_ref[sl, F:], d_ref[sl, :]
            sg = _sigmoid(a)
            o_ref[sl, :F] = (d * b * (sg + a * sg * (1.0 - sg))).astype(o_ref.dtype)
            o_ref[sl, F:] = (d * a * sg).astype(o_ref.dtype)
            return carry

        lax.fori_loop(0, tr // sr, step, 0)

    return pl.pallas_call(
        body, name=name, grid=(T // tr,),
        in_specs=[pl.BlockSpec((tr, F2), lambda i: (i, 0)), pl.BlockSpec((tr, F), lambda i: (i, 0))],
        out_specs=pl.BlockSpec((tr, F2), lambda i: (i, 0)), out_shape=jax.ShapeDtypeStruct((T, F2), BF16),
        compiler_params=_cparams(dimension_semantics=("parallel",)),
    )(ab, dsw)


def final_call(name, x3, f1, gt, fg, tgt, tr):
    T, D = x3.shape
    sr = 16

    def f(x, y, gtv, g, t):
        err = _rms(x + gtv * y, g) - t
        return 0.5 * jnp.sum(err * err) * (1.0 / D)

    def body(x_ref, y_ref, gt_ref, g_ref, t_ref, dx_ref, dy_ref, dgt_ref, dg_ref, loss_ref):
        @pl.when(pl.program_id(0) == 0)
        def _():
            dgt_ref[...] = jnp.zeros_like(dgt_ref)
            dg_ref[...] = jnp.zeros_like(dg_ref)
            loss_ref[...] = jnp.zeros_like(loss_ref)

        def step(j, carry):
            sl = pl.ds(pl.multiple_of(j * sr, sr), sr)
            val, vjp = jax.vjp(lambda x, y, a, b: f(x, y, a, b, t_ref[sl, :]), x_ref[sl, :], y_ref[sl, :],
                               gt_ref[...], g_ref[...])
            dx, dy, dgt, dg = vjp(jnp.ones((), F32))
            dx_ref[sl, :] = dx
            dy_ref[sl, :] = dy.astype(dy_ref.dtype)
            dgt_ref[...] += dgt
            dg_ref[...] += dg
            loss_ref[...] += jnp.full(loss_ref.shape, val, F32)
            return carry

        lax.fori_loop(0, tr // sr, step, 0)

    row = pl.BlockSpec((tr, D), lambda i: (i, 0))
    vec = pl.BlockSpec((1, D), lambda i: (0, 0))
    return pl.pallas_call(
        body, name=name, grid=(T // tr,), in_specs=[row, row, vec, vec, row],
        out_specs=[row, row, vec, vec, pl.BlockSpec((8, LANES), lambda i: (0, 0))],
        out_shape=[jax.ShapeDtypeStruct((T, D), F32), jax.ShapeDtypeStruct((T, D), BF16)]
        + [jax.ShapeDtypeStruct((1, D), F32)] * 2
        + [jax.ShapeDtypeStruct((8, LANES), F32)],
        compiler_params=_cparams(dimension_semantics=("arbitrary",)),
    )(x3, f1, gt, fg, tgt)


def _tshift(x, kind, period):
    n = x.shape[0]
    t = lax.broadcasted_iota(jnp.int32, x.shape, 0)
    if kind == 0:
        return jnp.where((t & (period - 1)) == 0, 0.0, pltpu.roll(x, 1, 0))
    if kind == 1:
        return jnp.where(((t & (period - 1)) == period - 1) | (t == n - 1), 0.0, pltpu.roll(x, n - 1, 0))
    if kind == 2:
        return jnp.where(t < GRID_W, 0.0, pltpu.roll(x, GRID_W, 0))
    return jnp.where(t >= n - GRID_W, 0.0, pltpu.roll(x, n - GRID_W, 0))


def _pow2_at_least(n):
    return 1 << (n - 1).bit_length()


def _shift_into(dst_ref, h_ref, n_ctx, cb, D, transpose):
    j = pl.program_id(0)
    quarter = (j * cb * 4) // D
    half = (j * cb * 2) // D
    flip = 1 if transpose else 0
    for q in range(4):
        @pl.when(quarter == q)
        def _(q=q):
            dst_ref[n_ctx:, :] = _tshift(h_ref[n_ctx:, :], q ^ flip, GRID_W)
    for q in range(2):
        @pl.when(half == q)
        def _(q=q):
            dst_ref[:n_ctx, :] = _tshift(h_ref[:n_ctx, :], q ^ flip, _pow2_at_least(n_ctx))


def mix_fwd(name, h, mix, n_ctx):
    R, D = h.shape
    cb = LANES

    def body(h_ref, mix_ref, *rest):
        outs, hs_ref = rest[:6], rest[6]
        _shift_into(hs_ref, h_ref, n_ctx, cb, D, False)
        hv = h_ref[...]
        xx = hs_ref[...] - hv
        for m in range(6):
            outs[m][...] = (hv + xx * mix_ref[m:m + 1, :]).astype(BF16)

    col = pl.BlockSpec((R, cb), lambda j: (0, j))
    return pl.pallas_call(
        body, name=name, grid=(D // cb,), in_specs=[col, pl.BlockSpec((mix.shape[0], cb), lambda j: (0, j))],
        out_specs=[col] * 6, out_shape=[jax.ShapeDtypeStruct((R, D), BF16)] * 6,
        scratch_shapes=[pltpu.VMEM((R, cb), F32)],
        compiler_params=_cparams(dimension_semantics=("parallel",)),
    )(h, mix)


def mix_bwd(name, h, mix, cots, slots, n_ctx):
    R, D = h.shape
    cb = LANES
    nc = len(cots)

    def body(h_ref, mix_ref, *rest):
        cot_refs, dh_ref, dmix_ref, hs_ref, dxx_ref = rest[:nc], rest[nc], rest[nc + 1], rest[nc + 2], rest[nc + 3]
        _shift_into(hs_ref, h_ref, n_ctx, cb, D, False)
        xx = hs_ref[...] - h_ref[...]
        per_slot = [None] * 6
        for cref, m in zip(cot_refs, slots):
            per_slot[m] = cref[...] if per_slot[m] is None else per_slot[m] + cref[...]
        dh = jnp.zeros((R, cb), F32)
        dxx = jnp.zeros((R, cb), F32)
        rows = []
        for m in range(6):
            d = per_slot[m]
            dh = dh + d
            dxx = dxx + d * mix_ref[m:m + 1, :]
            rows.append(jnp.sum(d * xx, axis=0, keepdims=True))
        dmix_ref[...] = jnp.concatenate(rows + [jnp.zeros((2, cb), F32)], axis=0)
        dxx_ref[...] = dxx
        _shift_into(hs_ref, dxx_ref, n_ctx, cb, D, True)
        dh_ref[...] = dh - dxx + hs_ref[...]

    col = pl.BlockSpec((R, cb), lambda j: (0, j))
    return pl.pallas_call(
        body, name=name, grid=(D // cb,), in_specs=[col, pl.BlockSpec((mix.shape[0], cb), lambda j: (0, j))] + [col] * nc,
        out_specs=[col, pl.BlockSpec((8, cb), lambda j: (0, j))],
        out_shape=[jax.ShapeDtypeStruct((R, D), F32), jax.ShapeDtypeStruct((8, D), F32)],
        scratch_shapes=[pltpu.VMEM((R, cb), F32), pltpu.VMEM((R, cb), F32)],
        compiler_params=_cparams(dimension_semantics=("parallel",)),
    )(h, mix, *cots)


def _conv_parts(gb_ref, gc_ref, u_ref, cw_ref):
    T = gb_ref.shape[0]
    z = gc_ref[...] * u_ref[...]
    zp, zn = _tshift(z, 0, _pow2_at_least(T)), _tshift(z, 1, _pow2_at_least(T))
    conv = zp * cw_ref[0:1, :] + z * cw_ref[1:2, :] + zn * cw_ref[2:3, :]
    return z, zp, zn, conv


def conv_fwd(name, gcu, cw):
    T, D3 = gcu.shape
    D = D3 // 3
    cb = LANES
    nb = D // cb

    def body(gb_ref, gc_ref, u_ref, cw_ref, o_ref):
        _, _, _, conv = _conv_parts(gb_ref, gc_ref, u_ref, cw_ref)
        o_ref[...] = (gb_ref[...] * conv).astype(BF16)

    def part(p):
        return pl.BlockSpec((T, cb), lambda j: (0, j + p * nb))

    return pl.pallas_call(
        body, name=name, grid=(nb,),
        in_specs=[part(0), part(1), part(2), pl.BlockSpec((cw.shape[0], cb), lambda j: (0, j))],
        out_specs=pl.BlockSpec((T, cb), lambda j: (0, j)), out_shape=jax.ShapeDtypeStruct((T, D), BF16),
        compiler_params=_cparams(dimension_semantics=("parallel",)),
    )(gcu, gcu, gcu, cw)


def conv_bwd(name, gcu, cw, dp):
    T, D3 = gcu.shape
    D = D3 // 3
    cb = LANES
    nb = D // cb

    def body(gb_ref, gc_ref, u_ref, cw_ref, dp_ref, o_ref, dcw_ref):
        part = pl.program_id(1)
        z, zp, zn, conv = _conv_parts(gb_ref, gc_ref, u_ref, cw_ref)
        dpv = dp_ref[...]
        dconv = dpv * gb_ref[...]
        period = _pow2_at_least(T)
        dz = (_tshift(dconv * cw_ref[0:1, :], 1, period) + dconv * cw_ref[1:2, :]
              + _tshift(dconv * cw_ref[2:3, :], 0, period))

        @pl.when(part == 0)
        def _():
            o_ref[...] = (dpv * conv).astype(o_ref.dtype)
            dcw_ref[...] = jnp.concatenate(
                [jnp.sum(dconv * s, axis=0, keepdims=True) for s in (zp, z, zn)] + [jnp.zeros((5, cb), F32)], axis=0)

        @pl.when(part == 1)
        def _():
            o_ref[...] = (dz * u_ref[...]).astype(o_ref.dtype)

        @pl.when(part == 2)
        def _():
            o_ref[...] = (dz * gc_ref[...]).astype(o_ref.dtype)

    def part_spec(p):
        return pl.BlockSpec((T, cb), lambda j, q: (0, j + p * nb))

    return pl.pallas_call(
        body, name=name, grid=(nb, 3),
        in_specs=[part_spec(0), part_spec(1), part_spec(2), pl.BlockSpec((cw.shape[0], cb), lambda j, q: (0, j)),
                  pl.BlockSpec((T, cb), lambda j, q: (0, j))],
        out_specs=[pl.BlockSpec((T, cb), lambda j, q: (0, j + q * nb)), pl.BlockSpec((8, cb), lambda j, q: (0, j))],
        out_shape=[jax.ShapeDtypeStruct((T, D3), BF16), jax.ShapeDtypeStruct((8, D), F32)],
        compiler_params=_cparams(dimension_semantics=("arbitrary", "arbitrary")),
    )(gcu, gcu, gcu, cw, dp)


SCAN_TC = 8


def _scan_consts():
    rows = lax.broadcasted_iota(jnp.int32, (HEAD, LANES), 0)
    cols = lax.broadcasted_iota(jnp.int32, (HEAD, LANES), 1)
    eye = rows == (cols & (HEAD - 1))
    r2 = lax.broadcasted_iota(jnp.int32, (2 * LANES, LANES), 0)
    c2 = lax.broadcasted_iota(jnp.int32, (2 * LANES, LANES), 1)
    ones2 = (((r2 & (LANES - 1)) >= HEAD) == (c2 >= HEAD)).astype(BF16)
    return eye, ones2, ones2[:LANES]


SCAN_ROW_CHUNKS = 4


def _chunks_of_heads(hp):
    per = max(1, hp // SCAN_ROW_CHUNKS)
    return [range(lo, lo + per) for lo in range(0, hp, per)]


def _rows_of(heads):
    return pl.ds(heads[0] * HEAD, len(heads) * HEAD)


def _split2(p):
    hi = p.astype(BF16)
    lo = (p - hi.astype(F32)).astype(BF16)
    return jnp.concatenate([hi, lo], axis=1)


def _head_rows(h):
    return pl.ds(h * HEAD, HEAD)


def _expand_into(dst, p1_ref, row_ref, t, hp, eye, ones1):
    for h in range(hp):
        p1_ref[_head_rows(h), :] = jnp.where(eye, row_ref[t, h:h + 1, :], 0.0).astype(BF16)
    dst[...] = jnp.dot(p1_ref[...], ones1, preferred_element_type=F32)


def _colsum_store(ref, t, h, x):
    ref[t, pl.ds(h, 1), :] = jnp.sum(x, axis=0, keepdims=True)


def _order(i, n_ctx, n_all, rev):
    if not rev:
        return i
    return jnp.where(i < n_ctx, n_ctx - 1 - i, n_all - 1 - (i - n_ctx))


def _with_side(work, side, n_in, n_out, n_scratch, grid):
    if side is None:
        return work
    nsi, nso = len(side.ins), len(side.out_shapes)
    grid = (grid,) if isinstance(grid, int) else tuple(grid)

    def at(which):
        cond = None
        for d, g in enumerate(grid):
            c = pl.program_id(d) == (0 if which == "first" else g - 1)
            cond = c if cond is None else cond & c
        return cond

    def body(*refs):
        ins, side_in = refs[:n_in], refs[n_in:n_in + nsi]
        outs = refs[n_in + nsi:n_in + nsi + n_out]
        side_out = refs[n_in + nsi + n_out:n_in + nsi + n_out + nso]
        scratch = refs[n_in + nsi + n_out + nso:]

        @pl.when(at("first"))
        def _():
            side.start(side_in, side_out, scratch[n_scratch:])

        work(*ins, *outs, *scratch[:n_scratch])

        @pl.when(at("last"))
        def _():
            side.finish(side_in, side_out, scratch[n_scratch:])

    return body


def _side_call_args(side, n_in, n_out):
    if side is None:
        return [], [], [], [], {}, []
    aliases = {n_in + i: n_out + o for i, o in side.aliases.items()}
    return ([ANY] * len(side.ins), [ANY] * len(side.out_shapes), list(side.out_shapes), list(side.sems), aliases,
            list(side.ins))


def scan_fwd(name, r, w, k, v, a, b, n_ctx_rows, rev, side=None):
    R, D = r.shape
    hp, tc = D // LANES, SCAN_TC
    n_all, n_ctx = R // tc, n_ctx_rows // tc
    ins = [t.reshape(R, hp, LANES) for t in (r, w, k, v, a, b)]

    def work(r_ref, w_ref, k_ref, v_ref, a_ref, b_ref, y_ref, st_ref, s_ref, ve_ref, sa_ref, p_ref, p1_ref, ys_ref):
        @pl.when(pl.program_id(0) == 0)
        def _():
            s_ref[...] = jnp.zeros_like(s_ref)

        eye, ones2, ones1 = _scan_consts()

        def row_of(q):
            return tc - 1 - q if rev else q

        def expand(q):
            _expand_into(ve_ref.at[q], p1_ref.at[q % 2], v_ref, row_of(q), hp, eye, ones1)

        def advance(q):
            t, prev_ref = row_of(q), (s_ref if q == 0 else st_ref.at[q - 1])
            for heads in _chunks_of_heads(hp):
                rows = _rows_of(heads)
                for h in heads:
                    p_ref[q % 2, _head_rows(h), :] = _split2(prev_ref[_head_rows(h), :] * a_ref[t, h:h + 1, :])
                sa_ref[q % 2, rows, :] = jnp.dot(p_ref[q % 2, rows, :], ones2, preferred_element_type=F32)
                for h in heads:
                    hr_ = _head_rows(h)
                    st_ref[q, hr_, :] = (prev_ref[hr_, :] * w_ref[t, h:h + 1, :]
                                         + sa_ref[q % 2, hr_, :] * b_ref[t, h:h + 1, :]
                                         + ve_ref[q, hr_, :] * k_ref[t, h:h + 1, :])

        def readout(q):
            t = row_of(q)
            for h in range(hp):
                p1_ref[2 + q % 2, _head_rows(h), :] = (st_ref[q, _head_rows(h), :]
                                                       * r_ref[t, h:h + 1, :]).astype(BF16)
            ys_ref[q % 2] = jnp.dot(p1_ref[2 + q % 2], ones1, preferred_element_type=F32)
            for h in range(hp):
                _colsum_store(y_ref, t, h, jnp.where(eye, ys_ref[q % 2, _head_rows(h), :], 0.0))

        expand(0)
        for q in range(tc):
            if q + 1 < tc:
                expand(q + 1)
            advance(q)
            if q > 0:
                readout(q - 1)
        readout(tc - 1)
        s_ref[...] = st_ref[tc - 1]

    row_spec = pl.BlockSpec((tc, hp, LANES), lambda i: (_order(i, n_ctx, n_all, rev), 0, 0))
    n = hp * HEAD
    s_in, s_out, s_shape, s_scratch, s_alias, s_ops = _side_call_args(side, 6, 2)
    y, st, *side_res = pl.pallas_call(
        _with_side(work, side, 6, 2, 6, n_all), name=name, grid=(n_all,), in_specs=[row_spec] * 6 + s_in,
        out_specs=[row_spec, pl.BlockSpec((tc, n, LANES), lambda i: (i, 0, 0))] + s_out,
        out_shape=[jax.ShapeDtypeStruct((R, hp, LANES), F32), jax.ShapeDtypeStruct((R, n, LANES), F32)] + s_shape,
        scratch_shapes=[pltpu.VMEM((n, LANES), F32), pltpu.VMEM((tc, n, LANES), F32), pltpu.VMEM((2, n, LANES), F32),
                        pltpu.VMEM((2, n, 2 * LANES), BF16), pltpu.VMEM((4, n, LANES), BF16),
                        pltpu.VMEM((2, n, LANES), F32)] + s_scratch,
        input_output_aliases=s_alias,
        compiler_params=_cparams(dimension_semantics=("arbitrary",)),
    )(*ins, *s_ops)
    return y.reshape(R, D), st, side_res


def scan_bwd(name, r, w, k, v, a, b, dy, st, n_ctx_rows, rev, side=None):
    R, D = r.shape
    hp, tc = D // LANES, SCAN_TC
    n_all, n_ctx = R // tc, n_ctx_rows // tc
    ins = [t.reshape(R, hp, LANES) for t in (r, w, k, v, a, b, dy)]

    def work(r_ref, w_ref, k_ref, v_ref, a_ref, b_ref, dy_ref, st_ref, prev_ref,
             dr_ref, dw_ref, dk_ref, dv_ref, da_ref, db_ref,
             g_ref, s0_ref, ve_ref, dye_ref, sa_ref, gs_ref, tmp_ref, p_ref, p1_ref, tmp2_ref):
        i = pl.program_id(0)

        @pl.when(i == 0)
        def _():
            g_ref[...] = jnp.zeros_like(g_ref)

        eye, ones2, ones1 = _scan_consts()

        @pl.when(i == n_all - 1)
        def _():
            s0_ref[...] = jnp.zeros_like(s0_ref)

        @pl.when(i != n_all - 1)
        def _():
            s0_ref[...] = prev_ref[0]

        def row_of(q):
            return tc - 1 - q if rev else q

        def prev_of(q):
            return s0_ref if q == 0 else st_ref.at[q - 1]

        def before(q):
            t, prev = row_of(q), prev_of(q)
            _expand_into(ve_ref.at[q], p1_ref.at[2 + q % 2], v_ref, t, hp, eye, ones1)
            _expand_into(dye_ref.at[q], p1_ref.at[4 + q % 2], dy_ref, t, hp, eye, ones1)
            for h in range(hp):
                p1_ref[q % 2, _head_rows(h), :] = (prev[_head_rows(h), :] * a_ref[t, h:h + 1, :]).astype(BF16)
            sa_ref[q] = jnp.dot(p1_ref[q % 2], ones1, preferred_element_type=F32)

        def back(q):
            t, prev = row_of(q), prev_of(q)
            for heads in _chunks_of_heads(hp):
                rows = _rows_of(heads)
                for h in heads:
                    hr_ = _head_rows(h)
                    g = g_ref[hr_, :] + dye_ref[q, hr_, :] * r_ref[t, h:h + 1, :]
                    gs_ref[q, hr_, :] = g
                    p_ref[q % 2, hr_, :] = _split2(g * b_ref[t, h:h + 1, :])
                tmp_ref[q % 2, rows, :] = jnp.dot(p_ref[q % 2, rows, :], ones2, preferred_element_type=F32)
                for h in heads:
                    hr_ = _head_rows(h)
                    dsa = tmp_ref[q % 2, hr_, :]
                    _colsum_store(da_ref, t, h, prev[hr_, :] * dsa)
                    g_ref[hr_, :] = gs_ref[q, hr_, :] * w_ref[t, h:h + 1, :] + dsa * a_ref[t, h:h + 1, :]

        def after(q):
            t, prev = row_of(q), prev_of(q)
            for h in range(hp):
                hr_ = _head_rows(h)
                g = gs_ref[q, hr_, :]
                p1_ref[6 + q % 2, hr_, :] = (g * k_ref[t, h:h + 1, :]).astype(BF16)
                _colsum_store(dr_ref, t, h, st_ref[q, hr_, :] * dye_ref[q, hr_, :])
                _colsum_store(dk_ref, t, h, g * ve_ref[q, hr_, :])
                _colsum_store(dw_ref, t, h, g * prev[hr_, :])
                _colsum_store(db_ref, t, h, g * sa_ref[q, hr_, :])
            tmp2_ref[q % 2] = jnp.dot(p1_ref[6 + q % 2], ones1, preferred_element_type=F32)
            for h in range(hp):
                _colsum_store(dv_ref, t, h, jnp.where(eye, tmp2_ref[q % 2, _head_rows(h), :], 0.0))

        before(tc - 1)
        for q in reversed(range(tc)):
            if q > 0:
                before(q - 1)
            back(q)
            if q < tc - 1:
                after(q + 1)
        after(0)

    def pos(i):
        return n_all - 1 - i

    n = hp * HEAD
    row_spec = pl.BlockSpec((tc, hp, LANES), lambda i: (_order(pos(i), n_ctx, n_all, rev), 0, 0))
    big = pltpu.VMEM((tc, n, LANES), F32)
    one = pltpu.VMEM((n, LANES), F32)
    s_in, s_out, s_shape, s_scratch, s_alias, s_ops = _side_call_args(side, 9, 6)
    outs = pl.pallas_call(
        _with_side(work, side, 9, 6, 10, n_all), name=name, grid=(n_all,),
        in_specs=[row_spec] * 7 + [
            pl.BlockSpec((tc, n, LANES), lambda i: (pos(i), 0, 0)),
            pl.BlockSpec((1, n, LANES), lambda i: (jnp.maximum(pos(i) * tc - 1, 0), 0, 0))] + s_in,
        out_specs=[row_spec] * 6 + s_out,
        out_shape=[jax.ShapeDtypeStruct((R, hp, LANES), F32)] * 6 + s_shape,
        scratch_shapes=[one, one, big, big, big, big, pltpu.VMEM((2, n, LANES), F32),
                        pltpu.VMEM((2, n, 2 * LANES), BF16), pltpu.VMEM((8, n, LANES), BF16),
                        pltpu.VMEM((2, n, LANES), F32)] + s_scratch,
        input_output_aliases=s_alias,
        compiler_params=_cparams(dimension_semantics=("arbitrary",)),
    )(*ins, st, st, *s_ops)
    return [o.reshape(R, D) for o in outs[:6]], list(outs[6:])


ANY = pl.BlockSpec(memory_space=pl.ANY)


def _peer(xi, yi, ci, k):
    return (1 - xi if k & 4 else xi, 1 - yi if k & 2 else yi, 1 - ci if k & 1 else ci)


def _rcopy(src, dst, send_sem, recv_sem, dev):
    return pltpu.make_async_remote_copy(src_ref=src, dst_ref=dst, send_sem=send_sem, recv_sem=recv_sem,
                                        device_id=dev, device_id_type=MESH)


def _drain(copies):
    for cp in copies:
        if cp.is_remote:
            cp.wait_send()
        else:
            cp.wait()


def all_gather8(name, x):
    r, c = x.shape

    def body(x_ref, out_ref, send_sems, recv_sems, local_sem):
        xi, yi, ci = _place()

        def blk(p):
            return out_ref.at[4 * p[0] + 2 * p[1] + p[2]]

        me = (xi, yi, ci)
        mine = pltpu.make_async_copy(x_ref, blk(me), local_sem.at[0])
        mine.start()
        sends = [_rcopy(x_ref, blk(me), send_sems.at[k - 1], recv_sems.at[k - 1], _peer(xi, yi, ci, k))
                 for k in range(1, N_DEV)]
        for cp in sends:
            cp.start()
        for k in range(1, N_DEV):
            p = _peer(xi, yi, ci, k)
            _rcopy(x_ref, blk(p), send_sems.at[k - 1], recv_sems.at[k - 1], p).wait_recv()
        for cp in sends:
            cp.wait_send()
        mine.wait()

    vm = pl.BlockSpec(memory_space=pltpu.VMEM)
    return pl.pallas_call(
        body, name=name, in_specs=[vm], out_specs=vm, out_shape=jax.ShapeDtypeStruct((N_DEV, r, c), x.dtype),
        scratch_shapes=[pltpu.SemaphoreType.DMA((N_DEV - 1,)), pltpu.SemaphoreType.DMA((N_DEV - 1,)),
                        pltpu.SemaphoreType.DMA((1,))],
        compiler_params=_cparams(),
    )(x)


def _chips(xi, yi):
    chips = [(1 - xi, yi), (xi, 1 - yi), (1 - xi, 1 - yi)]
    return chips, [2 * cx + cy for cx, cy in chips]


def gather_weights(name, stacked):
    n = len(stacked)

    def body(*refs):
        _gather_start(refs[n:2 * n], refs[2 * n:])
        _gather_finish(refs[n:2 * n], refs[2 * n:])

    return pl.pallas_call(
        body, name=name, in_specs=[ANY] * n, out_specs=[ANY] * n,
        out_shape=[jax.ShapeDtypeStruct(a.shape, a.dtype) for a in stacked],
        input_output_aliases={w: w for w in range(n)},
        scratch_shapes=_gather_sems(n),
        compiler_params=_cparams(),
    )(*stacked)


def _gather_sems(n):
    return [pltpu.SemaphoreType.DMA((n, 6)), pltpu.SemaphoreType.DMA((n, 6))]


def _gather_sends(out, sems):
    send_sems, recv_sems = sems
    xi, yi, ci = _place()
    s = 2 * xi + yi
    chips, _ = _chips(xi, yi)
    sends = []
    for w in range(len(out)):
        hr = out[w].shape[1] // 2
        mine = out[w].at[s, pl.ds(ci * hr, hr)]
        sends += [_rcopy(mine, mine, send_sems.at[w, j], recv_sems.at[w, j], (cx, cy, ci))
                  for j, (cx, cy) in enumerate(chips)]
    return sends


def _gather_start(out, sems):
    for cp in _gather_sends(out, sems):
        cp.start()


def _gather_finish(out, sems):
    send_sems, recv_sems = sems
    xi, yi, ci = _place()
    chips, sidx = _chips(xi, yi)
    sib = (xi, yi, 1 - ci)
    passed = []
    for w in range(len(out)):
        hr = out[w].shape[1] // 2
        for j, (cx, cy) in enumerate(chips):
            blk = out[w].at[sidx[j], pl.ds(ci * hr, hr)]
            _rcopy(blk, blk, send_sems.at[w, j], recv_sems.at[w, j], (cx, cy, ci)).wait_recv()
            fw = _rcopy(blk, blk, send_sems.at[w, 3 + j], recv_sems.at[w, 3 + j], sib)
            fw.start()
            passed.append(fw)
    for w in range(len(out)):
        hr = out[w].shape[1] // 2
        for j in range(3):
            blk = out[w].at[sidx[j], pl.ds((1 - ci) * hr, hr)]
            _rcopy(blk, blk, send_sems.at[w, 3 + j], recv_sems.at[w, 3 + j], sib).wait_recv()
    _drain(_gather_sends(out, sems) + passed)


class Side:
    def __init__(self, ins, out_shapes, aliases, sems, start, finish):
        self.ins, self.out_shapes, self.aliases, self.sems = ins, out_shapes, aliases, sems
        self.start, self.finish = start, finish


def _gather_wait_ici(out, sems):
    send_sems, recv_sems = sems
    xi, yi, ci = _place()
    chips, sidx = _chips(xi, yi)
    for w in range(len(out)):
        hr = out[w].shape[1] // 2
        for j, (cx, cy) in enumerate(chips):
            blk = out[w].at[sidx[j], pl.ds(ci * hr, hr)]
            _rcopy(blk, blk, send_sems.at[w, j], recv_sems.at[w, j], (cx, cy, ci)).wait_recv()
    _drain(_gather_sends(out, sems))


def _pass_copies(out, sems, half_of):
    send_sems, recv_sems = sems
    xi, yi, ci = _place()
    _, sidx = _chips(xi, yi)
    sib = (xi, yi, 1 - ci)
    cps = []
    for w in range(len(out)):
        hr = out[w].shape[1] // 2
        for j in range(3):
            blk = out[w].at[sidx[j], pl.ds(half_of(ci) * hr, hr)]
            cps.append(_rcopy(blk, blk, send_sems.at[w, j], recv_sems.at[w, j], sib))
    return cps


def _pass_start(out, sems):
    for cp in _pass_copies(out, sems, lambda ci: ci):
        cp.start()


def _pass_finish(out, sems):
    for cp in _pass_copies(out, sems, lambda ci: 1 - ci):
        cp.wait_recv()
    _drain(_pass_copies(out, sems, lambda ci: ci))


def gather_side(stacked):
    n = len(stacked)
    return Side(stacked, [jax.ShapeDtypeStruct(a.shape, a.dtype) for a in stacked], {w: w for w in range(n)},
                _gather_sems(n), lambda ins, outs, sems: _gather_start(outs, sems),
                lambda ins, outs, sems: _gather_wait_ici(outs, sems))


def gather_whole_side(stacked):
    n = len(stacked)
    return Side(stacked, [jax.ShapeDtypeStruct(a.shape, a.dtype) for a in stacked], {w: w for w in range(n)},
                _gather_sems(n), lambda ins, outs, sems: _gather_start(outs, sems),
                lambda ins, outs, sems: _gather_finish(outs, sems))


def pass_side(stacked):
    n = len(stacked)
    return Side(stacked, [jax.ShapeDtypeStruct(a.shape, a.dtype) for a in stacked], {w: w for w in range(n)},
                [pltpu.SemaphoreType.DMA((n, 3)), pltpu.SemaphoreType.DMA((n, 3))],
                lambda ins, outs, sems: _pass_start(outs, sems), lambda ins, outs, sems: _pass_finish(outs, sems))


def _rs1_copies(g, out, sems):
    send_sems, recv_sems = sems
    xi, yi, ci = _place()
    return [_rcopy(g[w].at[:, 1 - ci], out[w], send_sems.at[w], recv_sems.at[w], (xi, yi, 1 - ci))
            for w in range(len(g))]


def _rs1_start(g, out, sems):
    for cp in _rs1_copies(g, out, sems):
        cp.start()


def _rs1_finish(g, out, sems):
    for cp in _rs1_copies(g, out, sems):
        cp.wait_recv()
    _drain(_rs1_copies(g, out, sems))


def rs1_side(grads):
    n = len(grads)
    return Side(grads, [jax.ShapeDtypeStruct((N_CHIPS,) + a.shape[2:], a.dtype) for a in grads], {},
                [pltpu.SemaphoreType.DMA((n,)), pltpu.SemaphoreType.DMA((n,))], _rs1_start, _rs1_finish)


def join_sides(a, b):
    na, nao, nas = len(a.ins), len(a.out_shapes), len(a.sems)
    aliases = dict(a.aliases)
    aliases.update({na + i: nao + o for i, o in b.aliases.items()})

    def start(ins, outs, sems):
        a.start(ins[:na], outs[:nao], sems[:nas])
        b.start(ins[na:], outs[nao:], sems[nas:])

    def finish(ins, outs, sems):
        a.finish(ins[:na], outs[:nao], sems[:nas])
        b.finish(ins[na:], outs[nao:], sems[nas:])

    return Side(list(a.ins) + list(b.ins), list(a.out_shapes) + list(b.out_shapes), aliases,
                list(a.sems) + list(b.sems), start, finish)


def rs_pair(name, grads):
    n = len(grads)
    side = rs1_side(grads)

    def body(*refs):
        side.start(refs[:n], refs[n:2 * n], refs[2 * n:])
        side.finish(refs[:n], refs[n:2 * n], refs[2 * n:])

    return pl.pallas_call(
        body, name=name, in_specs=[ANY] * n, out_specs=[ANY] * n, out_shape=side.out_shapes,
        scratch_shapes=side.sems, compiler_params=_cparams(),
    )(*grads)


def _rows_tile(rows, cols, unit=16, limit=1 << 20):
    best = None
    for t in range(unit, rows + 1, unit):
        if rows % t == 0 and t * cols * 4 <= limit:
            best = t
    return best or rows


def rs_add_pair(name, g, got, ci, slot):
    _, _, hr, c = g.shape
    th = _rows_tile(hr, c, limit=1 << 21)

    def body(ci_ref, slot_ref, g_ref, r_ref, own_ref, ob_ref):
        tot = g_ref[...] + r_ref[...]
        ob_ref[...] = tot.astype(BF16)

        @pl.when(pl.program_id(1) == slot_ref[0])
        def _():
            own_ref[...] = tot

    blk = pl.BlockSpec((None, th, c), lambda i, s, ci_ref, slot_ref: (s, i, 0))
    return pl.pallas_call(
        body, name=name,
        grid_spec=pltpu.PrefetchScalarGridSpec(
            num_scalar_prefetch=2, grid=(hr // th, N_CHIPS),
            in_specs=[pl.BlockSpec((None, None, th, c), lambda i, s, ci_ref, slot_ref: (s, ci_ref[0], i, 0)), blk],
            out_specs=[pl.BlockSpec((th, c), lambda i, s, ci_ref, slot_ref: (i, 0)), blk]),
        out_shape=[jax.ShapeDtypeStruct((hr, c), F32), jax.ShapeDtypeStruct((N_CHIPS, hr, c), BF16)],
        compiler_params=_cparams(dimension_semantics=("arbitrary", "arbitrary")),
    )(ci, slot, g, got)


def _rs2_copies(pb, outs, sems):
    send_sems, recv_sems = sems
    xi, yi, ci = _place()
    chips, sidx = _chips(xi, yi)
    return [_rcopy(pb[w].at[sidx[j]], outs[3 * w + j], send_sems.at[w, j], recv_sems.at[w, j], (cx, cy, ci))
            for w in range(len(pb)) for j, (cx, cy) in enumerate(chips)]


def _rs2_start(pb, outs, sems):
    for cp in _rs2_copies(pb, outs, sems):
        cp.start()


def _rs2_finish(pb, outs, sems):
    for cp in _rs2_copies(pb, outs, sems):
        cp.wait_recv()
    _drain(_rs2_copies(pb, outs, sems))


def rs2_side(sums_bf16):
    n = len(sums_bf16)
    out_shapes = [jax.ShapeDtypeStruct(a.shape[1:], BF16) for a in sums_bf16 for _ in range(3)]
    return Side(sums_bf16, out_shapes, {}, [pltpu.SemaphoreType.DMA((n, 3)), pltpu.SemaphoreType.DMA((n, 3))],
                _rs2_start, _rs2_finish)


def rs_swap(name, halves):
    n = len(halves)

    def body(*refs):
        hv, out = refs[:n], refs[n:2 * n]
        send_sems, recv_sems = refs[2 * n:]
        xi, yi, ci = _place()
        sib = (xi, yi, 1 - ci)
        cps = [_rcopy(hv[w], out[w], send_sems.at[w], recv_sems.at[w], sib) for w in range(n)]
        for cp in cps:
            cp.start()
        for cp in cps:
            cp.wait_recv()
        _drain(cps)

    return pl.pallas_call(
        body, name=name, in_specs=[ANY] * n, out_specs=[ANY] * n,
        out_shape=[jax.ShapeDtypeStruct(a.shape, a.dtype) for a in halves],
        scratch_shapes=[pltpu.SemaphoreType.DMA((n,)), pltpu.SemaphoreType.DMA((n,))],
        compiler_params=_cparams(),
    )(*halves)


def cast_to_slot(name, x, slot):
    r, c = x.shape
    tr = _rows_tile(r, c, limit=1 << 22)

    def body(slot_ref, x_ref, o_ref):
        o_ref[...] = x_ref[...].astype(BF16)

    return pl.pallas_call(
        body, name=name,
        grid_spec=pltpu.PrefetchScalarGridSpec(
            num_scalar_prefetch=1, grid=(r // tr,),
            in_specs=[pl.BlockSpec((tr, c), lambda i, slot_ref: (i, 0))],
            out_specs=pl.BlockSpec((None, tr, c), lambda i, slot_ref: (slot_ref[0], i, 0))),
        out_shape=jax.ShapeDtypeStruct((N_CHIPS, r, c), BF16),
        compiler_params=_cparams(dimension_semantics=("parallel",)),
    )(slot, x)


def sum_blocks(name, x, picks):
    _, r, c = x.shape

    def body(x_ref, o_ref):
        acc = x_ref[picks[0]]
        for b in picks[1:]:
            acc = acc + x_ref[b]
        o_ref[...] = acc

    vm = pl.BlockSpec(memory_space=pltpu.VMEM)
    return pl.pallas_call(body, name=name, in_specs=[vm], out_specs=vm, out_shape=jax.ShapeDtypeStruct((r, c), F32),
                          compiler_params=_cparams())(x)


def adamw(name, w, g, m, v):
    r, c = w.shape
    tr = _rows_tile(r, c, unit=8, limit=1 << 20)

    def body(w_ref, g_ref, m_ref, v_ref, d_ref, mo_ref, vo_ref):
        d_ref[...], mo_ref[...], vo_ref[...] = _adam_update(w_ref[...], g_ref[...], m_ref[...], v_ref[...])

    blk = pl.BlockSpec((tr, c), lambda i: (i, 0))
    return pl.pallas_call(
        body, name=name, grid=(r // tr,), in_specs=[blk] * 4, out_specs=[blk] * 3,
        out_shape=[jax.ShapeDtypeStruct((r, c), F32)] * 3,
        compiler_params=_cparams(dimension_semantics=("parallel",)),
    )(w, g, m, v)


def _adam_update(w, gv, m, v):
    c1 = 1.0 / (1.0 - ADAM_B1 ** ADAM_STEP)
    c2 = 1.0 / (1.0 - ADAM_B2 ** ADAM_STEP)
    mn = ADAM_B1 * m + (1.0 - ADAM_B1) * gv
    vn = ADAM_B2 * v + (1.0 - ADAM_B2) * (gv * gv)
    return -ADAM_LR * ((mn * c1) / (jnp.sqrt(vn * c2) + ADAM_EPS) + ADAM_WD * w), mn, vn


def adamw_halves(name, w, m, v, mine, theirs, ci, side=None):
    hr, c = mine[0].shape
    npos = len(mine)
    th = _rows_tile(hr, c, unit=8, limit=1 << 20)
    per = hr // th

    def work(ci_ref, w_ref, m_ref, v_ref, *rest):
        g_refs, (go_ref, d_ref, mo_ref, vo_ref) = rest[:2 * npos], rest[2 * npos:]
        pos, half = pl.program_id(0), pl.program_id(1)
        from_me = half == ci_ref[0]
        gv = jnp.where(from_me, g_refs[0][...], g_refs[npos][...])
        for p in range(1, npos):
            gv = jnp.where(pos == p, jnp.where(from_me, g_refs[p][...], g_refs[npos + p][...]), gv)
        d_ref[...], mo_ref[...], vo_ref[...] = _adam_update(w_ref[...], gv, m_ref[...], v_ref[...])
        go_ref[...] = gv

    full = pl.BlockSpec((th, c), lambda p, h, i, ci_ref: ((p * 2 + h) * per + i, 0))
    part = pl.BlockSpec((th, c), lambda p, h, i, ci_ref: (i, 0))
    rows = 2 * hr * npos
    n_in = 4 + 2 * npos
    s_in, s_out, s_shape, s_scratch, _, s_ops = _side_call_args(side, n_in, 4)
    res = pl.pallas_call(
        _with_side(work, side, n_in, 4, 0, (npos, 2, per)), name=name,
        grid_spec=pltpu.PrefetchScalarGridSpec(
            num_scalar_prefetch=1, grid=(npos, 2, per), in_specs=[full] * 3 + [part] * (2 * npos) + s_in,
            out_specs=[full] * 4 + s_out, scratch_shapes=s_scratch),
        out_shape=[jax.ShapeDtypeStruct((rows, c), F32)] * 4 + s_shape,
        compiler_params=_cparams(dimension_semantics=("arbitrary", "arbitrary", "arbitrary")),
    )(ci, w, m, v, *mine, *theirs, *s_ops)
    return res[0], res[1], res[2], res[3], list(res[4:])


def pack_rows(name, parts, rows):
    width = parts[0].shape[1]
    n = len(parts)

    def body(*refs):
        o_ref = refs[n]
        o_ref[...] = jnp.zeros_like(o_ref)
        off = 0
        for r in refs[:n]:
            o_ref[off:off + r.shape[0], :] = r[...]
            off += r.shape[0]

    vm = pl.BlockSpec(memory_space=pltpu.VMEM)
    return pl.pallas_call(body, name=name, in_specs=[vm] * n, out_specs=vm,
                          out_shape=jax.ShapeDtypeStruct((rows, width), F32), compiler_params=_cparams())(*parts)


def _pad_rows(a, rows):
    return jnp.pad(a, ((0, rows - a.shape[0]), (0, 0)))


def _view2d(name, a):
    if name == 'rw_rk' or a.ndim == 1:
        return a.reshape(1, -1)
    return a.reshape(-1, a.shape[-1])


def _rs_pair_sums(tag, items, ci_arr, s_arr, got=None):
    g4 = [g.reshape(N_CHIPS, 2, g.shape[1] // 2, g.shape[2]) for _, g in items]
    if got is None:
        got = rs_pair("rs1_" + tag, g4)
    return [rs_add_pair(f"rs1add_{tag}{w}", g4[w], got[w], ci_arr, s_arr) for w in range(len(items))]


def _rs_finish(tag, items, sums, landed, s):
    names = []
    for nm, _ in items:
        if nm not in names:
            names.append(nm)
    halves = []
    for w, (r0, r1, r2) in enumerate(landed):
        own = sums[w][0]
        hr, c = own.shape
        halves.append(row_call(f"rs2add_{tag}{w}", f_sum, [own, r0, r1, r2], [], [(c, F32)],
                               tr=_rows_tile(hr, c), sr=16)[0])
    theirs = rs_swap("rs3_" + tag, halves)
    return {name: ([halves[w] for w, (nm, _) in enumerate(items) if nm == name],
                   [theirs[w] for w, (nm, _) in enumerate(items) if nm == name]) for name in names}


def _step(p):
    xi, yi, ci = _place()
    me = 4 * xi + 2 * yi + ci
    s = 2 * xi + yi
    ci_arr = jnp.reshape(ci, (1,)).astype(jnp.int32)
    s_arr = jnp.reshape(s, (1,)).astype(jnp.int32)
    x, ctx, tgt = p['x'][0], p['ctx'][0], p['loss_target'][0]
    T, D = x.shape
    L = ctx.shape[0]
    H, Dq = D // HEAD, D // N_CHIPS
    TR = math.gcd(math.gcd(L, T), 256)
    TS = min(TR, 64)
    nct = L // TR
    LG = p['rw_g1'].shape[-1]
    LW, LA = p['rw_w1'].shape[-1], p['rw_a1'].shape[-1]
    F4 = p['ffn_w2'].shape[1]

    pack = jnp.concatenate([
        _pad_rows(p['c'].reshape(N_CHIPS, Dq), 8), _pad_rows(p['rw_mix'][0], 8), _pad_rows(p['rw_w0'][0], 8),
        _pad_rows(p['rw_a0'][0], 8), _pad_rows(p['sc_conv'][0], 8)], axis=0)
    got = all_gather8("ag_small", pack)
    c_all = got[:, 0:N_CHIPS, :].reshape(N_DEV, D)
    full = jnp.transpose(got[::2], (1, 0, 2)).reshape(40, D)
    mix_f, w0_f, a0_f, conv_f = full[8:16], full[16:24], full[24:32], full[32:40]

    cond_in = jnp.concatenate([c_all, _pad_rows(p['c_ctx'].reshape(1, D), 8)], axis=0)
    cond = row_call("cond", f_silu, [cond_in], [], [(D, F32)], tr=16, sr=16)[0]
    ada = Stacked(p['ada_w'], "layer", D)
    modp = [mm_nn(f"modp{i}", cond, ada.at(i)) for i in range(2)]
    mg = all_gather8("ag_mod", jnp.concatenate(modp, axis=0))
    mod = jnp.transpose(mg[::2].reshape(N_CHIPS, 2, 16, 6 * Dq), (1, 2, 0, 3)).reshape(2, 16, 6 * D)
    mod = mod + p['ada_b'][:, None, :]
    mod_x = lax.dynamic_index_in_dim(mod, me, axis=1, keepdims=False)
    mod_c = mod[:, 8]

    def chunk(vec, j):
        return vec[j * D:(j + 1) * D].reshape(1, D)

    sh1x, sc1x, gt1x, sh2x, sc2x, gt2x = ([chunk(mod_x[i], j) for i in range(2)] for j in range(6))
    sh1c, sc1c = chunk(mod_c[0], 0), chunk(mod_c[0], 1)

    def slots(names):
        return [cast_to_slot("cast_" + n, _view2d(n, p[n]), s_arr) for n in names]

    rw_names = ('rw_wr', 'rw_wk', 'rw_wv', 'rw_wo', 'rw_w1', 'rw_w2', 'rw_a1', 'rw_a2', 'rw_g1', 'rw_g2')
    late_names = ('sc_win', 'sc_wout', 'ffn_w13', 'ffn_w2')
    gw = dict(zip(rw_names, gather_weights("ag_rw", slots(rw_names))))
    late_slots = slots(late_names)
    Wr, Wk, Wv, Wo = (Stacked(gw[n], "row", Dq) for n in ('rw_wr', 'rw_wk', 'rw_wv', 'rw_wo'))
    W1, A1, G1 = (Stacked(gw[n], "row", Dq) for n in ('rw_w1', 'rw_a1', 'rw_g1'))
    W2, A2, G2 = Stacked(gw['rw_w2'], "col", LW), Stacked(gw['rw_a2'], "col", LA), Stacked(gw['rw_g2'], "col", LG)

    ones2 = (((lax.broadcasted_iota(jnp.int32, (2 * LANES, LANES), 0) & (LANES - 1)) >= HEAD)
             == (lax.broadcasted_iota(jnp.int32, (2 * LANES, LANES), 1) >= HEAD)).astype(BF16)
    n1g, n2g = p['norm1_g'], p['norm2_g']
    kkp, ka, lnw, lnb = p['rw_kk'], p['rw_ka'], p['rw_lnw'], p['rw_lnb']
    rk = p['rw_rk'].reshape(1, D)
    fg = p['final_g'].reshape(1, D)

    xin = jnp.concatenate([ctx, x], axis=0)
    nm = functools.partial(f_norm_mod, nct)
    nm_consts = [n1g[0:1], sh1c, sc1c, sh1x[0], sc1x[0]]
    h = row_call("l0_norm", nm, [xin], nm_consts, [(D, F32)], tr=TR, sr=16)[0]
    xr, xw, xk, xv, xa, xg = mix_fwd("l0_mix", h, mix_f, L)
    r = mm_nn("l0_r", xr, Wr)
    k = mm_nn("l0_k", xk, Wk)
    v = mm_nn("l0_v", xv, Wv)
    gl = mm_nn("l0_gl", xg, G1)
    sg = row_call("l0_sg", f_sigmoid, [gl], [], [(LG, BF16)], tr=TR, sr=16)[0]
    g = mm_nn("l0_g", sg, G2)
    wl, tw, zw, al, za = [], [], [], [], []
    for d in range(2):
        wl.append(mm_nn(f"l0_wl{d}", xw, W1.at(d)))
        tw.append(row_call(f"l0_tw{d}", f_tanh, [wl[d]], [], [(LW, BF16)], tr=TR, sr=16)[0])
        zw.append(mm_nn(f"l0_zw{d}", tw[d], W2.at(d)))
        al.append(mm_nn(f"l0_al{d}", xa, A1.at(d), BF16))
        za.append(mm_nn(f"l0_za{d}", al[d], A2.at(d)))
    post_rows = [k, zw[0], zw[1], za[0], za[1]]
    post_consts = [kkp, ka, w0_f[0:1], w0_f[1:2], a0_f[0:1], a0_f[1:2], ones2]
    aa, dec0, dec1, kd0, kd1, bb0, bb1, ksum = row_call(
        "l0_post", f_post_fwd, post_rows, post_consts, [(D, F32)] * 8, tr=TS, sr=16)
    dec, kd, bb = (dec0, dec1), (kd0, kd1), (bb0, bb1)
    ys, sts = [], []
    y_d, st_d, first = scan_fwd("l0_scan0", r, dec[0], kd[0], v, aa, bb[0], L, False, side=gather_side(late_slots[:3]))
    ys.append(y_d)
    sts.append(st_d)
    y_d, st_d, filled = scan_fwd("l0_scan1", r, dec[1], kd[1], v, aa, bb[1], L, True,
                                 side=join_sides(pass_side(first), gather_whole_side(late_slots[3:])))
    ys.append(y_d)
    sts.append(st_d)
    gw.update(zip(late_names, filled))
    Win, Wout = Stacked(gw['sc_win'], "col", D), Stacked(gw['sc_wout'], "row", Dq)
    W13, W2f = Stacked(gw['ffn_w13'], "col", D), Stacked(gw['ffn_w2'], "row", F4)
    ro_rows = [ys[0], ys[1], r, ksum, v, g]
    ro_consts = [rk, lnw, lnb, ones2]
    og = row_call("l0_readout", f_readout, ro_rows, ro_consts, [(D, BF16)], tr=TS, sr=16)[0]
    yx = mm_nn("l0_o", og, Wo)
    res0_consts = [gt1x[0], n2g[0:1], sh2x[0], sc2x[0]]
    x1, h2 = row_call("l0_res", f_res_norm_mod, [x, yx], res0_consts, [(D, F32), (D, BF16)], tr=TR, sr=16,
                      offs=[0, nct])
    ab0 = mm_nn("l0_ffn13", h2, W13.at(0))
    sw0 = swiglu_fwd("l0_swiglu", ab0, TS)
    f0 = mm_nn("l0_ffn2", sw0, W2f.at(0))

    res1_consts = [gt2x[0], n1g[1:2], sh1x[1], sc1x[1]]
    x2, hb = row_call("l1_norm", f_res_norm_mod, [x1, f0], res1_consts, [(D, F32), (D, BF16)], tr=TR, sr=16)
    gcu = mm_nn("l1_win", hb, Win)
    pc = conv_fwd("l1_conv", gcu, conv_f)
    yx1 = mm_nn("l1_wout", pc, Wout)
    res2_consts = [gt1x[1], n2g[1:2], sh2x[1], sc2x[1]]
    x3, h2b = row_call("l1_res", f_res_norm_mod, [x2, yx1], res2_consts, [(D, F32), (D, BF16)], tr=TR, sr=16)
    ab1 = mm_nn("l1_ffn13", h2b, W13.at(1))
    sw1 = swiglu_fwd("l1_swiglu", ab1, TS)
    f1 = mm_nn("l1_ffn2", sw1, W2f.at(1))
    dx3, df1, dgt2_1, dfg, loss_blk = final_call("final", x3, f1, gt2x[1], fg, tgt, TR)
    loss = lax.psum(loss_blk[0, 0], ("x", "y", "c"))

    big = []
    dsw1 = mm_nt("b1_dsw", df1, W2f.at(1))
    gW2f1 = mm_tn("b1_gw2", sw1, df1, "row")
    dab1 = swiglu_bwd("b1_swiglu", ab1, dsw1, TS)
    dh2b = mm_nt("b1_dh2", dab1, W13.at(1))
    gW13_1 = mm_tn("b1_gw13", h2b, dab1, "col")
    rm = [True, True]
    cm = [True] * 4
    (dx2, dyx1), (dgt1_1, dn2g1, dsh2_1, dsc2_1) = row_vjp(
        "b1_res", f_res_norm_mod, [x2, yx1], res2_consts, [dx3, dh2b], row_mask=rm, const_mask=cm, tr=TR, sr=16,
        bf16_rows=(1,))
    dpc = mm_nt("b1_dpc", dyx1, Wout)
    big.append(('sc_wout', mm_tn("b1_gwout", pc, dyx1, "row")))
    dgcu, dconv = conv_bwd("b1_conv", gcu, conv_f, dpc)
    dhb = mm_nt("b1_dhb", dgcu, Win)
    big.append(('sc_win', mm_tn("b1_gwin", hb, dgcu, "col")))
    (dx1, df0), (dgt2_0, dn1g1, dsh1_1, dsc1_1) = row_vjp(
        "b1_norm", f_res_norm_mod, [x1, f0], res1_consts, [dx2, dhb], row_mask=rm, const_mask=cm, tr=TR, sr=16,
        bf16_rows=(1,))

    def halves4(items):
        return [g.reshape(N_CHIPS, 2, g.shape[1] // 2, g.shape[2]) for _, g in items]

    sc_items = [it for it in big if it[0] in ('sc_win', 'sc_wout')]
    ffn1_items = [('ffn_w13', gW13_1), ('ffn_w2', gW2f1)]
    dsw0, sc_got = mm_nt("b0_dsw", df0, W2f.at(0), side=rs1_side(halves4(sc_items)))
    gW2f0 = mm_tn("b0_gw2", sw0, df0, "row")
    dab0 = swiglu_bwd("b0_swiglu", ab0, dsw0, TS)
    dh2, ffn1_got = mm_nt("b0_dh2", dab0, W13.at(0), side=rs1_side(halves4(ffn1_items)))
    gW13_0, w2f0_got = mm_tn("b0_gw13", h2, dab0, "col", side=rs1_side(halves4([('ffn_w2', gW2f0)])))
    (dx_a, dyx), (dgt1_0, dn2g0, dsh2_0, dsc2_0) = row_vjp(
        "b0_res", f_res_norm_mod, [x, yx], res0_consts, [dx1, dh2], row_mask=rm, const_mask=cm, tr=TR, sr=16,
        offs=[0, nct], bf16_rows=(1,))
    dyx_all = jnp.concatenate([jnp.zeros((L, D), BF16), dyx], axis=0)
    dog = mm_nt("b0_dog", dyx_all, Wo)
    gWo = mm_tn("b0_gwo", og, dyx_all, "row")
    (dy, dr_ro, dksum, dv_ro, dg), (drk, dlnw, dlnb), w13_0_got = row_vjp(
        "b0_readout", f_readout, ro_rows, ro_consts, [dog], row_mask=[True, False, True, True, True, True],
        const_mask=[True, True, True, False], tr=TS, sr=16, bf16_rows=(5,),
        side=rs1_side(halves4([('ffn_w13', gW13_0)])))
    late_items = sc_items + [('ffn_w13', gW13_0), ffn1_items[0], ('ffn_w2', gW2f0), ffn1_items[1]]
    late_got = sc_got + [w13_0_got[0], ffn1_got[0], w2f0_got[0], ffn1_got[1]]
    late_sums = _rs_pair_sums("late", late_items, ci_arr, s_arr, late_got)
    (dr0, ddec0, dkd0, dv0, daa0, dbb0), landed_flat = scan_bwd(
        "b0_scan0", r, dec[0], kd[0], v, aa, bb[0], dy, sts[0], L, False, side=rs2_side([s_[1] for s_ in late_sums]))
    (dr1, ddec1, dkd1, dv1, daa1, dbb1), _ = scan_bwd("b0_scan1", r, dec[1], kd[1], v, aa, bb[1], dy, sts[1], L, True)
    late_landed = [landed_flat[3 * w:3 * w + 3] for w in range(len(late_items))]
    post_cots = [daa0, daa1, ddec0, ddec1, dkd0, dkd1, dbb0, dbb1, dksum]
    (dk, dzw0, dzw1, dza0, dza1), (dkkp, dka, dw00, dw01, da00, da01) = row_vjp(
        "b0_post", f_post, post_rows, post_consts, post_cots, row_mask=[True] * 5,
        const_mask=[True] * 6 + [False], tr=TS, sr=16, bf16_rows=(0, 1, 2, 3, 4))
    dzw, dza = (dzw0, dzw1), (dza0, dza1)
    dr_t = sum_cast("b0_drsum", [dr0, dr1, dr_ro], BF16, TR)
    dv_t = sum_cast("b0_dvsum", [dv0, dv1, dv_ro], BF16, TR)
    mix_cots, mix_slots = [], []

    def back(tag, cot, w, xin_m, kind, slot):
        mix_cots.append(mm_nt("b0_dx" + tag, cot, w))
        mix_slots.append(slot)
        return mm_tn("b0_gw" + tag, xin_m, cot, kind)

    big.append(('rw_wr', back("r", dr_t, Wr, xr, "row", 0)))
    big.append(('rw_wk', back("k", dk, Wk, xk, "row", 2)))
    big.append(('rw_wv', back("v", dv_t, Wv, xv, "row", 3)))
    big.append(('rw_wo', gWo))
    dsg = mm_nt("b0_dsg", dg, G2)
    gG2 = mm_tn("b0_gg2", sg, dg, "col")
    (dgl,), _ = row_vjp("b0_sg", f_sigmoid, [gl], [], [dsg], row_mask=[True], const_mask=[], tr=TR, sr=16,
                        bf16_rows=(0,))
    gG1 = back("g", dgl, G1, xg, "row", 5)
    gW1, gW2, gA1, gA2 = [], [], [], []
    for d in range(2):
        dtw = mm_nt(f"b0_dtw{d}", dzw[d], W2.at(d))
        gW2.append(mm_tn(f"b0_gw2{d}", tw[d], dzw[d], "col"))
        (dwl,), _ = row_vjp(f"b0_tw{d}", f_tanh, [wl[d]], [], [dtw], row_mask=[True], const_mask=[], tr=TR, sr=16,
                            bf16_rows=(0,))
        gW1.append(back(f"w{d}", dwl, W1.at(d), xw, "row", 1))
        dal = mm_nt(f"b0_dal{d}", dza[d], A2.at(d), BF16)
        gA2.append(mm_tn(f"b0_ga2{d}", al[d], dza[d], "col"))
        gA1.append(back(f"a{d}", dal, A1.at(d), xa, "row", 4))
    big += [('rw_w1', gW1[0]), ('rw_w1', gW1[1]), ('rw_w2', gW2[0]), ('rw_w2', gW2[1]),
            ('rw_a1', gA1[0]), ('rw_a1', gA1[1]), ('rw_a2', gA2[0]), ('rw_a2', gA2[1]),
            ('rw_g1', gG1), ('rw_g2', gG2)]
    dh, dmix = mix_bwd("b0_mix", h, mix_f, mix_cots, mix_slots, L)
    (dxin,), (dn1g0, dsh1c, dsc1c, dsh1x, dsc1x) = row_vjp(
        "b0_norm", nm, [xin], nm_consts, [dh], row_mask=[True], const_mask=[True] * 5, tr=TR, sr=16)
    grad_x = sum_cast("b0_dx", [dxin, dx_a], F32, TR, offs=[nct, 0])

    zero = jnp.zeros((1, D), F32)
    parts = [dsh1x, dsc1x, dgt1_0, dsh2_0, dsc2_0, dgt2_0, dsh1c, dsc1c, zero, zero, zero, zero,
             dsh1_1, dsc1_1, dgt1_1, dsh2_1, dsc2_1, dgt2_1, zero, zero, zero, zero, zero, zero,
             dn1g0, dn1g1, dn2g0, dn2g1, dkkp, dka, drk, dlnw, dlnb, dfg,
             dmix[0:6], dw00, dw01, da00, da01, dconv[0:3]]
    got2 = all_gather8("ag_grads", pack_rows("pack_grads", parts, 48))
    small = sum_blocks("sum_grads", got2, list(range(N_DEV)))
    per_ex = got2[:, 0:24].reshape(N_DEV, 2, 2, 6 * D)
    tot = small[0:24].reshape(2, 2, 6 * D)
    cols = lambda a, width: lax.dynamic_slice_in_dim(a, s * width, width, axis=1)
    g_ada_w, dcond_parts = [], []
    for i in range(2):
        dm16 = cols(jnp.concatenate([per_ex[:, i, 0], _pad_rows(tot[i, 1][None], 8)], axis=0), 6 * Dq)
        g_ada_w.append(mm_tn(f"g_ada{i}", cond, dm16))
        dcond_parts.append(mm_nt(f"dcond{i}", dm16, ada.at(i)))
    g_ada_b = sum_cast("g_adab", [_pad_rows(tot[:, 0].reshape(12, D), 16), _pad_rows(tot[:, 1].reshape(12, D), 16)],
                       F32, 16)[0:12].reshape(2, 6 * D)
    dcond_mine = sum_cast("dcond_sum", dcond_parts, F32, 16)
    dcond = sum_blocks("dcond_chips", all_gather8("ag_dcond", dcond_mine), [0, 2, 4, 6])
    (dcin,), _ = row_vjp("b_cond", f_silu, [cond_in], [], [dcond], row_mask=[True], const_mask=[], tr=16, sr=16)

    gsh = _rs_finish("late", late_items, late_sums, late_landed, s)
    rw_items = [it for it in big if it[0] in rw_names]
    rw_sums = _rs_pair_sums("rw", rw_items, ci_arr, s_arr)

    def view(n):
        return _view2d(n, p[n]), _view2d(n, p['m_' + n]), _view2d(n, p['v_' + n])

    done = {}
    *done['ffn_w13'], rw_flat = adamw_halves("adam_ffn_w13", *view('ffn_w13'), gsh['ffn_w13'][0], gsh['ffn_w13'][1],
                                             ci_arr, side=rs2_side([s_[1] for s_ in rw_sums]))
    gsh.update(_rs_finish("rw", rw_items, rw_sums, [rw_flat[3 * w:3 * w + 3] for w in range(len(rw_items))], s))

    grads = {}
    grads['c_ctx'] = dcin[8]
    grads['norm1_g'], grads['norm2_g'] = small[24:26], small[26:28]
    grads['ada_w'] = jnp.stack(g_ada_w)
    grads['ada_b'] = g_ada_b
    grads['rw_kk'], grads['rw_ka'], grads['rw_rk'] = small[28:29], small[29:30], small[30:31]
    grads['rw_lnw'], grads['rw_lnb'], grads['final_g'] = small[31:32], small[32:33], small[33]
    sharded = cols(small[34:48], Dq)
    grads['rw_mix'], grads['rw_w0'], grads['rw_a0'], grads['sc_conv'] = (
        sharded[0:6], sharded[6:8], sharded[8:10], sharded[10:13])

    outs_g, outs_d, outs_m, outs_v = [], [], [], []
    for n in WEIGHTS:
        shape = p[n].shape
        w2d, m2d, v2d = view(n)
        if n in done:
            g2d, d_, m_, v_ = done[n]
        elif n in gsh:
            g2d, d_, m_, v_, _ = adamw_halves("adam_" + n, w2d, m2d, v2d, gsh[n][0], gsh[n][1], ci_arr)
        else:
            g2d = _view2d(n, grads[n].reshape(shape))
            d_, m_, v_ = adamw("adam_" + n, w2d, g2d, m2d, v2d)
        outs_g.append(g2d.reshape(shape))
        outs_d.append(d_.reshape(shape))
        outs_m.append(m_.reshape(shape))
        outs_v.append(v_.reshape(shape))
    return (loss, grad_x.reshape(1, T, D), *outs_g, *outs_d, *outs_m, *outs_v)


def kernel(x, c, ctx, c_ctx, norm1_g, norm2_g, ada_w, ada_b, rw_mix, rw_wr, rw_wk, rw_wv, rw_wo, rw_w0, rw_w1, rw_w2, rw_a0, rw_a1, rw_a2, rw_g1, rw_g2, rw_kk, rw_ka, rw_rk, rw_lnw, rw_lnb, sc_win, sc_conv, sc_wout, ffn_w13, ffn_w2, final_g, loss_target, m_c_ctx, m_norm1_g, m_norm2_g, m_ada_w, m_ada_b, m_rw_mix, m_rw_wr, m_rw_wk, m_rw_wv, m_rw_wo, m_rw_w0, m_rw_w1, m_rw_w2, m_rw_a0, m_rw_a1, m_rw_a2, m_rw_g1, m_rw_g2, m_rw_kk, m_rw_ka, m_rw_rk, m_rw_lnw, m_rw_lnb, m_sc_win, m_sc_conv, m_sc_wout, m_ffn_w13, m_ffn_w2, m_final_g, v_c_ctx, v_norm1_g, v_norm2_g, v_ada_w, v_ada_b, v_rw_mix, v_rw_wr, v_rw_wk, v_rw_wv, v_rw_wo, v_rw_w0, v_rw_w1, v_rw_w2, v_rw_a0, v_rw_a1, v_rw_a2, v_rw_g1, v_rw_g2, v_rw_kk, v_rw_ka, v_rw_rk, v_rw_lnw, v_rw_lnb, v_sc_win, v_sc_conv, v_sc_wout, v_ffn_w13, v_ffn_w2, v_final_g):
    values = (x, c, ctx, c_ctx, norm1_g, norm2_g, ada_w, ada_b, rw_mix, rw_wr, rw_wk, rw_wv, rw_wo, rw_w0, rw_w1, rw_w2, rw_a0, rw_a1, rw_a2, rw_g1, rw_g2, rw_kk, rw_ka, rw_rk, rw_lnw, rw_lnb, sc_win, sc_conv, sc_wout, ffn_w13, ffn_w2, final_g, loss_target, m_c_ctx, m_norm1_g, m_norm2_g, m_ada_w, m_ada_b, m_rw_mix, m_rw_wr, m_rw_wk, m_rw_wv, m_rw_wo, m_rw_w0, m_rw_w1, m_rw_w2, m_rw_a0, m_rw_a1, m_rw_a2, m_rw_g1, m_rw_g2, m_rw_kk, m_rw_ka, m_rw_rk, m_rw_lnw, m_rw_lnb, m_sc_win, m_sc_conv, m_sc_wout, m_ffn_w13, m_ffn_w2, m_final_g, v_c_ctx, v_norm1_g, v_norm2_g, v_ada_w, v_ada_b, v_rw_mix, v_rw_wr, v_rw_wk, v_rw_wv, v_rw_wo, v_rw_w0, v_rw_w1, v_rw_w2, v_rw_a0, v_rw_a1, v_rw_a2, v_rw_g1, v_rw_g2, v_rw_kk, v_rw_ka, v_rw_rk, v_rw_lnw, v_rw_lnb, v_sc_win, v_sc_conv, v_sc_wout, v_ffn_w13, v_ffn_w2, v_final_g)
    return _step(dict(zip(INPUTS, values)))
```

```python
import functools
import math

import jax
import jax.numpy as jnp
from jax import lax
from jax.experimental import pallas as pl
from jax.experimental.pallas import tpu as pltpu

F32 = jnp.float32
BF16 = jnp.bfloat16
MESH = pl.DeviceIdType.MESH

GRID_W = 64
HEAD = 64
LANES = 128
N_CHIPS = 4
N_DEV = 8
NORM_EPS = 1e-6
GN_EPS = 64e-5
ADAM_LR, ADAM_B1, ADAM_B2, ADAM_EPS, ADAM_WD, ADAM_STEP = 0.001, 0.9, 0.999, 1e-08, 0.01, 10
VMEM_LIMIT = 56 * 1024 * 1024
HI = lax.Precision.HIGHEST
WEIGHTS = ['c_ctx', 'norm1_g', 'norm2_g', 'ada_w', 'ada_b', 'rw_mix', 'rw_wr', 'rw_wk', 'rw_wv', 'rw_wo', 'rw_w0',
           'rw_w1', 'rw_w2', 'rw_a0', 'rw_a1', 'rw_a2', 'rw_g1', 'rw_g2', 'rw_kk', 'rw_ka', 'rw_rk', 'rw_lnw',
           'rw_lnb', 'sc_win', 'sc_conv', 'sc_wout', 'ffn_w13', 'ffn_w2', 'final_g']
INPUTS = (['x', 'c', 'ctx'] + WEIGHTS + ['loss_target'] + ['m_' + w for w in WEIGHTS]
          + ['v_' + w for w in WEIGHTS])


def _cparams(**kw):
    return pltpu.CompilerParams(vmem_limit_bytes=VMEM_LIMIT, **kw)


def _pick(dim, cands):
    for c in cands:
        if dim % c == 0:
            return c
    return dim


def _place():
    return lax.axis_index("x"), lax.axis_index("y"), lax.axis_index("c")


_TILE_M = (1024, 768, 512, 1408, 256, 128)
_TILE_N = (1408, 1024, 768, 512, 256, 128)
_TILE_K = (2816, 2304, 2048, 1536, 1408, 1152, 1024, 768, 704, 512, 256, 128)
MM_TILE_BYTES = 40 * 1024 * 1024


def _pick_k(unit, tm, tn, a_dtype, b_dtype, o_dtype):
    ab, bb, ob = (jnp.dtype(d).itemsize for d in (a_dtype, b_dtype, o_dtype))
    for tk in _TILE_K:
        if unit % tk == 0 and 2 * (tm * tk * ab + tk * tn * bb) + 2 * tm * tn * ob + tm * tn * 4 <= MM_TILE_BYTES:
            return tk
    return unit


class Stacked:
    def __init__(self, arr, kind, r, layer=0):
        self.arr, self.kind, self.r, self.layer = arr, kind, r, layer
        self.c = arr.shape[2]
        self.shape = {"row": (N_CHIPS * r, self.c), "col": (r, N_CHIPS * self.c), "layer": (r, self.c)}[kind]

    def at(self, layer):
        return Stacked(self.arr, self.kind, self.r, layer)

    def spec(self, t0, t1, swap):
        r, c, layer = self.r, self.c, self.layer
        per_r, per_c = r // t0, c // t1
        assert r % t0 == 0 and c % t1 == 0
        kind = self.kind

        def index(i, j, k):
            ri, ci = (j, k) if swap else (k, j)
            if kind == "row":
                return (ri // per_r, layer * per_r + ri % per_r, ci)
            if kind == "layer":
                return (layer, ri, ci)
            return (ci // per_c, layer * per_r + ri, ci % per_c)

        return pl.BlockSpec((None, t0, t1), index)


def _mm_body(dims, nk, a_ref, b_ref, o_ref, acc_ref):
    if nk == 1:
        o_ref[...] = lax.dot_general(a_ref[...].astype(BF16), b_ref[...].astype(BF16), (dims, ((), ())),
                                     preferred_element_type=F32).astype(o_ref.dtype)
        return
    k = pl.program_id(2)

    @pl.when(k == 0)
    def _():
        acc_ref[...] = jnp.zeros_like(acc_ref)

    acc_ref[...] += lax.dot_general(a_ref[...].astype(BF16), b_ref[...].astype(BF16), (dims, ((), ())),
                                    preferred_element_type=F32)

    @pl.when(k == nk - 1)
    def _():
        o_ref[...] = acc_ref[...].astype(o_ref.dtype)


def _mm_call(name, dims, grid, in_specs, out_spec, out_shape, acc_shape, operands, side=None):
    acc = pltpu.VMEM(acc_shape if grid[2] > 1 else (8, LANES), F32)
    if side is None:
        return pl.pallas_call(
            functools.partial(_mm_body, dims, grid[2]), name=name, grid=grid, in_specs=in_specs, out_specs=out_spec,
            out_shape=out_shape, scratch_shapes=[acc],
            compiler_params=_cparams(dimension_semantics=("parallel", "parallel", "arbitrary")),
        )(*operands)
    s_in, s_out, s_shape, s_scratch, s_alias, s_ops = _side_call_args(side, 2, 1)
    res = pl.pallas_call(
        _with_side(functools.partial(_mm_body, dims, grid[2]), side, 2, 1, 1, grid), name=name, grid=grid,
        in_specs=in_specs + s_in, out_specs=[out_spec] + s_out, out_shape=[out_shape] + s_shape,
        scratch_shapes=[acc] + s_scratch, input_output_aliases=s_alias,
        compiler_params=_cparams(dimension_semantics=("arbitrary", "arbitrary", "arbitrary")),
    )(*operands, *s_ops)
    return res[0], list(res[1:])


def mm_nn(name, a, b, out_dtype=F32):
    M, K = a.shape
    st = isinstance(b, Stacked)
    N = b.shape[1]
    tm = _pick(M, _TILE_M)
    tn = _pick(b.c if st and b.kind == "col" else N, _TILE_N)
    tk = _pick_k(b.r if st else K, tm, tn, a.dtype, b.arr.dtype if st else b.dtype, out_dtype)
    b_spec = b.spec(tk, tn, False) if st else pl.BlockSpec((tk, tn), lambda i, j, k: (k, j))
    return _mm_call(name, ((1,), (0,)), (M // tm, N // tn, K // tk),
                    [pl.BlockSpec((tm, tk), lambda i, j, k: (i, k)), b_spec],
                    pl.BlockSpec((tm, tn), lambda i, j, k: (i, j)), jax.ShapeDtypeStruct((M, N), out_dtype),
                    (tm, tn), (a, b.arr if st else b))


def mm_nt(name, a, b, out_dtype=F32, side=None):
    M, N = a.shape
    st = isinstance(b, Stacked)
    K = b.shape[0]
    tm = _pick(M, _TILE_M)
    to = _pick(b.r if st else K, _TILE_N)
    tc = _pick_k(b.c if st and b.kind == "col" else N, tm, to, a.dtype, b.arr.dtype if st else b.dtype, out_dtype)
    b_spec = b.spec(to, tc, True) if st else pl.BlockSpec((to, tc), lambda i, j, k: (j, k))
    return _mm_call(name, ((1,), (1,)), (M // tm, K // to, N // tc),
                    [pl.BlockSpec((tm, tc), lambda i, j, k: (i, k)), b_spec],
                    pl.BlockSpec((tm, to), lambda i, j, k: (i, j)), jax.ShapeDtypeStruct((M, K), out_dtype),
                    (tm, to), (a, b.arr if st else b), side)


def mm_tn(name, a, b, kind=None, side=None):
    R, M = a.shape
    N = b.shape[1]
    r, c = (M // N_CHIPS, N) if kind == "row" else (M, N // N_CHIPS) if kind == "col" else (M, N)
    tm, tn = _pick(r, _TILE_M), _pick(c, _TILE_N)
    tk = _pick_k(R, tm, tn, a.dtype, b.dtype, F32)
    if kind:
        per_r, per_c = r // tm, c // tn
        if kind == "row":
            o_spec = pl.BlockSpec((None, tm, tn), lambda i, j, k: (i // per_r, i % per_r, j))
        else:
            o_spec = pl.BlockSpec((None, tm, tn), lambda i, j, k: (j // per_c, i, j % per_c))
        o_shape = jax.ShapeDtypeStruct((N_CHIPS, r, c), F32)
    else:
        o_spec = pl.BlockSpec((tm, tn), lambda i, j, k: (i, j))
        o_shape = jax.ShapeDtypeStruct((M, N), F32)
    return _mm_call(name, ((0,), (0,)), (M // tm, N // tn, R // tk),
                    [pl.BlockSpec((tk, tm), lambda i, j, k: (k, i)), pl.BlockSpec((tk, tn), lambda i, j, k: (k, j))],
                    o_spec, o_shape, (tm, tn), (a, b), side)


def _shifted(o):
    return lambda i: (i + o, 0)


def row_call(name, f, rows, consts, outs, *, tr, sr, offs=None):
    offs = offs or [0] * len(rows)
    n_rows = min(r.shape[0] - o * tr for r, o in zip(rows, offs))
    nr, nc = len(rows), len(consts)

    def body(*refs):
        row_refs, const_refs, out_refs = refs[:nr], refs[nr:nr + nc], refs[nr + nc:]
        i = pl.program_id(0)
        cvals = [r[...] for r in const_refs]

        def step(j, carry):
            sl = pl.ds(pl.multiple_of(j * sr, sr), sr)
            res = f(i, *[r[sl, :] for r in row_refs], *cvals)
            for o, v in zip(out_refs, res):
                o[sl, :] = v.astype(o.dtype)
            return carry

        lax.fori_loop(0, tr // sr, step, 0)

    in_specs = [pl.BlockSpec((tr, r.shape[1]), _shifted(o)) for r, o in zip(rows, offs)]
    in_specs += [pl.BlockSpec(c.shape, lambda i: (0, 0)) for c in consts]
    return pl.pallas_call(
        body, name=name, grid=(n_rows // tr,), in_specs=in_specs,
        out_specs=[pl.BlockSpec((tr, w), lambda i: (i, 0)) for w, _ in outs],
        out_shape=[jax.ShapeDtypeStruct((n_rows, w), dt) for w, dt in outs],
        compiler_params=_cparams(dimension_semantics=("parallel",)),
    )(*rows, *consts)


def row_vjp(name, f, rows, consts, cots, *, row_mask, const_mask, tr, sr, offs=None, bf16_rows=(), side=None):
    offs = offs or [0] * len(rows)
    n_rows = min(r.shape[0] - o * tr for r, o in zip(rows, offs))
    nr, nc = len(rows), len(consts)
    cot_in = [c for c in cots if c is not None]
    nct = len(cot_in)
    d_rows = [i for i in range(nr) if row_mask[i]]
    d_consts = [i for i in range(nc) if const_mask[i]]

    def body(*refs):
        row_refs, const_refs = refs[:nr], refs[nr:nr + nc]
        cot_refs = refs[nr + nc:nr + nc + nct]
        drow_refs = refs[nr + nc + nct:nr + nc + nct + len(d_rows)]
        dconst_refs = refs[nr + nc + nct + len(d_rows):]
        i = pl.program_id(0)

        @pl.when(i == 0)
        def _():
            for r in dconst_refs:
                r[...] = jnp.zeros_like(r)

        cvals = [r[...] for r in const_refs]

        def step(j, carry):
            sl = pl.ds(pl.multiple_of(j * sr, sr), sr)
            rvals = [r[sl, :] for r in row_refs]

            def g(*diff):
                rv, cv = list(rvals), list(cvals)
                for idx, val in zip(d_rows, diff[:len(d_rows)]):
                    rv[idx] = val
                for idx, val in zip(d_consts, diff[len(d_rows):]):
                    cv[idx] = val
                return f(i, *rv, *cv)

            primals = [rvals[idx].astype(F32) for idx in d_rows] + [cvals[idx] for idx in d_consts]
            res, vjp = jax.vjp(g, *primals)
            it = iter(cot_refs)
            cts = tuple(jnp.zeros_like(o) if c is None else next(it)[sl, :].astype(o.dtype) for o, c in zip(res, cots))
            grads = vjp(cts)
            for r, val in zip(drow_refs, grads[:len(d_rows)]):
                r[sl, :] = val.astype(r.dtype)
            for r, val in zip(dconst_refs, grads[len(d_rows):]):
                r[...] += val
            return carry

        lax.fori_loop(0, tr // sr, step, 0)

    in_specs = [pl.BlockSpec((tr, r.shape[1]), _shifted(o)) for r, o in zip(rows, offs)]
    in_specs += [pl.BlockSpec(c.shape, lambda i: (0, 0)) for c in consts]
    in_specs += [pl.BlockSpec((tr, c.shape[1]), lambda i: (i, 0)) for c in cot_in]
    out_specs = [pl.BlockSpec((tr, rows[i].shape[1]), lambda i: (i, 0)) for i in d_rows]
    out_specs += [pl.BlockSpec(consts[i].shape, lambda i: (0, 0)) for i in d_consts]
    out_shape = [jax.ShapeDtypeStruct((n_rows, rows[i].shape[1]), BF16 if i in bf16_rows else F32) for i in d_rows]
    out_shape += [jax.ShapeDtypeStruct(consts[i].shape, F32) for i in d_consts]
    n_in, n_out = nr + nc + nct, len(d_rows) + len(d_consts)
    s_in, s_out, s_shape, s_scratch, s_alias, s_ops = _side_call_args(side, n_in, n_out)
    res = pl.pallas_call(
        _with_side(body, side, n_in, n_out, 0, n_rows // tr), name=name, grid=(n_rows // tr,),
        in_specs=in_specs + s_in, out_specs=out_specs + s_out, out_shape=out_shape + s_shape,
        scratch_shapes=s_scratch, input_output_aliases=s_alias,
        compiler_params=_cparams(dimension_semantics=("arbitrary",)),
    )(*rows, *consts, *cot_in, *s_ops)
    if side is None:
        return list(res[:len(d_rows)]), list(res[len(d_rows):])
    return list(res[:len(d_rows)]), list(res[len(d_rows):n_out]), list(res[n_out:])


def _sigmoid(x):
    return 1.0 / (1.0 + jnp.exp(-x))


def _softplus(u):
    return jnp.maximum(u, 0.0) + jnp.log(1.0 + jnp.exp(-jnp.abs(u)))


def _rms(x, g):
    ms = jnp.sum(x * x, axis=-1, keepdims=True) * (1.0 / x.shape[-1])
    return x * lax.rsqrt(ms + NORM_EPS) * g


def _hsum_impl(x, ones2):
    rows, width = x.shape
    nch = width // LANES
    xs = jnp.concatenate([x[:, j * LANES:(j + 1) * LANES] for j in range(nch)], axis=0)
    hi = xs.astype(BF16)
    lo = (xs - hi.astype(F32)).astype(BF16)
    ys = jnp.dot(jnp.concatenate([hi, lo], axis=1), ones2, preferred_element_type=F32)
    return jnp.concatenate([ys[j * rows:(j + 1) * rows] for j in range(nch)], axis=1)


@jax.custom_vjp
def _hsum(x, ones2):
    return _hsum_impl(x, ones2)


def _hsum_fwd(x, ones2):
    return _hsum_impl(x, ones2), ones2


def _hsum_bwd(ones2, g):
    return _hsum_impl(g, ones2), jnp.zeros_like(ones2)


_hsum.defvjp(_hsum_fwd, _hsum_bwd)


def f_silu(i, x):
    return (x * _sigmoid(x),)


def f_sigmoid(i, x):
    return (_sigmoid(x),)


def f_tanh(i, x):
    return (jnp.tanh(x),)


def f_norm_mod(n_ctx_tiles, i, xin, g, sh_c, sc_c, sh_x, sc_x):
    is_x = i >= n_ctx_tiles
    sh = jnp.where(is_x, sh_x, sh_c)
    sc = jnp.where(is_x, sc_x, sc_c)
    return (_rms(xin, g) * (1.0 + sc) + sh,)


def f_res_norm_mod(i, x, y, gt, g, sh, sc):
    x1 = x + gt * y
    return x1, _rms(x1, g) * (1.0 + sc) + sh


def f_post(i, k, zw0, zw1, za0, za1, kkp, ka, w00, w01, a00, a01, ones2):
    kq = k * kkp
    kk = kq / jnp.maximum(jnp.sqrt(_hsum(kq * kq, ones2)), 1e-12)

    def direction(zw, za, w0, a0):
        log_w = -_softplus(-(w0 + zw)) - 0.5
        a = _sigmoid(a0 + za)
        return jnp.exp(-jnp.exp(log_w)), k * (1.0 + (a - 1.0) * ka), kk * a

    dec0, kd0, bb0 = direction(zw0, za0, w00, a00)
    dec1, kd1, bb1 = direction(zw1, za1, w01, a01)
    return -kk, -kk, dec0, dec1, kd0, kd1, bb0, bb1, kd0 + kd1


def f_post_fwd(*a):
    return f_post(*a)[1:]


def f_readout(i, y0, y1, r, ksum, v, g, rk, lnw, lnb, ones2):
    y = y0 + y1
    yc = y - _hsum(y, ones2) * (1.0 / HEAD)
    var = _hsum(yc * yc, ones2) * (1.0 / HEAD)
    o = yc * lax.rsqrt(var + GN_EPS) * lnw + lnb
    o = o + _hsum(r * ksum * rk, ones2) * v
    return (o * g,)


def f_sum(i, *xs):
    acc = xs[0].astype(F32)
    for x in xs[1:]:
        acc = acc + x.astype(F32)
    return (acc,)


def sum_cast(name, arrs, dtype, tr, offs=None):
    return row_call(name, f_sum, arrs, [], [(arrs[0].shape[1], dtype)], tr=tr, sr=16, offs=offs)[0]


def swiglu_fwd(name, ab, tr):
    T, F2 = ab.shape
    F = F2 // 2
    sr = 16

    def body(ab_ref, o_ref):
        def step(j, carry):
            sl = pl.ds(pl.multiple_of(j * sr, sr), sr)
            a, b = ab_ref[sl, :F], ab_ref[sl, F:]
            o_ref[sl, :] = (a * _sigmoid(a) * b).astype(o_ref.dtype)
            return carry

        lax.fori_loop(0, tr // sr, step, 0)

    return pl.pallas_call(
        body, name=name, grid=(T // tr,), in_specs=[pl.BlockSpec((tr, F2), lambda i: (i, 0))],
        out_specs=pl.BlockSpec((tr, F), lambda i: (i, 0)), out_shape=jax.ShapeDtypeStruct((T, F), BF16),
        compiler_params=_cparams(dimension_semantics=("parallel",)),
    )(ab)


def swiglu_bwd(name, ab, dsw, tr):
    T, F2 = ab.shape
    F = F2 // 2
    sr = 16

    def body(ab_ref, d_ref, o_ref):
        def step(j, carry):
            sl = pl.ds(pl.multiple_of(j * sr, sr), sr)
            a, b, d = ab_ref[sl, :F], ab_ref[sl, F:], d_ref[sl, :]
            sg = _sigmoid(a)
            o_ref[sl, :F] = (d * b * (sg + a * sg * (1.0 - sg))).astype(o_ref.dtype)
            o_ref[sl, F:] = (d * a * sg).astype(o_ref.dtype)
            return carry

        lax.fori_loop(0, tr // sr, step, 0)

    return pl.pallas_call(
        body, name=name, grid=(T // tr,),
        in_specs=[pl.BlockSpec((tr, F2), lambda i: (i, 0)), pl.BlockSpec((tr, F), lambda i: (i, 0))],
        out_specs=pl.BlockSpec((tr, F2), lambda i: (i, 0)), out_shape=jax.ShapeDtypeStruct((T, F2), BF16),
        compiler_params=_cparams(dimension_semantics=("parallel",)),
    )(ab, dsw)


def final_call(name, x3, f1, gt, fg, tgt, tr):
    T, D = x3.shape
    sr = 16

    def f(x, y, gtv, g, t):
        err = _rms(x + gtv * y, g) - t
        return 0.5 * jnp.sum(err * err) * (1.0 / D)

    def body(x_ref, y_ref, gt_ref, g_ref, t_ref, dx_ref, dy_ref, dgt_ref, dg_ref, loss_ref):
        @pl.when(pl.program_id(0) == 0)
        def _():
            dgt_ref[...] = jnp.zeros_like(dgt_ref)
            dg_ref[...] = jnp.zeros_like(dg_ref)
            loss_ref[...] = jnp.zeros_like(loss_ref)

        def step(j, carry):
            sl = pl.ds(pl.multiple_of(j * sr, sr), sr)
            val, vjp = jax.vjp(lambda x, y, a, b: f(x, y, a, b, t_ref[sl, :]), x_ref[sl, :], y_ref[sl, :],
                               gt_ref[...], g_ref[...])
            dx, dy, dgt, dg = vjp(jnp.ones((), F32))
            dx_ref[sl, :] = dx
            dy_ref[sl, :] = dy.astype(dy_ref.dtype)
            dgt_ref[...] += dgt
            dg_ref[...] += dg
            loss_ref[...] += jnp.full(loss_ref.shape, val, F32)
            return carry

        lax.fori_loop(0, tr // sr, step, 0)

    row = pl.BlockSpec((tr, D), lambda i: (i, 0))
    vec = pl.BlockSpec((1, D), lambda i: (0, 0))
    return pl.pallas_call(
        body, name=name, grid=(T // tr,), in_specs=[row, row, vec, vec, row],
        out_specs=[row, row, vec, vec, pl.BlockSpec((8, LANES), lambda i: (0, 0))],
        out_shape=[jax.ShapeDtypeStruct((T, D), F32), jax.ShapeDtypeStruct((T, D), BF16)]
        + [jax.ShapeDtypeStruct((1, D), F32)] * 2
        + [jax.ShapeDtypeStruct((8, LANES), F32)],
        compiler_params=_cparams(dimension_semantics=("arbitrary",)),
    )(x3, f1, gt, fg, tgt)


def _tshift(x, kind, period):
    n = x.shape[0]
    t = lax.broadcasted_iota(jnp.int32, x.shape, 0)
    if kind == 0:
        return jnp.where((t & (period - 1)) == 0, 0.0, pltpu.roll(x, 1, 0))
    if kind == 1:
        return jnp.where(((t & (period - 1)) == period - 1) | (t == n - 1), 0.0, pltpu.roll(x, n - 1, 0))
    if kind == 2:
        return jnp.where(t < GRID_W, 0.0, pltpu.roll(x, GRID_W, 0))
    return jnp.where(t >= n - GRID_W, 0.0, pltpu.roll(x, n - GRID_W, 0))


def _pow2_at_least(n):
    return 1 << (n - 1).bit_length()


def _shift_into(dst_ref, h_ref, n_ctx, cb, D, transpose):
    j = pl.program_id(0)
    quarter = (j * cb * 4) // D
    half = (j * cb * 2) // D
    flip = 1 if transpose else 0
    for q in range(4):
        @pl.when(quarter == q)
        def _(q=q):
            dst_ref[n_ctx:, :] = _tshift(h_ref[n_ctx:, :], q ^ flip, GRID_W)
    for q in range(2):
        @pl.when(half == q)
        def _(q=q):
            dst_ref[:n_ctx, :] = _tshift(h_ref[:n_ctx, :], q ^ flip, _pow2_at_least(n_ctx))


def mix_fwd(name, h, mix, n_ctx):
    R, D = h.shape
    cb = LANES

    def body(h_ref, mix_ref, *rest):
        outs, hs_ref = rest[:6], rest[6]
        _shift_into(hs_ref, h_ref, n_ctx, cb, D, False)
        hv = h_ref[...]
        xx = hs_ref[...] - hv
        for m in range(6):
            outs[m][...] = (hv + xx * mix_ref[m:m + 1, :]).astype(BF16)

    col = pl.BlockSpec((R, cb), lambda j: (0, j))
    return pl.pallas_call(
        body, name=name, grid=(D // cb,), in_specs=[col, pl.BlockSpec((mix.shape[0], cb), lambda j: (0, j))],
        out_specs=[col] * 6, out_shape=[jax.ShapeDtypeStruct((R, D), BF16)] * 6,
        scratch_shapes=[pltpu.VMEM((R, cb), F32)],
        compiler_params=_cparams(dimension_semantics=("parallel",)),
    )(h, mix)


def mix_bwd(name, h, mix, cots, slots, n_ctx):
    R, D = h.shape
    cb = LANES
    nc = len(cots)

    def body(h_ref, mix_ref, *rest):
        cot_refs, dh_ref, dmix_ref, hs_ref, dxx_ref = rest[:nc], rest[nc], rest[nc + 1], rest[nc + 2], rest[nc + 3]
        _shift_into(hs_ref, h_ref, n_ctx, cb, D, False)
        xx = hs_ref[...] - h_ref[...]
        per_slot = [None] * 6
        for cref, m in zip(cot_refs, slots):
            per_slot[m] = cref[...] if per_slot[m] is None else per_slot[m] + cref[...]
        dh = jnp.zeros((R, cb), F32)
        dxx = jnp.zeros((R, cb), F32)
        rows = []
        for m in range(6):
            d = per_slot[m]
            dh = dh + d
            dxx = dxx + d * mix_ref[m:m + 1, :]
            rows.append(jnp.sum(d * xx, axis=0, keepdims=True))
        dmix_ref[...] = jnp.concatenate(rows + [jnp.zeros((2, cb), F32)], axis=0)
        dxx_ref[...] = dxx
        _shift_into(hs_ref, dxx_ref, n_ctx, cb, D, True)
        dh_ref[...] = dh - dxx + hs_ref[...]

    col = pl.BlockSpec((R, cb), lambda j: (0, j))
    return pl.pallas_call(
        body, name=name, grid=(D // cb,), in_specs=[col, pl.BlockSpec((mix.shape[0], cb), lambda j: (0, j))] + [col] * nc,
        out_specs=[col, pl.BlockSpec((8, cb), lambda j: (0, j))],
        out_shape=[jax.ShapeDtypeStruct((R, D), F32), jax.ShapeDtypeStruct((8, D), F32)],
        scratch_shapes=[pltpu.VMEM((R, cb), F32), pltpu.VMEM((R, cb), F32)],
        compiler_params=_cparams(dimension_semantics=("parallel",)),
    )(h, mix, *cots)


def _conv_parts(gb_ref, gc_ref, u_ref, cw_ref):
    T = gb_ref.shape[0]
    z = gc_ref[...] * u_ref[...]
    zp, zn = _tshift(z, 0, _pow2_at_least(T)), _tshift(z, 1, _pow2_at_least(T))
    conv = zp * cw_ref[0:1, :] + z * cw_ref[1:2, :] + zn * cw_ref[2:3, :]
    return z, zp, zn, conv


def conv_fwd(name, gcu, cw):
    T, D3 = gcu.shape
    D = D3 // 3
    cb = LANES
    nb = D // cb

    def body(gb_ref, gc_ref, u_ref, cw_ref, o_ref):
        _, _, _, conv = _conv_parts(gb_ref, gc_ref, u_ref, cw_ref)
        o_ref[...] = (gb_ref[...] * conv).astype(BF16)

    def part(p):
        return pl.BlockSpec((T, cb), lambda j: (0, j + p * nb))

    return pl.pallas_call(
        body, name=name, grid=(nb,),
        in_specs=[part(0), part(1), part(2), pl.BlockSpec((cw.shape[0], cb), lambda j: (0, j))],
        out_specs=pl.BlockSpec((T, cb), lambda j: (0, j)), out_shape=jax.ShapeDtypeStruct((T, D), BF16),
        compiler_params=_cparams(dimension_semantics=("parallel",)),
    )(gcu, gcu, gcu, cw)


def conv_bwd(name, gcu, cw, dp):
    T, D3 = gcu.shape
    D = D3 // 3
    cb = LANES
    nb = D // cb

    def body(gb_ref, gc_ref, u_ref, cw_ref, dp_ref, o_ref, dcw_ref):
        part = pl.program_id(1)
        z, zp, zn, conv = _conv_parts(gb_ref, gc_ref, u_ref, cw_ref)
        dpv = dp_ref[...]
        dconv = dpv * gb_ref[...]
        period = _pow2_at_least(T)
        dz = (_tshift(dconv * cw_ref[0:1, :], 1, period) + dconv * cw_ref[1:2, :]
              + _tshift(dconv * cw_ref[2:3, :], 0, period))

        @pl.when(part == 0)
        def _():
            o_ref[...] = (dpv * conv).astype(o_ref.dtype)
            dcw_ref[...] = jnp.concatenate(
                [jnp.sum(dconv * s, axis=0, keepdims=True) for s in (zp, z, zn)] + [jnp.zeros((5, cb), F32)], axis=0)

        @pl.when(part == 1)
        def _():
            o_ref[...] = (dz * u_ref[...]).astype(o_ref.dtype)

        @pl.when(part == 2)
        def _():
            o_ref[...] = (dz * gc_ref[...]).astype(o_ref.dtype)

    def part_spec(p):
        return pl.BlockSpec((T, cb), lambda j, q: (0, j + p * nb))

    return pl.pallas_call(
        body, name=name, grid=(nb, 3),
        in_specs=[part_spec(0), part_spec(1), part_spec(2), pl.BlockSpec((cw.shape[0], cb), lambda j, q: (0, j)),
                  pl.BlockSpec((T, cb), lambda j, q: (0, j))],
        out_specs=[pl.BlockSpec((T, cb), lambda j, q: (0, j + q * nb)), pl.BlockSpec((8, cb), lambda j, q: (0, j))],
        out_shape=[jax.ShapeDtypeStruct((T, D3), BF16), jax.ShapeDtypeStruct((8, D), F32)],
        compiler_params=_cparams(dimension_semantics=("arbitrary", "arbitrary")),
    )(gcu, gcu, gcu, cw, dp)


SCAN_TC = 8


def _scan_consts():
    rows = lax.broadcasted_iota(jnp.int32, (HEAD, LANES), 0)
    cols = lax.broadcasted_iota(jnp.int32, (HEAD, LANES), 1)
    eye = rows == (cols & (HEAD - 1))
    r2 = lax.broadcasted_iota(jnp.int32, (2 * LANES, LANES), 0)
    c2 = lax.broadcasted_iota(jnp.int32, (2 * LANES, LANES), 1)
    ones2 = (((r2 & (LANES - 1)) >= HEAD) == (c2 >= HEAD)).astype(BF16)
    return eye, ones2, ones2[:LANES]


SCAN_ROW_CHUNKS = 4


def _chunks_of_heads(hp):
    per = max(1, hp // SCAN_ROW_CHUNKS)
    return [range(lo, lo + per) for lo in range(0, hp, per)]


def _rows_of(heads):
    return pl.ds(heads[0] * HEAD, len(heads) * HEAD)


def _split2(p):
    hi = p.astype(BF16)
    lo = (p - hi.astype(F32)).astype(BF16)
    return jnp.concatenate([hi, lo], axis=1)


def _head_rows(h):
    return pl.ds(h * HEAD, HEAD)


def _expand_into(dst, p1_ref, row_ref, t, hp, eye, ones1):
    for h in range(hp):
        p1_ref[_head_rows(h), :] = jnp.where(eye, row_ref[t, h:h + 1, :], 0.0).astype(BF16)
    dst[...] = jnp.dot(p1_ref[...], ones1, preferred_element_type=F32)


def _colsum_store(ref, t, h, x):
    ref[t, pl.ds(h, 1), :] = jnp.sum(x, axis=0, keepdims=True)


def _order(i, n_ctx, n_all, rev):
    if not rev:
        return i
    return jnp.where(i < n_ctx, n_ctx - 1 - i, n_all - 1 - (i - n_ctx))


def _with_side(work, side, n_in, n_out, n_scratch, grid):
    if side is None:
        return work
    nsi, nso = len(side.ins), len(side.out_shapes)
    grid = (grid,) if isinstance(grid, int) else tuple(grid)

    def at(which):
        cond = None
        for d, g in enumerate(grid):
            c = pl.program_id(d) == (0 if which == "first" else g - 1)
            cond = c if cond is None else cond & c
        return cond

    def body(*refs):
        ins, side_in = refs[:n_in], refs[n_in:n_in + nsi]
        outs = refs[n_in + nsi:n_in + nsi + n_out]
        side_out = refs[n_in + nsi + n_out:n_in + nsi + n_out + nso]
        scratch = refs[n_in + nsi + n_out + nso:]

        @pl.when(at("first"))
        def _():
            side.start(side_in, side_out, scratch[n_scratch:])

        work(*ins, *outs, *scratch[:n_scratch])

        @pl.when(at("last"))
        def _():
            side.finish(side_in, side_out, scratch[n_scratch:])

    return body


def _side_call_args(side, n_in, n_out):
    if side is None:
        return [], [], [], [], {}, []
    aliases = {n_in + i: n_out + o for i, o in side.aliases.items()}
    return ([ANY] * len(side.ins), [ANY] * len(side.out_shapes), list(side.out_shapes), list(side.sems), aliases,
            list(side.ins))


def scan_fwd(name, r, w, k, v, a, b, n_ctx_rows, rev, side=None):
    R, D = r.shape
    hp, tc = D // LANES, SCAN_TC
    n_all, n_ctx = R // tc, n_ctx_rows // tc
    ins = [t.reshape(R, hp, LANES) for t in (r, w, k, v, a, b)]

    def work(r_ref, w_ref, k_ref, v_ref, a_ref, b_ref, y_ref, st_ref, s_ref, ve_ref, sa_ref, p_ref, p1_ref, ys_ref):
        @pl.when(pl.program_id(0) == 0)
        def _():
            s_ref[...] = jnp.zeros_like(s_ref)

        eye, ones2, ones1 = _scan_consts()

        def row_of(q):
            return tc - 1 - q if rev else q

        def expand(q):
            _expand_into(ve_ref.at[q], p1_ref.at[q % 2], v_ref, row_of(q), hp, eye, ones1)

        def advance(q):
            t, prev_ref = row_of(q), (s_ref if q == 0 else st_ref.at[q - 1])
            for heads in _chunks_of_heads(hp):
                rows = _rows_of(heads)
                for h in heads:
                    p_ref[q % 2, _head_rows(h), :] = _split2(prev_ref[_head_rows(h), :] * a_ref[t, h:h + 1, :])
                sa_ref[q % 2, rows, :] = jnp.dot(p_ref[q % 2, rows, :], ones2, preferred_element_type=F32)
                for h in heads:
                    hr_ = _head_rows(h)
                    st_ref[q, hr_, :] = (prev_ref[hr_, :] * w_ref[t, h:h + 1, :]
                                         + sa_ref[q % 2, hr_, :] * b_ref[t, h:h + 1, :]
                                         + ve_ref[q, hr_, :] * k_ref[t, h:h + 1, :])

        def readout(q):
            t = row_of(q)
            for h in range(hp):
                p1_ref[2 + q % 2, _head_rows(h), :] = (st_ref[q, _head_rows(h), :]
                                                       * r_ref[t, h:h + 1, :]).astype(BF16)
            ys_ref[q % 2] = jnp.dot(p1_ref[2 + q % 2], ones1, preferred_element_type=F32)
            for h in range(hp):
                _colsum_store(y_ref, t, h, jnp.where(eye, ys_ref[q % 2, _head_rows(h), :], 0.0))

        expand(0)
        for q in range(tc):
            if q + 1 < tc:
                expand(q + 1)
            advance(q)
            if q > 0:
                readout(q - 1)
        readout(tc - 1)
        s_ref[...] = st_ref[tc - 1]

    row_spec = pl.BlockSpec((tc, hp, LANES), lambda i: (_order(i, n_ctx, n_all, rev), 0, 0))
    n = hp * HEAD
    s_in, s_out, s_shape, s_scratch, s_alias, s_ops = _side_call_args(side, 6, 2)
    y, st, *side_res = pl.pallas_call(
        _with_side(work, side, 6, 2, 6, n_all), name=name, grid=(n_all,), in_specs=[row_spec] * 6 + s_in,
        out_specs=[row_spec, pl.BlockSpec((tc, n, LANES), lambda i: (i, 0, 0))] + s_out,
        out_shape=[jax.ShapeDtypeStruct((R, hp, LANES), F32), jax.ShapeDtypeStruct((R, n, LANES), F32)] + s_shape,
        scratch_shapes=[pltpu.VMEM((n, LANES), F32), pltpu.VMEM((tc, n, LANES), F32), pltpu.VMEM((2, n, LANES), F32),
                        pltpu.VMEM((2, n, 2 * LANES), BF16), pltpu.VMEM((4, n, LANES), BF16),
                        pltpu.VMEM((2, n, LANES), F32)] + s_scratch,
        input_output_aliases=s_alias,
        compiler_params=_cparams(dimension_semantics=("arbitrary",)),
    )(*ins, *s_ops)
    return y.reshape(R, D), st, side_res


def scan_bwd(name, r, w, k, v, a, b, dy, st, n_ctx_rows, rev, side=None):
    R, D = r.shape
    hp, tc = D // LANES, SCAN_TC
    n_all, n_ctx = R // tc, n_ctx_rows // tc
    ins = [t.reshape(R, hp, LANES) for t in (r, w, k, v, a, b, dy)]

    def work(r_ref, w_ref, k_ref, v_ref, a_ref, b_ref, dy_ref, st_ref, prev_ref,
             dr_ref, dw_ref, dk_ref, dv_ref, da_ref, db_ref,
             g_ref, s0_ref, ve_ref, dye_ref, sa_ref, gs_ref, tmp_ref, p_ref, p1_ref, tmp2_ref):
        i = pl.program_id(0)

        @pl.when(i == 0)
        def _():
            g_ref[...] = jnp.zeros_like(g_ref)

        eye, ones2, ones1 = _scan_consts()

        @pl.when(i == n_all - 1)
        def _():
            s0_ref[...] = jnp.zeros_like(s0_ref)

        @pl.when(i != n_all - 1)
        def _():
            s0_ref[...] = prev_ref[0]

        def row_of(q):
            return tc - 1 - q if rev else q

        def prev_of(q):
            return s0_ref if q == 0 else st_ref.at[q - 1]

        def before(q):
            t, prev = row_of(q), prev_of(q)
            _expand_into(ve_ref.at[q], p1_ref.at[2 + q % 2], v_ref, t, hp, eye, ones1)
            _expand_into(dye_ref.at[q], p1_ref.at[4 + q % 2], dy_ref, t, hp, eye, ones1)
            for h in range(hp):
                p1_ref[q % 2, _head_rows(h), :] = (prev[_head_rows(h), :] * a_ref[t, h:h + 1, :]).astype(BF16)
            sa_ref[q] = jnp.dot(p1_ref[q % 2], ones1, preferred_element_type=F32)

        def back(q):
            t, prev = row_of(q), prev_of(q)
            for heads in _chunks_of_heads(hp):
                rows = _rows_of(heads)
                for h in heads:
                    hr_ = _head_rows(h)
                    g = g_ref[hr_, :] + dye_ref[q, hr_, :] * r_ref[t, h:h + 1, :]
                    gs_ref[q, hr_, :] = g
                    p_ref[q % 2, hr_, :] = _split2(g * b_ref[t, h:h + 1, :])
                tmp_ref[q % 2, rows, :] = jnp.dot(p_ref[q % 2, rows, :], ones2, preferred_element_type=F32)
                for h in heads:
                    hr_ = _head_rows(h)
                    dsa = tmp_ref[q % 2, hr_, :]
                    _colsum_store(da_ref, t, h, prev[hr_, :] * dsa)
                    g_ref[hr_, :] = gs_ref[q, hr_, :] * w_ref[t, h:h + 1, :] + dsa * a_ref[t, h:h + 1, :]

        def after(q):
            t, prev = row_of(q), prev_of(q)
            for h in range(hp):
                hr_ = _head_rows(h)
                g = gs_ref[q, hr_, :]
                p1_ref[6 + q % 2, hr_, :] = (g * k_ref[t, h:h + 1, :]).astype(BF16)
                _colsum_store(dr_ref, t, h, st_ref[q, hr_, :] * dye_ref[q, hr_, :])
                _colsum_store(dk_ref, t, h, g * ve_ref[q, hr_, :])
                _colsum_store(dw_ref, t, h, g * prev[hr_, :])
                _colsum_store(db_ref, t, h, g * sa_ref[q, hr_, :])
            tmp2_ref[q % 2] = jnp.dot(p1_ref[6 + q % 2], ones1, preferred_element_type=F32)
            for h in range(hp):
                _colsum_store(dv_ref, t, h, jnp.where(eye, tmp2_ref[q % 2, _head_rows(h), :], 0.0))

        before(tc - 1)
        for q in reversed(range(tc)):
            if q > 0:
                before(q - 1)
            back(q)
            if q < tc - 1:
                after(q + 1)
        after(0)

    def pos(i):
        return n_all - 1 - i

    n = hp * HEAD
    row_spec = pl.BlockSpec((tc, hp, LANES), lambda i: (_order(pos(i), n_ctx, n_all, rev), 0, 0))
    big = pltpu.VMEM((tc, n, LANES), F32)
    one = pltpu.VMEM((n, LANES), F32)
    s_in, s_out, s_shape, s_scratch, s_alias, s_ops = _side_call_args(side, 9, 6)
    outs = pl.pallas_call(
        _with_side(work, side, 9, 6, 10, n_all), name=name, grid=(n_all,),
        in_specs=[row_spec] * 7 + [
            pl.BlockSpec((tc, n, LANES), lambda i: (pos(i), 0, 0)),
            pl.BlockSpec((1, n, LANES), lambda i: (jnp.maximum(pos(i) * tc - 1, 0), 0, 0))] + s_in,
        out_specs=[row_spec] * 6 + s_out,
        out_shape=[jax.ShapeDtypeStruct((R, hp, LANES), F32)] * 6 + s_shape,
        scratch_shapes=[one, one, big, big, big, big, pltpu.VMEM((2, n, LANES), F32),
                        pltpu.VMEM((2, n, 2 * LANES), BF16), pltpu.VMEM((8, n, LANES), BF16),
                        pltpu.VMEM((2, n, LANES), F32)] + s_scratch,
        input_output_aliases=s_alias,
        compiler_params=_cparams(dimension_semantics=("arbitrary",)),
    )(*ins, st, st, *s_ops)
    return [o.reshape(R, D) for o in outs[:6]], list(outs[6:])


ANY = pl.BlockSpec(memory_space=pl.ANY)


def _peer(xi, yi, ci, k):
    return (1 - xi if k & 4 else xi, 1 - yi if k & 2 else yi, 1 - ci if k & 1 else ci)


def _rcopy(src, dst, send_sem, recv_sem, dev):
    return pltpu.make_async_remote_copy(src_ref=src, dst_ref=dst, send_sem=send_sem, recv_sem=recv_sem,
                                        device_id=dev, device_id_type=MESH)


def _drain(copies):
    for cp in copies:
        if cp.is_remote:
            cp.wait_send()
        else:
            cp.wait()


def all_gather8(name, x):
    r, c = x.shape

    def body(x_ref, out_ref, send_sems, recv_sems, local_sem):
        xi, yi, ci = _place()

        def blk(p):
            return out_ref.at[4 * p[0] + 2 * p[1] + p[2]]

        me = (xi, yi, ci)
        mine = pltpu.make_async_copy(x_ref, blk(me), local_sem.at[0])
        mine.start()
        sends = [_rcopy(x_ref, blk(me), send_sems.at[k - 1], recv_sems.at[k - 1], _peer(xi, yi, ci, k))
                 for k in range(1, N_DEV)]
        for cp in sends:
            cp.start()
        for k in range(1, N_DEV):
            p = _peer(xi, yi, ci, k)
            _rcopy(x_ref, blk(p), send_sems.at[k - 1], recv_sems.at[k - 1], p).wait_recv()
        for cp in sends:
            cp.wait_send()
        mine.wait()

    vm = pl.BlockSpec(memory_space=pltpu.VMEM)
    return pl.pallas_call(
        body, name=name, in_specs=[vm], out_specs=vm, out_shape=jax.ShapeDtypeStruct((N_DEV, r, c), x.dtype),
        scratch_shapes=[pltpu.SemaphoreType.DMA((N_DEV - 1,)), pltpu.SemaphoreType.DMA((N_DEV - 1,)),
                        pltpu.SemaphoreType.DMA((1,))],
        compiler_params=_cparams(),
    )(x)


def _chips(xi, yi):
    chips = [(1 - xi, yi), (xi, 1 - yi), (1 - xi, 1 - yi)]
    return chips, [2 * cx + cy for cx, cy in chips]


def gather_weights(name, stacked):
    n = len(stacked)

    def body(*refs):
        _gather_start(refs[n:2 * n], refs[2 * n:])
        _gather_finish(refs[n:2 * n], refs[2 * n:])

    return pl.pallas_call(
        body, name=name, in_specs=[ANY] * n, out_specs=[ANY] * n,
        out_shape=[jax.ShapeDtypeStruct(a.shape, a.dtype) for a in stacked],
        input_output_aliases={w: w for w in range(n)},
        scratch_shapes=_gather_sems(n),
        compiler_params=_cparams(),
    )(*stacked)


def _gather_sems(n):
    return [pltpu.SemaphoreType.DMA((n, 6)), pltpu.SemaphoreType.DMA((n, 6))]


def _gather_sends(out, sems):
    send_sems, recv_sems = sems
    xi, yi, ci = _place()
    s = 2 * xi + yi
    chips, _ = _chips(xi, yi)
    sends = []
    for w in range(len(out)):
        hr = out[w].shape[1] // 2
        mine = out[w].at[s, pl.ds(ci * hr, hr)]
        sends += [_rcopy(mine, mine, send_sems.at[w, j], recv_sems.at[w, j], (cx, cy, ci))
                  for j, (cx, cy) in enumerate(chips)]
    return sends


def _gather_start(out, sems):
    for cp in _gather_sends(out, sems):
        cp.start()


def _gather_finish(out, sems):
    send_sems, recv_sems = sems
    xi, yi, ci = _place()
    chips, sidx = _chips(xi, yi)
    sib = (xi, yi, 1 - ci)
    passed = []
    for w in range(len(out)):
        hr = out[w].shape[1] // 2
        for j, (cx, cy) in enumerate(chips):
            blk = out[w].at[sidx[j], pl.ds(ci * hr, hr)]
            _rcopy(blk, blk, send_sems.at[w, j], recv_sems.at[w, j], (cx, cy, ci)).wait_recv()
            fw = _rcopy(blk, blk, send_sems.at[w, 3 + j], recv_sems.at[w, 3 + j], sib)
            fw.start()
            passed.append(fw)
    for w in range(len(out)):
        hr = out[w].shape[1] // 2
        for j in range(3):
            blk = out[w].at[sidx[j], pl.ds((1 - ci) * hr, hr)]
            _rcopy(blk, blk, send_sems.at[w, 3 + j], recv_sems.at[w, 3 + j], sib).wait_recv()
    _drain(_gather_sends(out, sems) + passed)


class Side:
    def __init__(self, ins, out_shapes, aliases, sems, start, finish):
        self.ins, self.out_shapes, self.aliases, self.sems = ins, out_shapes, aliases, sems
        self.start, self.finish = start, finish


def _gather_wait_ici(out, sems):
    send_sems, recv_sems = sems
    xi, yi, ci = _place()
    chips, sidx = _chips(xi, yi)
    for w in range(len(out)):
        hr = out[w].shape[1] // 2
        for j, (cx, cy) in enumerate(chips):
            blk = out[w].at[sidx[j], pl.ds(ci * hr, hr)]
            _rcopy(blk, blk, send_sems.at[w, j], recv_sems.at[w, j], (cx, cy, ci)).wait_recv()
    _drain(_gather_sends(out, sems))


def _pass_copies(out, sems, half_of):
    send_sems, recv_sems = sems
    xi, yi, ci = _place()
    _, sidx = _chips(xi, yi)
    sib = (xi, yi, 1 - ci)
    cps = []
    for w in range(len(out)):
        hr = out[w].shape[1] // 2
        for j in range(3):
            blk = out[w].at[sidx[j], pl.ds(half_of(ci) * hr, hr)]
            cps.append(_rcopy(blk, blk, send_sems.at[w, j], recv_sems.at[w, j], sib))
    return cps


def _pass_start(out, sems):
    for cp in _pass_copies(out, sems, lambda ci: ci):
        cp.start()


def _pass_finish(out, sems):
    for cp in _pass_copies(out, sems, lambda ci: 1 - ci):
        cp.wait_recv()
    _drain(_pass_copies(out, sems, lambda ci: ci))


def gather_side(stacked):
    n = len(stacked)
    return Side(stacked, [jax.ShapeDtypeStruct(a.shape, a.dtype) for a in stacked], {w: w for w in range(n)},
                _gather_sems(n), lambda ins, outs, sems: _gather_start(outs, sems),
                lambda ins, outs, sems: _gather_wait_ici(outs, sems))


def gather_whole_side(stacked):
    n = len(stacked)
    return Side(stacked, [jax.ShapeDtypeStruct(a.shape, a.dtype) for a in stacked], {w: w for w in range(n)},
                _gather_sems(n), lambda ins, outs, sems: _gather_start(outs, sems),
                lambda ins, outs, sems: _gather_finish(outs, sems))


def pass_side(stacked):
    n = len(stacked)
    return Side(stacked, [jax.ShapeDtypeStruct(a.shape, a.dtype) for a in stacked], {w: w for w in range(n)},
                [pltpu.SemaphoreType.DMA((n, 3)), pltpu.SemaphoreType.DMA((n, 3))],
                lambda ins, outs, sems: _pass_start(outs, sems), lambda ins, outs, sems: _pass_finish(outs, sems))


def _rs1_copies(g, out, sems):
    send_sems, recv_sems = sems
    xi, yi, ci = _place()
    return [_rcopy(g[w].at[:, 1 - ci], out[w], send_sems.at[w], recv_sems.at[w], (xi, yi, 1 - ci))
            for w in range(len(g))]


def _rs1_start(g, out, sems):
    for cp in _rs1_copies(g, out, sems):
        cp.start()


def _rs1_finish(g, out, sems):
    for cp in _rs1_copies(g, out, sems):
        cp.wait_recv()
    _drain(_rs1_copies(g, out, sems))


def rs1_side(grads):
    n = len(grads)
    return Side(grads, [jax.ShapeDtypeStruct((N_CHIPS,) + a.shape[2:], a.dtype) for a in grads], {},
                [pltpu.SemaphoreType.DMA((n,)), pltpu.SemaphoreType.DMA((n,))], _rs1_start, _rs1_finish)


def join_sides(a, b):
    na, nao, nas = len(a.ins), len(a.out_shapes), len(a.sems)
    aliases = dict(a.aliases)
    aliases.update({na + i: nao + o for i, o in b.aliases.items()})

    def start(ins, outs, sems):
        a.start(ins[:na], outs[:nao], sems[:nas])
        b.start(ins[na:], outs[nao:], sems[nas:])

    def finish(ins, outs, sems):
        a.finish(ins[:na], outs[:nao], sems[:nas])
        b.finish(ins[na:], outs[nao:], sems[nas:])

    return Side(list(a.ins) + list(b.ins), list(a.out_shapes) + list(b.out_shapes), aliases,
                list(a.sems) + list(b.sems), start, finish)


def rs_pair(name, grads):
    n = len(grads)
    side = rs1_side(grads)

    def body(*refs):
        side.start(refs[:n], refs[n:2 * n], refs[2 * n:])
        side.finish(refs[:n], refs[n:2 * n], refs[2 * n:])

    return pl.pallas_call(
        body, name=name, in_specs=[ANY] * n, out_specs=[ANY] * n, out_shape=side.out_shapes,
        scratch_shapes=side.sems, compiler_params=_cparams(),
    )(*grads)


def _rows_tile(rows, cols, unit=16, limit=1 << 20):
    best = None
    for t in range(unit, rows + 1, unit):
        if rows % t == 0 and t * cols * 4 <= limit:
            best = t
    return best or rows


def rs_add_pair(name, g, got, ci, slot):
    _, _, hr, c = g.shape
    th = _rows_tile(hr, c, limit=1 << 21)

    def body(ci_ref, slot_ref, g_ref, r_ref, own_ref, ob_ref):
        tot = g_ref[...] + r_ref[...]
        ob_ref[...] = tot.astype(BF16)

        @pl.when(pl.program_id(1) == slot_ref[0])
        def _():
            own_ref[...] = tot

    blk = pl.BlockSpec((None, th, c), lambda i, s, ci_ref, slot_ref: (s, i, 0))
    return pl.pallas_call(
        body, name=name,
        grid_spec=pltpu.PrefetchScalarGridSpec(
            num_scalar_prefetch=2, grid=(hr // th, N_CHIPS),
            in_specs=[pl.BlockSpec((None, None, th, c), lambda i, s, ci_ref, slot_ref: (s, ci_ref[0], i, 0)), blk],
            out_specs=[pl.BlockSpec((th, c), lambda i, s, ci_ref, slot_ref: (i, 0)), blk]),
        out_shape=[jax.ShapeDtypeStruct((hr, c), F32), jax.ShapeDtypeStruct((N_CHIPS, hr, c), BF16)],
        compiler_params=_cparams(dimension_semantics=("arbitrary", "arbitrary")),
    )(ci, slot, g, got)


def _rs2_copies(pb, outs, sems):
    send_sems, recv_sems = sems
    xi, yi, ci = _place()
    chips, sidx = _chips(xi, yi)
    return [_rcopy(pb[w].at[sidx[j]], outs[3 * w + j], send_sems.at[w, j], recv_sems.at[w, j], (cx, cy, ci))
            for w in range(len(pb)) for j, (cx, cy) in enumerate(chips)]


def _rs2_start(pb, outs, sems):
    for cp in _rs2_copies(pb, outs, sems):
        cp.start()


def _rs2_finish(pb, outs, sems):
    for cp in _rs2_copies(pb, outs, sems):
        cp.wait_recv()
    _drain(_rs2_copies(pb, outs, sems))


def rs2_side(sums_bf16):
    n = len(sums_bf16)
    out_shapes = [jax.ShapeDtypeStruct(a.shape[1:], BF16) for a in sums_bf16 for _ in range(3)]
    return Side(sums_bf16, out_shapes, {}, [pltpu.SemaphoreType.DMA((n, 3)), pltpu.SemaphoreType.DMA((n, 3))],
                _rs2_start, _rs2_finish)


def rs_swap(name, halves):
    n = len(halves)

    def body(*refs):
        hv, out = refs[:n], refs[n:2 * n]
        send_sems, recv_sems = refs[2 * n:]
        xi, yi, ci = _place()
        sib = (xi, yi, 1 - ci)
        cps = [_rcopy(hv[w], out[w], send_sems.at[w], recv_sems.at[w], sib) for w in range(n)]
        for cp in cps:
            cp.start()
        for cp in cps:
            cp.wait_recv()
        _drain(cps)

    return pl.pallas_call(
        body, name=name, in_specs=[ANY] * n, out_specs=[ANY] * n,
        out_shape=[jax.ShapeDtypeStruct(a.shape, a.dtype) for a in halves],
        scratch_shapes=[pltpu.SemaphoreType.DMA((n,)), pltpu.SemaphoreType.DMA((n,))],
        compiler_params=_cparams(),
    )(*halves)


def cast_to_slot(name, x, slot):
    r, c = x.shape
    tr = _rows_tile(r, c, limit=1 << 22)

    def body(slot_ref, x_ref, o_ref):
        o_ref[...] = x_ref[...].astype(BF16)

    return pl.pallas_call(
        body, name=name,
        grid_spec=pltpu.PrefetchScalarGridSpec(
            num_scalar_prefetch=1, grid=(r // tr,),
            in_specs=[pl.BlockSpec((tr, c), lambda i, slot_ref: (i, 0))],
            out_specs=pl.BlockSpec((None, tr, c), lambda i, slot_ref: (slot_ref[0], i, 0))),
        out_shape=jax.ShapeDtypeStruct((N_CHIPS, r, c), BF16),
        compiler_params=_cparams(dimension_semantics=("parallel",)),
    )(slot, x)


def sum_blocks(name, x, picks):
    _, r, c = x.shape

    def body(x_ref, o_ref):
        acc = x_ref[picks[0]]
        for b in picks[1:]:
            acc = acc + x_ref[b]
        o_ref[...] = acc

    vm = pl.BlockSpec(memory_space=pltpu.VMEM)
    return pl.pallas_call(body, name=name, in_specs=[vm], out_specs=vm, out_shape=jax.ShapeDtypeStruct((r, c), F32),
                          compiler_params=_cparams())(x)


def adamw(name, w, g, m, v):
    r, c = w.shape
    tr = _rows_tile(r, c, unit=8, limit=1 << 20)

    def body(w_ref, g_ref, m_ref, v_ref, d_ref, mo_ref, vo_ref):
        d_ref[...], mo_ref[...], vo_ref[...] = _adam_update(w_ref[...], g_ref[...], m_ref[...], v_ref[...])

    blk = pl.BlockSpec((tr, c), lambda i: (i, 0))
    return pl.pallas_call(
        body, name=name, grid=(r // tr,), in_specs=[blk] * 4, out_specs=[blk] * 3,
        out_shape=[jax.ShapeDtypeStruct((r, c), F32)] * 3,
        compiler_params=_cparams(dimension_semantics=("parallel",)),
    )(w, g, m, v)


def _adam_update(w, gv, m, v):
    c1 = 1.0 / (1.0 - ADAM_B1 ** ADAM_STEP)
    c2 = 1.0 / (1.0 - ADAM_B2 ** ADAM_STEP)
    mn = ADAM_B1 * m + (1.0 - ADAM_B1) * gv
    vn = ADAM_B2 * v + (1.0 - ADAM_B2) * (gv * gv)
    return -ADAM_LR * ((mn * c1) / (jnp.sqrt(vn * c2) + ADAM_EPS) + ADAM_WD * w), mn, vn


def adamw_halves(name, w, m, v, mine, theirs, ci, side=None):
    hr, c = mine[0].shape
    npos = len(mine)
    th = _rows_tile(hr, c, unit=8, limit=1 << 20)
    per = hr // th

    def work(ci_ref, w_ref, m_ref, v_ref, *rest):
        g_refs, (go_ref, d_ref, mo_ref, vo_ref) = rest[:2 * npos], rest[2 * npos:]
        pos, half = pl.program_id(0), pl.program_id(1)
        from_me = half == ci_ref[0]
        gv = jnp.where(from_me, g_refs[0][...], g_refs[npos][...])
        for p in range(1, npos):
            gv = jnp.where(pos == p, jnp.where(from_me, g_refs[p][...], g_refs[npos + p][...]), gv)
        d_ref[...], mo_ref[...], vo_ref[...] = _adam_update(w_ref[...], gv, m_ref[...], v_ref[...])
        go_ref[...] = gv

    full = pl.BlockSpec((th, c), lambda p, h, i, ci_ref: ((p * 2 + h) * per + i, 0))
    part = pl.BlockSpec((th, c), lambda p, h, i, ci_ref: (i, 0))
    rows = 2 * hr * npos
    n_in = 4 + 2 * npos
    s_in, s_out, s_shape, s_scratch, _, s_ops = _side_call_args(side, n_in, 4)
    res = pl.pallas_call(
        _with_side(work, side, n_in, 4, 0, (npos, 2, per)), name=name,
        grid_spec=pltpu.PrefetchScalarGridSpec(
            num_scalar_prefetch=1, grid=(npos, 2, per), in_specs=[full] * 3 + [part] * (2 * npos) + s_in,
            out_specs=[full] * 4 + s_out, scratch_shapes=s_scratch),
        out_shape=[jax.ShapeDtypeStruct((rows, c), F32)] * 4 + s_shape,
        compiler_params=_cparams(dimension_semantics=("arbitrary", "arbitrary", "arbitrary")),
    )(ci, w, m, v, *mine, *theirs, *s_ops)
    return res[0], res[1], res[2], res[3], list(res[4:])


def pack_rows(name, parts, rows):
    width = parts[0].shape[1]
    n = len(parts)

    def body(*refs):
        o_ref = refs[n]
        o_ref[...] = jnp.zeros_like(o_ref)
        off = 0
        for r in refs[:n]:
            o_ref[off:off + r.shape[0], :] = r[...]
            off += r.shape[0]

    vm = pl.BlockSpec(memory_space=pltpu.VMEM)
    return pl.pallas_call(body, name=name, in_specs=[vm] * n, out_specs=vm,
                          out_shape=jax.ShapeDtypeStruct((rows, width), F32), compiler_params=_cparams())(*parts)


def _pad_rows(a, rows):
    return jnp.pad(a, ((0, rows - a.shape[0]), (0, 0)))


def _view2d(name, a):
    if name == 'rw_rk' or a.ndim == 1:
        return a.reshape(1, -1)
    return a.reshape(-1, a.shape[-1])


def _rs_pair_sums(tag, items, ci_arr, s_arr, got=None):
    g4 = [g.reshape(N_CHIPS, 2, g.shape[1] // 2, g.shape[2]) for _, g in items]
    if got is None:
        got = rs_pair("rs1_" + tag, g4)
    return [rs_add_pair(f"rs1add_{tag}{w}", g4[w], got[w], ci_arr, s_arr) for w in range(len(items))]


def _rs_finish(tag, items, sums, landed, s):
    names = []
    for nm, _ in items:
        if nm not in names:
            names.append(nm)
    halves = []
    for w, (r0, r1, r2) in enumerate(landed):
        own = sums[w][0]
        hr, c = own.shape
        halves.append(row_call(f"rs2add_{tag}{w}", f_sum, [own, r0, r1, r2], [], [(c, F32)],
                               tr=_rows_tile(hr, c), sr=16)[0])
    theirs = rs_swap("rs3_" + tag, halves)
    return {name: ([halves[w] for w, (nm, _) in enumerate(items) if nm == name],
                   [theirs[w] for w, (nm, _) in enumerate(items) if nm == name]) for name in names}


def _step(p):
    xi, yi, ci = _place()
    me = 4 * xi + 2 * yi + ci
    s = 2 * xi + yi
    ci_arr = jnp.reshape(ci, (1,)).astype(jnp.int32)
    s_arr = jnp.reshape(s, (1,)).astype(jnp.int32)
    x, ctx, tgt = p['x'][0], p['ctx'][0], p['loss_target'][0]
    T, D = x.shape
    L = ctx.shape[0]
    H, Dq = D // HEAD, D // N_CHIPS
    TR = math.gcd(math.gcd(L, T), 256)
    TS = min(TR, 64)
    nct = L // TR
    LG = p['rw_g1'].shape[-1]
    LW, LA = p['rw_w1'].shape[-1], p['rw_a1'].shape[-1]
    F4 = p['ffn_w2'].shape[1]

    pack = jnp.concatenate([
        _pad_rows(p['c'].reshape(N_CHIPS, Dq), 8), _pad_rows(p['rw_mix'][0], 8), _pad_rows(p['rw_w0'][0], 8),
        _pad_rows(p['rw_a0'][0], 8), _pad_rows(p['sc_conv'][0], 8)], axis=0)
    got = all_gather8("ag_small", pack)
    c_all = got[:, 0:N_CHIPS, :].reshape(N_DEV, D)
    full = jnp.transpose(got[::2], (1, 0, 2)).reshape(40, D)
    mix_f, w0_f, a0_f, conv_f = full[8:16], full[16:24], full[24:32], full[32:40]

    cond_in = jnp.concatenate([c_all, _pad_rows(p['c_ctx'].reshape(1, D), 8)], axis=0)
    cond = row_call("cond", f_silu, [cond_in], [], [(D, F32)], tr=16, sr=16)[0]
    ada = Stacked(p['ada_w'], "layer", D)
    modp = [mm_nn(f"modp{i}", cond, ada.at(i)) for i in range(2)]
    mg = all_gather8("ag_mod", jnp.concatenate(modp, axis=0))
    mod = jnp.transpose(mg[::2].reshape(N_CHIPS, 2, 16, 6 * Dq), (1, 2, 0, 3)).reshape(2, 16, 6 * D)
    mod = mod + p['ada_b'][:, None, :]
    mod_x = lax.dynamic_index_in_dim(mod, me, axis=1, keepdims=False)
    mod_c = mod[:, 8]

    def chunk(vec, j):
        return vec[j * D:(j + 1) * D].reshape(1, D)

    sh1x, sc1x, gt1x, sh2x, sc2x, gt2x = ([chunk(mod_x[i], j) for i in range(2)] for j in range(6))
    sh1c, sc1c = chunk(mod_c[0], 0), chunk(mod_c[0], 1)

    def slots(names):
        return [cast_to_slot("cast_" + n, _view2d(n, p[n]), s_arr) for n in names]

    rw_names = ('rw_wr', 'rw_wk', 'rw_wv', 'rw_wo', 'rw_w1', 'rw_w2', 'rw_a1', 'rw_a2', 'rw_g1', 'rw_g2')
    first_names = tuple(n for n in rw_names if n != 'rw_wo')
    late_names = ('sc_win', 'sc_wout', 'ffn_w13', 'rw_wo', 'ffn_w2')
    gw = dict(zip(first_names, gather_weights("ag_rw", slots(first_names))))
    late_slots = slots(late_names)
    Wr, Wk, Wv = (Stacked(gw[n], "row", Dq) for n in ('rw_wr', 'rw_wk', 'rw_wv'))
    W1, A1, G1 = (Stacked(gw[n], "row", Dq) for n in ('rw_w1', 'rw_a1', 'rw_g1'))
    W2, A2, G2 = Stacked(gw['rw_w2'], "col", LW), Stacked(gw['rw_a2'], "col", LA), Stacked(gw['rw_g2'], "col", LG)

    ones2 = (((lax.broadcasted_iota(jnp.int32, (2 * LANES, LANES), 0) & (LANES - 1)) >= HEAD)
             == (lax.broadcasted_iota(jnp.int32, (2 * LANES, LANES), 1) >= HEAD)).astype(BF16)
    n1g, n2g = p['norm1_g'], p['norm2_g']
    kkp, ka, lnw, lnb = p['rw_kk'], p['rw_ka'], p['rw_lnw'], p['rw_lnb']
    rk = p['rw_rk'].reshape(1, D)
    fg = p['final_g'].reshape(1, D)

    xin = jnp.concatenate([ctx, x], axis=0)
    nm = functools.partial(f_norm_mod, nct)
    nm_consts = [n1g[0:1], sh1c, sc1c, sh1x[0], sc1x[0]]
    h = row_call("l0_norm", nm, [xin], nm_consts, [(D, F32)], tr=TR, sr=16)[0]
    xr, xw, xk, xv, xa, xg = mix_fwd("l0_mix", h, mix_f, L)
    r = mm_nn("l0_r", xr, Wr)
    k = mm_nn("l0_k", xk, Wk)
    v = mm_nn("l0_v", xv, Wv)
    gl = mm_nn("l0_gl", xg, G1)
    sg = row_call("l0_sg", f_sigmoid, [gl], [], [(LG, BF16)], tr=TR, sr=16)[0]
    g = mm_nn("l0_g", sg, G2)
    wl, tw, zw, al, za = [], [], [], [], []
    for d in range(2):
        wl.append(mm_nn(f"l0_wl{d}", xw, W1.at(d)))
        tw.append(row_call(f"l0_tw{d}", f_tanh, [wl[d]], [], [(LW, BF16)], tr=TR, sr=16)[0])
        zw.append(mm_nn(f"l0_zw{d}", tw[d], W2.at(d)))
        al.append(mm_nn(f"l0_al{d}", xa, A1.at(d), BF16))
        za.append(mm_nn(f"l0_za{d}", al[d], A2.at(d)))
    post_rows = [k, zw[0], zw[1], za[0], za[1]]
    post_consts = [kkp, ka, w0_f[0:1], w0_f[1:2], a0_f[0:1], a0_f[1:2], ones2]
    aa, dec0, dec1, kd0, kd1, bb0, bb1, ksum = row_call(
        "l0_post", f_post_fwd, post_rows, post_consts, [(D, F32)] * 8, tr=TS, sr=16)
    dec, kd, bb = (dec0, dec1), (kd0, kd1), (bb0, bb1)
    ys, sts = [], []
    y_d, st_d, first = scan_fwd("l0_scan0", r, dec[0], kd[0], v, aa, bb[0], L, False, side=gather_side(late_slots[:4]))
    ys.append(y_d)
    sts.append(st_d)
    y_d, st_d, filled = scan_fwd("l0_scan1", r, dec[1], kd[1], v, aa, bb[1], L, True,
                                 side=join_sides(pass_side(first), gather_whole_side(late_slots[4:])))
    ys.append(y_d)
    sts.append(st_d)
    gw.update(zip(late_names, filled))
    Wo = Stacked(gw['rw_wo'], "row", Dq)
    Win, Wout = Stacked(gw['sc_win'], "col", D), Stacked(gw['sc_wout'], "row", Dq)
    W13, W2f = Stacked(gw['ffn_w13'], "col", D), Stacked(gw['ffn_w2'], "row", F4)
    ro_rows = [ys[0], ys[1], r, ksum, v, g]
    ro_consts = [rk, lnw, lnb, ones2]
    og = row_call("l0_readout", f_readout, ro_rows, ro_consts, [(D, BF16)], tr=TS, sr=16)[0]
    yx = mm_nn("l0_o", og, Wo)
    res0_consts = [gt1x[0], n2g[0:1], sh2x[0], sc2x[0]]
    x1, h2 = row_call("l0_res", f_res_norm_mod, [x, yx], res0_consts, [(D, F32), (D, BF16)], tr=TR, sr=16,
                      offs=[0, nct])
    ab0 = mm_nn("l0_ffn13", h2, W13.at(0))
    sw0 = swiglu_fwd("l0_swiglu", ab0, TS)
    f0 = mm_nn("l0_ffn2", sw0, W2f.at(0))

    res1_consts = [gt2x[0], n1g[1:2], sh1x[1], sc1x[1]]
    x2, hb = row_call("l1_norm", f_res_norm_mod, [x1, f0], res1_consts, [(D, F32), (D, BF16)], tr=TR, sr=16)
    gcu = mm_nn("l1_win", hb, Win)
    pc = conv_fwd("l1_conv", gcu, conv_f)
    yx1 = mm_nn("l1_wout", pc, Wout)
    res2_consts = [gt1x[1], n2g[1:2], sh2x[1], sc2x[1]]
    x3, h2b = row_call("l1_res", f_res_norm_mod, [x2, yx1], res2_consts, [(D, F32), (D, BF16)], tr=TR, sr=16)
    ab1 = mm_nn("l1_ffn13", h2b, W13.at(1))
    sw1 = swiglu_fwd("l1_swiglu", ab1, TS)
    f1 = mm_nn("l1_ffn2", sw1, W2f.at(1))
    dx3, df1, dgt2_1, dfg, loss_blk = final_call("final", x3, f1, gt2x[1], fg, tgt, TR)
    loss = lax.psum(loss_blk[0, 0], ("x", "y", "c"))

    big = []
    dsw1 = mm_nt("b1_dsw", df1, W2f.at(1))
    gW2f1 = mm_tn("b1_gw2", sw1, df1, "row")
    dab1 = swiglu_bwd("b1_swiglu", ab1, dsw1, TS)
    dh2b = mm_nt("b1_dh2", dab1, W13.at(1))
    gW13_1 = mm_tn("b1_gw13", h2b, dab1, "col")
    rm = [True, True]
    cm = [True] * 4
    (dx2, dyx1), (dgt1_1, dn2g1, dsh2_1, dsc2_1) = row_vjp(
        "b1_res", f_res_norm_mod, [x2, yx1], res2_consts, [dx3, dh2b], row_mask=rm, const_mask=cm, tr=TR, sr=16,
        bf16_rows=(1,))
    dpc = mm_nt("b1_dpc", dyx1, Wout)
    big.append(('sc_wout', mm_tn("b1_gwout", pc, dyx1, "row")))
    dgcu, dconv = conv_bwd("b1_conv", gcu, conv_f, dpc)
    dhb = mm_nt("b1_dhb", dgcu, Win)
    big.append(('sc_win', mm_tn("b1_gwin", hb, dgcu, "col")))
    (dx1, df0), (dgt2_0, dn1g1, dsh1_1, dsc1_1) = row_vjp(
        "b1_norm", f_res_norm_mod, [x1, f0], res1_consts, [dx2, dhb], row_mask=rm, const_mask=cm, tr=TR, sr=16,
        bf16_rows=(1,))

    def halves4(items):
        return [g.reshape(N_CHIPS, 2, g.shape[1] // 2, g.shape[2]) for _, g in items]

    sc_items = [it for it in big if it[0] in ('sc_win', 'sc_wout')]
    ffn1_items = [('ffn_w13', gW13_1), ('ffn_w2', gW2f1)]
    dsw0, sc_got = mm_nt("b0_dsw", df0, W2f.at(0), side=rs1_side(halves4(sc_items)))
    gW2f0 = mm_tn("b0_gw2", sw0, df0, "row")
    dab0 = swiglu_bwd("b0_swiglu", ab0, dsw0, TS)
    dh2, ffn1_got = mm_nt("b0_dh2", dab0, W13.at(0), side=rs1_side(halves4(ffn1_items)))
    gW13_0, w2f0_got = mm_tn("b0_gw13", h2, dab0, "col", side=rs1_side(halves4([('ffn_w2', gW2f0)])))
    (dx_a, dyx), (dgt1_0, dn2g0, dsh2_0, dsc2_0) = row_vjp(
        "b0_res", f_res_norm_mod, [x, yx], res0_consts, [dx1, dh2], row_mask=rm, const_mask=cm, tr=TR, sr=16,
        offs=[0, nct], bf16_rows=(1,))
    dyx_all = jnp.concatenate([jnp.zeros((L, D), BF16), dyx], axis=0)
    dog = mm_nt("b0_dog", dyx_all, Wo)
    gWo = mm_tn("b0_gwo", og, dyx_all, "row")
    (dy, dr_ro, dksum, dv_ro, dg), (drk, dlnw, dlnb), w13_0_got = row_vjp(
        "b0_readout", f_readout, ro_rows, ro_consts, [dog], row_mask=[True, False, True, True, True, True],
        const_mask=[True, True, True, False], tr=TS, sr=16, bf16_rows=(5,),
        side=rs1_side(halves4([('ffn_w13', gW13_0)])))
    late_items = sc_items + [('ffn_w13', gW13_0), ffn1_items[0], ('ffn_w2', gW2f0), ffn1_items[1]]
    late_got = sc_got + [w13_0_got[0], ffn1_got[0], w2f0_got[0], ffn1_got[1]]
    late_sums = _rs_pair_sums("late", late_items, ci_arr, s_arr, late_got)
    (dr0, ddec0, dkd0, dv0, daa0, dbb0), landed_flat = scan_bwd(
        "b0_scan0", r, dec[0], kd[0], v, aa, bb[0], dy, sts[0], L, False, side=rs2_side([s_[1] for s_ in late_sums]))
    (dr1, ddec1, dkd1, dv1, daa1, dbb1), _ = scan_bwd("b0_scan1", r, dec[1], kd[1], v, aa, bb[1], dy, sts[1], L, True)
    late_landed = [landed_flat[3 * w:3 * w + 3] for w in range(len(late_items))]
    post_cots = [daa0, daa1, ddec0, ddec1, dkd0, dkd1, dbb0, dbb1, dksum]
    (dk, dzw0, dzw1, dza0, dza1), (dkkp, dka, dw00, dw01, da00, da01) = row_vjp(
        "b0_post", f_post, post_rows, post_consts, post_cots, row_mask=[True] * 5,
        const_mask=[True] * 6 + [False], tr=TS, sr=16, bf16_rows=(0, 1, 2, 3, 4))
    dzw, dza = (dzw0, dzw1), (dza0, dza1)
    dr_t = sum_cast("b0_drsum", [dr0, dr1, dr_ro], BF16, TR)
    dv_t = sum_cast("b0_dvsum", [dv0, dv1, dv_ro], BF16, TR)
    mix_cots, mix_slots = [], []

    def back(tag, cot, w, xin_m, kind, slot):
        mix_cots.append(mm_nt("b0_dx" + tag, cot, w))
        mix_slots.append(slot)
        return mm_tn("b0_gw" + tag, xin_m, cot, kind)

    big.append(('rw_wr', back("r", dr_t, Wr, xr, "row", 0)))
    big.append(('rw_wk', back("k", dk, Wk, xk, "row", 2)))
    big.append(('rw_wv', back("v", dv_t, Wv, xv, "row", 3)))
    big.append(('rw_wo', gWo))
    dsg = mm_nt("b0_dsg", dg, G2)
    gG2 = mm_tn("b0_gg2", sg, dg, "col")
    (dgl,), _ = row_vjp("b0_sg", f_sigmoid, [gl], [], [dsg], row_mask=[True], const_mask=[], tr=TR, sr=16,
                        bf16_rows=(0,))
    gG1 = back("g", dgl, G1, xg, "row", 5)
    gW1, gW2, gA1, gA2 = [], [], [], []
    for d in range(2):
        dtw = mm_nt(f"b0_dtw{d}", dzw[d], W2.at(d))
        gW2.append(mm_tn(f"b0_gw2{d}", tw[d], dzw[d], "col"))
        (dwl,), _ = row_vjp(f"b0_tw{d}", f_tanh, [wl[d]], [], [dtw], row_mask=[True], const_mask=[], tr=TR, sr=16,
                            bf16_rows=(0,))
        gW1.append(back(f"w{d}", dwl, W1.at(d), xw, "row", 1))
        dal = mm_nt(f"b0_dal{d}", dza[d], A2.at(d), BF16)
        gA2.append(mm_tn(f"b0_ga2{d}", al[d], dza[d], "col"))
        gA1.append(back(f"a{d}", dal, A1.at(d), xa, "row", 4))
    big += [('rw_w1', gW1[0]), ('rw_w1', gW1[1]), ('rw_w2', gW2[0]), ('rw_w2', gW2[1]),
            ('rw_a1', gA1[0]), ('rw_a1', gA1[1]), ('rw_a2', gA2[0]), ('rw_a2', gA2[1]),
            ('rw_g1', gG1), ('rw_g2', gG2)]
    dh, dmix = mix_bwd("b0_mix", h, mix_f, mix_cots, mix_slots, L)
    (dxin,), (dn1g0, dsh1c, dsc1c, dsh1x, dsc1x) = row_vjp(
        "b0_norm", nm, [xin], nm_consts, [dh], row_mask=[True], const_mask=[True] * 5, tr=TR, sr=16)
    grad_x = sum_cast("b0_dx", [dxin, dx_a], F32, TR, offs=[nct, 0])

    zero = jnp.zeros((1, D), F32)
    parts = [dsh1x, dsc1x, dgt1_0, dsh2_0, dsc2_0, dgt2_0, dsh1c, dsc1c, zero, zero, zero, zero,
             dsh1_1, dsc1_1, dgt1_1, dsh2_1, dsc2_1, dgt2_1, zero, zero, zero, zero, zero, zero,
             dn1g0, dn1g1, dn2g0, dn2g1, dkkp, dka, drk, dlnw, dlnb, dfg,
             dmix[0:6], dw00, dw01, da00, da01, dconv[0:3]]
    got2 = all_gather8("ag_grads", pack_rows("pack_grads", parts, 48))
    small = sum_blocks("sum_grads", got2, list(range(N_DEV)))
    per_ex = got2[:, 0:24].reshape(N_DEV, 2, 2, 6 * D)
    tot = small[0:24].reshape(2, 2, 6 * D)
    cols = lambda a, width: lax.dynamic_slice_in_dim(a, s * width, width, axis=1)
    g_ada_w, dcond_parts = [], []
    for i in range(2):
        dm16 = cols(jnp.concatenate([per_ex[:, i, 0], _pad_rows(tot[i, 1][None], 8)], axis=0), 6 * Dq)
        g_ada_w.append(mm_tn(f"g_ada{i}", cond, dm16))
        dcond_parts.append(mm_nt(f"dcond{i}", dm16, ada.at(i)))
    g_ada_b = sum_cast("g_adab", [_pad_rows(tot[:, 0].reshape(12, D), 16), _pad_rows(tot[:, 1].reshape(12, D), 16)],
                       F32, 16)[0:12].reshape(2, 6 * D)
    dcond_mine = sum_cast("dcond_sum", dcond_parts, F32, 16)
    dcond = sum_blocks("dcond_chips", all_gather8("ag_dcond", dcond_mine), [0, 2, 4, 6])
    (dcin,), _ = row_vjp("b_cond", f_silu, [cond_in], [], [dcond], row_mask=[True], const_mask=[], tr=16, sr=16)

    gsh = _rs_finish("late", late_items, late_sums, late_landed, s)
    rw_items = [it for it in big if it[0] in rw_names]
    rw_sums = _rs_pair_sums("rw", rw_items, ci_arr, s_arr)

    def view(n):
        return _view2d(n, p[n]), _view2d(n, p['m_' + n]), _view2d(n, p['v_' + n])

    done = {}
    *done['ffn_w13'], rw_flat = adamw_halves("adam_ffn_w13", *view('ffn_w13'), gsh['ffn_w13'][0], gsh['ffn_w13'][1],
                                             ci_arr, side=rs2_side([s_[1] for s_ in rw_sums]))
    gsh.update(_rs_finish("rw", rw_items, rw_sums, [rw_flat[3 * w:3 * w + 3] for w in range(len(rw_items))], s))

    grads = {}
    grads['c_ctx'] = dcin[8]
    grads['norm1_g'], grads['norm2_g'] = small[24:26], small[26:28]
    grads['ada_w'] = jnp.stack(g_ada_w)
    grads['ada_b'] = g_ada_b
    grads['rw_kk'], grads['rw_ka'], grads['rw_rk'] = small[28:29], small[29:30], small[30:31]
    grads['rw_lnw'], grads['rw_lnb'], grads['final_g'] = small[31:32], small[32:33], small[33]
    sharded = cols(small[34:48], Dq)
    grads['rw_mix'], grads['rw_w0'], grads['rw_a0'], grads['sc_conv'] = (
        sharded[0:6], sharded[6:8], sharded[8:10], sharded[10:13])

    outs_g, outs_d, outs_m, outs_v = [], [], [], []
    for n in WEIGHTS:
        shape = p[n].shape
        w2d, m2d, v2d = view(n)
        if n in done:
            g2d, d_, m_, v_ = done[n]
        elif n in gsh:
            g2d, d_, m_, v_, _ = adamw_halves("adam_" + n, w2d, m2d, v2d, gsh[n][0], gsh[n][1], ci_arr)
        else:
            g2d = _view2d(n, grads[n].reshape(shape))
            d_, m_, v_ = adamw("adam_" + n, w2d, g2d, m2d, v2d)
        outs_g.append(g2d.reshape(shape))
        outs_d.append(d_.reshape(shape))
        outs_m.append(m_.reshape(shape))
        outs_v.append(v_.reshape(shape))
    return (loss, grad_x.reshape(1, T, D), *outs_g, *outs_d, *outs_m, *outs_v)


def kernel(x, c, ctx, c_ctx, norm1_g, norm2_g, ada_w, ada_b, rw_mix, rw_wr, rw_wk, rw_wv, rw_wo, rw_w0, rw_w1, rw_w2, rw_a0, rw_a1, rw_a2, rw_g1, rw_g2, rw_kk, rw_ka, rw_rk, rw_lnw, rw_lnb, sc_win, sc_conv, sc_wout, ffn_w13, ffn_w2, final_g, loss_target, m_c_ctx, m_norm1_g, m_norm2_g, m_ada_w, m_ada_b, m_rw_mix, m_rw_wr, m_rw_wk, m_rw_wv, m_rw_wo, m_rw_w0, m_rw_w1, m_rw_w2, m_rw_a0, m_rw_a1, m_rw_a2, m_rw_g1, m_rw_g2, m_rw_kk, m_rw_ka, m_rw_rk, m_rw_lnw, m_rw_lnb, m_sc_win, m_sc_conv, m_sc_wout, m_ffn_w13, m_ffn_w2, m_final_g, v_c_ctx, v_norm1_g, v_norm2_g, v_ada_w, v_ada_b, v_rw_mix, v_rw_wr, v_rw_wk, v_rw_wv, v_rw_wo, v_rw_w0, v_rw_w1, v_rw_w2, v_rw_a0, v_rw_a1, v_rw_a2, v_rw_g1, v_rw_g2, v_rw_kk, v_rw_ka, v_rw_rk, v_rw_lnw, v_rw_lnb, v_sc_win, v_sc_conv, v_sc_wout, v_ffn_w13, v_ffn_w2, v_final_g):
    values = (x, c, ctx, c_ctx, norm1_g, norm2_g, ada_w, ada_b, rw_mix, rw_wr, rw_wk, rw_wv, rw_wo, rw_w0, rw_w1, rw_w2, rw_a0, rw_a1, rw_a2, rw_g1, rw_g2, rw_kk, rw_ka, rw_rk, rw_lnw, rw_lnb, sc_win, sc_conv, sc_wout, ffn_w13, ffn_w2, final_g, loss_target, m_c_ctx, m_norm1_g, m_norm2_g, m_ada_w, m_ada_b, m_rw_mix, m_rw_wr, m_rw_wk, m_rw_wv, m_rw_wo, m_rw_w0, m_rw_w1, m_rw_w2, m_rw_a0, m_rw_a1, m_rw_a2, m_rw_g1, m_rw_g2, m_rw_kk, m_rw_ka, m_rw_rk, m_rw_lnw, m_rw_lnb, m_sc_win, m_sc_conv, m_sc_wout, m_ffn_w13, m_ffn_w2, m_final_g, v_c_ctx, v_norm1_g, v_norm2_g, v_ada_w, v_ada_b, v_rw_mix, v_rw_wr, v_rw_wk, v_rw_wv, v_rw_wo, v_rw_w0, v_rw_w1, v_rw_w2, v_rw_a0, v_rw_a1, v_rw_a2, v_rw_g1, v_rw_g2, v_rw_kk, v_rw_ka, v_rw_rk, v_rw_lnw, v_rw_lnb, v_sc_win, v_sc_conv, v_sc_wout, v_ffn_w13, v_ffn_w2, v_final_g)
    return _step(dict(zip(INPUTS, values)))
```

```python
import functools
import math

import jax
import jax.numpy as jnp
from jax import lax
from jax.experimental import pallas as pl
from jax.experimental.pallas import tpu as pltpu

F32 = jnp.float32
BF16 = jnp.bfloat16
MESH = pl.DeviceIdType.MESH

GRID_W = 64
HEAD = 64
LANES = 128
N_CHIPS = 4
N_DEV = 8
NORM_EPS = 1e-6
GN_EPS = 64e-5
ADAM_LR, ADAM_B1, ADAM_B2, ADAM_EPS, ADAM_WD, ADAM_STEP = 0.001, 0.9, 0.999, 1e-08, 0.01, 10
VMEM_LIMIT = 56 * 1024 * 1024
HI = lax.Precision.HIGHEST
WEIGHTS = ['c_ctx', 'norm1_g', 'norm2_g', 'ada_w', 'ada_b', 'rw_mix', 'rw_wr', 'rw_wk', 'rw_wv', 'rw_wo', 'rw_w0',
           'rw_w1', 'rw_w2', 'rw_a0', 'rw_a1', 'rw_a2', 'rw_g1', 'rw_g2', 'rw_kk', 'rw_ka', 'rw_rk', 'rw_lnw',
           'rw_lnb', 'sc_win', 'sc_conv', 'sc_wout', 'ffn_w13', 'ffn_w2', 'final_g']
INPUTS = (['x', 'c', 'ctx'] + WEIGHTS + ['loss_target'] + ['m_' + w for w in WEIGHTS]
          + ['v_' + w for w in WEIGHTS])


def _cparams(**kw):
    return pltpu.CompilerParams(vmem_limit_bytes=VMEM_LIMIT, **kw)


def _pick(dim, cands):
    for c in cands:
        if dim % c == 0:
            return c
    return dim


def _place():
    return lax.axis_index("x"), lax.axis_index("y"), lax.axis_index("c")


_TILE_M = (1024, 768, 512, 1408, 256, 128)
_TILE_N = (1408, 1024, 768, 512, 256, 128)
_TILE_K = (2816, 2304, 2048, 1536, 1408, 1152, 1024, 768, 704, 512, 256, 128)
MM_TILE_BYTES = 40 * 1024 * 1024


def _pick_k(unit, tm, tn, a_dtype, b_dtype, o_dtype):
    ab, bb, ob = (jnp.dtype(d).itemsize for d in (a_dtype, b_dtype, o_dtype))
    for tk in _TILE_K:
        if unit % tk == 0 and 2 * (tm * tk * ab + tk * tn * bb) + 2 * tm * tn * ob + tm * tn * 4 <= MM_TILE_BYTES:
            return tk
    return unit


class Stacked:
    def __init__(self, arr, kind, r, layer=0):
        self.arr, self.kind, self.r, self.layer = arr, kind, r, layer
        self.c = arr.shape[2]
        self.shape = {"row": (N_CHIPS * r, self.c), "col": (r, N_CHIPS * self.c), "layer": (r, self.c)}[kind]

    def at(self, layer):
        return Stacked(self.arr, self.kind, self.r, layer)

    def spec(self, t0, t1, swap):
        r, c, layer = self.r, self.c, self.layer
        per_r, per_c = r // t0, c // t1
        assert r % t0 == 0 and c % t1 == 0
        kind = self.kind

        def index(i, j, k):
            ri, ci = (j, k) if swap else (k, j)
            if kind == "row":
                return (ri // per_r, layer * per_r + ri % per_r, ci)
            if kind == "layer":
                return (layer, ri, ci)
            return (ci // per_c, layer * per_r + ri, ci % per_c)

        return pl.BlockSpec((None, t0, t1), index)


def _mm_body(dims, nk, a_ref, b_ref, o_ref, acc_ref):
    if nk == 1:
        o_ref[...] = lax.dot_general(a_ref[...].astype(BF16), b_ref[...].astype(BF16), (dims, ((), ())),
                                     preferred_element_type=F32).astype(o_ref.dtype)
        return
    k = pl.program_id(2)

    @pl.when(k == 0)
    def _():
        acc_ref[...] = jnp.zeros_like(acc_ref)

    acc_ref[...] += lax.dot_general(a_ref[...].astype(BF16), b_ref[...].astype(BF16), (dims, ((), ())),
                                    preferred_element_type=F32)

    @pl.when(k == nk - 1)
    def _():
        o_ref[...] = acc_ref[...].astype(o_ref.dtype)


def _mm_call(name, dims, grid, in_specs, out_spec, out_shape, acc_shape, operands, side=None):
    acc = pltpu.VMEM(acc_shape if grid[2] > 1 else (8, LANES), F32)
    if side is None:
        return pl.pallas_call(
            functools.partial(_mm_body, dims, grid[2]), name=name, grid=grid, in_specs=in_specs, out_specs=out_spec,
            out_shape=out_shape, scratch_shapes=[acc],
            compiler_params=_cparams(dimension_semantics=("parallel", "parallel", "arbitrary")),
        )(*operands)
    s_in, s_out, s_shape, s_scratch, s_alias, s_ops = _side_call_args(side, 2, 1)
    res = pl.pallas_call(
        _with_side(functools.partial(_mm_body, dims, grid[2]), side, 2, 1, 1, grid), name=name, grid=grid,
        in_specs=in_specs + s_in, out_specs=[out_spec] + s_out, out_shape=[out_shape] + s_shape,
        scratch_shapes=[acc] + s_scratch, input_output_aliases=s_alias,
        compiler_params=_cparams(dimension_semantics=("arbitrary", "arbitrary", "arbitrary")),
    )(*operands, *s_ops)
    return res[0], list(res[1:])


def mm_nn(name, a, b, out_dtype=F32):
    M, K = a.shape
    st = isinstance(b, Stacked)
    N = b.shape[1]
    tm = _pick(M, _TILE_M)
    tn = _pick(b.c if st and b.kind == "col" else N, _TILE_N)
    tk = _pick_k(b.r if st else K, tm, tn, a.dtype, b.arr.dtype if st else b.dtype, out_dtype)
    b_spec = b.spec(tk, tn, False) if st else pl.BlockSpec((tk, tn), lambda i, j, k: (k, j))
    return _mm_call(name, ((1,), (0,)), (M // tm, N // tn, K // tk),
                    [pl.BlockSpec((tm, tk), lambda i, j, k: (i, k)), b_spec],
                    pl.BlockSpec((tm, tn), lambda i, j, k: (i, j)), jax.ShapeDtypeStruct((M, N), out_dtype),
                    (tm, tn), (a, b.arr if st else b))


def mm_nt(name, a, b, out_dtype=F32, side=None):
    M, N = a.shape
    st = isinstance(b, Stacked)
    K = b.shape[0]
    tm = _pick(M, _TILE_M)
    to = _pick(b.r if st else K, _TILE_N)
    tc = _pick_k(b.c if st and b.kind == "col" else N, tm, to, a.dtype, b.arr.dtype if st else b.dtype, out_dtype)
    b_spec = b.spec(to, tc, True) if st else pl.BlockSpec((to, tc), lambda i, j, k: (j, k))
    return _mm_call(name, ((1,), (1,)), (M // tm, K // to, N // tc),
                    [pl.BlockSpec((tm, tc), lambda i, j, k: (i, k)), b_spec],
                    pl.BlockSpec((tm, to), lambda i, j, k: (i, j)), jax.ShapeDtypeStruct((M, K), out_dtype),
                    (tm, to), (a, b.arr if st else b), side)


def mm_tn(name, a, b, kind=None, side=None):
    R, M = a.shape
    N = b.shape[1]
    r, c = (M // N_CHIPS, N) if kind == "row" else (M, N // N_CHIPS) if kind == "col" else (M, N)
    tm, tn = _pick(r, _TILE_M), _pick(c, _TILE_N)
    tk = _pick_k(R, tm, tn, a.dtype, b.dtype, F32)
    if kind:
        per_r, per_c = r // tm, c // tn
        if kind == "row":
            o_spec = pl.BlockSpec((None, tm, tn), lambda i, j, k: (i // per_r, i % per_r, j))
        else:
            o_spec = pl.BlockSpec((None, tm, tn), lambda i, j, k: (j // per_c, i, j % per_c))
        o_shape = jax.ShapeDtypeStruct((N_CHIPS, r, c), F32)
    else:
        o_spec = pl.BlockSpec((tm, tn), lambda i, j, k: (i, j))
        o_shape = jax.ShapeDtypeStruct((M, N), F32)
    return _mm_call(name, ((0,), (0,)), (M // tm, N // tn, R // tk),
                    [pl.BlockSpec((tk, tm), lambda i, j, k: (k, i)), pl.BlockSpec((tk, tn), lambda i, j, k: (k, j))],
                    o_spec, o_shape, (tm, tn), (a, b), side)


def _shifted(o):
    return lambda i: (i + o, 0)


def row_call(name, f, rows, consts, outs, *, tr, sr, offs=None):
    offs = offs or [0] * len(rows)
    n_rows = min(r.shape[0] - o * tr for r, o in zip(rows, offs))
    nr, nc = len(rows), len(consts)

    def body(*refs):
        row_refs, const_refs, out_refs = refs[:nr], refs[nr:nr + nc], refs[nr + nc:]
        i = pl.program_id(0)
        cvals = [r[...] for r in const_refs]

        def step(j, carry):
            sl = pl.ds(pl.multiple_of(j * sr, sr), sr)
            res = f(i, *[r[sl, :] for r in row_refs], *cvals)
            for o, v in zip(out_refs, res):
                o[sl, :] = v.astype(o.dtype)
            return carry

        lax.fori_loop(0, tr // sr, step, 0)

    in_specs = [pl.BlockSpec((tr, r.shape[1]), _shifted(o)) for r, o in zip(rows, offs)]
    in_specs += [pl.BlockSpec(c.shape, lambda i: (0, 0)) for c in consts]
    return pl.pallas_call(
        body, name=name, grid=(n_rows // tr,), in_specs=in_specs,
        out_specs=[pl.BlockSpec((tr, w), lambda i: (i, 0)) for w, _ in outs],
        out_shape=[jax.ShapeDtypeStruct((n_rows, w), dt) for w, dt in outs],
        compiler_params=_cparams(dimension_semantics=("parallel",)),
    )(*rows, *consts)


def row_vjp(name, f, rows, consts, cots, *, row_mask, const_mask, tr, sr, offs=None, bf16_rows=(), side=None):
    offs = offs or [0] * len(rows)
    n_rows = min(r.shape[0] - o * tr for r, o in zip(rows, offs))
    nr, nc = len(rows), len(consts)
    cot_in = [c for c in cots if c is not None]
    nct = len(cot_in)
    d_rows = [i for i in range(nr) if row_mask[i]]
    d_consts = [i for i in range(nc) if const_mask[i]]

    def body(*refs):
        row_refs, const_refs = refs[:nr], refs[nr:nr + nc]
        cot_refs = refs[nr + nc:nr + nc + nct]
        drow_refs = refs[nr + nc + nct:nr + nc + nct + len(d_rows)]
        dconst_refs = refs[nr + nc + nct + len(d_rows):]
        i = pl.program_id(0)

        @pl.when(i == 0)
        def _():
            for r in dconst_refs:
                r[...] = jnp.zeros_like(r)

        cvals = [r[...] for r in const_refs]

        def step(j, carry):
            sl = pl.ds(pl.multiple_of(j * sr, sr), sr)
            rvals = [r[sl, :] for r in row_refs]

            def g(*diff):
                rv, cv = list(rvals), list(cvals)
                for idx, val in zip(d_rows, diff[:len(d_rows)]):
                    rv[idx] = val
                for idx, val in zip(d_consts, diff[len(d_rows):]):
                    cv[idx] = val
                return f(i, *rv, *cv)

            primals = [rvals[idx].astype(F32) for idx in d_rows] + [cvals[idx] for idx in d_consts]
            res, vjp = jax.vjp(g, *primals)
            it = iter(cot_refs)
            cts = tuple(jnp.zeros_like(o) if c is None else next(it)[sl, :].astype(o.dtype) for o, c in zip(res, cots))
            grads = vjp(cts)
            for r, val in zip(drow_refs, grads[:len(d_rows)]):
                r[sl, :] = val.astype(r.dtype)
            for r, val in zip(dconst_refs, grads[len(d_rows):]):
                r[...] += val
            return carry

        lax.fori_loop(0, tr // sr, step, 0)

    in_specs = [pl.BlockSpec((tr, r.shape[1]), _shifted(o)) for r, o in zip(rows, offs)]
    in_specs += [pl.BlockSpec(c.shape, lambda i: (0, 0)) for c in consts]
    in_specs += [pl.BlockSpec((tr, c.shape[1]), lambda i: (i, 0)) for c in cot_in]
    out_specs = [pl.BlockSpec((tr, rows[i].shape[1]), lambda i: (i, 0)) for i in d_rows]
    out_specs += [pl.BlockSpec(consts[i].shape, lambda i: (0, 0)) for i in d_consts]
    out_shape = [jax.ShapeDtypeStruct((n_rows, rows[i].shape[1]), BF16 if i in bf16_rows else F32) for i in d_rows]
    out_shape += [jax.ShapeDtypeStruct(consts[i].shape, F32) for i in d_consts]
    n_in, n_out = nr + nc + nct, len(d_rows) + len(d_consts)
    s_in, s_out, s_shape, s_scratch, s_alias, s_ops = _side_call_args(side, n_in, n_out)
    res = pl.pallas_call(
        _with_side(body, side, n_in, n_out, 0, n_rows // tr), name=name, grid=(n_rows // tr,),
        in_specs=in_specs + s_in, out_specs=out_specs + s_out, out_shape=out_shape + s_shape,
        scratch_shapes=s_scratch, input_output_aliases=s_alias,
        compiler_params=_cparams(dimension_semantics=("arbitrary",)),
    )(*rows, *consts, *cot_in, *s_ops)
    if side is None:
        return list(res[:len(d_rows)]), list(res[len(d_rows):])
    return list(res[:len(d_rows)]), list(res[len(d_rows):n_out]), list(res[n_out:])


def _sigmoid(x):
    return 1.0 / (1.0 + jnp.exp(-x))


def _softplus(u):
    return jnp.maximum(u, 0.0) + jnp.log(1.0 + jnp.exp(-jnp.abs(u)))


def _rms(x, g):
    ms = jnp.sum(x * x, axis=-1, keepdims=True) * (1.0 / x.shape[-1])
    return x * lax.rsqrt(ms + NORM_EPS) * g


def _hsum_impl(x, ones2):
    rows, width = x.shape
    nch = width // LANES
    xs = jnp.concatenate([x[:, j * LANES:(j + 1) * LANES] for j in range(nch)], axis=0)
    hi = xs.astype(BF16)
    lo = (xs - hi.astype(F32)).astype(BF16)
    ys = jnp.dot(jnp.concatenate([hi, lo], axis=1), ones2, preferred_element_type=F32)
    return jnp.concatenate([ys[j * rows:(j + 1) * rows] for j in range(nch)], axis=1)


@jax.custom_vjp
def _hsum(x, ones2):
    return _hsum_impl(x, ones2)


def _hsum_fwd(x, ones2):
    return _hsum_impl(x, ones2), ones2


def _hsum_bwd(ones2, g):
    return _hsum_impl(g, ones2), jnp.zeros_like(ones2)


_hsum.defvjp(_hsum_fwd, _hsum_bwd)


def f_silu(i, x):
    return (x * _sigmoid(x),)


def f_sigmoid(i, x):
    return (_sigmoid(x),)


def f_tanh(i, x):
    return (jnp.tanh(x),)


def f_norm_mod(n_ctx_tiles, i, xin, g, sh_c, sc_c, sh_x, sc_x):
    is_x = i >= n_ctx_tiles
    sh = jnp.where(is_x, sh_x, sh_c)
    sc = jnp.where(is_x, sc_x, sc_c)
    return (_rms(xin, g) * (1.0 + sc) + sh,)


def f_res_norm_mod(i, x, y, gt, g, sh, sc):
    x1 = x + gt * y
    return x1, _rms(x1, g) * (1.0 + sc) + sh


def f_post(i, k, zw0, zw1, za0, za1, kkp, ka, w00, w01, a00, a01, ones2):
    kq = k * kkp
    kk = kq / jnp.maximum(jnp.sqrt(_hsum(kq * kq, ones2)), 1e-12)

    def direction(zw, za, w0, a0):
        log_w = -_softplus(-(w0 + zw)) - 0.5
        a = _sigmoid(a0 + za)
        return jnp.exp(-jnp.exp(log_w)), k * (1.0 + (a - 1.0) * ka), kk * a

    dec0, kd0, bb0 = direction(zw0, za0, w00, a00)
    dec1, kd1, bb1 = direction(zw1, za1, w01, a01)
    return -kk, -kk, dec0, dec1, kd0, kd1, bb0, bb1, kd0 + kd1


def f_post_fwd(*a):
    return f_post(*a)[1:]


def f_readout(i, y0, y1, r, ksum, v, g, rk, lnw, lnb, ones2):
    y = y0 + y1
    yc = y - _hsum(y, ones2) * (1.0 / HEAD)
    var = _hsum(yc * yc, ones2) * (1.0 / HEAD)
    o = yc * lax.rsqrt(var + GN_EPS) * lnw + lnb
    o = o + _hsum(r * ksum * rk, ones2) * v
    return (o * g,)


def f_sum(i, *xs):
    acc = xs[0].astype(F32)
    for x in xs[1:]:
        acc = acc + x.astype(F32)
    return (acc,)


def sum_cast(name, arrs, dtype, tr, offs=None):
    return row_call(name, f_sum, arrs, [], [(arrs[0].shape[1], dtype)], tr=tr, sr=16, offs=offs)[0]


def swiglu_fwd(name, ab, tr):
    T, F2 = ab.shape
    F = F2 // 2
    sr = 16

    def body(ab_ref, o_ref):
        def step(j, carry):
            sl = pl.ds(pl.multiple_of(j * sr, sr), sr)
            a, b = ab_ref[sl, :F], ab_ref[sl, F:]
            o_ref[sl, :] = (a * _sigmoid(a) * b).astype(o_ref.dtype)
            return carry

        lax.fori_loop(0, tr // sr, step, 0)

    return pl.pallas_call(
        body, name=name, grid=(T // tr,), in_specs=[pl.BlockSpec((tr, F2), lambda i: (i, 0))],
        out_specs=pl.BlockSpec((tr, F), lambda i: (i, 0)), out_shape=jax.ShapeDtypeStruct((T, F), BF16),
        compiler_params=_cparams(dimension_semantics=("parallel",)),
    )(ab)


def swiglu_bwd(name, ab, dsw, tr):
    T, F2 = ab.shape
    F = F2 // 2
    sr = 16

    def body(ab_ref, d_ref, o_ref):
        def step(j, carry):
            sl = pl.ds(pl.multiple_of(j * sr, sr), sr)
            a, b, d = ab_ref[sl, :F], ab_ref[sl, F:], d_ref[sl, :]
            sg = _sigmoid(a)
            o_ref[sl, :F] = (d * b * (sg + a * sg * (1.0 - sg))).astype(o_ref.dtype)
            o_ref[sl, F:] = (d * a * sg).astype(o_ref.dtype)
            return carry

        lax.fori_loop(0, tr // sr, step, 0)

    return pl.pallas_call(
        body, name=name, grid=(T // tr,),
        in_specs=[pl.BlockSpec((tr, F2), lambda i: (i, 0)), pl.BlockSpec((tr, F), lambda i: (i, 0))],
        out_specs=pl.BlockSpec((tr, F2), lambda i: (i, 0)), out_shape=jax.ShapeDtypeStruct((T, F2), BF16),
        compiler_params=_cparams(dimension_semantics=("parallel",)),
    )(ab, dsw)


def final_call(name, x3, f1, gt, fg, tgt, tr):
    T, D = x3.shape
    sr = 16

    def f(x, y, gtv, g, t):
        err = _rms(x + gtv * y, g) - t
        return 0.5 * jnp.sum(err * err) * (1.0 / D)

    def body(x_ref, y_ref, gt_ref, g_ref, t_ref, dx_ref, dy_ref, dgt_ref, dg_ref, loss_ref):
        @pl.when(pl.program_id(0) == 0)
        def _():
            dgt_ref[...] = jnp.zeros_like(dgt_ref)
            dg_ref[...] = jnp.zeros_like(dg_ref)
            loss_ref[...] = jnp.zeros_like(loss_ref)

        def step(j, carry):
            sl = pl.ds(pl.multiple_of(j * sr, sr), sr)
            val, vjp = jax.vjp(lambda x, y, a, b: f(x, y, a, b, t_ref[sl, :]), x_ref[sl, :], y_ref[sl, :],
                               gt_ref[...], g_ref[...])
            dx, dy, dgt, dg = vjp(jnp.ones((), F32))
            dx_ref[sl, :] = dx
            dy_ref[sl, :] = dy.astype(dy_ref.dtype)
            dgt_ref[...] += dgt
            dg_ref[...] += dg
            loss_ref[...] += jnp.full(loss_ref.shape, val, F32)
            return carry

        lax.fori_loop(0, tr // sr, step, 0)

    row = pl.BlockSpec((tr, D), lambda i: (i, 0))
    vec = pl.BlockSpec((1, D), lambda i: (0, 0))
    return pl.pallas_call(
        body, name=name, grid=(T // tr,), in_specs=[row, row, vec, vec, row],
        out_specs=[row, row, vec, vec, pl.BlockSpec((8, LANES), lambda i: (0, 0))],
        out_shape=[jax.ShapeDtypeStruct((T, D), F32), jax.ShapeDtypeStruct((T, D), BF16)]
        + [jax.ShapeDtypeStruct((1, D), F32)] * 2
        + [jax.ShapeDtypeStruct((8, LANES), F32)],
        compiler_params=_cparams(dimension_semantics=("arbitrary",)),
    )(x3, f1, gt, fg, tgt)


def _tshift(x, kind, period):
    n = x.shape[0]
    t = lax.broadcasted_iota(jnp.int32, x.shape, 0)
    if kind == 0:
        return jnp.where((t & (period - 1)) == 0, 0.0, pltpu.roll(x, 1, 0))
    if kind == 1:
        return jnp.where(((t & (period - 1)) == period - 1) | (t == n - 1), 0.0, pltpu.roll(x, n - 1, 0))
    if kind == 2:
        return jnp.where(t < GRID_W, 0.0, pltpu.roll(x, GRID_W, 0))
    return jnp.where(t >= n - GRID_W, 0.0, pltpu.roll(x, n - GRID_W, 0))


def _pow2_at_least(n):
    return 1 << (n - 1).bit_length()


def _shift_into(dst_ref, h_ref, n_ctx, cb, D, transpose):
    j = pl.program_id(0)
    quarter = (j * cb * 4) // D
    half = (j * cb * 2) // D
    flip = 1 if transpose else 0
    for q in range(4):
        @pl.when(quarter == q)
        def _(q=q):
            dst_ref[n_ctx:, :] = _tshift(h_ref[n_ctx:, :], q ^ flip, GRID_W)
    for q in range(2):
        @pl.when(half == q)
        def _(q=q):
            dst_ref[:n_ctx, :] = _tshift(h_ref[:n_ctx, :], q ^ flip, _pow2_at_least(n_ctx))


def mix_fwd(name, h, mix, n_ctx):
    R, D = h.shape
    cb = LANES

    def body(h_ref, mix_ref, *rest):
        outs, hs_ref = rest[:6], rest[6]
        _shift_into(hs_ref, h_ref, n_ctx, cb, D, False)
        hv = h_ref[...]
        xx = hs_ref[...] - hv
        for m in range(6):
            outs[m][...] = (hv + xx * mix_ref[m:m + 1, :]).astype(BF16)

    col = pl.BlockSpec((R, cb), lambda j: (0, j))
    return pl.pallas_call(
        body, name=name, grid=(D // cb,), in_specs=[col, pl.BlockSpec((mix.shape[0], cb), lambda j: (0, j))],
        out_specs=[col] * 6, out_shape=[jax.ShapeDtypeStruct((R, D), BF16)] * 6,
        scratch_shapes=[pltpu.VMEM((R, cb), F32)],
        compiler_params=_cparams(dimension_semantics=("parallel",)),
    )(h, mix)


def mix_bwd(name, h, mix, cots, slots, n_ctx):
    R, D = h.shape
    cb = LANES
    nc = len(cots)

    def body(h_ref, mix_ref, *rest):
        cot_refs, dh_ref, dmix_ref, hs_ref, dxx_ref = rest[:nc], rest[nc], rest[nc + 1], rest[nc + 2], rest[nc + 3]
        _shift_into(hs_ref, h_ref, n_ctx, cb, D, False)
        xx = hs_ref[...] - h_ref[...]
        per_slot = [None] * 6
        for cref, m in zip(cot_refs, slots):
            per_slot[m] = cref[...] if per_slot[m] is None else per_slot[m] + cref[...]
        dh = jnp.zeros((R, cb), F32)
        dxx = jnp.zeros((R, cb), F32)
        rows = []
        for m in range(6):
            d = per_slot[m]
            dh = dh + d
            dxx = dxx + d * mix_ref[m:m + 1, :]
            rows.append(jnp.sum(d * xx, axis=0, keepdims=True))
        dmix_ref[...] = jnp.concatenate(rows + [jnp.zeros((2, cb), F32)], axis=0)
        dxx_ref[...] = dxx
        _shift_into(hs_ref, dxx_ref, n_ctx, cb, D, True)
        dh_ref[...] = dh - dxx + hs_ref[...]

    col = pl.BlockSpec((R, cb), lambda j: (0, j))
    return pl.pallas_call(
        body, name=name, grid=(D // cb,), in_specs=[col, pl.BlockSpec((mix.shape[0], cb), lambda j: (0, j))] + [col] * nc,
        out_specs=[col, pl.BlockSpec((8, cb), lambda j: (0, j))],
        out_shape=[jax.ShapeDtypeStruct((R, D), F32), jax.ShapeDtypeStruct((8, D), F32)],
        scratch_shapes=[pltpu.VMEM((R, cb), F32), pltpu.VMEM((R, cb), F32)],
        compiler_params=_cparams(dimension_semantics=("parallel",)),
    )(h, mix, *cots)


def _conv_parts(gb_ref, gc_ref, u_ref, cw_ref):
    T = gb_ref.shape[0]
    z = gc_ref[...] * u_ref[...]
    zp, zn = _tshift(z, 0, _pow2_at_least(T)), _tshift(z, 1, _pow2_at_least(T))
    conv = zp * cw_ref[0:1, :] + z * cw_ref[1:2, :] + zn * cw_ref[2:3, :]
    return z, zp, zn, conv


def conv_fwd(name, gcu, cw):
    T, D3 = gcu.shape
    D = D3 // 3
    cb = LANES
    nb = D // cb

    def body(gb_ref, gc_ref, u_ref, cw_ref, o_ref):
        _, _, _, conv = _conv_parts(gb_ref, gc_ref, u_ref, cw_ref)
        o_ref[...] = (gb_ref[...] * conv).astype(BF16)

    def part(p):
        return pl.BlockSpec((T, cb), lambda j: (0, j + p * nb))

    return pl.pallas_call(
        body, name=name, grid=(nb,),
        in_specs=[part(0), part(1), part(2), pl.BlockSpec((cw.shape[0], cb), lambda j: (0, j))],
        out_specs=pl.BlockSpec((T, cb), lambda j: (0, j)), out_shape=jax.ShapeDtypeStruct((T, D), BF16),
        compiler_params=_cparams(dimension_semantics=("parallel",)),
    )(gcu, gcu, gcu, cw)


def conv_bwd(name, gcu, cw, dp):
    T, D3 = gcu.shape
    D = D3 // 3
    cb = LANES
    nb = D // cb

    def body(gb_ref, gc_ref, u_ref, cw_ref, dp_ref, o_ref, dcw_ref):
        part = pl.program_id(1)
        z, zp, zn, conv = _conv_parts(gb_ref, gc_ref, u_ref, cw_ref)
        dpv = dp_ref[...]
        dconv = dpv * gb_ref[...]
        period = _pow2_at_least(T)
        dz = (_tshift(dconv * cw_ref[0:1, :], 1, period) + dconv * cw_ref[1:2, :]
              + _tshift(dconv * cw_ref[2:3, :], 0, period))

        @pl.when(part == 0)
        def _():
            o_ref[...] = (dpv * conv).astype(o_ref.dtype)
            dcw_ref[...] = jnp.concatenate(
                [jnp.sum(dconv * s, axis=0, keepdims=True) for s in (zp, z, zn)] + [jnp.zeros((5, cb), F32)], axis=0)

        @pl.when(part == 1)
        def _():
            o_ref[...] = (dz * u_ref[...]).astype(o_ref.dtype)

        @pl.when(part == 2)
        def _():
            o_ref[...] = (dz * gc_ref[...]).astype(o_ref.dtype)

    def part_spec(p):
        return pl.BlockSpec((T, cb), lambda j, q: (0, j + p * nb))

    return pl.pallas_call(
        body, name=name, grid=(nb, 3),
        in_specs=[part_spec(0), part_spec(1), part_spec(2), pl.BlockSpec((cw.shape[0], cb), lambda j, q: (0, j)),
                  pl.BlockSpec((T, cb), lambda j, q: (0, j))],
        out_specs=[pl.BlockSpec((T, cb), lambda j, q: (0, j + q * nb)), pl.BlockSpec((8, cb), lambda j, q: (0, j))],
        out_shape=[jax.ShapeDtypeStruct((T, D3), BF16), jax.ShapeDtypeStruct((8, D), F32)],
        compiler_params=_cparams(dimension_semantics=("arbitrary", "arbitrary")),
    )(gcu, gcu, gcu, cw, dp)


SCAN_TC = 8


def _scan_consts():
    rows = lax.broadcasted_iota(jnp.int32, (HEAD, LANES), 0)
    cols = lax.broadcasted_iota(jnp.int32, (HEAD, LANES), 1)
    eye = rows == (cols & (HEAD - 1))
    r2 = lax.broadcasted_iota(jnp.int32, (2 * LANES, LANES), 0)
    c2 = lax.broadcasted_iota(jnp.int32, (2 * LANES, LANES), 1)
    ones2 = (((r2 & (LANES - 1)) >= HEAD) == (c2 >= HEAD)).astype(BF16)
    return eye, ones2, ones2[:LANES]


SCAN_ROW_CHUNKS = 4


def _chunks_of_heads(hp):
    per = max(1, hp // SCAN_ROW_CHUNKS)
    return [range(lo, lo + per) for lo in range(0, hp, per)]


def _rows_of(heads):
    return pl.ds(heads[0] * HEAD, len(heads) * HEAD)


def _split2(p):
    hi = p.astype(BF16)
    lo = (p - hi.astype(F32)).astype(BF16)
    return jnp.concatenate([hi, lo], axis=1)


def _head_rows(h):
    return pl.ds(h * HEAD, HEAD)


def _expand_into(dst, p1_ref, row_ref, t, hp, eye, ones1):
    for h in range(hp):
        p1_ref[_head_rows(h), :] = jnp.where(eye, row_ref[t, h:h + 1, :], 0.0).astype(BF16)
    dst[...] = jnp.dot(p1_ref[...], ones1, preferred_element_type=F32)


def _colsum_store(ref, t, h, x):
    ref[t, pl.ds(h, 1), :] = jnp.sum(x, axis=0, keepdims=True)


def _order(i, n_ctx, n_all, rev):
    if not rev:
        return i
    return jnp.where(i < n_ctx, n_ctx - 1 - i, n_all - 1 - (i - n_ctx))


def _with_side(work, side, n_in, n_out, n_scratch, grid):
    if side is None:
        return work
    nsi, nso = len(side.ins), len(side.out_shapes)
    grid = (grid,) if isinstance(grid, int) else tuple(grid)

    def at(which):
        cond = None
        for d, g in enumerate(grid):
            c = pl.program_id(d) == (0 if which == "first" else g - 1)
            cond = c if cond is None else cond & c
        return cond

    def body(*refs):
        ins, side_in = refs[:n_in], refs[n_in:n_in + nsi]
        outs = refs[n_in + nsi:n_in + nsi + n_out]
        side_out = refs[n_in + nsi + n_out:n_in + nsi + n_out + nso]
        scratch = refs[n_in + nsi + n_out + nso:]

        @pl.when(at("first"))
        def _():
            side.start(side_in, side_out, scratch[n_scratch:])

        work(*ins, *outs, *scratch[:n_scratch])

        @pl.when(at("last"))
        def _():
            side.finish(side_in, side_out, scratch[n_scratch:])

    return body


def _side_call_args(side, n_in, n_out):
    if side is None:
        return [], [], [], [], {}, []
    aliases = {n_in + i: n_out + o for i, o in side.aliases.items()}
    return ([ANY] * len(side.ins), [ANY] * len(side.out_shapes), list(side.out_shapes), list(side.sems), aliases,
            list(side.ins))


def scan_fwd(name, r, w, k, v, a, b, n_ctx_rows, rev, side=None):
    R, D = r.shape
    hp, tc = D // LANES, 2 * SCAN_TC
    n_all, n_ctx = R // tc, n_ctx_rows // tc
    ins = [t.reshape(R, hp, LANES) for t in (r, w, k, v, a, b)]

    def work(r_ref, w_ref, k_ref, v_ref, a_ref, b_ref, y_ref, st_ref, s_ref, ve_ref, sa_ref, p_ref, p1_ref, ys_ref):
        @pl.when(pl.program_id(0) == 0)
        def _():
            s_ref[...] = jnp.zeros_like(s_ref)

        eye, ones2, ones1 = _scan_consts()

        def row_of(q):
            return tc - 1 - q if rev else q

        def expand(q):
            _expand_into(ve_ref.at[q], p1_ref.at[q % 2], v_ref, row_of(q), hp, eye, ones1)

        def advance(q):
            t, prev_ref = row_of(q), (s_ref if q == 0 else st_ref.at[q - 1])
            for heads in _chunks_of_heads(hp):
                rows = _rows_of(heads)
                for h in heads:
                    p_ref[q % 2, _head_rows(h), :] = _split2(prev_ref[_head_rows(h), :] * a_ref[t, h:h + 1, :])
                sa_ref[q % 2, rows, :] = jnp.dot(p_ref[q % 2, rows, :], ones2, preferred_element_type=F32)
                for h in heads:
                    hr_ = _head_rows(h)
                    st_ref[q, hr_, :] = (prev_ref[hr_, :] * w_ref[t, h:h + 1, :]
                                         + sa_ref[q % 2, hr_, :] * b_ref[t, h:h + 1, :]
                                         + ve_ref[q, hr_, :] * k_ref[t, h:h + 1, :])

        def readout(q):
            t = row_of(q)
            for h in range(hp):
                p1_ref[2 + q % 2, _head_rows(h), :] = (st_ref[q, _head_rows(h), :]
                                                       * r_ref[t, h:h + 1, :]).astype(BF16)
            ys_ref[q % 2] = jnp.dot(p1_ref[2 + q % 2], ones1, preferred_element_type=F32)
            for h in range(hp):
                _colsum_store(y_ref, t, h, jnp.where(eye, ys_ref[q % 2, _head_rows(h), :], 0.0))

        expand(0)
        for q in range(tc):
            if q + 1 < tc:
                expand(q + 1)
            advance(q)
            if q > 0:
                readout(q - 1)
        readout(tc - 1)
        s_ref[...] = st_ref[tc - 1]

    row_spec = pl.BlockSpec((tc, hp, LANES), lambda i: (_order(i, n_ctx, n_all, rev), 0, 0))
    n = hp * HEAD
    s_in, s_out, s_shape, s_scratch, s_alias, s_ops = _side_call_args(side, 6, 2)
    y, st, *side_res = pl.pallas_call(
        _with_side(work, side, 6, 2, 6, n_all), name=name, grid=(n_all,), in_specs=[row_spec] * 6 + s_in,
        out_specs=[row_spec, pl.BlockSpec((tc, n, LANES), lambda i: (i, 0, 0))] + s_out,
        out_shape=[jax.ShapeDtypeStruct((R, hp, LANES), F32), jax.ShapeDtypeStruct((R, n, LANES), F32)] + s_shape,
        scratch_shapes=[pltpu.VMEM((n, LANES), F32), pltpu.VMEM((tc, n, LANES), F32), pltpu.VMEM((2, n, LANES), F32),
                        pltpu.VMEM((2, n, 2 * LANES), BF16), pltpu.VMEM((4, n, LANES), BF16),
                        pltpu.VMEM((2, n, LANES), F32)] + s_scratch,
        input_output_aliases=s_alias,
        compiler_params=_cparams(dimension_semantics=("arbitrary",)),
    )(*ins, *s_ops)
    return y.reshape(R, D), st, side_res


def scan_bwd(name, r, w, k, v, a, b, dy, st, n_ctx_rows, rev, side=None):
    R, D = r.shape
    hp, tc = D // LANES, SCAN_TC
    n_all, n_ctx = R // tc, n_ctx_rows // tc
    ins = [t.reshape(R, hp, LANES) for t in (r, w, k, v, a, b, dy)]

    def work(r_ref, w_ref, k_ref, v_ref, a_ref, b_ref, dy_ref, st_ref, prev_ref,
             dr_ref, dw_ref, dk_ref, dv_ref, da_ref, db_ref,
             g_ref, s0_ref, ve_ref, dye_ref, sa_ref, gs_ref, tmp_ref, p_ref, p1_ref, tmp2_ref):
        i = pl.program_id(0)

        @pl.when(i == 0)
        def _():
            g_ref[...] = jnp.zeros_like(g_ref)

        eye, ones2, ones1 = _scan_consts()

        @pl.when(i == n_all - 1)
        def _():
            s0_ref[...] = jnp.zeros_like(s0_ref)

        @pl.when(i != n_all - 1)
        def _():
            s0_ref[...] = prev_ref[0]

        def row_of(q):
            return tc - 1 - q if rev else q

        def prev_of(q):
            return s0_ref if q == 0 else st_ref.at[q - 1]

        def before(q):
            t, prev = row_of(q), prev_of(q)
            _expand_into(ve_ref.at[q], p1_ref.at[2 + q % 2], v_ref, t, hp, eye, ones1)
            _expand_into(dye_ref.at[q], p1_ref.at[4 + q % 2], dy_ref, t, hp, eye, ones1)
            for h in range(hp):
                p1_ref[q % 2, _head_rows(h), :] = (prev[_head_rows(h), :] * a_ref[t, h:h + 1, :]).astype(BF16)
            sa_ref[q] = jnp.dot(p1_ref[q % 2], ones1, preferred_element_type=F32)

        def back(q):
            t, prev = row_of(q), prev_of(q)
            for heads in _chunks_of_heads(hp):
                rows = _rows_of(heads)
                for h in heads:
                    hr_ = _head_rows(h)
                    g = g_ref[hr_, :] + dye_ref[q, hr_, :] * r_ref[t, h:h + 1, :]
                    gs_ref[q, hr_, :] = g
                    p_ref[q % 2, hr_, :] = _split2(g * b_ref[t, h:h + 1, :])
                tmp_ref[q % 2, rows, :] = jnp.dot(p_ref[q % 2, rows, :], ones2, preferred_element_type=F32)
                for h in heads:
                    hr_ = _head_rows(h)
                    dsa = tmp_ref[q % 2, hr_, :]
                    _colsum_store(da_ref, t, h, prev[hr_, :] * dsa)
                    g_ref[hr_, :] = gs_ref[q, hr_, :] * w_ref[t, h:h + 1, :] + dsa * a_ref[t, h:h + 1, :]

        def after(q):
            t, prev = row_of(q), prev_of(q)
            for h in range(hp):
                hr_ = _head_rows(h)
                g = gs_ref[q, hr_, :]
                p1_ref[6 + q % 2, hr_, :] = (g * k_ref[t, h:h + 1, :]).astype(BF16)
                _colsum_store(dr_ref, t, h, st_ref[q, hr_, :] * dye_ref[q, hr_, :])
                _colsum_store(dk_ref, t, h, g * ve_ref[q, hr_, :])
                _colsum_store(dw_ref, t, h, g * prev[hr_, :])
                _colsum_store(db_ref, t, h, g * sa_ref[q, hr_, :])
            tmp2_ref[q % 2] = jnp.dot(p1_ref[6 + q % 2], ones1, preferred_element_type=F32)
            for h in range(hp):
                _colsum_store(dv_ref, t, h, jnp.where(eye, tmp2_ref[q % 2, _head_rows(h), :], 0.0))

        before(tc - 1)
        for q in reversed(range(tc)):
            if q > 0:
                before(q - 1)
            back(q)
            if q < tc - 1:
                after(q + 1)
        after(0)

    def pos(i):
        return n_all - 1 - i

    n = hp * HEAD
    row_spec = pl.BlockSpec((tc, hp, LANES), lambda i: (_order(pos(i), n_ctx, n_all, rev), 0, 0))
    big = pltpu.VMEM((tc, n, LANES), F32)
    one = pltpu.VMEM((n, LANES), F32)
    s_in, s_out, s_shape, s_scratch, s_alias, s_ops = _side_call_args(side, 9, 6)
    outs = pl.pallas_call(
        _with_side(work, side, 9, 6, 10, n_all), name=name, grid=(n_all,),
        in_specs=[row_spec] * 7 + [
            pl.BlockSpec((tc, n, LANES), lambda i: (pos(i), 0, 0)),
            pl.BlockSpec((1, n, LANES), lambda i: (jnp.maximum(pos(i) * tc - 1, 0), 0, 0))] + s_in,
        out_specs=[row_spec] * 6 + s_out,
        out_shape=[jax.ShapeDtypeStruct((R, hp, LANES), F32)] * 6 + s_shape,
        scratch_shapes=[one, one, big, big, big, big, pltpu.VMEM((2, n, LANES), F32),
                        pltpu.VMEM((2, n, 2 * LANES), BF16), pltpu.VMEM((8, n, LANES), BF16),
                        pltpu.VMEM((2, n, LANES), F32)] + s_scratch,
        input_output_aliases=s_alias,
        compiler_params=_cparams(dimension_semantics=("arbitrary",)),
    )(*ins, st, st, *s_ops)
    return [o.reshape(R, D) for o in outs[:6]], list(outs[6:])


ANY = pl.BlockSpec(memory_space=pl.ANY)


def _peer(xi, yi, ci, k):
    return (1 - xi if k & 4 else xi, 1 - yi if k & 2 else yi, 1 - ci if k & 1 else ci)


def _rcopy(src, dst, send_sem, recv_sem, dev):
    return pltpu.make_async_remote_copy(src_ref=src, dst_ref=dst, send_sem=send_sem, recv_sem=recv_sem,
                                        device_id=dev, device_id_type=MESH)


def _drain(copies):
    for cp in copies:
        if cp.is_remote:
            cp.wait_send()
        else:
            cp.wait()


def all_gather8(name, x):
    r, c = x.shape

    def body(x_ref, out_ref, send_sems, recv_sems, local_sem):
        xi, yi, ci = _place()

        def blk(p):
            return out_ref.at[4 * p[0] + 2 * p[1] + p[2]]

        me = (xi, yi, ci)
        mine = pltpu.make_async_copy(x_ref, blk(me), local_sem.at[0])
        mine.start()
        sends = [_rcopy(x_ref, blk(me), send_sems.at[k - 1], recv_sems.at[k - 1], _peer(xi, yi, ci, k))
                 for k in range(1, N_DEV)]
        for cp in sends:
            cp.start()
        for k in range(1, N_DEV):
            p = _peer(xi, yi, ci, k)
            _rcopy(x_ref, blk(p), send_sems.at[k - 1], recv_sems.at[k - 1], p).wait_recv()
        for cp in sends:
            cp.wait_send()
        mine.wait()

    vm = pl.BlockSpec(memory_space=pltpu.VMEM)
    return pl.pallas_call(
        body, name=name, in_specs=[vm], out_specs=vm, out_shape=jax.ShapeDtypeStruct((N_DEV, r, c), x.dtype),
        scratch_shapes=[pltpu.SemaphoreType.DMA((N_DEV - 1,)), pltpu.SemaphoreType.DMA((N_DEV - 1,)),
                        pltpu.SemaphoreType.DMA((1,))],
        compiler_params=_cparams(),
    )(x)


def _chips(xi, yi):
    chips = [(1 - xi, yi), (xi, 1 - yi), (1 - xi, 1 - yi)]
    return chips, [2 * cx + cy for cx, cy in chips]


def gather_weights(name, stacked):
    n = len(stacked)

    def body(*refs):
        _gather_start(refs[n:2 * n], refs[2 * n:])
        _gather_finish(refs[n:2 * n], refs[2 * n:])

    return pl.pallas_call(
        body, name=name, in_specs=[ANY] * n, out_specs=[ANY] * n,
        out_shape=[jax.ShapeDtypeStruct(a.shape, a.dtype) for a in stacked],
        input_output_aliases={w: w for w in range(n)},
        scratch_shapes=_gather_sems(n),
        compiler_params=_cparams(),
    )(*stacked)


def _gather_sems(n):
    return [pltpu.SemaphoreType.DMA((n, 6)), pltpu.SemaphoreType.DMA((n, 6))]


def _gather_sends(out, sems):
    send_sems, recv_sems = sems
    xi, yi, ci = _place()
    s = 2 * xi + yi
    chips, _ = _chips(xi, yi)
    sends = []
    for w in range(len(out)):
        hr = out[w].shape[1] // 2
        mine = out[w].at[s, pl.ds(ci * hr, hr)]
        sends += [_rcopy(mine, mine, send_sems.at[w, j], recv_sems.at[w, j], (cx, cy, ci))
                  for j, (cx, cy) in enumerate(chips)]
    return sends


def _gather_start(out, sems):
    for cp in _gather_sends(out, sems):
        cp.start()


def _gather_finish(out, sems):
    send_sems, recv_sems = sems
    xi, yi, ci = _place()
    chips, sidx = _chips(xi, yi)
    sib = (xi, yi, 1 - ci)
    passed = []
    for w in range(len(out)):
        hr = out[w].shape[1] // 2
        for j, (cx, cy) in enumerate(chips):
            blk = out[w].at[sidx[j], pl.ds(ci * hr, hr)]
            _rcopy(blk, blk, send_sems.at[w, j], recv_sems.at[w, j], (cx, cy, ci)).wait_recv()
            fw = _rcopy(blk, blk, send_sems.at[w, 3 + j], recv_sems.at[w, 3 + j], sib)
            fw.start()
            passed.append(fw)
    for w in range(len(out)):
        hr = out[w].shape[1] // 2
        for j in range(3):
            blk = out[w].at[sidx[j], pl.ds((1 - ci) * hr, hr)]
            _rcopy(blk, blk, send_sems.at[w, 3 + j], recv_sems.at[w, 3 + j], sib).wait_recv()
    _drain(_gather_sends(out, sems) + passed)


class Side:
    def __init__(self, ins, out_shapes, aliases, sems, start, finish):
        self.ins, self.out_shapes, self.aliases, self.sems = ins, out_shapes, aliases, sems
        self.start, self.finish = start, finish


def _gather_wait_ici(out, sems):
    send_sems, recv_sems = sems
    xi, yi, ci = _place()
    chips, sidx = _chips(xi, yi)
    for w in range(len(out)):
        hr = out[w].shape[1] // 2
        for j, (cx, cy) in enumerate(chips):
            blk = out[w].at[sidx[j], pl.ds(ci * hr, hr)]
            _rcopy(blk, blk, send_sems.at[w, j], recv_sems.at[w, j], (cx, cy, ci)).wait_recv()
    _drain(_gather_sends(out, sems))


def _pass_copies(out, sems, half_of):
    send_sems, recv_sems = sems
    xi, yi, ci = _place()
    _, sidx = _chips(xi, yi)
    sib = (xi, yi, 1 - ci)
    cps = []
    for w in range(len(out)):
        hr = out[w].shape[1] // 2
        for j in range(3):
            blk = out[w].at[sidx[j], pl.ds(half_of(ci) * hr, hr)]
            cps.append(_rcopy(blk, blk, send_sems.at[w, j], recv_sems.at[w, j], sib))
    return cps


def _pass_start(out, sems):
    for cp in _pass_copies(out, sems, lambda ci: ci):
        cp.start()


def _pass_finish(out, sems):
    for cp in _pass_copies(out, sems, lambda ci: 1 - ci):
        cp.wait_recv()
    _drain(_pass_copies(out, sems, lambda ci: ci))


def gather_side(stacked):
    n = len(stacked)
    return Side(stacked, [jax.ShapeDtypeStruct(a.shape, a.dtype) for a in stacked], {w: w for w in range(n)},
                _gather_sems(n), lambda ins, outs, sems: _gather_start(outs, sems),
                lambda ins, outs, sems: _gather_wait_ici(outs, sems))


def gather_whole_side(stacked):
    n = len(stacked)
    return Side(stacked, [jax.ShapeDtypeStruct(a.shape, a.dtype) for a in stacked], {w: w for w in range(n)},
                _gather_sems(n), lambda ins, outs, sems: _gather_start(outs, sems),
                lambda ins, outs, sems: _gather_finish(outs, sems))


def pass_side(stacked):
    n = len(stacked)
    return Side(stacked, [jax.ShapeDtypeStruct(a.shape, a.dtype) for a in stacked], {w: w for w in range(n)},
                [pltpu.SemaphoreType.DMA((n, 3)), pltpu.SemaphoreType.DMA((n, 3))],
                lambda ins, outs, sems: _pass_start(outs, sems), lambda ins, outs, sems: _pass_finish(outs, sems))


def _rs1_copies(g, out, sems):
    send_sems, recv_sems = sems
    xi, yi, ci = _place()
    return [_rcopy(g[w].at[:, 1 - ci], out[w], send_sems.at[w], recv_sems.at[w], (xi, yi, 1 - ci))
            for w in range(len(g))]


def _rs1_start(g, out, sems):
    for cp in _rs1_copies(g, out, sems):
        cp.start()


def _rs1_finish(g, out, sems):
    for cp in _rs1_copies(g, out, sems):
        cp.wait_recv()
    _drain(_rs1_copies(g, out, sems))


def rs1_side(grads):
    n = len(grads)
    return Side(grads, [jax.ShapeDtypeStruct((N_CHIPS,) + a.shape[2:], a.dtype) for a in grads], {},
                [pltpu.SemaphoreType.DMA((n,)), pltpu.SemaphoreType.DMA((n,))], _rs1_start, _rs1_finish)


def join_sides(a, b):
    na, nao, nas = len(a.ins), len(a.out_shapes), len(a.sems)
    aliases = dict(a.aliases)
    aliases.update({na + i: nao + o for i, o in b.aliases.items()})

    def start(ins, outs, sems):
        a.start(ins[:na], outs[:nao], sems[:nas])
        b.start(ins[na:], outs[nao:], sems[nas:])

    def finish(ins, outs, sems):
        a.finish(ins[:na], outs[:nao], sems[:nas])
        b.finish(ins[na:], outs[nao:], sems[nas:])

    return Side(list(a.ins) + list(b.ins), list(a.out_shapes) + list(b.out_shapes), aliases,
                list(a.sems) + list(b.sems), start, finish)


def rs_pair(name, grads):
    n = len(grads)
    side = rs1_side(grads)

    def body(*refs):
        side.start(refs[:n], refs[n:2 * n], refs[2 * n:])
        side.finish(refs[:n], refs[n:2 * n], refs[2 * n:])

    return pl.pallas_call(
        body, name=name, in_specs=[ANY] * n, out_specs=[ANY] * n, out_shape=side.out_shapes,
        scratch_shapes=side.sems, compiler_params=_cparams(),
    )(*grads)


def _rows_tile(rows, cols, unit=16, limit=1 << 20):
    best = None
    for t in range(unit, rows + 1, unit):
        if rows % t == 0 and t * cols * 4 <= limit:
            best = t
    return best or rows


def rs_add_pair(name, g, got, ci, slot):
    _, _, hr, c = g.shape
    th = _rows_tile(hr, c, limit=1 << 21)

    def body(ci_ref, slot_ref, g_ref, r_ref, own_ref, ob_ref):
        tot = g_ref[...] + r_ref[...]
        ob_ref[...] = tot.astype(BF16)

        @pl.when(pl.program_id(1) == slot_ref[0])
        def _():
            own_ref[...] = tot

    blk = pl.BlockSpec((None, th, c), lambda i, s, ci_ref, slot_ref: (s, i, 0))
    return pl.pallas_call(
        body, name=name,
        grid_spec=pltpu.PrefetchScalarGridSpec(
            num_scalar_prefetch=2, grid=(hr // th, N_CHIPS),
            in_specs=[pl.BlockSpec((None, None, th, c), lambda i, s, ci_ref, slot_ref: (s, ci_ref[0], i, 0)), blk],
            out_specs=[pl.BlockSpec((th, c), lambda i, s, ci_ref, slot_ref: (i, 0)), blk]),
        out_shape=[jax.ShapeDtypeStruct((hr, c), F32), jax.ShapeDtypeStruct((N_CHIPS, hr, c), BF16)],
        compiler_params=_cparams(dimension_semantics=("arbitrary", "arbitrary")),
    )(ci, slot, g, got)


def _rs2_copies(pb, outs, sems):
    send_sems, recv_sems = sems
    xi, yi, ci = _place()
    chips, sidx = _chips(xi, yi)
    return [_rcopy(pb[w].at[sidx[j]], outs[3 * w + j], send_sems.at[w, j], recv_sems.at[w, j], (cx, cy, ci))
            for w in range(len(pb)) for j, (cx, cy) in enumerate(chips)]


def _rs2_start(pb, outs, sems):
    for cp in _rs2_copies(pb, outs, sems):
        cp.start()


def _rs2_finish(pb, outs, sems):
    for cp in _rs2_copies(pb, outs, sems):
        cp.wait_recv()
    _drain(_rs2_copies(pb, outs, sems))


def rs2_side(sums_bf16):
    n = len(sums_bf16)
    out_shapes = [jax.ShapeDtypeStruct(a.shape[1:], BF16) for a in sums_bf16 for _ in range(3)]
    return Side(sums_bf16, out_shapes, {}, [pltpu.SemaphoreType.DMA((n, 3)), pltpu.SemaphoreType.DMA((n, 3))],
                _rs2_start, _rs2_finish)


def rs_swap(name, halves):
    n = len(halves)

    def body(*refs):
        hv, out = refs[:n], refs[n:2 * n]
        send_sems, recv_sems = refs[2 * n:]
        xi, yi, ci = _place()
        sib = (xi, yi, 1 - ci)
        cps = [_rcopy(hv[w], out[w], send_sems.at[w], recv_sems.at[w], sib) for w in range(n)]
        for cp in cps:
            cp.start()
        for cp in cps:
            cp.wait_recv()
        _drain(cps)

    return pl.pallas_call(
        body, name=name, in_specs=[ANY] * n, out_specs=[ANY] * n,
        out_shape=[jax.ShapeDtypeStruct(a.shape, a.dtype) for a in halves],
        scratch_shapes=[pltpu.SemaphoreType.DMA((n,)), pltpu.SemaphoreType.DMA((n,))],
        compiler_params=_cparams(),
    )(*halves)


def cast_to_slot(name, x, slot):
    r, c = x.shape
    tr = _rows_tile(r, c, limit=1 << 22)

    def body(slot_ref, x_ref, o_ref):
        o_ref[...] = x_ref[...].astype(BF16)

    return pl.pallas_call(
        body, name=name,
        grid_spec=pltpu.PrefetchScalarGridSpec(
            num_scalar_prefetch=1, grid=(r // tr,),
            in_specs=[pl.BlockSpec((tr, c), lambda i, slot_ref: (i, 0))],
            out_specs=pl.BlockSpec((None, tr, c), lambda i, slot_ref: (slot_ref[0], i, 0))),
        out_shape=jax.ShapeDtypeStruct((N_CHIPS, r, c), BF16),
        compiler_params=_cparams(dimension_semantics=("parallel",)),
    )(slot, x)


def sum_blocks(name, x, picks):
    _, r, c = x.shape

    def body(x_ref, o_ref):
        acc = x_ref[picks[0]]
        for b in picks[1:]:
            acc = acc + x_ref[b]
        o_ref[...] = acc

    vm = pl.BlockSpec(memory_space=pltpu.VMEM)
    return pl.pallas_call(body, name=name, in_specs=[vm], out_specs=vm, out_shape=jax.ShapeDtypeStruct((r, c), F32),
                          compiler_params=_cparams())(x)


def adamw(name, w, g, m, v):
    r, c = w.shape
    tr = _rows_tile(r, c, unit=8, limit=1 << 20)

    def body(w_ref, g_ref, m_ref, v_ref, d_ref, mo_ref, vo_ref):
        d_ref[...], mo_ref[...], vo_ref[...] = _adam_update(w_ref[...], g_ref[...], m_ref[...], v_ref[...])

    blk = pl.BlockSpec((tr, c), lambda i: (i, 0))
    return pl.pallas_call(
        body, name=name, grid=(r // tr,), in_specs=[blk] * 4, out_specs=[blk] * 3,
        out_shape=[jax.ShapeDtypeStruct((r, c), F32)] * 3,
        compiler_params=_cparams(dimension_semantics=("parallel",)),
    )(w, g, m, v)


def _adam_update(w, gv, m, v):
    c1 = 1.0 / (1.0 - ADAM_B1 ** ADAM_STEP)
    c2 = 1.0 / (1.0 - ADAM_B2 ** ADAM_STEP)
    mn = ADAM_B1 * m + (1.0 - ADAM_B1) * gv
    vn = ADAM_B2 * v + (1.0 - ADAM_B2) * (gv * gv)
    return -ADAM_LR * ((mn * c1) / (jnp.sqrt(vn * c2) + ADAM_EPS) + ADAM_WD * w), mn, vn


def adamw_halves(name, w, m, v, mine, theirs, ci, side=None):
    hr, c = mine[0].shape
    npos = len(mine)
    th = _rows_tile(hr, c, unit=8, limit=1 << 20)
    per = hr // th

    def work(ci_ref, w_ref, m_ref, v_ref, *rest):
        g_refs, (go_ref, d_ref, mo_ref, vo_ref) = rest[:2 * npos], rest[2 * npos:]
        pos, half = pl.program_id(0), pl.program_id(1)
        from_me = half == ci_ref[0]
        gv = jnp.where(from_me, g_refs[0][...], g_refs[npos][...])
        for p in range(1, npos):
            gv = jnp.where(pos == p, jnp.where(from_me, g_refs[p][...], g_refs[npos + p][...]), gv)
        d_ref[...], mo_ref[...], vo_ref[...] = _adam_update(w_ref[...], gv, m_ref[...], v_ref[...])
        go_ref[...] = gv

    full = pl.BlockSpec((th, c), lambda p, h, i, ci_ref: ((p * 2 + h) * per + i, 0))
    part = pl.BlockSpec((th, c), lambda p, h, i, ci_ref: (i, 0))
    rows = 2 * hr * npos
    n_in = 4 + 2 * npos
    s_in, s_out, s_shape, s_scratch, _, s_ops = _side_call_args(side, n_in, 4)
    res = pl.pallas_call(
        _with_side(work, side, n_in, 4, 0, (npos, 2, per)), name=name,
        grid_spec=pltpu.PrefetchScalarGridSpec(
            num_scalar_prefetch=1, grid=(npos, 2, per), in_specs=[full] * 3 + [part] * (2 * npos) + s_in,
            out_specs=[full] * 4 + s_out, scratch_shapes=s_scratch),
        out_shape=[jax.ShapeDtypeStruct((rows, c), F32)] * 4 + s_shape,
        compiler_params=_cparams(dimension_semantics=("arbitrary", "arbitrary", "arbitrary")),
    )(ci, w, m, v, *mine, *theirs, *s_ops)
    return res[0], res[1], res[2], res[3], list(res[4:])


def pack_rows(name, parts, rows):
    width = parts[0].shape[1]
    n = len(parts)

    def body(*refs):
        o_ref = refs[n]
        o_ref[...] = jnp.zeros_like(o_ref)
        off = 0
        for r in refs[:n]:
            o_ref[off:off + r.shape[0], :] = r[...]
            off += r.shape[0]

    vm = pl.BlockSpec(memory_space=pltpu.VMEM)
    return pl.pallas_call(body, name=name, in_specs=[vm] * n, out_specs=vm,
                          out_shape=jax.ShapeDtypeStruct((rows, width), F32), compiler_params=_cparams())(*parts)


def _pad_rows(a, rows):
    return jnp.pad(a, ((0, rows - a.shape[0]), (0, 0)))


def _view2d(name, a):
    if name == 'rw_rk' or a.ndim == 1:
        return a.reshape(1, -1)
    return a.reshape(-1, a.shape[-1])


def _rs_pair_sums(tag, items, ci_arr, s_arr, got=None):
    g4 = [g.reshape(N_CHIPS, 2, g.shape[1] // 2, g.shape[2]) for _, g in items]
    if got is None:
        got = rs_pair("rs1_" + tag, g4)
    return [rs_add_pair(f"rs1add_{tag}{w}", g4[w], got[w], ci_arr, s_arr) for w in range(len(items))]


def _rs_finish(tag, items, sums, landed, s):
    names = []
    for nm, _ in items:
        if nm not in names:
            names.append(nm)
    halves = []
    for w, (r0, r1, r2) in enumerate(landed):
        own = sums[w][0]
        hr, c = own.shape
        halves.append(row_call(f"rs2add_{tag}{w}", f_sum, [own, r0, r1, r2], [], [(c, F32)],
                               tr=_rows_tile(hr, c), sr=16)[0])
    theirs = rs_swap("rs3_" + tag, halves)
    return {name: ([halves[w] for w, (nm, _) in enumerate(items) if nm == name],
                   [theirs[w] for w, (nm, _) in enumerate(items) if nm == name]) for name in names}


def _step(p):
    xi, yi, ci = _place()
    me = 4 * xi + 2 * yi + ci
    s = 2 * xi + yi
    ci_arr = jnp.reshape(ci, (1,)).astype(jnp.int32)
    s_arr = jnp.reshape(s, (1,)).astype(jnp.int32)
    x, ctx, tgt = p['x'][0], p['ctx'][0], p['loss_target'][0]
    T, D = x.shape
    L = ctx.shape[0]
    H, Dq = D // HEAD, D // N_CHIPS
    TR = math.gcd(math.gcd(L, T), 256)
    TS = min(TR, 64)
    nct = L // TR
    LG = p['rw_g1'].shape[-1]
    LW, LA = p['rw_w1'].shape[-1], p['rw_a1'].shape[-1]
    F4 = p['ffn_w2'].shape[1]

    pack = jnp.concatenate([
        _pad_rows(p['c'].reshape(N_CHIPS, Dq), 8), _pad_rows(p['rw_mix'][0], 8), _pad_rows(p['rw_w0'][0], 8),
        _pad_rows(p['rw_a0'][0], 8), _pad_rows(p['sc_conv'][0], 8)], axis=0)
    got = all_gather8("ag_small", pack)
    c_all = got[:, 0:N_CHIPS, :].reshape(N_DEV, D)
    full = jnp.transpose(got[::2], (1, 0, 2)).reshape(40, D)
    mix_f, w0_f, a0_f, conv_f = full[8:16], full[16:24], full[24:32], full[32:40]

    cond_in = jnp.concatenate([c_all, _pad_rows(p['c_ctx'].reshape(1, D), 8)], axis=0)
    cond = row_call("cond", f_silu, [cond_in], [], [(D, F32)], tr=16, sr=16)[0]
    ada = Stacked(p['ada_w'], "layer", D)
    modp = [mm_nn(f"modp{i}", cond, ada.at(i)) for i in range(2)]
    mg = all_gather8("ag_mod", jnp.concatenate(modp, axis=0))
    mod = jnp.transpose(mg[::2].reshape(N_CHIPS, 2, 16, 6 * Dq), (1, 2, 0, 3)).reshape(2, 16, 6 * D)
    mod = mod + p['ada_b'][:, None, :]
    mod_x = lax.dynamic_index_in_dim(mod, me, axis=1, keepdims=False)
    mod_c = mod[:, 8]

    def chunk(vec, j):
        return vec[j * D:(j + 1) * D].reshape(1, D)

    sh1x, sc1x, gt1x, sh2x, sc2x, gt2x = ([chunk(mod_x[i], j) for i in range(2)] for j in range(6))
    sh1c, sc1c = chunk(mod_c[0], 0), chunk(mod_c[0], 1)

    def slots(names):
        return [cast_to_slot("cast_" + n, _view2d(n, p[n]), s_arr) for n in names]

    rw_names = ('rw_wr', 'rw_wk', 'rw_wv', 'rw_wo', 'rw_w1', 'rw_w2', 'rw_a1', 'rw_a2', 'rw_g1', 'rw_g2')
    first_names = tuple(n for n in rw_names if n != 'rw_wo')
    late_names = ('sc_win', 'sc_wout', 'ffn_w13', 'rw_wo', 'ffn_w2')
    gw = dict(zip(first_names, gather_weights("ag_rw", slots(first_names))))
    late_slots = slots(late_names)
    Wr, Wk, Wv = (Stacked(gw[n], "row", Dq) for n in ('rw_wr', 'rw_wk', 'rw_wv'))
    W1, A1, G1 = (Stacked(gw[n], "row", Dq) for n in ('rw_w1', 'rw_a1', 'rw_g1'))
    W2, A2, G2 = Stacked(gw['rw_w2'], "col", LW), Stacked(gw['rw_a2'], "col", LA), Stacked(gw['rw_g2'], "col", LG)

    ones2 = (((lax.broadcasted_iota(jnp.int32, (2 * LANES, LANES), 0) & (LANES - 1)) >= HEAD)
             == (lax.broadcasted_iota(jnp.int32, (2 * LANES, LANES), 1) >= HEAD)).astype(BF16)
    n1g, n2g = p['norm1_g'], p['norm2_g']
    kkp, ka, lnw, lnb = p['rw_kk'], p['rw_ka'], p['rw_lnw'], p['rw_lnb']
    rk = p['rw_rk'].reshape(1, D)
    fg = p['final_g'].reshape(1, D)

    xin = jnp.concatenate([ctx, x], axis=0)
    nm = functools.partial(f_norm_mod, nct)
    nm_consts = [n1g[0:1], sh1c, sc1c, sh1x[0], sc1x[0]]
    h = row_call("l0_norm", nm, [xin], nm_consts, [(D, F32)], tr=TR, sr=16)[0]
    xr, xw, xk, xv, xa, xg = mix_fwd("l0_mix", h, mix_f, L)
    r = mm_nn("l0_r", xr, Wr)
    k = mm_nn("l0_k", xk, Wk)
    v = mm_nn("l0_v", xv, Wv)
    gl = mm_nn("l0_gl", xg, G1)
    sg = row_call("l0_sg", f_sigmoid, [gl], [], [(LG, BF16)], tr=TR, sr=16)[0]
    g = mm_nn("l0_g", sg, G2)
    wl, tw, zw, al, za = [], [], [], [], []
    for d in range(2):
        wl.append(mm_nn(f"l0_wl{d}", xw, W1.at(d)))
        tw.append(row_call(f"l0_tw{d}", f_tanh, [wl[d]], [], [(LW, BF16)], tr=TR, sr=16)[0])
        zw.append(mm_nn(f"l0_zw{d}", tw[d], W2.at(d)))
        al.append(mm_nn(f"l0_al{d}", xa, A1.at(d), BF16))
        za.append(mm_nn(f"l0_za{d}", al[d], A2.at(d)))
    post_rows = [k, zw[0], zw[1], za[0], za[1]]
    post_consts = [kkp, ka, w0_f[0:1], w0_f[1:2], a0_f[0:1], a0_f[1:2], ones2]
    aa, dec0, dec1, kd0, kd1, bb0, bb1, ksum = row_call(
        "l0_post", f_post_fwd, post_rows, post_consts, [(D, F32)] * 8, tr=TS, sr=16)
    dec, kd, bb = (dec0, dec1), (kd0, kd1), (bb0, bb1)
    ys, sts = [], []
    y_d, st_d, first = scan_fwd("l0_scan0", r, dec[0], kd[0], v, aa, bb[0], L, False, side=gather_side(late_slots[:4]))
    ys.append(y_d)
    sts.append(st_d)
    y_d, st_d, filled = scan_fwd("l0_scan1", r, dec[1], kd[1], v, aa, bb[1], L, True,
                                 side=join_sides(pass_side(first), gather_whole_side(late_slots[4:])))
    ys.append(y_d)
    sts.append(st_d)
    gw.update(zip(late_names, filled))
    Wo = Stacked(gw['rw_wo'], "row", Dq)
    Win, Wout = Stacked(gw['sc_win'], "col", D), Stacked(gw['sc_wout'], "row", Dq)
    W13, W2f = Stacked(gw['ffn_w13'], "col", D), Stacked(gw['ffn_w2'], "row", F4)
    ro_rows = [ys[0], ys[1], r, ksum, v, g]
    ro_consts = [rk, lnw, lnb, ones2]
    og = row_call("l0_readout", f_readout, ro_rows, ro_consts, [(D, BF16)], tr=TS, sr=16)[0]
    yx = mm_nn("l0_o", og, Wo)
    res0_consts = [gt1x[0], n2g[0:1], sh2x[0], sc2x[0]]
    x1, h2 = row_call("l0_res", f_res_norm_mod, [x, yx], res0_consts, [(D, F32), (D, BF16)], tr=TR, sr=16,
                      offs=[0, nct])
    ab0 = mm_nn("l0_ffn13", h2, W13.at(0))
    sw0 = swiglu_fwd("l0_swiglu", ab0, TS)
    f0 = mm_nn("l0_ffn2", sw0, W2f.at(0))

    res1_consts = [gt2x[0], n1g[1:2], sh1x[1], sc1x[1]]
    x2, hb = row_call("l1_norm", f_res_norm_mod, [x1, f0], res1_consts, [(D, F32), (D, BF16)], tr=TR, sr=16)
    gcu = mm_nn("l1_win", hb, Win)
    pc = conv_fwd("l1_conv", gcu, conv_f)
    yx1 = mm_nn("l1_wout", pc, Wout)
    res2_consts = [gt1x[1], n2g[1:2], sh2x[1], sc2x[1]]
    x3, h2b = row_call("l1_res", f_res_norm_mod, [x2, yx1], res2_consts, [(D, F32), (D, BF16)], tr=TR, sr=16)
    ab1 = mm_nn("l1_ffn13", h2b, W13.at(1))
    sw1 = swiglu_fwd("l1_swiglu", ab1, TS)
    f1 = mm_nn("l1_ffn2", sw1, W2f.at(1))
    dx3, df1, dgt2_1, dfg, loss_blk = final_call("final", x3, f1, gt2x[1], fg, tgt, TR)
    loss = lax.psum(loss_blk[0, 0], ("x", "y", "c"))

    big = []
    dsw1 = mm_nt("b1_dsw", df1, W2f.at(1))
    gW2f1 = mm_tn("b1_gw2", sw1, df1, "row")
    dab1 = swiglu_bwd("b1_swiglu", ab1, dsw1, TS)
    dh2b = mm_nt("b1_dh2", dab1, W13.at(1))
    gW13_1 = mm_tn("b1_gw13", h2b, dab1, "col")
    rm = [True, True]
    cm = [True] * 4
    (dx2, dyx1), (dgt1_1, dn2g1, dsh2_1, dsc2_1) = row_vjp(
        "b1_res", f_res_norm_mod, [x2, yx1], res2_consts, [dx3, dh2b], row_mask=rm, const_mask=cm, tr=TR, sr=16,
        bf16_rows=(1,))
    dpc = mm_nt("b1_dpc", dyx1, Wout)
    big.append(('sc_wout', mm_tn("b1_gwout", pc, dyx1, "row")))
    dgcu, dconv = conv_bwd("b1_conv", gcu, conv_f, dpc)
    dhb = mm_nt("b1_dhb", dgcu, Win)
    big.append(('sc_win', mm_tn("b1_gwin", hb, dgcu, "col")))
    (dx1, df0), (dgt2_0, dn1g1, dsh1_1, dsc1_1) = row_vjp(
        "b1_norm", f_res_norm_mod, [x1, f0], res1_consts, [dx2, dhb], row_mask=rm, const_mask=cm, tr=TR, sr=16,
        bf16_rows=(1,))

    def halves4(items):
        return [g.reshape(N_CHIPS, 2, g.shape[1] // 2, g.shape[2]) for _, g in items]

    sc_items = [it for it in big if it[0] in ('sc_win', 'sc_wout')]
    ffn1_items = [('ffn_w13', gW13_1), ('ffn_w2', gW2f1)]
    dsw0, sc_got = mm_nt("b0_dsw", df0, W2f.at(0), side=rs1_side(halves4(sc_items)))
    gW2f0 = mm_tn("b0_gw2", sw0, df0, "row")
    dab0 = swiglu_bwd("b0_swiglu", ab0, dsw0, TS)
    dh2, ffn1_got = mm_nt("b0_dh2", dab0, W13.at(0), side=rs1_side(halves4(ffn1_items)))
    gW13_0, w2f0_got = mm_tn("b0_gw13", h2, dab0, "col", side=rs1_side(halves4([('ffn_w2', gW2f0)])))
    (dx_a, dyx), (dgt1_0, dn2g0, dsh2_0, dsc2_0) = row_vjp(
        "b0_res", f_res_norm_mod, [x, yx], res0_consts, [dx1, dh2], row_mask=rm, const_mask=cm, tr=TR, sr=16,
        offs=[0, nct], bf16_rows=(1,))
    dyx_all = jnp.concatenate([jnp.zeros((L, D), BF16), dyx], axis=0)
    dog = mm_nt("b0_dog", dyx_all, Wo)
    gWo = mm_tn("b0_gwo", og, dyx_all, "row")
    (dy, dr_ro, dksum, dv_ro, dg), (drk, dlnw, dlnb), w13_0_got = row_vjp(
        "b0_readout", f_readout, ro_rows, ro_consts, [dog], row_mask=[True, False, True, True, True, True],
        const_mask=[True, True, True, False], tr=TS, sr=16, bf16_rows=(5,),
        side=rs1_side(halves4([('ffn_w13', gW13_0)])))
    late_items = sc_items + [('ffn_w13', gW13_0), ffn1_items[0], ('ffn_w2', gW2f0), ffn1_items[1]]
    late_got = sc_got + [w13_0_got[0], ffn1_got[0], w2f0_got[0], ffn1_got[1]]
    late_sums = _rs_pair_sums("late", late_items, ci_arr, s_arr, late_got)
    (dr0, ddec0, dkd0, dv0, daa0, dbb0), landed_flat = scan_bwd(
        "b0_scan0", r, dec[0], kd[0], v, aa, bb[0], dy, sts[0], L, False, side=rs2_side([s_[1] for s_ in late_sums]))
    (dr1, ddec1, dkd1, dv1, daa1, dbb1), _ = scan_bwd("b0_scan1", r, dec[1], kd[1], v, aa, bb[1], dy, sts[1], L, True)
    late_landed = [landed_flat[3 * w:3 * w + 3] for w in range(len(late_items))]
    post_cots = [daa0, daa1, ddec0, ddec1, dkd0, dkd1, dbb0, dbb1, dksum]
    (dk, dzw0, dzw1, dza0, dza1), (dkkp, dka, dw00, dw01, da00, da01) = row_vjp(
        "b0_post", f_post, post_rows, post_consts, post_cots, row_mask=[True] * 5,
        const_mask=[True] * 6 + [False], tr=TS, sr=16, bf16_rows=(0, 1, 2, 3, 4))
    dzw, dza = (dzw0, dzw1), (dza0, dza1)
    dr_t = sum_cast("b0_drsum", [dr0, dr1, dr_ro], BF16, TR)
    dv_t = sum_cast("b0_dvsum", [dv0, dv1, dv_ro], BF16, TR)
    mix_cots, mix_slots = [], []

    def back(tag, cot, w, xin_m, kind, slot):
        mix_cots.append(mm_nt("b0_dx" + tag, cot, w))
        mix_slots.append(slot)
        return mm_tn("b0_gw" + tag, xin_m, cot, kind)

    big.append(('rw_wr', back("r", dr_t, Wr, xr, "row", 0)))
    big.append(('rw_wk', back("k", dk, Wk, xk, "row", 2)))
    big.append(('rw_wv', back("v", dv_t, Wv, xv, "row", 3)))
    big.append(('rw_wo', gWo))
    dsg = mm_nt("b0_dsg", dg, G2)
    gG2 = mm_tn("b0_gg2", sg, dg, "col")
    (dgl,), _ = row_vjp("b0_sg", f_sigmoid, [gl], [], [dsg], row_mask=[True], const_mask=[], tr=TR, sr=16,
                        bf16_rows=(0,))
    gG1 = back("g", dgl, G1, xg, "row", 5)
    gW1, gW2, gA1, gA2 = [], [], [], []
    for d in range(2):
        dtw = mm_nt(f"b0_dtw{d}", dzw[d], W2.at(d))
        gW2.append(mm_tn(f"b0_gw2{d}", tw[d], dzw[d], "col"))
        (dwl,), _ = row_vjp(f"b0_tw{d}", f_tanh, [wl[d]], [], [dtw], row_mask=[True], const_mask=[], tr=TR, sr=16,
                            bf16_rows=(0,))
        gW1.append(back(f"w{d}", dwl, W1.at(d), xw, "row", 1))
        dal = mm_nt(f"b0_dal{d}", dza[d], A2.at(d), BF16)
        gA2.append(mm_tn(f"b0_ga2{d}", al[d], dza[d], "col"))
        gA1.append(back(f"a{d}", dal, A1.at(d), xa, "row", 4))
    big += [('rw_w1', gW1[0]), ('rw_w1', gW1[1]), ('rw_w2', gW2[0]), ('rw_w2', gW2[1]),
            ('rw_a1', gA1[0]), ('rw_a1', gA1[1]), ('rw_a2', gA2[0]), ('rw_a2', gA2[1]),
            ('rw_g1', gG1), ('rw_g2', gG2)]
    dh, dmix = mix_bwd("b0_mix", h, mix_f, mix_cots, mix_slots, L)
    (dxin,), (dn1g0, dsh1c, dsc1c, dsh1x, dsc1x) = row_vjp(
        "b0_norm", nm, [xin], nm_consts, [dh], row_mask=[True], const_mask=[True] * 5, tr=TR, sr=16)
    grad_x = sum_cast("b0_dx", [dxin, dx_a], F32, TR, offs=[nct, 0])

    zero = jnp.zeros((1, D), F32)
    parts = [dsh1x, dsc1x, dgt1_0, dsh2_0, dsc2_0, dgt2_0, dsh1c, dsc1c, zero, zero, zero, zero,
             dsh1_1, dsc1_1, dgt1_1, dsh2_1, dsc2_1, dgt2_1, zero, zero, zero, zero, zero, zero,
             dn1g0, dn1g1, dn2g0, dn2g1, dkkp, dka, drk, dlnw, dlnb, dfg,
             dmix[0:6], dw00, dw01, da00, da01, dconv[0:3]]
    got2 = all_gather8("ag_grads", pack_rows("pack_grads", parts, 48))
    small = sum_blocks("sum_grads", got2, list(range(N_DEV)))
    per_ex = got2[:, 0:24].reshape(N_DEV, 2, 2, 6 * D)
    tot = small[0:24].reshape(2, 2, 6 * D)
    cols = lambda a, width: lax.dynamic_slice_in_dim(a, s * width, width, axis=1)
    g_ada_w, dcond_parts = [], []
    for i in range(2):
        dm16 = cols(jnp.concatenate([per_ex[:, i, 0], _pad_rows(tot[i, 1][None], 8)], axis=0), 6 * Dq)
        g_ada_w.append(mm_tn(f"g_ada{i}", cond, dm16))
        dcond_parts.append(mm_nt(f"dcond{i}", dm16, ada.at(i)))
    g_ada_b = sum_cast("g_adab", [_pad_rows(tot[:, 0].reshape(12, D), 16), _pad_rows(tot[:, 1].reshape(12, D), 16)],
                       F32, 16)[0:12].reshape(2, 6 * D)
    dcond_mine = sum_cast("dcond_sum", dcond_parts, F32, 16)
    dcond = sum_blocks("dcond_chips", all_gather8("ag_dcond", dcond_mine), [0, 2, 4, 6])
    (dcin,), _ = row_vjp("b_cond", f_silu, [cond_in], [], [dcond], row_mask=[True], const_mask=[], tr=16, sr=16)

    gsh = _rs_finish("late", late_items, late_sums, late_landed, s)
    rw_items = [it for it in big if it[0] in rw_names]
    rw_sums = _rs_pair_sums("rw", rw_items, ci_arr, s_arr)

    def view(n):
        return _view2d(n, p[n]), _view2d(n, p['m_' + n]), _view2d(n, p['v_' + n])

    done = {}
    *done['ffn_w13'], rw_flat = adamw_halves("adam_ffn_w13", *view('ffn_w13'), gsh['ffn_w13'][0], gsh['ffn_w13'][1],
                                             ci_arr, side=rs2_side([s_[1] for s_ in rw_sums]))
    gsh.update(_rs_finish("rw", rw_items, rw_sums, [rw_flat[3 * w:3 * w + 3] for w in range(len(rw_items))], s))

    grads = {}
    grads['c_ctx'] = dcin[8]
    grads['norm1_g'], grads['norm2_g'] = small[24:26], small[26:28]
    grads['ada_w'] = jnp.stack(g_ada_w)
    grads['ada_b'] = g_ada_b
    grads['rw_kk'], grads['rw_ka'], grads['rw_rk'] = small[28:29], small[29:30], small[30:31]
    grads['rw_lnw'], grads['rw_lnb'], grads['final_g'] = small[31:32], small[32:33], small[33]
    sharded = cols(small[34:48], Dq)
    grads['rw_mix'], grads['rw_w0'], grads['rw_a0'], grads['sc_conv'] = (
        sharded[0:6], sharded[6:8], sharded[8:10], sharded[10:13])

    outs_g, outs_d, outs_m, outs_v = [], [], [], []
    for n in WEIGHTS:
        shape = p[n].shape
        w2d, m2d, v2d = view(n)
        if n in done:
            g2d, d_, m_, v_ = done[n]
        elif n in gsh:
            g2d, d_, m_, v_, _ = adamw_halves("adam_" + n, w2d, m2d, v2d, gsh[n][0], gsh[n][1], ci_arr)
        else:
            g2d = _view2d(n, grads[n].reshape(shape))
            d_, m_, v_ = adamw("adam_" + n, w2d, g2d, m2d, v2d)
        outs_g.append(g2d.reshape(shape))
        outs_d.append(d_.reshape(shape))
        outs_m.append(m_.reshape(shape))
        outs_v.append(v_.reshape(shape))
    return (loss, grad_x.reshape(1, T, D), *outs_g, *outs_d, *outs_m, *outs_v)


def kernel(x, c, ctx, c_ctx, norm1_g, norm2_g, ada_w, ada_b, rw_mix, rw_wr, rw_wk, rw_wv, rw_wo, rw_w0, rw_w1, rw_w2, rw_a0, rw_a1, rw_a2, rw_g1, rw_g2, rw_kk, rw_ka, rw_rk, rw_lnw, rw_lnb, sc_win, sc_conv, sc_wout, ffn_w13, ffn_w2, final_g, loss_target, m_c_ctx, m_norm1_g, m_norm2_g, m_ada_w, m_ada_b, m_rw_mix, m_rw_wr, m_rw_wk, m_rw_wv, m_rw_wo, m_rw_w0, m_rw_w1, m_rw_w2, m_rw_a0, m_rw_a1, m_rw_a2, m_rw_g1, m_rw_g2, m_rw_kk, m_rw_ka, m_rw_rk, m_rw_lnw, m_rw_lnb, m_sc_win, m_sc_conv, m_sc_wout, m_ffn_w13, m_ffn_w2, m_final_g, v_c_ctx, v_norm1_g, v_norm2_g, v_ada_w, v_ada_b, v_rw_mix, v_rw_wr, v_rw_wk, v_rw_wv, v_rw_wo, v_rw_w0, v_rw_w1, v_rw_w2, v_rw_a0, v_rw_a1, v_rw_a2, v_rw_g1, v_rw_g2, v_rw_kk, v_rw_ka, v_rw_rk, v_rw_lnw, v_rw_lnb, v_sc_win, v_sc_conv, v_sc_wout, v_ffn_w13, v_ffn_w2, v_final_g):
    values = (x, c, ctx, c_ctx, norm1_g, norm2_g, ada_w, ada_b, rw_mix, rw_wr, rw_wk, rw_wv, rw_wo, rw_w0, rw_w1, rw_w2, rw_a0, rw_a1, rw_a2, rw_g1, rw_g2, rw_kk, rw_ka, rw_rk, rw_lnw, rw_lnb, sc_win, sc_conv, sc_wout, ffn_w13, ffn_w2, final_g, loss_target, m_c_ctx, m_norm1_g, m_norm2_g, m_ada_w, m_ada_b, m_rw_mix, m_rw_wr, m_rw_wk, m_rw_wv, m_rw_wo, m_rw_w0, m_rw_w1, m_rw_w2, m_rw_a0, m_rw_a1, m_rw_a2, m_rw_g1, m_rw_g2, m_rw_kk, m_rw_ka, m_rw_rk, m_rw_lnw, m_rw_lnb, m_sc_win, m_sc_conv, m_sc_wout, m_ffn_w13, m_ffn_w2, m_final_g, v_c_ctx, v_norm1_g, v_norm2_g, v_ada_w, v_ada_b, v_rw_mix, v_rw_wr, v_rw_wk, v_rw_wv, v_rw_wo, v_rw_w0, v_rw_w1, v_rw_w2, v_rw_a0, v_rw_a1, v_rw_a2, v_rw_g1, v_rw_g2, v_rw_kk, v_rw_ka, v_rw_rk, v_rw_lnw, v_rw_lnb, v_sc_win, v_sc_conv, v_sc_wout, v_ffn_w13, v_ffn_w2, v_final_g)
    return _step(dict(zip(INPUTS, values)))
```
